```python
import math
import jax, jax.numpy as jnp
from jax import lax
import numpy as np

D_MODEL = 1024
BATCH = 4
SEQ = 8192
DEPTH = 1

CHUNK = 64
PLE_DIM = 256
RMS_EPS = 1e-6
LN_EPS = 1e-5

SGU_HEADS = 8
SGU_HEAD_DIM = 128
SGU_WIDTH = SGU_HEADS * SGU_HEAD_DIM
SGU_LEN = 128

SSM_HEADS = 16
SSM_HEAD_DIM = 64
SSM_INNER = SSM_HEADS * SSM_HEAD_DIM
SSM_GROUPS = 4
SSM_STATE = 128
SSM_CONV = 4
SSD_CHUNK = 128
SSM_CONV_DIM = SSM_INNER + 2 * SSM_GROUPS * SSM_STATE

N_EXPERTS = 32
TOP_K = 4
D_FF_EXPERT = 1024
SWIGLU_LIMIT = 7.0
SWIGLU_ALPHA = 1.702
MOE_BLOCK = 256

OFF_U = 0
OFF_V = OFF_U + SGU_WIDTH
OFF_Z = OFF_V + SGU_WIDTH
OFF_XBC = OFF_Z + SSM_INNER
OFF_DT = OFF_XBC + SSM_CONV_DIM
OFF_GA = OFF_DT + SSM_HEADS
OFF_GB = OFF_GA + D_MODEL
IN_PROJ_DIM = OFF_GB + D_MODEL

kernel_name = "hybrid_sgu_ssd_moe_streaming_block"


def rms_norm(x, g):
    xf = x.astype(jnp.float32)
    y = xf * lax.rsqrt(jnp.mean(xf * xf, axis=-1, keepdims=True) + RMS_EPS)
    return (y * g.astype(jnp.float32)).astype(x.dtype)


def layer_norm(x, g, b):
    xf = x.astype(jnp.float32)
    mu = jnp.mean(xf, axis=-1, keepdims=True)
    var = jnp.mean(jnp.square(xf - mu), axis=-1, keepdims=True)
    y = (xf - mu) * lax.rsqrt(var + LN_EPS)
    return (y * g.astype(jnp.float32) + b.astype(jnp.float32)).astype(x.dtype)


def sgu_branch(uv, ln_g, ln_b, w_s, b_s):
    bsz, seq, _ = uv.shape
    uv = jax.nn.gelu(uv, approximate=False)
    u, v = jnp.split(uv, 2, axis=-1)
    v = layer_norm(v, ln_g, ln_b)
    v = v.reshape(bsz, seq // SGU_LEN, SGU_LEN, SGU_HEADS, SGU_HEAD_DIM)
    pos = jnp.arange(SGU_LEN)
    allowed = (pos[None, :] // CHUNK) <= (pos[:, None] // CHUNK)
    w = jnp.where(allowed[None], w_s, jnp.zeros_like(w_s))
    mixed = jnp.einsum("gts,bnsgc->bntgc", w, v) + b_s.T[:, :, None]
    return u * mixed.reshape(bsz, seq, SGU_WIDTH)


def causal_depthwise_conv(x, w, b):
    y = lax.conv_general_dilated(
        x, w[:, None, :].astype(x.dtype), window_strides=(1,),
        padding=[(SSM_CONV - 1, 0)], dimension_numbers=("NWC", "WIO", "NWC"),
        feature_group_count=x.shape[-1])
    return y + b


def ssd_scan(x, dt, a, b_mat, c_mat):
    bsz, seq = x.shape[:2]
    nc, q = seq // SSD_CHUNK, SSD_CHUNK
    g, r = SSM_GROUPS, SSM_HEADS // SSM_GROUPS
    p, n = SSM_HEAD_DIM, SSM_STATE
    xf = x.astype(jnp.float32).reshape(bsz, nc, q, g, r, p)
    dt = dt.reshape(bsz, nc, q, g, r)
    bm = b_mat.astype(jnp.float32).reshape(bsz, nc, q, g, n)
    cm = c_mat.astype(jnp.float32).reshape(bsz, nc, q, g, n)
    a_dt = (dt * a.reshape(g, r)).transpose(0, 3, 4, 1, 2)
    a_cs = jnp.cumsum(a_dt, axis=-1)
    tri = jnp.tril(jnp.ones((q, q), dtype=bool))
    seg = a_cs[..., :, None] - a_cs[..., None, :]
    decay_in = jnp.exp(jnp.where(tri, seg, -jnp.inf))
    xdt = xf * dt[..., None]
    cb = jnp.einsum("bcqgn,bcsgn->bcgqs", cm, bm)
    y_diag = jnp.einsum("bcgqs,bgrcqs,bcsgrp->bcqgrp", cb, decay_in, xdt)
    decay_to_end = jnp.exp(a_cs[..., -1:] - a_cs)
    chunk_states = jnp.einsum("bcsgn,bgrcs,bcsgrp->cbgrpn", bm, decay_to_end, xdt)
    chunk_decay = jnp.exp(a_cs[..., -1]).transpose(3, 0, 1, 2)

    def step(h, inp):
        s_c, a_c = inp
        return a_c[..., None, None] * h + s_c, h

    h0 = jnp.zeros((bsz, g, r, p, n), jnp.float32)
    _, prev_states = lax.scan(step, h0, (chunk_states, chunk_decay))
    y_off = jnp.einsum("bcqgn,cbgrpn,bgrcq->bcqgrp", cm, prev_states, jnp.exp(a_cs))
    return (y_diag + y_off).reshape(bsz, seq, SSM_HEADS, SSM_HEAD_DIM)


def mamba2_branch(z, xbc, dt_raw, conv_w, conv_b, dt_bias, a_log, d_skip, norm_w):
    bsz, seq, _ = z.shape
    xbc = jax.nn.silu(causal_depthwise_conv(xbc, conv_w, conv_b))
    xs, bm, cm = jnp.split(xbc, [SSM_INNER, SSM_INNER + SSM_GROUPS * SSM_STATE], axis=-1)
    xs = xs.reshape(bsz, seq, SSM_HEADS, SSM_HEAD_DIM)
    bm = bm.reshape(bsz, seq, SSM_GROUPS, SSM_STATE)
    cm = cm.reshape(bsz, seq, SSM_GROUPS, SSM_STATE)
    dt = jax.nn.softplus(dt_raw.astype(jnp.float32) + dt_bias.astype(jnp.float32))
    a = -jnp.exp(a_log.astype(jnp.float32))
    y = ssd_scan(xs, dt, a, bm, cm) + d_skip.astype(jnp.float32)[:, None] * xs.astype(jnp.float32)
    y = y.reshape(bsz, seq, SSM_GROUPS, SSM_INNER // SSM_GROUPS)
    y = y * jax.nn.silu(z.astype(jnp.float32)).reshape(y.shape)
    y = y * lax.rsqrt(jnp.mean(y * y, axis=-1, keepdims=True) + RMS_EPS)
    y = y.reshape(bsz, seq, SSM_INNER) * norm_w.astype(jnp.float32)
    return y.astype(z.dtype)


def moe_ffn(h, w_router, b_router, w1, b1, w2, b2):
    bsz, seq, d = h.shape
    t = h.reshape(-1, d)
    n = t.shape[0]
    logits = (t @ w_router + b_router).astype(jnp.float32)
    top_val, top_idx = lax.top_k(logits, TOP_K)
    gate = jax.nn.softmax(top_val, axis=-1)
    flat_e = top_idx.reshape(-1).astype(jnp.int32)
    flat_tok = jnp.repeat(jnp.arange(n, dtype=jnp.int32), TOP_K)
    flat_gate = gate.reshape(-1)
    nk = n * TOP_K
    order = jnp.argsort(flat_e)
    sorted_e = flat_e[order]
    counts = jnp.bincount(flat_e, length=N_EXPERTS).astype(jnp.int32)
    padded = (counts + MOE_BLOCK - 1) // MOE_BLOCK * MOE_BLOCK
    start = jnp.cumsum(counts) - counts
    pend = jnp.cumsum(padded)
    pstart = pend - padded
    rank = jnp.arange(nk, dtype=jnp.int32) - start[sorted_e]
    dest = pstart[sorted_e] + rank
    cap = (nk + MOE_BLOCK - 1) // MOE_BLOCK * MOE_BLOCK + N_EXPERTS * MOE_BLOCK
    n_blocks = cap // MOE_BLOCK
    buf_tok = jnp.zeros((cap,), jnp.int32).at[dest].set(flat_tok[order])
    buf_gate = jnp.zeros((cap,), jnp.float32).at[dest].set(flat_gate[order])
    blk_start = jnp.arange(n_blocks, dtype=jnp.int32) * MOE_BLOCK
    blk_expert = jnp.minimum(jnp.searchsorted(pend, blk_start, side="right"), N_EXPERTS - 1)
    xs = t[buf_tok].reshape(n_blocks, MOE_BLOCK, d)

    def expert_block(args):
        xb, e = args
        hid = xb @ w1[e] + b1[e]
        glu, lin = jnp.split(hid, 2, axis=-1)
        glu = jnp.minimum(glu, SWIGLU_LIMIT)
        lin = jnp.clip(lin, -SWIGLU_LIMIT, SWIGLU_LIMIT)
        act = glu * jax.nn.sigmoid(SWIGLU_ALPHA * glu) * (lin + 1)
        return act @ w2[e] + b2[e]

    ys = lax.map(expert_block, (xs, blk_expert)).reshape(cap, d)
    out = jax.ops.segment_sum(ys.astype(jnp.float32) * buf_gate[:, None], buf_tok, num_segments=n)
    return out.astype(h.dtype).reshape(bsz, seq, d)


def setup_inputs(seed: int = 0) -> dict:
    key = jax.random.key(seed)
    ks = iter(jax.random.split(key, 40))

    def nrm(shape, scale):
        return jax.random.normal(next(ks), shape, jnp.float32) * scale

    def gain(shape):
        return 1.0 + nrm(shape, 0.05)

    L = DEPTH
    dt0 = jnp.exp(jax.random.uniform(next(ks), (L, SSM_HEADS), jnp.float32)
                  * (math.log(0.1) - math.log(0.001)) + math.log(0.001))
    dt_bias = dt0 + jnp.log(-jnp.expm1(-dt0))
    a_log = jnp.log(jax.random.uniform(next(ks), (L, SSM_HEADS), jnp.float32, 1.0, 16.0))
    return {
        "x": nrm((BATCH, SEQ, D_MODEL), 1.0),
        "p": nrm((L, BATCH, SEQ, PLE_DIM), 1.0),
        "mix_norm": gain((L, D_MODEL)),
        "w_in": nrm((L, D_MODEL, IN_PROJ_DIM), D_MODEL ** -0.5),
        "sgu_ln_g": gain((L, SGU_WIDTH)),
        "sgu_ln_b": nrm((L, SGU_WIDTH), 0.02),
        "sgu_w": nrm((L, SGU_HEADS, SGU_LEN, SGU_LEN), SGU_LEN ** -0.5),
        "sgu_b": 1.0 + nrm((L, SGU_HEADS, SGU_LEN), 0.1),
        "conv_w": nrm((L, SSM_CONV, SSM_CONV_DIM), SSM_CONV ** -0.5),
        "conv_b": nrm((L, SSM_CONV_DIM), 0.02),
        "dt_bias": dt_bias,
        "a_log": a_log,
        "d_skip": 1.0 + nrm((L, SSM_HEADS), 0.1),
        "ssm_norm": gain((L, SSM_INNER)),
        "w_branch_a": nrm((L, SGU_WIDTH, D_MODEL), SGU_WIDTH ** -0.5),
        "w_branch_b": nrm((L, SSM_INNER, D_MODEL), SSM_INNER ** -0.5),
        "w_out": nrm((L, D_MODEL, D_MODEL), D_MODEL ** -0.5),
        "ffn_norm": gain((L, D_MODEL)),
        "w_router": nrm((L, D_MODEL, N_EXPERTS), D_MODEL ** -0.5),
        "b_router": nrm((L, N_EXPERTS), 0.01),
        "w1": nrm((L, N_EXPERTS, D_MODEL, 2 * D_FF_EXPERT), D_MODEL ** -0.5),
        "b1": nrm((L, N_EXPERTS, 2 * D_FF_EXPERT), 0.01),
        "w2": nrm((L, N_EXPERTS, D_FF_EXPERT, D_MODEL), D_FF_EXPERT ** -0.5),
        "b2": nrm((L, N_EXPERTS, D_MODEL), 0.01),
        "ple_norm": gain((L, D_MODEL)),
        "w_ple_gate": nrm((L, D_MODEL, D_MODEL), D_MODEL ** -0.5),
        "w_ple_proj": nrm((L, PLE_DIM, D_MODEL), PLE_DIM ** -0.5),
        "final_norm": gain((D_MODEL,)),
    }


def reference(x, p, mix_norm, w_in, sgu_ln_g, sgu_ln_b, sgu_w, sgu_b, conv_w, conv_b,
              dt_bias, a_log, d_skip, ssm_norm, w_branch_a, w_branch_b, w_out,
              ffn_norm, w_router, b_router, w1, b1, w2, b2,
              ple_norm, w_ple_gate, w_ple_proj, final_norm):
    for i in range(DEPTH):
        h = rms_norm(x, mix_norm[i])
        proj = h @ w_in[i]
        y_a = sgu_branch(proj[..., OFF_U:OFF_Z], sgu_ln_g[i], sgu_ln_b[i], sgu_w[i], sgu_b[i])
        y_b = mamba2_branch(proj[..., OFF_Z:OFF_XBC], proj[..., OFF_XBC:OFF_DT],
                            proj[..., OFF_DT:OFF_GA], conv_w[i], conv_b[i], dt_bias[i],
                            a_log[i], d_skip[i], ssm_norm[i])
        g_a = jax.nn.sigmoid(proj[..., OFF_GA:OFF_GB])
        g_b = jax.nn.sigmoid(proj[..., OFF_GB:IN_PROJ_DIM])
        merged = g_a * (y_a @ w_branch_a[i]) + g_b * (y_b @ w_branch_b[i])
        x = x + merged @ w_out[i]
        x = x + moe_ffn(rms_norm(x, ffn_norm[i]), w_router[i], b_router[i],
                        w1[i], b1[i], w2[i], b2[i])
        hp = rms_norm(x, ple_norm[i])
        x = x + jax.nn.sigmoid(hp @ w_ple_gate[i]) * (p[i] @ w_ple_proj[i])
    return rms_norm(x, final_norm)
```

```python
import functools
import math

import jax
import jax.numpy as jnp
from jax import lax
from jax.experimental import pallas as pl
from jax.experimental.pallas import tpu as pltpu

F32 = jnp.float32
BF16 = jnp.bfloat16
I32 = jnp.int32
U32 = jnp.uint32

D_MODEL = 1024
CHUNK = 64
PLE_DIM = 256
RMS_EPS = 1e-6
LN_EPS = 1e-5

SGU_HEADS = 8
SGU_HEAD_DIM = 128
SGU_WIDTH = SGU_HEADS * SGU_HEAD_DIM
SGU_LEN = 128

SSM_HEADS = 16
SSM_HEAD_DIM = 64
SSM_INNER = SSM_HEADS * SSM_HEAD_DIM
SSM_GROUPS = 4
SSM_STATE = 128
SSM_CONV = 4
SSD_CHUNK = 128
SSM_CONV_DIM = SSM_INNER + 2 * SSM_GROUPS * SSM_STATE

N_EXPERTS = 32
TOP_K = 4
D_FF_EXPERT = 1024
SWIGLU_LIMIT = 7.0
SWIGLU_ALPHA = 1.702
MOE_BLOCK = 256

OFF_U = 0
OFF_V = OFF_U + SGU_WIDTH
OFF_Z = OFF_V + SGU_WIDTH
OFF_XBC = OFF_Z + SSM_INNER
OFF_DT = OFF_XBC + SSM_CONV_DIM
OFF_GA = OFF_DT + SSM_HEADS
OFF_GB = OFF_GA + D_MODEL
IN_PROJ_DIM = OFF_GB + D_MODEL

LANES = 128
PROJ_MAIN = IN_PROJ_DIM - SSM_HEADS
CONV_PAD = 8

VMEM_LIMIT = 56 * 1024 * 1024

IN_TM = 1024
IN_TN = 1024
MIX_T = 512
ROUTE_T = 512
DISP_T = 512
COMB_T = 512


def _sigmoid(x):
    return 1.0 / (1.0 + jnp.exp(-x))


def _gelu_exact(x):
    return 0.5 * x * (1.0 + lax.erf(x * (1.0 / math.sqrt(2.0))))


def _softplus(x):
    return jnp.maximum(x, 0.0) + jnp.log1p(jnp.exp(-jnp.abs(x)))


def _rms(x, g):
    ms = jnp.mean(x * x, axis=-1, keepdims=True)
    return x * lax.rsqrt(ms + RMS_EPS) * g


def _inproj_kernel(x_ref, g_ref, w_ref, wdt_ref, proj_ref, dt_ref, h_scr):
    @pl.when(pl.program_id(1) == 0)
    def _():
        h = _rms(x_ref[...], g_ref[...]).astype(BF16)
        h_scr[...] = h
        dt_ref[...] = jnp.dot(h, wdt_ref[...], preferred_element_type=F32)

    proj_ref[...] = jnp.dot(h_scr[...], w_ref[...], preferred_element_type=F32).astype(BF16)


def _in_proj(x2d, g, w_main, w_dt):
    n = x2d.shape[0]
    grid = (n // IN_TM, PROJ_MAIN // IN_TN)
    return pl.pallas_call(
        _inproj_kernel,
        grid=grid,
        in_specs=[
            pl.BlockSpec((IN_TM, D_MODEL), lambda i, j: (i, 0)),
            pl.BlockSpec((1, D_MODEL), lambda i, j: (0, 0)),
            pl.BlockSpec((D_MODEL, IN_TN), lambda i, j: (0, j)),
            pl.BlockSpec((D_MODEL, LANES), lambda i, j: (0, 0)),
        ],
        out_specs=[
            pl.BlockSpec((IN_TM, IN_TN), lambda i, j: (i, j)),
            pl.BlockSpec((IN_TM, LANES), lambda i, j: (i, 0)),
        ],
        out_shape=[
            jax.ShapeDtypeStruct((n, PROJ_MAIN), BF16),
            jax.ShapeDtypeStruct((n, LANES), F32),
        ],
        scratch_shapes=[pltpu.VMEM((IN_TM, D_MODEL), BF16)],
        compiler_params=pltpu.CompilerParams(
            dimension_semantics=("arbitrary", "arbitrary"), vmem_limit_bytes=VMEM_LIMIT),
        name="in_proj",
    )(x2d, g, w_main, w_dt)


def _mixer_kernel(u_ref, v_ref, z_ref, xb0_ref, xb1_ref, ga_ref, gb_ref, dt_ref, x_ref,
                  lng_ref, lnb_ref, wsgu_ref, bsgu_ref, convw_ref, convb_ref, dtb_ref, arow_ref,
                  dskip_ref, normw_ref, eexp_ref, wa_ref, wb_ref, wo_ref,
                  x1_ref,
                  xpad_scr, xc_scr, state_scr, ya_scr, yb_scr):
    nchunk = MIX_T // SSD_CHUNK

    @pl.when(pl.program_id(1) == 0)
    def _():
        xpad_scr[0:CONV_PAD, :] = jnp.zeros((CONV_PAD, SSM_CONV_DIM), F32)
        state_scr[...] = jnp.zeros_like(state_scr)

    xpad_scr[CONV_PAD:CONV_PAD + MIX_T, 0:1024] = xb0_ref[...].astype(F32)
    xpad_scr[CONV_PAD:CONV_PAD + MIX_T, 1024:2048] = xb1_ref[...].astype(F32)
    for c in range(nchunk):
        r0 = c * SSD_CHUNK
        acc = jnp.broadcast_to(convb_ref[...], (SSD_CHUNK, SSM_CONV_DIM))
        for j in range(SSM_CONV):
            off = CONV_PAD - (SSM_CONV - 1) + j + r0
            acc = acc + convw_ref[j:j + 1, :] * xpad_scr[off:off + SSD_CHUNK, :]
        xc_scr[r0:r0 + SSD_CHUNK, :] = acc * _sigmoid(acc)
    xpad_scr[0:CONV_PAD, :] = xpad_scr[MIX_T:MIX_T + CONV_PAD, :]

    row_i = lax.broadcasted_iota(I32, (SSD_CHUNK, SSD_CHUNK), 0)
    col_i = lax.broadcasted_iota(I32, (SSD_CHUNK, SSD_CHUNK), 1)
    tril = row_i >= col_i
    tril_f = tril.astype(F32)
    lane_hi = col_i >= SSM_HEAD_DIM

    def chunk_body(c, carry):
        r0 = pl.multiple_of(c * SSD_CHUNK, SSD_CHUNK)
        rows = pl.ds(r0, SSD_CHUNK)

        ug = _gelu_exact(u_ref[rows, :].astype(F32))
        vg = _gelu_exact(v_ref[rows, :].astype(F32))
        mu = jnp.mean(vg, axis=-1, keepdims=True)
        vc = vg - mu
        var = jnp.mean(vc * vc, axis=-1, keepdims=True)
        vn = (vc * lax.rsqrt(var + LN_EPS) * lng_ref[...] + lnb_ref[...]).astype(BF16)
        for g in range(SGU_HEADS):
            cols = slice(g * SGU_HEAD_DIM, (g + 1) * SGU_HEAD_DIM)
            mixed = jnp.dot(wsgu_ref[g], vn[:, cols], preferred_element_type=F32)
            ya_scr[rows, cols] = (ug[:, cols] * (mixed + bsgu_ref[:, cols])).astype(BF16)

        dtc = _softplus(dt_ref[rows, :] + dtb_ref[...])
        adt = dtc * arow_ref[...]
        acs = jnp.dot(tril_f, adt, preferred_element_type=F32, precision=lax.Precision.HIGHEST)
        acs_t = acs.T
        dt_t = dtc.T
        w_t = jnp.exp(acs_t[:, SSD_CHUNK - 1:SSD_CHUNK] - acs_t) * dt_t
        aend = jnp.broadcast_to(acs[SSD_CHUNK - 1:SSD_CHUNK, :], (8, LANES))
        dec = jnp.exp(jnp.dot(aend, eexp_ref[...], preferred_element_type=F32,
                              precision=lax.Precision.HIGHEST)[0:1, :])

        for gi in range(SSM_GROUPS):
            bg = xc_scr[rows, SSM_INNER + gi * SSM_STATE:SSM_INNER + (gi + 1) * SSM_STATE]
            cg = xc_scr[rows, SSM_INNER + (SSM_GROUPS + gi) * SSM_STATE:
                        SSM_INNER + (SSM_GROUPS + gi + 1) * SSM_STATE]
            bg_t = bg.T
            cb = jnp.dot(cg.astype(BF16), bg_t.astype(BF16), preferred_element_type=F32)
            for pr in range(2):
                lb = gi * 2 + pr
                cols = slice(lb * LANES, (lb + 1) * LANES)
                xpair = xc_scr[rows, cols]
                hpair = state_scr[:, cols]
                acc = jnp.zeros((SSD_CHUNK, LANES), F32)
                st = jnp.zeros((SSM_STATE, LANES), F32)
                for hh in range(2):
                    h = lb * 2 + hh
                    lmask = lane_hi if hh == 1 else jnp.logical_not(lane_hi)
                    xm = jnp.where(lmask, xpair, 0.0).astype(BF16)
                    hm = jnp.where(lmask, hpair, 0.0).astype(BF16)
                    colb = acs[:, h:h + 1]
                    rowb = acs_t[h:h + 1, :]
                    decay = jnp.exp(jnp.where(tril, colb - rowb, -jnp.inf))
                    m_h = (cb * decay * dt_t[h:h + 1, :]).astype(BF16)
                    c_h = (cg * jnp.exp(colb)).astype(BF16)
                    acc = acc + jnp.dot(m_h, xm, preferred_element_type=F32)
                    acc = acc + jnp.dot(c_h, hm, preferred_element_type=F32)
                    btw = (bg_t * w_t[h:h + 1, :]).astype(BF16)
                    st = st + jnp.dot(btw, xm, preferred_element_type=F32)
                state_scr[:, cols] = hpair * dec[:, cols] + st
                y = acc + dskip_ref[:, cols] * xpair
                zz = z_ref[rows, cols].astype(F32)
                xc_scr[rows, cols] = y * (zz * _sigmoid(zz))
            gcols = slice(gi * 2 * LANES, (gi + 1) * 2 * LANES)
            yg = xc_scr[rows, gcols]
            ms = jnp.mean(yg * yg, axis=-1, keepdims=True)
            yb_scr[rows, gcols] = (yg * lax.rsqrt(ms + RMS_EPS) * normw_ref[:, gcols]).astype(BF16)
        return carry

    lax.fori_loop(0, nchunk, chunk_body, 0)

    ma = jnp.dot(ya_scr[...], wa_ref[...], preferred_element_type=F32)
    mb = jnp.dot(yb_scr[...], wb_ref[...], preferred_element_type=F32)
    merged = (_sigmoid(ga_ref[...].astype(F32)) * ma + _sigmoid(gb_ref[...].astype(F32)) * mb)
    x1_ref[...] = x_ref[...] + jnp.dot(merged.astype(BF16), wo_ref[...], preferred_element_type=F32)


def _mixer(proj, dt_raw, x2d, bsz, seq, consts, wa, wb, wo):
    n = x2d.shape[0]
    tps = seq // MIX_T
    row = lambda b, j: b * tps + j

    def pcol(k):
        return pl.BlockSpec((MIX_T, 1024), lambda b, j, k=k: (row(b, j), k))

    def full(a):
        nd = a.ndim
        return pl.BlockSpec(a.shape, lambda b, j, nd=nd: (0,) * nd)

    in_specs = [pcol(0), pcol(1), pcol(2), pcol(3), pcol(4), pcol(5), pcol(6),
                pl.BlockSpec((MIX_T, LANES), lambda b, j: (row(b, j), 0)),
                pl.BlockSpec((MIX_T, D_MODEL), lambda b, j: (row(b, j), 0))]
    in_specs += [full(a) for a in consts] + [full(wa), full(wb), full(wo)]
    return pl.pallas_call(
        _mixer_kernel,
        grid=(bsz, tps),
        in_specs=in_specs,
        out_specs=pl.BlockSpec((MIX_T, D_MODEL), lambda b, j: (row(b, j), 0)),
        out_shape=jax.ShapeDtypeStruct((n, D_MODEL), F32),
        scratch_shapes=[
            pltpu.VMEM((MIX_T + 2 * CONV_PAD, SSM_CONV_DIM), F32),
            pltpu.VMEM((MIX_T, SSM_CONV_DIM), F32),
            pltpu.VMEM((SSM_STATE, SSM_INNER), F32),
            pltpu.VMEM((MIX_T, SGU_WIDTH), BF16),
            pltpu.VMEM((MIX_T, SSM_INNER), BF16),
        ],
        compiler_params=pltpu.CompilerParams(
            dimension_semantics=("arbitrary", "arbitrary"), vmem_limit_bytes=VMEM_LIMIT),
        name="mixer",
    )(proj, proj, proj, proj, proj, proj, proj, dt_raw, x2d, *consts, wa, wb, wo)


def _router_kernel(x_ref, g_ref, wr_ref, br_ref, hp_ref, idx_ref, gate_ref, rank_ref, cnt_ref, carry_scr):
    @pl.when(pl.program_id(0) == 0)
    def _():
        carry_scr[...] = jnp.zeros_like(carry_scr)

    h = _rms(x_ref[...], g_ref[...])
    hb = h.astype(BF16).astype(F32)
    half = D_MODEL // 2
    hi = pltpu.bitcast(hb[:, :half], U32)
    lo = pltpu.bitcast(hb[:, half:], U32)
    hp_ref[...] = (hi & jnp.uint32(0xFFFF0000)) | (lo >> 16)

    logits = jnp.dot(h, wr_ref[...], preferred_element_type=F32, precision=lax.Precision.HIGHEST)
    lt = logits.T[0:N_EXPERTS, :] + br_ref[...]
    eidx = lax.broadcasted_iota(I32, (N_EXPERTS, ROUTE_T), 0).astype(F32)
    vals = lt
    sel_any = jnp.zeros((N_EXPERTS, ROUTE_T), F32)
    sels, tops = [], []
    for k in range(TOP_K):
        m = jnp.max(vals, axis=0, keepdims=True)
        first = jnp.min(jnp.where(vals == m, eidx, float(N_EXPERTS)), axis=0, keepdims=True)
        sel = eidx == first
        vals = jnp.where(sel, -jnp.inf, vals)
        sel_f = sel.astype(F32)
        sel_any = sel_any + sel_f
        sels.append(sel_f)
        tops.append(m)
        idx_ref[k:k + 1, :] = first.astype(I32)
    es = [jnp.exp(t - tops[0]) for t in tops]
    denom = es[0] + es[1] + es[2] + es[3]
    for k in range(TOP_K):
        gate_ref[k:k + 1, :] = es[k] / denom

    r_i = lax.broadcasted_iota(I32, (ROUTE_T, ROUTE_T), 0)
    c_i = lax.broadcasted_iota(I32, (ROUTE_T, ROUTE_T), 1)
    upper = (r_i < c_i).astype(BF16)
    excl = jnp.dot(sel_any.astype(BF16), upper, preferred_element_type=F32) + carry_scr[:, 0:1]
    for k in range(TOP_K):
        rank_ref[k:k + 1, :] = jnp.sum(sels[k] * excl, axis=0, keepdims=True).astype(I32)
    new_carry = carry_scr[...] + jnp.sum(sel_any, axis=1, keepdims=True)
    carry_scr[...] = new_carry
    cnt_ref[...] = new_carry.astype(I32)


def _router(x1, g, wr_pad, br_col):
    n = x1.shape[0]
    return pl.pallas_call(
        _router_kernel,
        grid=(n // ROUTE_T,),
        in_specs=[
            pl.BlockSpec((ROUTE_T, D_MODEL), lambda i: (i, 0)),
            pl.BlockSpec((1, D_MODEL), lambda i: (0, 0)),
            pl.BlockSpec((D_MODEL, LANES), lambda i: (0, 0)),
            pl.BlockSpec((N_EXPERTS, 1), lambda i: (0, 0)),
        ],
        out_specs=[
            pl.BlockSpec((ROUTE_T, D_MODEL // 2), lambda i: (i, 0)),
            pl.BlockSpec((TOP_K, ROUTE_T), lambda i: (0, i)),
            pl.BlockSpec((TOP_K, ROUTE_T), lambda i: (0, i)),
            pl.BlockSpec((TOP_K, ROUTE_T), lambda i: (0, i)),
            pl.BlockSpec((N_EXPERTS, LANES), lambda i: (0, 0)),
        ],
        out_shape=[
            jax.ShapeDtypeStruct((n, D_MODEL // 2), U32),
            jax.ShapeDtypeStruct((TOP_K, n), I32),
            jax.ShapeDtypeStruct((TOP_K, n), F32),
            jax.ShapeDtypeStruct((TOP_K, n), I32),
            jax.ShapeDtypeStruct((N_EXPERTS, LANES), I32),
        ],
        scratch_shapes=[pltpu.VMEM((N_EXPERTS, LANES), F32)],
        compiler_params=pltpu.CompilerParams(
            dimension_semantics=("arbitrary",), vmem_limit_bytes=VMEM_LIMIT),
        name="router",
    )(x1, g, wr_pad, br_col)


def _dispatch_kernel(dest_hbm, hp_hbm, xs_in_hbm, xs_hbm, dest_smem, sem_idx, sem_rows):
    del xs_in_hbm
    i = pl.program_id(0)
    base = pl.multiple_of(i * DISP_T, DISP_T)
    idx_copy = pltpu.make_async_copy(dest_hbm.at[:, pl.ds(base, DISP_T)], dest_smem, sem_idx)
    idx_copy.start()
    idx_copy.wait()

    def row_copy(t, k):
        d = dest_smem[k, t]
        return pltpu.make_async_copy(hp_hbm.at[pl.ds(base + t, 1), :], xs_hbm.at[pl.ds(d, 1), :], sem_rows)

    def issue(t, carry):
        for k in range(TOP_K):
            row_copy(t, k).start()
        return carry

    lax.fori_loop(0, DISP_T, issue, 0)

    def drain(t, carry):
        for k in range(TOP_K):
            row_copy(t, k).wait()
        return carry

    lax.fori_loop(0, DISP_T, drain, 0)


def _dispatch(dest, hp, cap):
    n = hp.shape[0]
    xs0 = jnp.zeros((cap, D_MODEL // 2), U32)
    return pl.pallas_call(
        _dispatch_kernel,
        grid=(n // DISP_T,),
        in_specs=[pl.BlockSpec(memory_space=pl.ANY)] * 3,
        out_specs=pl.BlockSpec(memory_space=pl.ANY),
        out_shape=jax.ShapeDtypeStruct((cap, D_MODEL // 2), U32),
        scratch_shapes=[pltpu.SMEM((TOP_K, DISP_T), I32),
                        pltpu.SemaphoreType.DMA, pltpu.SemaphoreType.DMA],
        input_output_aliases={2: 0},
        compiler_params=pltpu.CompilerParams(dimension_semantics=("arbitrary",)),
        name="dispatch",
    )(dest, hp, xs0)


def _expert_kernel(be_ref, nu_ref, xs_ref, w1_ref, b1_ref, w2_ref, b2_ref, ys_ref):
    @pl.when(pl.program_id(0) < nu_ref[0])
    def _():
        xw = xs_ref[...]
        half = D_MODEL // 2
        x_hi = pltpu.bitcast(xw & jnp.uint32(0xFFFF0000), F32).astype(BF16)
        x_lo = pltpu.bitcast(xw << 16, F32).astype(BF16)
        hid = (jnp.dot(x_hi, w1_ref[0, :half, :], preferred_element_type=F32)
               + jnp.dot(x_lo, w1_ref[0, half:, :], preferred_element_type=F32) + b1_ref[0])
        glu = jnp.minimum(hid[:, :D_FF_EXPERT], SWIGLU_LIMIT)
        lin = jnp.clip(hid[:, D_FF_EXPERT:], -SWIGLU_LIMIT, SWIGLU_LIMIT)
        act = glu * _sigmoid(SWIGLU_ALPHA * glu) * (lin + 1.0)
        ys_ref[...] = jnp.dot(act.astype(BF16), w2_ref[0], preferred_element_type=F32) + b2_ref[0]


def _experts(blk_expert, n_used, xs, w1, b1, w2, b2):
    cap = xs.shape[0]
    n_blocks = cap // MOE_BLOCK
    grid_spec = pltpu.PrefetchScalarGridSpec(
        num_scalar_prefetch=2,
        grid=(n_blocks,),
        in_specs=[
            pl.BlockSpec((MOE_BLOCK, D_MODEL // 2), lambda b, be, nu: (b, 0)),
            pl.BlockSpec((1, D_MODEL, 2 * D_FF_EXPERT), lambda b, be, nu: (be[b], 0, 0)),
            pl.BlockSpec((1, 1, 2 * D_FF_EXPERT), lambda b, be, nu: (be[b], 0, 0)),
            pl.BlockSpec((1, D_FF_EXPERT, D_MODEL), lambda b, be, nu: (be[b], 0, 0)),
            pl.BlockSpec((1, 1, D_MODEL), lambda b, be, nu: (be[b], 0, 0)),
        ],
        out_specs=pl.BlockSpec((MOE_BLOCK, D_MODEL), lambda b, be, nu: (b, 0)),
    )
    return pl.pallas_call(
        _expert_kernel,
        grid_spec=grid_spec,
        out_shape=jax.ShapeDtypeStruct((cap, D_MODEL), F32),
        compiler_params=pltpu.CompilerParams(
            dimension_semantics=("arbitrary",), vmem_limit_bytes=VMEM_LIMIT),
        name="experts",
    )(blk_expert, n_used, xs, w1, b1, w2, b2)


def _combine_kernel(dest_hbm, ys_hbm, x1_ref, gate_ref, p_ref, pg_ref, wg_ref, wp_ref, fg_ref,
                    out_ref, dest_smem, ybuf, sem_idx, sem_rows, *, final):
    i = pl.program_id(0)
    base = pl.multiple_of(i * COMB_T, COMB_T)
    idx_copy = pltpu.make_async_copy(dest_hbm.at[:, pl.ds(base, COMB_T)], dest_smem, sem_idx)
    idx_copy.start()
    idx_copy.wait()

    def row_copy(t, k):
        d = dest_smem[k, t]
        return pltpu.make_async_copy(ys_hbm.at[pl.ds(d, 1), :], ybuf.at[k, pl.ds(t, 1), :], sem_rows)

    def issue(t, carry):
        for k in range(TOP_K):
            row_copy(t, k).start()
        return carry

    lax.fori_loop(0, COMB_T, issue, 0)

    def drain(t, carry):
        for k in range(TOP_K):
            row_copy(t, k).wait()
        return carry

    lax.fori_loop(0, COMB_T, drain, 0)

    moe = gate_ref[:, 0:1] * ybuf[0]
    for k in range(1, TOP_K):
        moe = moe + gate_ref[:, k:k + 1] * ybuf[k]
    x2 = x1_ref[...] + moe
    hp = _rms(x2, pg_ref[...]).astype(BF16)
    gate = _sigmoid(jnp.dot(hp, wg_ref[...], preferred_element_type=F32))
    emb = jnp.dot(p_ref[...].astype(BF16), wp_ref[...], preferred_element_type=F32)
    x3 = x2 + gate * emb
    if final:
        x3 = _rms(x3, fg_ref[...])
    out_ref[...] = x3


def _combine(dest, ys, x1, gates_t, p2d, pg, wg, wp, fg, final):
    n = x1.shape[0]
    return pl.pallas_call(
        functools.partial(_combine_kernel, final=final),
        grid=(n // COMB_T,),
        in_specs=[
            pl.BlockSpec(memory_space=pl.ANY),
            pl.BlockSpec(memory_space=pl.ANY),
            pl.BlockSpec((COMB_T, D_MODEL), lambda i: (i, 0)),
            pl.BlockSpec((COMB_T, 8), lambda i: (i, 0)),
            pl.BlockSpec((COMB_T, PLE_DIM), lambda i: (i, 0)),
            pl.BlockSpec((1, D_MODEL), lambda i: (0, 0)),
            pl.BlockSpec((D_MODEL, D_MODEL), lambda i: (0, 0)),
            pl.BlockSpec((PLE_DIM, D_MODEL), lambda i: (0, 0)),
            pl.BlockSpec((1, D_MODEL), lambda i: (0, 0)),
        ],
        out_specs=pl.BlockSpec((COMB_T, D_MODEL), lambda i: (i, 0)),
        out_shape=jax.ShapeDtypeStruct((n, D_MODEL), F32),
        scratch_shapes=[pltpu.SMEM((TOP_K, COMB_T), I32),
                        pltpu.VMEM((TOP_K, COMB_T, D_MODEL), F32),
                        pltpu.SemaphoreType.DMA, pltpu.SemaphoreType.DMA],
        compiler_params=pltpu.CompilerParams(
            dimension_semantics=("arbitrary",), vmem_limit_bytes=VMEM_LIMIT),
        name="combine",
    )(dest, ys, x1, gates_t, p2d, pg, wg, wp, fg)


def _layer(x2d, p2d, bsz, seq, mix_norm, w_in, sgu_ln_g, sgu_ln_b, sgu_w, sgu_b, conv_w, conv_b,
           dt_bias, a_log, d_skip, ssm_norm, w_branch_a, w_branch_b, w_out, ffn_norm, w_router,
           b_router, w1, b1, w2, b2, ple_norm, w_ple_gate, w_ple_proj, final_norm, final):
    n = x2d.shape[0]
    row = lambda a: a.reshape(1, -1).astype(F32)

    w_main = jnp.concatenate([w_in[:, :OFF_DT], w_in[:, OFF_GA:]], axis=1).astype(BF16)
    w_dt = jnp.pad(w_in[:, OFF_DT:OFF_GA], ((0, 0), (0, LANES - SSM_HEADS))).astype(BF16)
    pos = jnp.arange(SGU_LEN)
    allowed = (pos[None, :] // CHUNK) <= (pos[:, None] // CHUNK)
    wsgu = jnp.where(allowed[None], sgu_w, 0.0).astype(BF16)
    bsgu = jnp.repeat(sgu_b.T, SGU_HEAD_DIM, axis=1).astype(F32)
    pad_h = lambda a: jnp.pad(a.reshape(1, -1).astype(F32), ((0, 0), (0, LANES - SSM_HEADS)))
    dtb = pad_h(dt_bias)
    arow = pad_h(-jnp.exp(a_log.astype(F32)))
    dskip = jnp.repeat(d_skip.astype(F32), SSM_HEAD_DIM).reshape(1, -1)
    head_of_col = jnp.arange(SSM_INNER) // SSM_HEAD_DIM
    eexp = (jnp.arange(LANES)[:, None] == head_of_col[None, :]).astype(F32)
    consts = [row(sgu_ln_g), row(sgu_ln_b), wsgu, bsgu, conv_w.astype(F32), row(conv_b), dtb, arow,
              dskip, row(ssm_norm), eexp]

    proj, dt_raw = _in_proj(x2d, row(mix_norm), w_main, w_dt)
    x1 = _mixer(proj, dt_raw, x2d, bsz, seq, consts, w_branch_a.astype(BF16),
                w_branch_b.astype(BF16), w_out.astype(BF16))

    wr_pad = jnp.pad(w_router.astype(F32), ((0, 0), (0, LANES - N_EXPERTS)))
    hp, idx, gates, rank, cnt = _router(x1, row(ffn_norm), wr_pad, b_router.reshape(-1, 1).astype(F32))

    counts = cnt[:, 0]
    padded = (counts + MOE_BLOCK - 1) // MOE_BLOCK * MOE_BLOCK
    pend = jnp.cumsum(padded)
    pstart = pend - padded
    nk = n * TOP_K
    cap = (nk + MOE_BLOCK - 1) // MOE_BLOCK * MOE_BLOCK + N_EXPERTS * MOE_BLOCK
    n_blocks = cap // MOE_BLOCK
    ex = jnp.arange(N_EXPERTS, dtype=I32)
    dest = rank + jnp.sum(jnp.where(idx[..., None] == ex, pstart.astype(I32), 0), axis=-1)
    blk_start = jnp.arange(n_blocks, dtype=I32) * MOE_BLOCK
    blk_expert = jnp.minimum(jnp.sum((pend[None, :] <= blk_start[:, None]).astype(I32), axis=1),
                             N_EXPERTS - 1).astype(I32)
    n_used = (pend[-1] // MOE_BLOCK).astype(I32).reshape(1)

    xs = _dispatch(dest, hp, cap)
    ys = _experts(blk_expert, n_used, xs, w1.astype(BF16), b1.reshape(N_EXPERTS, 1, -1).astype(F32),
                  w2.astype(BF16), b2.reshape(N_EXPERTS, 1, -1).astype(F32))
    gates_t = jnp.pad(gates.T, ((0, 0), (0, 8 - TOP_K)))
    return _combine(dest, ys, x1, gates_t, p2d, row(ple_norm), w_ple_gate.astype(BF16),
                    w_ple_proj.astype(BF16), row(final_norm), final)


def kernel(x, p, mix_norm, w_in, sgu_ln_g, sgu_ln_b, sgu_w, sgu_b, conv_w, conv_b, dt_bias, a_log,
           d_skip, ssm_norm, w_branch_a, w_branch_b, w_out, ffn_norm, w_router, b_router, w1, b1,
           w2, b2, ple_norm, w_ple_gate, w_ple_proj, final_norm):
    bsz, seq, d = x.shape
    depth = w_in.shape[0]
    assert d == D_MODEL and seq % max(MIX_T, IN_TM) == 0
    x2d = x.reshape(bsz * seq, d)
    for i in range(depth):
        x2d = _layer(x2d, p[i].reshape(bsz * seq, PLE_DIM), bsz, seq, mix_norm[i], w_in[i],
                     sgu_ln_g[i], sgu_ln_b[i], sgu_w[i], sgu_b[i], conv_w[i], conv_b[i], dt_bias[i],
                     a_log[i], d_skip[i], ssm_norm[i], w_branch_a[i], w_branch_b[i], w_out[i],
                     ffn_norm[i], w_router[i], b_router[i], w1[i], b1[i], w2[i], b2[i], ple_norm[i],
                     w_ple_gate[i], w_ple_proj[i], final_norm, final=(i == depth - 1))
    return x2d.reshape(bsz, seq, d)
```

```python
import functools
import math

import jax
import jax.numpy as jnp
from jax import lax
from jax.experimental import pallas as pl
from jax.experimental.pallas import tpu as pltpu

F32 = jnp.float32
BF16 = jnp.bfloat16
I32 = jnp.int32
U32 = jnp.uint32

D_MODEL = 1024
CHUNK = 64
PLE_DIM = 256
RMS_EPS = 1e-6
LN_EPS = 1e-5

SGU_HEADS = 8
SGU_HEAD_DIM = 128
SGU_WIDTH = SGU_HEADS * SGU_HEAD_DIM
SGU_LEN = 128

SSM_HEADS = 16
SSM_HEAD_DIM = 64
SSM_INNER = SSM_HEADS * SSM_HEAD_DIM
SSM_GROUPS = 4
SSM_STATE = 128
SSM_CONV = 4
SSD_CHUNK = 128
SSM_CONV_DIM = SSM_INNER + 2 * SSM_GROUPS * SSM_STATE

N_EXPERTS = 32
TOP_K = 4
D_FF_EXPERT = 1024
SWIGLU_LIMIT = 7.0
SWIGLU_ALPHA = 1.702
MOE_BLOCK = 256

OFF_U = 0
OFF_V = OFF_U + SGU_WIDTH
OFF_Z = OFF_V + SGU_WIDTH
OFF_XBC = OFF_Z + SSM_INNER
OFF_DT = OFF_XBC + SSM_CONV_DIM
OFF_GA = OFF_DT + SSM_HEADS
OFF_GB = OFF_GA + D_MODEL
IN_PROJ_DIM = OFF_GB + D_MODEL

LANES = 128
PROJ_MAIN = IN_PROJ_DIM - SSM_HEADS
TOK_SUB = D_MODEL // LANES
CONV_PAD = 8

VMEM_LIMIT = 56 * 1024 * 1024

IN_TM = 1024
IN_TN = 1024
MIX_T = 512
ROUTE_T = 512
DISP_T = 512
COMB_T = DISP_T
DMA_UNROLL = 8


def _sigmoid(x):
    return 1.0 / (1.0 + jnp.exp(-x))


def _gelu_exact(x):
    return 0.5 * x * (1.0 + lax.erf(x * (1.0 / math.sqrt(2.0))))


def _softplus(x):
    return jnp.maximum(x, 0.0) + jnp.log1p(jnp.exp(-jnp.abs(x)))


def _rms(x, g):
    ms = jnp.mean(x * x, axis=-1, keepdims=True)
    return x * lax.rsqrt(ms + RMS_EPS) * g


def _store_token_tiles(ref, val, rows):
    for j in range(TOK_SUB):
        ref[pl.ds(j, rows, stride=TOK_SUB), :] = val[:, j * LANES:(j + 1) * LANES]


def _load_token_tiles(ref, rows, start=0):
    return jnp.concatenate(
        [ref[pl.ds(start * TOK_SUB + j, rows, stride=TOK_SUB), :] for j in range(TOK_SUB)], axis=1)


def _inproj_kernel(x_ref, g_ref, w_ref, wdt_ref, proj_ref, dt_ref, h_scr):
    @pl.when(pl.program_id(1) == 0)
    def _():
        h = _rms(x_ref[...], g_ref[...]).astype(BF16)
        h_scr[...] = h
        dt_ref[...] = jnp.dot(h, wdt_ref[...], preferred_element_type=F32)

    proj_ref[...] = jnp.dot(h_scr[...], w_ref[...], preferred_element_type=F32).astype(BF16)


def _in_proj(x2d, g, w_main, w_dt):
    n = x2d.shape[0]
    grid = (n // IN_TM, PROJ_MAIN // IN_TN)
    return pl.pallas_call(
        _inproj_kernel,
        grid=grid,
        in_specs=[
            pl.BlockSpec((IN_TM, D_MODEL), lambda i, j: (i, 0)),
            pl.BlockSpec((1, D_MODEL), lambda i, j: (0, 0)),
            pl.BlockSpec((D_MODEL, IN_TN), lambda i, j: (0, j)),
            pl.BlockSpec((D_MODEL, LANES), lambda i, j: (0, 0)),
        ],
        out_specs=[
            pl.BlockSpec((IN_TM, IN_TN), lambda i, j: (i, j)),
            pl.BlockSpec((IN_TM, LANES), lambda i, j: (i, 0)),
        ],
        out_shape=[
            jax.ShapeDtypeStruct((n, PROJ_MAIN), BF16),
            jax.ShapeDtypeStruct((n, LANES), F32),
        ],
        scratch_shapes=[pltpu.VMEM((IN_TM, D_MODEL), BF16)],
        compiler_params=pltpu.CompilerParams(
            dimension_semantics=("arbitrary", "arbitrary"), vmem_limit_bytes=VMEM_LIMIT),
        name="in_proj",
    )(x2d, g, w_main, w_dt)


def _mixer_kernel(u_ref, v_ref, z_ref, xb0_ref, xb1_ref, ga_ref, gb_ref, dt_ref, x_ref,
                  lng_ref, lnb_ref, wsgu_ref, bsgu_ref, convw_ref, convb_ref, dtb_ref, arow_ref,
                  dskip_ref, normw_ref, eexp_ref, wa_ref, wb_ref, wo_ref,
                  x1_ref,
                  xpad_scr, xc_scr, state_scr, ya_scr, yb_scr):
    nchunk = MIX_T // SSD_CHUNK

    @pl.when(pl.program_id(1) == 0)
    def _():
        xpad_scr[0:CONV_PAD, :] = jnp.zeros((CONV_PAD, SSM_CONV_DIM), F32)
        state_scr[...] = jnp.zeros_like(state_scr)

    xpad_scr[CONV_PAD:CONV_PAD + MIX_T, 0:1024] = xb0_ref[...].astype(F32)
    xpad_scr[CONV_PAD:CONV_PAD + MIX_T, 1024:2048] = xb1_ref[...].astype(F32)
    for c in range(nchunk):
        r0 = c * SSD_CHUNK
        acc = jnp.broadcast_to(convb_ref[...], (SSD_CHUNK, SSM_CONV_DIM))
        for j in range(SSM_CONV):
            off = CONV_PAD - (SSM_CONV - 1) + j + r0
            acc = acc + convw_ref[j:j + 1, :] * xpad_scr[off:off + SSD_CHUNK, :]
        xc_scr[r0:r0 + SSD_CHUNK, :] = acc * _sigmoid(acc)
    xpad_scr[0:CONV_PAD, :] = xpad_scr[MIX_T:MIX_T + CONV_PAD, :]

    row_i = lax.broadcasted_iota(I32, (SSD_CHUNK, SSD_CHUNK), 0)
    col_i = lax.broadcasted_iota(I32, (SSD_CHUNK, SSD_CHUNK), 1)
    tril = row_i >= col_i
    tril_f = tril.astype(F32)
    lane_hi = col_i >= SSM_HEAD_DIM

    def chunk_body(c, carry):
        r0 = pl.multiple_of(c * SSD_CHUNK, SSD_CHUNK)
        rows = pl.ds(r0, SSD_CHUNK)

        ug = _gelu_exact(u_ref[rows, :].astype(F32))
        vg = _gelu_exact(v_ref[rows, :].astype(F32))
        mu = jnp.mean(vg, axis=-1, keepdims=True)
        vc = vg - mu
        var = jnp.mean(vc * vc, axis=-1, keepdims=True)
        vn = (vc * lax.rsqrt(var + LN_EPS) * lng_ref[...] + lnb_ref[...]).astype(BF16)
        for g in range(SGU_HEADS):
            cols = slice(g * SGU_HEAD_DIM, (g + 1) * SGU_HEAD_DIM)
            mixed = jnp.dot(wsgu_ref[g], vn[:, cols], preferred_element_type=F32)
            ya_scr[rows, cols] = (ug[:, cols] * (mixed + bsgu_ref[:, cols])).astype(BF16)

        dtc = _softplus(dt_ref[rows, :] + dtb_ref[...])
        adt = dtc * arow_ref[...]
        acs = jnp.dot(tril_f, adt, preferred_element_type=F32, precision=lax.Precision.HIGHEST)
        acs_t = acs.T
        dt_t = dtc.T
        w_t = jnp.exp(acs_t[:, SSD_CHUNK - 1:SSD_CHUNK] - acs_t) * dt_t
        aend = jnp.broadcast_to(acs[SSD_CHUNK - 1:SSD_CHUNK, :], (8, LANES))
        dec = jnp.exp(jnp.dot(aend, eexp_ref[...], preferred_element_type=F32,
                              precision=lax.Precision.HIGHEST)[0:1, :])

        for gi in range(SSM_GROUPS):
            bg = xc_scr[rows, SSM_INNER + gi * SSM_STATE:SSM_INNER + (gi + 1) * SSM_STATE]
            cg = xc_scr[rows, SSM_INNER + (SSM_GROUPS + gi) * SSM_STATE:
                        SSM_INNER + (SSM_GROUPS + gi + 1) * SSM_STATE]
            bg_t = bg.T
            cb = jnp.dot(cg.astype(BF16), bg_t.astype(BF16), preferred_element_type=F32)
            for pr in range(2):
                lb = gi * 2 + pr
                cols = slice(lb * LANES, (lb + 1) * LANES)
                xpair = xc_scr[rows, cols]
                hpair = state_scr[:, cols]
                acc = jnp.zeros((SSD_CHUNK, LANES), F32)
                st = jnp.zeros((SSM_STATE, LANES), F32)
                for hh in range(2):
                    h = lb * 2 + hh
                    lmask = lane_hi if hh == 1 else jnp.logical_not(lane_hi)
                    xm = jnp.where(lmask, xpair, 0.0).astype(BF16)
                    hm = jnp.where(lmask, hpair, 0.0).astype(BF16)
                    colb = acs[:, h:h + 1]
                    rowb = acs_t[h:h + 1, :]
                    decay = jnp.exp(jnp.where(tril, colb - rowb, -jnp.inf))
                    m_h = (cb * decay * dt_t[h:h + 1, :]).astype(BF16)
                    c_h = (cg * jnp.exp(colb)).astype(BF16)
                    acc = acc + jnp.dot(m_h, xm, preferred_element_type=F32)
                    acc = acc + jnp.dot(c_h, hm, preferred_element_type=F32)
                    btw = (bg_t * w_t[h:h + 1, :]).astype(BF16)
                    st = st + jnp.dot(btw, xm, preferred_element_type=F32)
                state_scr[:, cols] = hpair * dec[:, cols] + st
                y = acc + dskip_ref[:, cols] * xpair
                zz = z_ref[rows, cols].astype(F32)
                xc_scr[rows, cols] = y * (zz * _sigmoid(zz))
            gcols = slice(gi * 2 * LANES, (gi + 1) * 2 * LANES)
            yg = xc_scr[rows, gcols]
            ms = jnp.mean(yg * yg, axis=-1, keepdims=True)
            yb_scr[rows, gcols] = (yg * lax.rsqrt(ms + RMS_EPS) * normw_ref[:, gcols]).astype(BF16)
        return carry

    lax.fori_loop(0, nchunk, chunk_body, 0)

    ma = jnp.dot(ya_scr[...], wa_ref[...], preferred_element_type=F32)
    mb = jnp.dot(yb_scr[...], wb_ref[...], preferred_element_type=F32)
    merged = (_sigmoid(ga_ref[...].astype(F32)) * ma + _sigmoid(gb_ref[...].astype(F32)) * mb)
    x1_ref[...] = x_ref[...] + jnp.dot(merged.astype(BF16), wo_ref[...], preferred_element_type=F32)


def _mixer(proj, dt_raw, x2d, bsz, seq, consts, wa, wb, wo):
    n = x2d.shape[0]
    tps = seq // MIX_T
    row = lambda b, j: b * tps + j

    def pcol(k):
        return pl.BlockSpec((MIX_T, 1024), lambda b, j, k=k: (row(b, j), k))

    def full(a):
        nd = a.ndim
        return pl.BlockSpec(a.shape, lambda b, j, nd=nd: (0,) * nd)

    in_specs = [pcol(0), pcol(1), pcol(2), pcol(3), pcol(4), pcol(5), pcol(6),
                pl.BlockSpec((MIX_T, LANES), lambda b, j: (row(b, j), 0)),
                pl.BlockSpec((MIX_T, D_MODEL), lambda b, j: (row(b, j), 0))]
    in_specs += [full(a) for a in consts] + [full(wa), full(wb), full(wo)]
    return pl.pallas_call(
        _mixer_kernel,
        grid=(bsz, tps),
        in_specs=in_specs,
        out_specs=pl.BlockSpec((MIX_T, D_MODEL), lambda b, j: (row(b, j), 0)),
        out_shape=jax.ShapeDtypeStruct((n, D_MODEL), F32),
        scratch_shapes=[
            pltpu.VMEM((MIX_T + 2 * CONV_PAD, SSM_CONV_DIM), F32),
            pltpu.VMEM((MIX_T, SSM_CONV_DIM), F32),
            pltpu.VMEM((SSM_STATE, SSM_INNER), F32),
            pltpu.VMEM((MIX_T, SGU_WIDTH), BF16),
            pltpu.VMEM((MIX_T, SSM_INNER), BF16),
        ],
        compiler_params=pltpu.CompilerParams(
            dimension_semantics=("arbitrary", "arbitrary"), vmem_limit_bytes=VMEM_LIMIT),
        name="mixer",
    )(proj, proj, proj, proj, proj, proj, proj, dt_raw, x2d, *consts, wa, wb, wo)


def _router_kernel(x_ref, g_ref, wr_ref, br_ref, hp_ref, idx_ref, gate_ref, rank_ref, cnt_ref, carry_scr):
    @pl.when(pl.program_id(0) == 0)
    def _():
        carry_scr[...] = jnp.zeros_like(carry_scr)

    h = _rms(x_ref[...], g_ref[...])
    _store_token_tiles(hp_ref, h, ROUTE_T)

    logits = jnp.dot(h, wr_ref[...], preferred_element_type=F32, precision=lax.Precision.HIGHEST)
    lt = logits.T[0:N_EXPERTS, :] + br_ref[...]
    eidx = lax.broadcasted_iota(I32, (N_EXPERTS, ROUTE_T), 0).astype(F32)
    vals = lt
    sel_any = jnp.zeros((N_EXPERTS, ROUTE_T), F32)
    sels, tops = [], []
    for k in range(TOP_K):
        m = jnp.max(vals, axis=0, keepdims=True)
        first = jnp.min(jnp.where(vals == m, eidx, float(N_EXPERTS)), axis=0, keepdims=True)
        sel = eidx == first
        vals = jnp.where(sel, -jnp.inf, vals)
        sel_f = sel.astype(F32)
        sel_any = sel_any + sel_f
        sels.append(sel_f)
        tops.append(m)
        idx_ref[k:k + 1, :] = first.astype(I32)
    es = [jnp.exp(t - tops[0]) for t in tops]
    denom = es[0] + es[1] + es[2] + es[3]
    for k in range(TOP_K):
        gate_ref[k:k + 1, :] = es[k] / denom

    r_i = lax.broadcasted_iota(I32, (ROUTE_T, ROUTE_T), 0)
    c_i = lax.broadcasted_iota(I32, (ROUTE_T, ROUTE_T), 1)
    upper = (r_i < c_i).astype(BF16)
    excl = jnp.dot(sel_any.astype(BF16), upper, preferred_element_type=F32) + carry_scr[:, 0:1]
    for k in range(TOP_K):
        rank_ref[k:k + 1, :] = jnp.sum(sels[k] * excl, axis=0, keepdims=True).astype(I32)
    new_carry = carry_scr[...] + jnp.sum(sel_any, axis=1, keepdims=True)
    carry_scr[...] = new_carry
    cnt_ref[...] = new_carry.astype(I32)


def _router(x1, g, wr_pad, br_col):
    n = x1.shape[0]
    return pl.pallas_call(
        _router_kernel,
        grid=(n // ROUTE_T,),
        in_specs=[
            pl.BlockSpec((ROUTE_T, D_MODEL), lambda i: (i, 0)),
            pl.BlockSpec((1, D_MODEL), lambda i: (0, 0)),
            pl.BlockSpec((D_MODEL, LANES), lambda i: (0, 0)),
            pl.BlockSpec((N_EXPERTS, 1), lambda i: (0, 0)),
        ],
        out_specs=[
            pl.BlockSpec((ROUTE_T * TOK_SUB, LANES), lambda i: (i, 0)),
            pl.BlockSpec((TOP_K, ROUTE_T), lambda i: (0, i)),
            pl.BlockSpec((TOP_K, ROUTE_T), lambda i: (0, i)),
            pl.BlockSpec((TOP_K, ROUTE_T), lambda i: (0, i)),
            pl.BlockSpec((N_EXPERTS, LANES), lambda i: (0, 0)),
        ],
        out_shape=[
            jax.ShapeDtypeStruct((n * TOK_SUB, LANES), F32),
            jax.ShapeDtypeStruct((TOP_K, n), I32),
            jax.ShapeDtypeStruct((TOP_K, n), F32),
            jax.ShapeDtypeStruct((TOP_K, n), I32),
            jax.ShapeDtypeStruct((N_EXPERTS, LANES), I32),
        ],
        scratch_shapes=[pltpu.VMEM((N_EXPERTS, LANES), F32)],
        compiler_params=pltpu.CompilerParams(
            dimension_semantics=("arbitrary",), vmem_limit_bytes=VMEM_LIMIT),
        name="router",
    )(x1, g, wr_pad, br_col)


def _tile_rows(idx):
    return pl.ds(pl.multiple_of(idx * TOK_SUB, TOK_SUB), TOK_SUB)


def _dispatch_kernel(dest_hbm, hp_ref, xs_in_hbm, xs_hbm, dest_smem, sem_idx, sem_rows):
    del xs_in_hbm
    idx_copy = pltpu.make_async_copy(dest_hbm.at[pl.program_id(0)], dest_smem, sem_idx)
    idx_copy.start()
    idx_copy.wait()

    def tile_copy(t, k):
        d = dest_smem[k * DISP_T + t]
        return pltpu.make_async_copy(hp_ref.at[_tile_rows(t), :], xs_hbm.at[_tile_rows(d), :], sem_rows)

    def issue(tb, carry):
        for r in range(DMA_UNROLL):
            for k in range(TOP_K):
                tile_copy(tb * DMA_UNROLL + r, k).start()
        return carry

    lax.fori_loop(0, DISP_T // DMA_UNROLL, issue, 0)

    def drain(tb, carry):
        for r in range(DMA_UNROLL):
            for k in range(TOP_K):
                tile_copy(tb * DMA_UNROLL + r, k).wait()
        return carry

    lax.fori_loop(0, DISP_T // DMA_UNROLL, drain, 0)


def _dispatch(dest_tiles, hp, cap):
    n = hp.shape[0] // TOK_SUB
    xs0 = jnp.zeros((cap * TOK_SUB, LANES), F32)
    return pl.pallas_call(
        _dispatch_kernel,
        grid=(n // DISP_T,),
        in_specs=[pl.BlockSpec(memory_space=pl.ANY),
                  pl.BlockSpec((DISP_T * TOK_SUB, LANES), lambda i: (i, 0)),
                  pl.BlockSpec(memory_space=pl.ANY)],
        out_specs=pl.BlockSpec(memory_space=pl.ANY),
        out_shape=jax.ShapeDtypeStruct((cap * TOK_SUB, LANES), F32),
        scratch_shapes=[pltpu.SMEM((TOP_K * DISP_T,), I32),
                        pltpu.SemaphoreType.DMA, pltpu.SemaphoreType.DMA],
        input_output_aliases={2: 0},
        compiler_params=pltpu.CompilerParams(dimension_semantics=("arbitrary",)),
        name="dispatch",
    )(dest_tiles, hp, xs0)


def _expert_kernel(be_ref, nu_ref, xs_ref, w1_ref, b1_ref, w2_ref, b2_ref, ys_ref):
    @pl.when(pl.program_id(0) < nu_ref[0])
    def _():
        x = _load_token_tiles(xs_ref, MOE_BLOCK).astype(BF16)
        hid = jnp.dot(x, w1_ref[0], preferred_element_type=F32) + b1_ref[0]
        glu = jnp.minimum(hid[:, :D_FF_EXPERT], SWIGLU_LIMIT)
        lin = jnp.clip(hid[:, D_FF_EXPERT:], -SWIGLU_LIMIT, SWIGLU_LIMIT)
        act = glu * _sigmoid(SWIGLU_ALPHA * glu) * (lin + 1.0)
        y = jnp.dot(act.astype(BF16), w2_ref[0], preferred_element_type=F32) + b2_ref[0]
        _store_token_tiles(ys_ref, y, MOE_BLOCK)


def _experts(blk_expert, n_used, xs, w1, b1, w2, b2):
    cap = xs.shape[0] // TOK_SUB
    n_blocks = cap // MOE_BLOCK
    grid_spec = pltpu.PrefetchScalarGridSpec(
        num_scalar_prefetch=2,
        grid=(n_blocks,),
        in_specs=[
            pl.BlockSpec((MOE_BLOCK * TOK_SUB, LANES), lambda b, be, nu: (b, 0)),
            pl.BlockSpec((1, D_MODEL, 2 * D_FF_EXPERT), lambda b, be, nu: (be[b], 0, 0)),
            pl.BlockSpec((1, 1, 2 * D_FF_EXPERT), lambda b, be, nu: (be[b], 0, 0)),
            pl.BlockSpec((1, D_FF_EXPERT, D_MODEL), lambda b, be, nu: (be[b], 0, 0)),
            pl.BlockSpec((1, 1, D_MODEL), lambda b, be, nu: (be[b], 0, 0)),
        ],
        out_specs=pl.BlockSpec((MOE_BLOCK * TOK_SUB, LANES), lambda b, be, nu: (b, 0)),
    )
    return pl.pallas_call(
        _expert_kernel,
        grid_spec=grid_spec,
        out_shape=jax.ShapeDtypeStruct((cap * TOK_SUB, LANES), F32),
        compiler_params=pltpu.CompilerParams(
            dimension_semantics=("arbitrary",), vmem_limit_bytes=VMEM_LIMIT),
        name="experts",
    )(blk_expert, n_used, xs, w1, b1, w2, b2)


def _combine_kernel(dest_hbm, ys_hbm, x1_ref, gate_ref, p_ref, pg_ref, wg_ref, wp_ref, fg_ref,
                    out_ref, dest_smem, ybuf, sem_idx, sem_rows, *, final):
    idx_copy = pltpu.make_async_copy(dest_hbm.at[pl.program_id(0)], dest_smem, sem_idx)
    idx_copy.start()
    idx_copy.wait()

    def tile_copy(t, k):
        d = dest_smem[k * COMB_T + t]
        return pltpu.make_async_copy(ys_hbm.at[_tile_rows(d), :], ybuf.at[_tile_rows(k * COMB_T + t), :],
                                     sem_rows)

    def issue(tb, carry):
        for r in range(DMA_UNROLL):
            for k in range(TOP_K):
                tile_copy(tb * DMA_UNROLL + r, k).start()
        return carry

    lax.fori_loop(0, COMB_T // DMA_UNROLL, issue, 0)

    def drain(tb, carry):
        for r in range(DMA_UNROLL):
            for k in range(TOP_K):
                tile_copy(tb * DMA_UNROLL + r, k).wait()
        return carry

    lax.fori_loop(0, COMB_T // DMA_UNROLL, drain, 0)

    moe = gate_ref[:, 0:1] * _load_token_tiles(ybuf, COMB_T, 0)
    for k in range(1, TOP_K):
        moe = moe + gate_ref[:, k:k + 1] * _load_token_tiles(ybuf, COMB_T, k * COMB_T)
    x2 = x1_ref[...] + moe
    hp = _rms(x2, pg_ref[...]).astype(BF16)
    gate = _sigmoid(jnp.dot(hp, wg_ref[...], preferred_element_type=F32))
    emb = jnp.dot(p_ref[...].astype(BF16), wp_ref[...], preferred_element_type=F32)
    x3 = x2 + gate * emb
    if final:
        x3 = _rms(x3, fg_ref[...])
    out_ref[...] = x3


def _combine(dest, ys, x1, gates_t, p2d, pg, wg, wp, fg, final):
    n = x1.shape[0]
    return pl.pallas_call(
        functools.partial(_combine_kernel, final=final),
        grid=(n // COMB_T,),
        in_specs=[
            pl.BlockSpec(memory_space=pl.ANY),
            pl.BlockSpec(memory_space=pl.ANY),
            pl.BlockSpec((COMB_T, D_MODEL), lambda i: (i, 0)),
            pl.BlockSpec((COMB_T, 8), lambda i: (i, 0)),
            pl.BlockSpec((COMB_T, PLE_DIM), lambda i: (i, 0)),
            pl.BlockSpec((1, D_MODEL), lambda i: (0, 0)),
            pl.BlockSpec((D_MODEL, D_MODEL), lambda i: (0, 0)),
            pl.BlockSpec((PLE_DIM, D_MODEL), lambda i: (0, 0)),
            pl.BlockSpec((1, D_MODEL), lambda i: (0, 0)),
        ],
        out_specs=pl.BlockSpec((COMB_T, D_MODEL), lambda i: (i, 0)),
        out_shape=jax.ShapeDtypeStruct((n, D_MODEL), F32),
        scratch_shapes=[pltpu.SMEM((TOP_K * COMB_T,), I32),
                        pltpu.VMEM((TOP_K * COMB_T * TOK_SUB, LANES), F32),
                        pltpu.SemaphoreType.DMA, pltpu.SemaphoreType.DMA],
        compiler_params=pltpu.CompilerParams(
            dimension_semantics=("arbitrary",), vmem_limit_bytes=VMEM_LIMIT),
        name="combine",
    )(dest, ys, x1, gates_t, p2d, pg, wg, wp, fg)


def _layer(x2d, p2d, bsz, seq, mix_norm, w_in, sgu_ln_g, sgu_ln_b, sgu_w, sgu_b, conv_w, conv_b,
           dt_bias, a_log, d_skip, ssm_norm, w_branch_a, w_branch_b, w_out, ffn_norm, w_router,
           b_router, w1, b1, w2, b2, ple_norm, w_ple_gate, w_ple_proj, final_norm, final):
    n = x2d.shape[0]
    row = lambda a: a.reshape(1, -1).astype(F32)

    w_main = jnp.concatenate([w_in[:, :OFF_DT], w_in[:, OFF_GA:]], axis=1).astype(BF16)
    w_dt = jnp.pad(w_in[:, OFF_DT:OFF_GA], ((0, 0), (0, LANES - SSM_HEADS))).astype(BF16)
    pos = jnp.arange(SGU_LEN)
    allowed = (pos[None, :] // CHUNK) <= (pos[:, None] // CHUNK)
    wsgu = jnp.where(allowed[None], sgu_w, 0.0).astype(BF16)
    bsgu = jnp.repeat(sgu_b.T, SGU_HEAD_DIM, axis=1).astype(F32)
    pad_h = lambda a: jnp.pad(a.reshape(1, -1).astype(F32), ((0, 0), (0, LANES - SSM_HEADS)))
    dtb = pad_h(dt_bias)
    arow = pad_h(-jnp.exp(a_log.astype(F32)))
    dskip = jnp.repeat(d_skip.astype(F32), SSM_HEAD_DIM).reshape(1, -1)
    head_of_col = jnp.arange(SSM_INNER) // SSM_HEAD_DIM
    eexp = (jnp.arange(LANES)[:, None] == head_of_col[None, :]).astype(F32)
    consts = [row(sgu_ln_g), row(sgu_ln_b), wsgu, bsgu, conv_w.astype(F32), row(conv_b), dtb, arow,
              dskip, row(ssm_norm), eexp]

    proj, dt_raw = _in_proj(x2d, row(mix_norm), w_main, w_dt)
    x1 = _mixer(proj, dt_raw, x2d, bsz, seq, consts, w_branch_a.astype(BF16),
                w_branch_b.astype(BF16), w_out.astype(BF16))

    wr_pad = jnp.pad(w_router.astype(F32), ((0, 0), (0, LANES - N_EXPERTS)))
    hp, idx, gates, rank, cnt = _router(x1, row(ffn_norm), wr_pad, b_router.reshape(-1, 1).astype(F32))

    counts = cnt[:, 0]
    padded = (counts + MOE_BLOCK - 1) // MOE_BLOCK * MOE_BLOCK
    pend = jnp.cumsum(padded)
    pstart = pend - padded
    nk = n * TOP_K
    cap = (nk + MOE_BLOCK - 1) // MOE_BLOCK * MOE_BLOCK + N_EXPERTS * MOE_BLOCK
    n_blocks = cap // MOE_BLOCK
    ex = jnp.arange(N_EXPERTS, dtype=I32)
    dest = rank + jnp.sum(jnp.where(idx[..., None] == ex, pstart.astype(I32), 0), axis=-1)
    blk_start = jnp.arange(n_blocks, dtype=I32) * MOE_BLOCK
    blk_expert = jnp.minimum(jnp.sum((pend[None, :] <= blk_start[:, None]).astype(I32), axis=1),
                             N_EXPERTS - 1).astype(I32)
    n_used = (pend[-1] // MOE_BLOCK).astype(I32).reshape(1)

    dest_tiles = dest.reshape(TOP_K, n // DISP_T, DISP_T).transpose(1, 0, 2).reshape(n // DISP_T, -1)

    xs = _dispatch(dest_tiles, hp, cap)
    ys = _experts(blk_expert, n_used, xs, w1.astype(BF16), b1.reshape(N_EXPERTS, 1, -1).astype(F32),
                  w2.astype(BF16), b2.reshape(N_EXPERTS, 1, -1).astype(F32))
    gates_t = jnp.pad(gates.T, ((0, 0), (0, 8 - TOP_K)))
    return _combine(dest_tiles, ys, x1, gates_t, p2d, row(ple_norm), w_ple_gate.astype(BF16),
                    w_ple_proj.astype(BF16), row(final_norm), final)


def kernel(x, p, mix_norm, w_in, sgu_ln_g, sgu_ln_b, sgu_w, sgu_b, conv_w, conv_b, dt_bias, a_log,
           d_skip, ssm_norm, w_branch_a, w_branch_b, w_out, ffn_norm, w_router, b_router, w1, b1,
           w2, b2, ple_norm, w_ple_gate, w_ple_proj, final_norm):
    bsz, seq, d = x.shape
    depth = w_in.shape[0]
    assert d == D_MODEL and seq % max(MIX_T, IN_TM) == 0
    x2d = x.reshape(bsz * seq, d)
    for i in range(depth):
        x2d = _layer(x2d, p[i].reshape(bsz * seq, PLE_DIM), bsz, seq, mix_norm[i], w_in[i],
                     sgu_ln_g[i], sgu_ln_b[i], sgu_w[i], sgu_b[i], conv_w[i], conv_b[i], dt_bias[i],
                     a_log[i], d_skip[i], ssm_norm[i], w_branch_a[i], w_branch_b[i], w_out[i],
                     ffn_norm[i], w_router[i], b_router[i], w1[i], b1[i], w2[i], b2[i], ple_norm[i],
                     w_ple_gate[i], w_ple_proj[i], final_norm, final=(i == depth - 1))
    return x2d.reshape(bsz, seq, d)
```

```python
import functools
import math

import jax
import jax.numpy as jnp
from jax import lax
from jax.experimental import pallas as pl
from jax.experimental.pallas import tpu as pltpu

F32 = jnp.float32
BF16 = jnp.bfloat16
I32 = jnp.int32
U32 = jnp.uint32

D_MODEL = 1024
CHUNK = 64
PLE_DIM = 256
RMS_EPS = 1e-6
LN_EPS = 1e-5

SGU_HEADS = 8
SGU_HEAD_DIM = 128
SGU_WIDTH = SGU_HEADS * SGU_HEAD_DIM
SGU_LEN = 128

SSM_HEADS = 16
SSM_HEAD_DIM = 64
SSM_INNER = SSM_HEADS * SSM_HEAD_DIM
SSM_GROUPS = 4
SSM_STATE = 128
SSM_CONV = 4
SSD_CHUNK = 128
SSM_CONV_DIM = SSM_INNER + 2 * SSM_GROUPS * SSM_STATE

N_EXPERTS = 32
TOP_K = 4
D_FF_EXPERT = 1024
SWIGLU_LIMIT = 7.0
SWIGLU_ALPHA = 1.702
MOE_BLOCK = 256

OFF_U = 0
OFF_V = OFF_U + SGU_WIDTH
OFF_Z = OFF_V + SGU_WIDTH
OFF_XBC = OFF_Z + SSM_INNER
OFF_DT = OFF_XBC + SSM_CONV_DIM
OFF_GA = OFF_DT + SSM_HEADS
OFF_GB = OFF_GA + D_MODEL
IN_PROJ_DIM = OFF_GB + D_MODEL

LANES = 128
PROJ_MAIN = IN_PROJ_DIM - SSM_HEADS
TOK_SUB = D_MODEL // LANES
CONV_PAD = 8

VMEM_LIMIT = 56 * 1024 * 1024

IN_TM = 1024
IN_TN = 1024
MIX_T = 512
ROUTE_T = 512
DISP_T = 512
COMB_T = DISP_T
DMA_UNROLL = 8
CAST_ROWS = 128


def _sigmoid(x):
    return 1.0 / (1.0 + jnp.exp(-x))


def _gelu_exact(x):
    return 0.5 * x * (1.0 + lax.erf(x * (1.0 / math.sqrt(2.0))))


def _softplus(x):
    return jnp.maximum(x, 0.0) + jnp.log1p(jnp.exp(-jnp.abs(x)))


def _rms(x, g):
    ms = jnp.mean(x * x, axis=-1, keepdims=True)
    return x * lax.rsqrt(ms + RMS_EPS) * g


def _store_token_tiles(ref, val, rows):
    for j in range(TOK_SUB):
        ref[pl.ds(j, rows, stride=TOK_SUB), :] = val[:, j * LANES:(j + 1) * LANES]


def _load_token_tiles(ref, rows, start=0):
    return jnp.concatenate(
        [ref[pl.ds(start * TOK_SUB + j, rows, stride=TOK_SUB), :] for j in range(TOK_SUB)], axis=1)


def _inproj_kernel(x_ref, g_ref, w_ref, wdt_ref, proj_ref, dt_ref, h_scr):
    @pl.when(pl.program_id(1) == 0)
    def _():
        h = _rms(x_ref[...], g_ref[...]).astype(BF16)
        h_scr[...] = h
        dt_ref[...] = jnp.dot(h, wdt_ref[...], preferred_element_type=F32)

    proj_ref[...] = jnp.dot(h_scr[...], w_ref[...], preferred_element_type=F32).astype(BF16)


def _in_proj(x2d, g, w_main, w_dt):
    n = x2d.shape[0]
    grid = (n // IN_TM, PROJ_MAIN // IN_TN)
    return pl.pallas_call(
        _inproj_kernel,
        grid=grid,
        in_specs=[
            pl.BlockSpec((IN_TM, D_MODEL), lambda i, j: (i, 0)),
            pl.BlockSpec((1, D_MODEL), lambda i, j: (0, 0)),
            pl.BlockSpec((D_MODEL, IN_TN), lambda i, j: (0, j)),
            pl.BlockSpec((D_MODEL, LANES), lambda i, j: (0, 0)),
        ],
        out_specs=[
            pl.BlockSpec((IN_TM, IN_TN), lambda i, j: (i, j)),
            pl.BlockSpec((IN_TM, LANES), lambda i, j: (i, 0)),
        ],
        out_shape=[
            jax.ShapeDtypeStruct((n, PROJ_MAIN), BF16),
            jax.ShapeDtypeStruct((n, LANES), F32),
        ],
        scratch_shapes=[pltpu.VMEM((IN_TM, D_MODEL), BF16)],
        compiler_params=pltpu.CompilerParams(
            dimension_semantics=("arbitrary", "arbitrary"), vmem_limit_bytes=VMEM_LIMIT),
        name="in_proj",
    )(x2d, g, w_main, w_dt)


def _mixer_kernel(u_ref, v_ref, z_ref, xb0_ref, xb1_ref, ga_ref, gb_ref, dt_ref, x_ref,
                  lng_ref, lnb_ref, wsgu_ref, bsgu_ref, convw_ref, convb_ref, dtb_ref, arow_ref,
                  dskip_ref, normw_ref, eexp_ref, wa_ref, wb_ref, wo_ref,
                  x1_ref,
                  xpad_scr, xc_scr, state_scr, ya_scr, yb_scr):
    nchunk = MIX_T // SSD_CHUNK

    @pl.when(pl.program_id(1) == 0)
    def _():
        xpad_scr[0:CONV_PAD, :] = jnp.zeros((CONV_PAD, SSM_CONV_DIM), F32)
        state_scr[...] = jnp.zeros_like(state_scr)

    xpad_scr[CONV_PAD:CONV_PAD + MIX_T, 0:1024] = xb0_ref[...].astype(F32)
    xpad_scr[CONV_PAD:CONV_PAD + MIX_T, 1024:2048] = xb1_ref[...].astype(F32)
    for c in range(nchunk):
        r0 = c * SSD_CHUNK
        acc = jnp.broadcast_to(convb_ref[...], (SSD_CHUNK, SSM_CONV_DIM))
        for j in range(SSM_CONV):
            off = CONV_PAD - (SSM_CONV - 1) + j + r0
            acc = acc + convw_ref[j:j + 1, :] * xpad_scr[off:off + SSD_CHUNK, :]
        xc_scr[r0:r0 + SSD_CHUNK, :] = acc * _sigmoid(acc)
    xpad_scr[0:CONV_PAD, :] = xpad_scr[MIX_T:MIX_T + CONV_PAD, :]

    row_i = lax.broadcasted_iota(I32, (SSD_CHUNK, SSD_CHUNK), 0)
    col_i = lax.broadcasted_iota(I32, (SSD_CHUNK, SSD_CHUNK), 1)
    tril = row_i >= col_i
    tril_f = tril.astype(F32)
    lane_hi = col_i >= SSM_HEAD_DIM

    def chunk_body(c, carry):
        r0 = pl.multiple_of(c * SSD_CHUNK, SSD_CHUNK)
        rows = pl.ds(r0, SSD_CHUNK)

        ug = _gelu_exact(u_ref[rows, :].astype(F32))
        vg = _gelu_exact(v_ref[rows, :].astype(F32))
        mu = jnp.mean(vg, axis=-1, keepdims=True)
        vc = vg - mu
        var = jnp.mean(vc * vc, axis=-1, keepdims=True)
        vn = (vc * lax.rsqrt(var + LN_EPS) * lng_ref[...] + lnb_ref[...]).astype(BF16)
        for g in range(SGU_HEADS):
            cols = slice(g * SGU_HEAD_DIM, (g + 1) * SGU_HEAD_DIM)
            mixed = jnp.dot(wsgu_ref[g], vn[:, cols], preferred_element_type=F32)
            ya_scr[rows, cols] = (ug[:, cols] * (mixed + bsgu_ref[:, cols])).astype(BF16)

        dtc = _softplus(dt_ref[rows, :] + dtb_ref[...])
        adt = dtc * arow_ref[...]
        acs = jnp.dot(tril_f, adt, preferred_element_type=F32, precision=lax.Precision.HIGHEST)
        acs_t = acs.T
        dt_t = dtc.T
        w_t = jnp.exp(acs_t[:, SSD_CHUNK - 1:SSD_CHUNK] - acs_t) * dt_t
        aend = jnp.broadcast_to(acs[SSD_CHUNK - 1:SSD_CHUNK, :], (8, LANES))
        dec = jnp.exp(jnp.dot(aend, eexp_ref[...], preferred_element_type=F32,
                              precision=lax.Precision.HIGHEST)[0:1, :])

        for gi in range(SSM_GROUPS):
            bg = xc_scr[rows, SSM_INNER + gi * SSM_STATE:SSM_INNER + (gi + 1) * SSM_STATE]
            cg = xc_scr[rows, SSM_INNER + (SSM_GROUPS + gi) * SSM_STATE:
                        SSM_INNER + (SSM_GROUPS + gi + 1) * SSM_STATE]
            bg_t = bg.T
            cb = jnp.dot(cg.astype(BF16), bg_t.astype(BF16), preferred_element_type=F32)
            for pr in range(2):
                lb = gi * 2 + pr
                cols = slice(lb * LANES, (lb + 1) * LANES)
                xpair = xc_scr[rows, cols]
                hpair = state_scr[:, cols]
                acc = jnp.zeros((SSD_CHUNK, LANES), F32)
                st = jnp.zeros((SSM_STATE, LANES), F32)
                for hh in range(2):
                    h = lb * 2 + hh
                    lmask = lane_hi if hh == 1 else jnp.logical_not(lane_hi)
                    xm = jnp.where(lmask, xpair, 0.0).astype(BF16)
                    hm = jnp.where(lmask, hpair, 0.0).astype(BF16)
                    colb = acs[:, h:h + 1]
                    rowb = acs_t[h:h + 1, :]
                    decay = jnp.exp(jnp.where(tril, colb - rowb, -jnp.inf))
                    m_h = (cb * decay * dt_t[h:h + 1, :]).astype(BF16)
                    c_h = (cg * jnp.exp(colb)).astype(BF16)
                    acc = acc + jnp.dot(m_h, xm, preferred_element_type=F32)
                    acc = acc + jnp.dot(c_h, hm, preferred_element_type=F32)
                    btw = (bg_t * w_t[h:h + 1, :]).astype(BF16)
                    st = st + jnp.dot(btw, xm, preferred_element_type=F32)
                state_scr[:, cols] = hpair * dec[:, cols] + st
                y = acc + dskip_ref[:, cols] * xpair
                zz = z_ref[rows, cols].astype(F32)
                xc_scr[rows, cols] = y * (zz * _sigmoid(zz))
            gcols = slice(gi * 2 * LANES, (gi + 1) * 2 * LANES)
            yg = xc_scr[rows, gcols]
            ms = jnp.mean(yg * yg, axis=-1, keepdims=True)
            yb_scr[rows, gcols] = (yg * lax.rsqrt(ms + RMS_EPS) * normw_ref[:, gcols]).astype(BF16)
        return carry

    lax.fori_loop(0, nchunk, chunk_body, 0)

    ma = jnp.dot(ya_scr[...], wa_ref[...], preferred_element_type=F32)
    mb = jnp.dot(yb_scr[...], wb_ref[...], preferred_element_type=F32)
    merged = (_sigmoid(ga_ref[...].astype(F32)) * ma + _sigmoid(gb_ref[...].astype(F32)) * mb)
    x1_ref[...] = x_ref[...] + jnp.dot(merged.astype(BF16), wo_ref[...], preferred_element_type=F32)


def _mixer(proj, dt_raw, x2d, bsz, seq, consts, wa, wb, wo):
    n = x2d.shape[0]
    tps = seq // MIX_T
    row = lambda b, j: b * tps + j

    def pcol(k):
        return pl.BlockSpec((MIX_T, 1024), lambda b, j, k=k: (row(b, j), k))

    def full(a):
        nd = a.ndim
        return pl.BlockSpec(a.shape, lambda b, j, nd=nd: (0,) * nd)

    in_specs = [pcol(0), pcol(1), pcol(2), pcol(3), pcol(4), pcol(5), pcol(6),
                pl.BlockSpec((MIX_T, LANES), lambda b, j: (row(b, j), 0)),
                pl.BlockSpec((MIX_T, D_MODEL), lambda b, j: (row(b, j), 0))]
    in_specs += [full(a) for a in consts] + [full(wa), full(wb), full(wo)]
    return pl.pallas_call(
        _mixer_kernel,
        grid=(bsz, tps),
        in_specs=in_specs,
        out_specs=pl.BlockSpec((MIX_T, D_MODEL), lambda b, j: (row(b, j), 0)),
        out_shape=jax.ShapeDtypeStruct((n, D_MODEL), F32),
        scratch_shapes=[
            pltpu.VMEM((MIX_T + 2 * CONV_PAD, SSM_CONV_DIM), F32),
            pltpu.VMEM((MIX_T, SSM_CONV_DIM), F32),
            pltpu.VMEM((SSM_STATE, SSM_INNER), F32),
            pltpu.VMEM((MIX_T, SGU_WIDTH), BF16),
            pltpu.VMEM((MIX_T, SSM_INNER), BF16),
        ],
        compiler_params=pltpu.CompilerParams(
            dimension_semantics=("arbitrary", "arbitrary"), vmem_limit_bytes=VMEM_LIMIT),
        name="mixer",
    )(proj, proj, proj, proj, proj, proj, proj, dt_raw, x2d, *consts, wa, wb, wo)


def _router_kernel(x_ref, g_ref, wr_ref, br_ref, hp_ref, idx_ref, gate_ref, rank_ref, cnt_ref, carry_scr):
    @pl.when(pl.program_id(0) == 0)
    def _():
        carry_scr[...] = jnp.zeros_like(carry_scr)

    h = _rms(x_ref[...], g_ref[...])
    _store_token_tiles(hp_ref, h, ROUTE_T)

    logits = jnp.dot(h, wr_ref[...], preferred_element_type=F32, precision=lax.Precision.HIGHEST)
    lt = logits.T[0:N_EXPERTS, :] + br_ref[...]
    eidx = lax.broadcasted_iota(I32, (N_EXPERTS, ROUTE_T), 0).astype(F32)
    vals = lt
    sel_any = jnp.zeros((N_EXPERTS, ROUTE_T), F32)
    sels, tops = [], []
    for k in range(TOP_K):
        m = jnp.max(vals, axis=0, keepdims=True)
        first = jnp.min(jnp.where(vals == m, eidx, float(N_EXPERTS)), axis=0, keepdims=True)
        sel = eidx == first
        vals = jnp.where(sel, -jnp.inf, vals)
        sel_f = sel.astype(F32)
        sel_any = sel_any + sel_f
        sels.append(sel_f)
        tops.append(m)
        idx_ref[k:k + 1, :] = first.astype(I32)
    es = [jnp.exp(t - tops[0]) for t in tops]
    denom = es[0] + es[1] + es[2] + es[3]
    for k in range(TOP_K):
        gate_ref[k:k + 1, :] = es[k] / denom

    r_i = lax.broadcasted_iota(I32, (ROUTE_T, ROUTE_T), 0)
    c_i = lax.broadcasted_iota(I32, (ROUTE_T, ROUTE_T), 1)
    upper = (r_i < c_i).astype(BF16)
    excl = jnp.dot(sel_any.astype(BF16), upper, preferred_element_type=F32) + carry_scr[:, 0:1]
    for k in range(TOP_K):
        rank_ref[k:k + 1, :] = jnp.sum(sels[k] * excl, axis=0, keepdims=True).astype(I32)
    new_carry = carry_scr[...] + jnp.sum(sel_any, axis=1, keepdims=True)
    carry_scr[...] = new_carry
    cnt_ref[...] = new_carry.astype(I32)


def _router(x1, g, wr_pad, br_col):
    n = x1.shape[0]
    return pl.pallas_call(
        _router_kernel,
        grid=(n // ROUTE_T,),
        in_specs=[
            pl.BlockSpec((ROUTE_T, D_MODEL), lambda i: (i, 0)),
            pl.BlockSpec((1, D_MODEL), lambda i: (0, 0)),
            pl.BlockSpec((D_MODEL, LANES), lambda i: (0, 0)),
            pl.BlockSpec((N_EXPERTS, 1), lambda i: (0, 0)),
        ],
        out_specs=[
            pl.BlockSpec((ROUTE_T * TOK_SUB, LANES), lambda i: (i, 0)),
            pl.BlockSpec((TOP_K, ROUTE_T), lambda i: (0, i)),
            pl.BlockSpec((TOP_K, ROUTE_T), lambda i: (0, i)),
            pl.BlockSpec((TOP_K, ROUTE_T), lambda i: (0, i)),
            pl.BlockSpec((N_EXPERTS, LANES), lambda i: (0, 0)),
        ],
        out_shape=[
            jax.ShapeDtypeStruct((n * TOK_SUB, LANES), F32),
            jax.ShapeDtypeStruct((TOP_K, n), I32),
            jax.ShapeDtypeStruct((TOP_K, n), F32),
            jax.ShapeDtypeStruct((TOP_K, n), I32),
            jax.ShapeDtypeStruct((N_EXPERTS, LANES), I32),
        ],
        scratch_shapes=[pltpu.VMEM((N_EXPERTS, LANES), F32)],
        compiler_params=pltpu.CompilerParams(
            dimension_semantics=("arbitrary",), vmem_limit_bytes=VMEM_LIMIT),
        name="router",
    )(x1, g, wr_pad, br_col)


def _tile_rows(idx):
    return pl.ds(pl.multiple_of(idx * TOK_SUB, TOK_SUB), TOK_SUB)


def _dispatch_kernel(dest_ref, pad_ref, hp_ref, xs_hbm, zero_scr, sem_rows):
    i = pl.program_id(0)
    tbl = i * (TOP_K * DISP_T)

    def zero_copy(s):
        return pltpu.make_async_copy(zero_scr, xs_hbm.at[_tile_rows(s), :], sem_rows)

    @pl.when(i == 0)
    def _():
        zero_scr[...] = jnp.zeros_like(zero_scr)
        for e in range(N_EXPERTS):
            lax.fori_loop(pad_ref[e], pad_ref[N_EXPERTS + e], lambda s, c: (zero_copy(s).start(), c)[1], 0)
        for e in range(N_EXPERTS):
            lax.fori_loop(pad_ref[e], pad_ref[N_EXPERTS + e], lambda s, c: (zero_copy(s).wait(), c)[1], 0)

    def tile_copy(t, k):
        d = dest_ref[tbl + k * DISP_T + t]
        return pltpu.make_async_copy(hp_ref.at[_tile_rows(t), :], xs_hbm.at[_tile_rows(d), :], sem_rows)

    def issue(tb, carry):
        for r in range(DMA_UNROLL):
            for k in range(TOP_K):
                tile_copy(tb * DMA_UNROLL + r, k).start(priority=(r * TOP_K + k) % 2)
        return carry

    lax.fori_loop(0, DISP_T // DMA_UNROLL, issue, 0)

    def drain(tb, carry):
        for _ in range(DMA_UNROLL * TOP_K):
            zero_copy(0).wait()
        return carry

    lax.fori_loop(0, DISP_T // DMA_UNROLL, drain, 0)


def _dispatch(dest_flat, pad_tbl, hp, cap):
    n = hp.shape[0] // TOK_SUB
    grid_spec = pltpu.PrefetchScalarGridSpec(
        num_scalar_prefetch=2,
        grid=(n // DISP_T,),
        in_specs=[pl.BlockSpec((DISP_T * TOK_SUB, LANES), lambda i, d, p: (i, 0))],
        out_specs=pl.BlockSpec(memory_space=pl.ANY),
        scratch_shapes=[pltpu.VMEM((TOK_SUB, LANES), F32), pltpu.SemaphoreType.DMA],
    )
    return pl.pallas_call(
        _dispatch_kernel,
        grid_spec=grid_spec,
        out_shape=jax.ShapeDtypeStruct((cap * TOK_SUB, LANES), F32),
        compiler_params=pltpu.CompilerParams(dimension_semantics=("arbitrary",)),
        name="dispatch",
    )(dest_flat, pad_tbl, hp)


def _expert_kernel(be_ref, nu_ref, xs_ref, w1_ref, b1_ref, w2_ref, b2_ref, ys_ref, w1b_scr, w2b_scr):
    b = pl.program_id(0)
    used = b < nu_ref[0]
    new_expert = jnp.logical_or(b == 0, be_ref[b] != be_ref[jnp.maximum(b - 1, 0)])

    @pl.when(jnp.logical_and(used, new_expert))
    def _():
        def cast(rb, carry):
            rows = pl.ds(pl.multiple_of(rb * CAST_ROWS, CAST_ROWS), CAST_ROWS)
            w1b_scr[rows, :] = w1_ref[0, rows, :].astype(BF16)
            w2b_scr[rows, :] = w2_ref[0, rows, :].astype(BF16)
            return carry

        lax.fori_loop(0, D_MODEL // CAST_ROWS, cast, 0)

    @pl.when(used)
    def _():
        x = _load_token_tiles(xs_ref, MOE_BLOCK).astype(BF16)
        hid = jnp.dot(x, w1b_scr[...], preferred_element_type=F32) + b1_ref[0]
        glu = jnp.minimum(hid[:, :D_FF_EXPERT], SWIGLU_LIMIT)
        lin = jnp.clip(hid[:, D_FF_EXPERT:], -SWIGLU_LIMIT, SWIGLU_LIMIT)
        act = glu * _sigmoid(SWIGLU_ALPHA * glu) * (lin + 1.0)
        y = jnp.dot(act.astype(BF16), w2b_scr[...], preferred_element_type=F32) + b2_ref[0]
        _store_token_tiles(ys_ref, y, MOE_BLOCK)


def _experts(blk_expert, n_used, xs, w1, b1, w2, b2):
    cap = xs.shape[0] // TOK_SUB
    n_blocks = cap // MOE_BLOCK
    grid_spec = pltpu.PrefetchScalarGridSpec(
        num_scalar_prefetch=2,
        grid=(n_blocks,),
        in_specs=[
            pl.BlockSpec((MOE_BLOCK * TOK_SUB, LANES), lambda b, be, nu: (b, 0)),
            pl.BlockSpec((1, D_MODEL, 2 * D_FF_EXPERT), lambda b, be, nu: (be[b], 0, 0)),
            pl.BlockSpec((1, 1, 2 * D_FF_EXPERT), lambda b, be, nu: (be[b], 0, 0)),
            pl.BlockSpec((1, D_FF_EXPERT, D_MODEL), lambda b, be, nu: (be[b], 0, 0)),
            pl.BlockSpec((1, 1, D_MODEL), lambda b, be, nu: (be[b], 0, 0)),
        ],
        out_specs=pl.BlockSpec((MOE_BLOCK * TOK_SUB, LANES), lambda b, be, nu: (b, 0)),
        scratch_shapes=[pltpu.VMEM((D_MODEL, 2 * D_FF_EXPERT), BF16),
                        pltpu.VMEM((D_FF_EXPERT, D_MODEL), BF16)],
    )
    assert D_MODEL == D_FF_EXPERT
    return pl.pallas_call(
        _expert_kernel,
        grid_spec=grid_spec,
        out_shape=jax.ShapeDtypeStruct((cap * TOK_SUB, LANES), F32),
        compiler_params=pltpu.CompilerParams(
            dimension_semantics=("arbitrary",), vmem_limit_bytes=VMEM_LIMIT),
        name="experts",
    )(blk_expert, n_used, xs, w1, b1, w2, b2)


def _combine_kernel(dest_ref, ys_hbm, x1_ref, gate_ref, p_ref, pg_ref, wg_ref, wp_ref, fg_ref,
                    out_ref, ybuf, sems, *, final):
    i = pl.program_id(0)
    nsteps = pl.num_programs(0)
    buf_tokens = TOP_K * COMB_T

    def issue_all(step, slot):
        tbl = step * buf_tokens

        def issue(tb, carry):
            for r in range(DMA_UNROLL):
                for k in range(TOP_K):
                    row = k * COMB_T + tb * DMA_UNROLL + r
                    pltpu.make_async_copy(
                        ys_hbm.at[_tile_rows(dest_ref[tbl + row]), :],
                        ybuf.at[_tile_rows(slot * buf_tokens + row), :],
                        sems.at[slot]).start(priority=(r * TOP_K + k) % 2)
            return carry

        lax.fori_loop(0, COMB_T // DMA_UNROLL, issue, 0)

    def drain_all(slot):
        def drain(tb, carry):
            for _ in range(DMA_UNROLL * TOP_K):
                pltpu.make_async_copy(ys_hbm.at[_tile_rows(0), :],
                                      ybuf.at[_tile_rows(slot * buf_tokens), :], sems.at[slot]).wait()
            return carry

        lax.fori_loop(0, COMB_T // DMA_UNROLL, drain, 0)

    @pl.when(i == 0)
    def _():
        issue_all(0, 0)

    for slot in range(2):
        @pl.when(jnp.logical_and(i % 2 == slot, i + 1 < nsteps))
        def _():
            issue_all(i + 1, 1 - slot)

    for slot in range(2):
        @pl.when(i % 2 == slot)
        def _():
            drain_all(slot)
            moe = gate_ref[:, 0:1] * _load_token_tiles(ybuf, COMB_T, slot * buf_tokens)
            for k in range(1, TOP_K):
                moe = moe + gate_ref[:, k:k + 1] * _load_token_tiles(
                    ybuf, COMB_T, slot * buf_tokens + k * COMB_T)
            x2 = x1_ref[...] + moe
            hp = _rms(x2, pg_ref[...]).astype(BF16)
            gate = _sigmoid(jnp.dot(hp, wg_ref[...], preferred_element_type=F32))
            emb = jnp.dot(p_ref[...].astype(BF16), wp_ref[...], preferred_element_type=F32)
            x3 = x2 + gate * emb
            if final:
                x3 = _rms(x3, fg_ref[...])
            out_ref[...] = x3


def _combine(dest_flat, ys, x1, gates_t, p2d, pg, wg, wp, fg, final):
    n = x1.shape[0]
    grid_spec = pltpu.PrefetchScalarGridSpec(
        num_scalar_prefetch=1,
        grid=(n // COMB_T,),
        in_specs=[
            pl.BlockSpec(memory_space=pl.ANY),
            pl.BlockSpec((COMB_T, D_MODEL), lambda i, d: (i, 0)),
            pl.BlockSpec((COMB_T, 8), lambda i, d: (i, 0)),
            pl.BlockSpec((COMB_T, PLE_DIM), lambda i, d: (i, 0)),
            pl.BlockSpec((1, D_MODEL), lambda i, d: (0, 0)),
            pl.BlockSpec((D_MODEL, D_MODEL), lambda i, d: (0, 0)),
            pl.BlockSpec((PLE_DIM, D_MODEL), lambda i, d: (0, 0)),
            pl.BlockSpec((1, D_MODEL), lambda i, d: (0, 0)),
        ],
        out_specs=pl.BlockSpec((COMB_T, D_MODEL), lambda i, d: (i, 0)),
        scratch_shapes=[pltpu.VMEM((2 * TOP_K * COMB_T * TOK_SUB, LANES), F32),
                        pltpu.SemaphoreType.DMA((2,))],
    )
    return pl.pallas_call(
        functools.partial(_combine_kernel, final=final),
        grid_spec=grid_spec,
        out_shape=jax.ShapeDtypeStruct((n, D_MODEL), F32),
        compiler_params=pltpu.CompilerParams(
            dimension_semantics=("arbitrary",), vmem_limit_bytes=VMEM_LIMIT),
        name="combine",
    )(dest_flat, ys, x1, gates_t, p2d, pg, wg, wp, fg)


def _layer(x2d, p2d, bsz, seq, mix_norm, w_in, sgu_ln_g, sgu_ln_b, sgu_w, sgu_b, conv_w, conv_b,
           dt_bias, a_log, d_skip, ssm_norm, w_branch_a, w_branch_b, w_out, ffn_norm, w_router,
           b_router, w1, b1, w2, b2, ple_norm, w_ple_gate, w_ple_proj, final_norm, final):
    n = x2d.shape[0]
    row = lambda a: a.reshape(1, -1).astype(F32)

    w_main = jnp.concatenate([w_in[:, :OFF_DT], w_in[:, OFF_GA:]], axis=1).astype(BF16)
    w_dt = jnp.pad(w_in[:, OFF_DT:OFF_GA], ((0, 0), (0, LANES - SSM_HEADS))).astype(BF16)
    pos = jnp.arange(SGU_LEN)
    allowed = (pos[None, :] // CHUNK) <= (pos[:, None] // CHUNK)
    wsgu = jnp.where(allowed[None], sgu_w, 0.0).astype(BF16)
    bsgu = jnp.repeat(sgu_b.T, SGU_HEAD_DIM, axis=1).astype(F32)
    pad_h = lambda a: jnp.pad(a.reshape(1, -1).astype(F32), ((0, 0), (0, LANES - SSM_HEADS)))
    dtb = pad_h(dt_bias)
    arow = pad_h(-jnp.exp(a_log.astype(F32)))
    dskip = jnp.repeat(d_skip.astype(F32), SSM_HEAD_DIM).reshape(1, -1)
    head_of_col = jnp.arange(SSM_INNER) // SSM_HEAD_DIM
    eexp = (jnp.arange(LANES)[:, None] == head_of_col[None, :]).astype(F32)
    consts = [row(sgu_ln_g), row(sgu_ln_b), wsgu, bsgu, conv_w.astype(F32), row(conv_b), dtb, arow,
              dskip, row(ssm_norm), eexp]

    proj, dt_raw = _in_proj(x2d, row(mix_norm), w_main, w_dt)
    x1 = _mixer(proj, dt_raw, x2d, bsz, seq, consts, w_branch_a.astype(BF16),
                w_branch_b.astype(BF16), w_out.astype(BF16))

    wr_pad = jnp.pad(w_router.astype(F32), ((0, 0), (0, LANES - N_EXPERTS)))
    hp, idx, gates, rank, cnt = _router(x1, row(ffn_norm), wr_pad, b_router.reshape(-1, 1).astype(F32))

    counts = cnt[:, 0]
    padded = (counts + MOE_BLOCK - 1) // MOE_BLOCK * MOE_BLOCK
    pend = jnp.cumsum(padded)
    pstart = pend - padded
    nk = n * TOP_K
    cap = (nk + MOE_BLOCK - 1) // MOE_BLOCK * MOE_BLOCK + N_EXPERTS * MOE_BLOCK
    n_blocks = cap // MOE_BLOCK
    ex = jnp.arange(N_EXPERTS, dtype=I32)
    dest = rank + jnp.sum(jnp.where(idx[..., None] == ex, pstart.astype(I32), 0), axis=-1)
    blk_start = jnp.arange(n_blocks, dtype=I32) * MOE_BLOCK
    blk_expert = jnp.minimum(jnp.sum((pend[None, :] <= blk_start[:, None]).astype(I32), axis=1),
                             N_EXPERTS - 1).astype(I32)
    n_used = (pend[-1] // MOE_BLOCK).astype(I32).reshape(1)

    dest_flat = dest.reshape(TOP_K, n // DISP_T, DISP_T).transpose(1, 0, 2).reshape(-1)
    pad_tbl = jnp.concatenate([pstart + counts, pend]).astype(I32)

    xs = _dispatch(dest_flat, pad_tbl, hp, cap)
    ys = _experts(blk_expert, n_used, xs, w1.astype(F32), b1.reshape(N_EXPERTS, 1, -1).astype(F32),
                  w2.astype(F32), b2.reshape(N_EXPERTS, 1, -1).astype(F32))
    gates_t = jnp.pad(gates.T, ((0, 0), (0, 8 - TOP_K)))
    return _combine(dest_flat, ys, x1, gates_t, p2d, row(ple_norm), w_ple_gate.astype(BF16),
                    w_ple_proj.astype(BF16), row(final_norm), final)


def kernel(x, p, mix_norm, w_in, sgu_ln_g, sgu_ln_b, sgu_w, sgu_b, conv_w, conv_b, dt_bias, a_log,
           d_skip, ssm_norm, w_branch_a, w_branch_b, w_out, ffn_norm, w_router, b_router, w1, b1,
           w2, b2, ple_norm, w_ple_gate, w_ple_proj, final_norm):
    bsz, seq, d = x.shape
    depth = w_in.shape[0]
    assert d == D_MODEL and seq % max(MIX_T, IN_TM) == 0
    x2d = x.reshape(bsz * seq, d)
    for i in range(depth):
        x2d = _layer(x2d, p[i].reshape(bsz * seq, PLE_DIM), bsz, seq, mix_norm[i], w_in[i],
                     sgu_ln_g[i], sgu_ln_b[i], sgu_w[i], sgu_b[i], conv_w[i], conv_b[i], dt_bias[i],
                     a_log[i], d_skip[i], ssm_norm[i], w_branch_a[i], w_branch_b[i], w_out[i],
                     ffn_norm[i], w_router[i], b_router[i], w1[i], b1[i], w2[i], b2[i], ple_norm[i],
                     w_ple_gate[i], w_ple_proj[i], final_norm, final=(i == depth - 1))
    return x2d.reshape(bsz, seq, d)
```

```python
import functools
import math

import jax
import jax.numpy as jnp
from jax import lax
from jax.experimental import pallas as pl
from jax.experimental.pallas import tpu as pltpu

F32 = jnp.float32
BF16 = jnp.bfloat16
I32 = jnp.int32
U32 = jnp.uint32

D_MODEL = 1024
CHUNK = 64
PLE_DIM = 256
RMS_EPS = 1e-6
LN_EPS = 1e-5

SGU_HEADS = 8
SGU_HEAD_DIM = 128
SGU_WIDTH = SGU_HEADS * SGU_HEAD_DIM
SGU_LEN = 128

SSM_HEADS = 16
SSM_HEAD_DIM = 64
SSM_INNER = SSM_HEADS * SSM_HEAD_DIM
SSM_GROUPS = 4
SSM_STATE = 128
SSM_CONV = 4
SSD_CHUNK = 128
SSM_CONV_DIM = SSM_INNER + 2 * SSM_GROUPS * SSM_STATE

N_EXPERTS = 32
TOP_K = 4
D_FF_EXPERT = 1024
SWIGLU_LIMIT = 7.0
SWIGLU_ALPHA = 1.702
MOE_BLOCK = 512
MOE_SUB = 256

OFF_U = 0
OFF_V = OFF_U + SGU_WIDTH
OFF_Z = OFF_V + SGU_WIDTH
OFF_XBC = OFF_Z + SSM_INNER
OFF_DT = OFF_XBC + SSM_CONV_DIM
OFF_GA = OFF_DT + SSM_HEADS
OFF_GB = OFF_GA + D_MODEL
IN_PROJ_DIM = OFF_GB + D_MODEL

LANES = 128
PROJ_MAIN = IN_PROJ_DIM - SSM_HEADS
TOK_SUB = D_MODEL // LANES

VMEM_LIMIT = 56 * 1024 * 1024

IN_TM = 1024
IN_TN = 1024
MIX_T = 512
ROUTE_T = 512
DISP_T = 512
COMB_T = DISP_T
DMA_UNROLL = 8
CAST_ROWS = 128


def _sigmoid(x):
    return 1.0 / (1.0 + jnp.exp(-x))


def _gelu_exact(x):
    return 0.5 * x * (1.0 + lax.erf(x * (1.0 / math.sqrt(2.0))))


def _softplus(x):
    return jnp.maximum(x, 0.0) + jnp.log1p(jnp.exp(-jnp.abs(x)))


def _rms(x, g):
    ms = jnp.mean(x * x, axis=-1, keepdims=True)
    return x * lax.rsqrt(ms + RMS_EPS) * g


def _store_token_tiles(ref, val, rows, start=0):
    for j in range(TOK_SUB):
        ref[pl.ds(start * TOK_SUB + j, rows, stride=TOK_SUB), :] = val[:, j * LANES:(j + 1) * LANES]


def _load_token_tiles(ref, rows, start=0):
    return jnp.concatenate(
        [ref[pl.ds(start * TOK_SUB + j, rows, stride=TOK_SUB), :] for j in range(TOK_SUB)], axis=1)


def _inproj_kernel(x_ref, g_ref, w_ref, wdt_ref, proj_ref, dt_ref, h_scr):
    @pl.when(pl.program_id(1) == 0)
    def _():
        h = _rms(x_ref[...], g_ref[...]).astype(BF16)
        h_scr[...] = h
        dt_ref[...] = jnp.dot(h, wdt_ref[...], preferred_element_type=F32)

    proj_ref[...] = jnp.dot(h_scr[...], w_ref[...], preferred_element_type=F32).astype(BF16)


def _in_proj(x2d, g, w_main, w_dt):
    n = x2d.shape[0]
    grid = (n // IN_TM, PROJ_MAIN // IN_TN)
    return pl.pallas_call(
        _inproj_kernel,
        grid=grid,
        in_specs=[
            pl.BlockSpec((IN_TM, D_MODEL), lambda i, j: (i, 0)),
            pl.BlockSpec((1, D_MODEL), lambda i, j: (0, 0)),
            pl.BlockSpec((D_MODEL, IN_TN), lambda i, j: (0, j)),
            pl.BlockSpec((D_MODEL, LANES), lambda i, j: (0, 0)),
        ],
        out_specs=[
            pl.BlockSpec((IN_TM, IN_TN), lambda i, j: (i, j)),
            pl.BlockSpec((IN_TM, LANES), lambda i, j: (i, 0)),
        ],
        out_shape=[
            jax.ShapeDtypeStruct((n, PROJ_MAIN), BF16),
            jax.ShapeDtypeStruct((n, LANES), F32),
        ],
        scratch_shapes=[pltpu.VMEM((IN_TM, D_MODEL), BF16)],
        compiler_params=pltpu.CompilerParams(
            dimension_semantics=("arbitrary", "arbitrary"), vmem_limit_bytes=VMEM_LIMIT),
        name="in_proj",
    )(x2d, g, w_main, w_dt)


def _mixer_kernel(u_ref, v_ref, z_ref, xb0_ref, xb1_ref, ga_ref, gb_ref, dt_ref, x_ref,
                  lng_ref, lnb_ref, wsgu_ref, bsgu_ref, convw_ref, convb_ref, dtb_ref, arow_ref,
                  dskip_ref, normw_ref, eexp_ref, shift_ref, wa_ref, wb_ref, wo_ref,
                  x1_ref,
                  tail_scr, xc_scr, state_scr, ya_scr, yb_scr):
    nchunk = MIX_T // SSD_CHUNK

    @pl.when(pl.program_id(1) == 0)
    def _():
        tail_scr[...] = jnp.zeros_like(tail_scr)
        state_scr[...] = jnp.zeros_like(state_scr)

    for c in range(nchunk):
        r0 = c * SSD_CHUNK
        for hf, xb_ref in enumerate((xb0_ref, xb1_ref)):
            cols = slice(hf * 1024, (hf + 1) * 1024)
            cur = xb_ref[r0:r0 + SSD_CHUNK, :]
            prev = tail_scr[:, cols] if c == 0 else xb_ref[r0 - SSD_CHUNK:r0, :]
            sh = jnp.dot(shift_ref[...], jnp.concatenate([prev, cur], axis=0), preferred_element_type=F32)
            acc = convb_ref[:, cols] + convw_ref[SSM_CONV - 1:SSM_CONV, cols] * cur.astype(F32)
            for j in range(SSM_CONV - 1):
                acc = acc + convw_ref[j:j + 1, cols] * sh[j * SSD_CHUNK:(j + 1) * SSD_CHUNK, :]
            xc_scr[r0:r0 + SSD_CHUNK, cols] = acc * _sigmoid(acc)
    tail_scr[:, 0:1024] = xb0_ref[MIX_T - SSD_CHUNK:MIX_T, :]
    tail_scr[:, 1024:2048] = xb1_ref[MIX_T - SSD_CHUNK:MIX_T, :]

    row_i = lax.broadcasted_iota(I32, (SSD_CHUNK, SSD_CHUNK), 0)
    col_i = lax.broadcasted_iota(I32, (SSD_CHUNK, SSD_CHUNK), 1)
    tril = row_i >= col_i
    tril_f = tril.astype(F32)
    lane_hi = col_i >= SSM_HEAD_DIM

    def chunk_body(c, carry):
        r0 = pl.multiple_of(c * SSD_CHUNK, SSD_CHUNK)
        rows = pl.ds(r0, SSD_CHUNK)

        ug = _gelu_exact(u_ref[rows, :].astype(F32))
        vg = _gelu_exact(v_ref[rows, :].astype(F32))
        mu = jnp.mean(vg, axis=-1, keepdims=True)
        vc = vg - mu
        var = jnp.mean(vc * vc, axis=-1, keepdims=True)
        vn = (vc * lax.rsqrt(var + LN_EPS) * lng_ref[...] + lnb_ref[...]).astype(BF16)
        for g in range(SGU_HEADS):
            cols = slice(g * SGU_HEAD_DIM, (g + 1) * SGU_HEAD_DIM)
            mixed = jnp.dot(wsgu_ref[g], vn[:, cols], preferred_element_type=F32)
            ya_scr[rows, cols] = (ug[:, cols] * (mixed + bsgu_ref[:, cols])).astype(BF16)

        dtc = _softplus(dt_ref[rows, :] + dtb_ref[...])
        adt = dtc * arow_ref[...]
        acs = jnp.dot(tril_f, adt, preferred_element_type=F32, precision=lax.Precision.HIGHEST)
        acs_t = acs.T
        dt_t = dtc.T
        w_t = jnp.exp(acs_t[:, SSD_CHUNK - 1:SSD_CHUNK] - acs_t) * dt_t
        aend = jnp.broadcast_to(acs[SSD_CHUNK - 1:SSD_CHUNK, :], (8, LANES))
        dec = jnp.exp(jnp.dot(aend, eexp_ref[...], preferred_element_type=F32,
                              precision=lax.Precision.HIGHEST)[0:1, :])

        for gi in range(SSM_GROUPS):
            bg = xc_scr[rows, SSM_INNER + gi * SSM_STATE:SSM_INNER + (gi + 1) * SSM_STATE]
            cg = xc_scr[rows, SSM_INNER + (SSM_GROUPS + gi) * SSM_STATE:
                        SSM_INNER + (SSM_GROUPS + gi + 1) * SSM_STATE]
            bg_t = bg.T
            cb = jnp.dot(cg.astype(BF16), bg_t.astype(BF16), preferred_element_type=F32)
            for pr in range(2):
                lb = gi * 2 + pr
                cols = slice(lb * LANES, (lb + 1) * LANES)
                xpair = xc_scr[rows, cols]
                hpair = state_scr[:, cols]
                acc = jnp.zeros((SSD_CHUNK, LANES), F32)
                st = jnp.zeros((SSM_STATE, LANES), F32)
                for hh in range(2):
                    h = lb * 2 + hh
                    lmask = lane_hi if hh == 1 else jnp.logical_not(lane_hi)
                    xm = jnp.where(lmask, xpair, 0.0).astype(BF16)
                    hm = jnp.where(lmask, hpair, 0.0).astype(BF16)
                    colb = acs[:, h:h + 1]
                    rowb = acs_t[h:h + 1, :]
                    decay = jnp.exp(jnp.where(tril, colb - rowb, -jnp.inf))
                    m_h = (cb * decay * dt_t[h:h + 1, :]).astype(BF16)
                    c_h = (cg * jnp.exp(colb)).astype(BF16)
                    acc = acc + jnp.dot(m_h, xm, preferred_element_type=F32)
                    acc = acc + jnp.dot(c_h, hm, preferred_element_type=F32)
                    btw = (bg_t * w_t[h:h + 1, :]).astype(BF16)
                    st = st + jnp.dot(btw, xm, preferred_element_type=F32)
                state_scr[:, cols] = hpair * dec[:, cols] + st
                y = acc + dskip_ref[:, cols] * xpair
                zz = z_ref[rows, cols].astype(F32)
                xc_scr[rows, cols] = y * (zz * _sigmoid(zz))
            gcols = slice(gi * 2 * LANES, (gi + 1) * 2 * LANES)
            yg = xc_scr[rows, gcols]
            ms = jnp.mean(yg * yg, axis=-1, keepdims=True)
            yb_scr[rows, gcols] = (yg * lax.rsqrt(ms + RMS_EPS) * normw_ref[:, gcols]).astype(BF16)
        return carry

    lax.fori_loop(0, nchunk, chunk_body, 0)

    ma = jnp.dot(ya_scr[...], wa_ref[...], preferred_element_type=F32)
    mb = jnp.dot(yb_scr[...], wb_ref[...], preferred_element_type=F32)
    merged = (_sigmoid(ga_ref[...].astype(F32)) * ma + _sigmoid(gb_ref[...].astype(F32)) * mb)
    x1_ref[...] = x_ref[...] + jnp.dot(merged.astype(BF16), wo_ref[...], preferred_element_type=F32)


def _mixer(proj, dt_raw, x2d, bsz, seq, consts, wa, wb, wo):
    n = x2d.shape[0]
    tps = seq // MIX_T
    row = lambda b, j: b * tps + j

    def pcol(k):
        return pl.BlockSpec((MIX_T, 1024), lambda b, j, k=k: (row(b, j), k))

    def full(a):
        nd = a.ndim
        return pl.BlockSpec(a.shape, lambda b, j, nd=nd: (0,) * nd)

    in_specs = [pcol(0), pcol(1), pcol(2), pcol(3), pcol(4), pcol(5), pcol(6),
                pl.BlockSpec((MIX_T, LANES), lambda b, j: (row(b, j), 0)),
                pl.BlockSpec((MIX_T, D_MODEL), lambda b, j: (row(b, j), 0))]
    in_specs += [full(a) for a in consts] + [full(wa), full(wb), full(wo)]
    return pl.pallas_call(
        _mixer_kernel,
        grid=(bsz, tps),
        in_specs=in_specs,
        out_specs=pl.BlockSpec((MIX_T, D_MODEL), lambda b, j: (row(b, j), 0)),
        out_shape=jax.ShapeDtypeStruct((n, D_MODEL), F32),
        scratch_shapes=[
            pltpu.VMEM((SSD_CHUNK, SSM_CONV_DIM), BF16),
            pltpu.VMEM((MIX_T, SSM_CONV_DIM), F32),
            pltpu.VMEM((SSM_STATE, SSM_INNER), F32),
            pltpu.VMEM((MIX_T, SGU_WIDTH), BF16),
            pltpu.VMEM((MIX_T, SSM_INNER), BF16),
        ],
        compiler_params=pltpu.CompilerParams(
            dimension_semantics=("arbitrary", "arbitrary"), vmem_limit_bytes=VMEM_LIMIT),
        name="mixer",
    )(proj, proj, proj, proj, proj, proj, proj, dt_raw, x2d, *consts, wa, wb, wo)


def _router_kernel(x_ref, g_ref, wr_ref, br_ref, hp_ref, idx_ref, gate_ref, rank_ref, cnt_ref, carry_scr):
    @pl.when(pl.program_id(0) == 0)
    def _():
        carry_scr[...] = jnp.zeros_like(carry_scr)

    h = _rms(x_ref[...], g_ref[...])
    _store_token_tiles(hp_ref, h, ROUTE_T)

    logits = jnp.dot(h, wr_ref[...], preferred_element_type=F32, precision=lax.Precision.HIGHEST)
    lt = logits.T[0:N_EXPERTS, :] + br_ref[...]
    eidx = lax.broadcasted_iota(I32, (N_EXPERTS, ROUTE_T), 0).astype(F32)
    vals = lt
    sel_any = jnp.zeros((N_EXPERTS, ROUTE_T), F32)
    sels, tops = [], []
    for k in range(TOP_K):
        m = jnp.max(vals, axis=0, keepdims=True)
        first = jnp.min(jnp.where(vals == m, eidx, float(N_EXPERTS)), axis=0, keepdims=True)
        sel = eidx == first
        vals = jnp.where(sel, -jnp.inf, vals)
        sel_f = sel.astype(F32)
        sel_any = sel_any + sel_f
        sels.append(sel_f)
        tops.append(m)
        idx_ref[k:k + 1, :] = first.astype(I32)
    es = [jnp.exp(t - tops[0]) for t in tops]
    denom = es[0] + es[1] + es[2] + es[3]
    for k in range(TOP_K):
        gate_ref[k:k + 1, :] = es[k] / denom

    r_i = lax.broadcasted_iota(I32, (ROUTE_T, ROUTE_T), 0)
    c_i = lax.broadcasted_iota(I32, (ROUTE_T, ROUTE_T), 1)
    upper = (r_i < c_i).astype(BF16)
    excl = jnp.dot(sel_any.astype(BF16), upper, preferred_element_type=F32) + carry_scr[:, 0:1]
    for k in range(TOP_K):
        rank_ref[k:k + 1, :] = jnp.sum(sels[k] * excl, axis=0, keepdims=True).astype(I32)
    new_carry = carry_scr[...] + jnp.sum(sel_any, axis=1, keepdims=True)
    carry_scr[...] = new_carry
    cnt_ref[...] = new_carry.astype(I32)


def _router(x1, g, wr_pad, br_col):
    n = x1.shape[0]
    return pl.pallas_call(
        _router_kernel,
        grid=(n // ROUTE_T,),
        in_specs=[
            pl.BlockSpec((ROUTE_T, D_MODEL), lambda i: (i, 0)),
            pl.BlockSpec((1, D_MODEL), lambda i: (0, 0)),
            pl.BlockSpec((D_MODEL, LANES), lambda i: (0, 0)),
            pl.BlockSpec((N_EXPERTS, 1), lambda i: (0, 0)),
        ],
        out_specs=[
            pl.BlockSpec((ROUTE_T * TOK_SUB, LANES), lambda i: (i, 0)),
            pl.BlockSpec((TOP_K, ROUTE_T), lambda i: (0, i)),
            pl.BlockSpec((TOP_K, ROUTE_T), lambda i: (0, i)),
            pl.BlockSpec((TOP_K, ROUTE_T), lambda i: (0, i)),
            pl.BlockSpec((N_EXPERTS, LANES), lambda i: (0, 0)),
        ],
        out_shape=[
            jax.ShapeDtypeStruct((n * TOK_SUB, LANES), F32),
            jax.ShapeDtypeStruct((TOP_K, n), I32),
            jax.ShapeDtypeStruct((TOP_K, n), F32),
            jax.ShapeDtypeStruct((TOP_K, n), I32),
            jax.ShapeDtypeStruct((N_EXPERTS, LANES), I32),
        ],
        scratch_shapes=[pltpu.VMEM((N_EXPERTS, LANES), F32)],
        compiler_params=pltpu.CompilerParams(
            dimension_semantics=("arbitrary",), vmem_limit_bytes=VMEM_LIMIT),
        name="router",
    )(x1, g, wr_pad, br_col)


def _tile_rows(idx):
    return pl.ds(pl.multiple_of(idx * TOK_SUB, TOK_SUB), TOK_SUB)


def _dispatch_kernel(dest_ref, pad_ref, hp_ref, xs_hbm, zero_scr, sem_rows):
    i = pl.program_id(0)
    tbl = i * (TOP_K * DISP_T)

    def zero_copy(s):
        return pltpu.make_async_copy(zero_scr, xs_hbm.at[_tile_rows(s), :], sem_rows)

    @pl.when(i == 0)
    def _():
        zero_scr[...] = jnp.zeros_like(zero_scr)
        for e in range(N_EXPERTS):
            lax.fori_loop(pad_ref[e], pad_ref[N_EXPERTS + e], lambda s, c: (zero_copy(s).start(), c)[1], 0)
        for e in range(N_EXPERTS):
            lax.fori_loop(pad_ref[e], pad_ref[N_EXPERTS + e], lambda s, c: (zero_copy(s).wait(), c)[1], 0)

    def tile_copy(t, k):
        d = dest_ref[tbl + k * DISP_T + t]
        return pltpu.make_async_copy(hp_ref.at[_tile_rows(t), :], xs_hbm.at[_tile_rows(d), :], sem_rows)

    def issue(tb, carry):
        for r in range(DMA_UNROLL):
            for k in range(TOP_K):
                tile_copy(tb * DMA_UNROLL + r, k).start(priority=(r * TOP_K + k) % 2)
        return carry

    lax.fori_loop(0, DISP_T // DMA_UNROLL, issue, 0)

    def drain(tb, carry):
        for _ in range(DMA_UNROLL * TOP_K):
            zero_copy(0).wait()
        return carry

    lax.fori_loop(0, DISP_T // DMA_UNROLL, drain, 0)


def _dispatch(dest_flat, pad_tbl, hp, cap):
    n = hp.shape[0] // TOK_SUB
    grid_spec = pltpu.PrefetchScalarGridSpec(
        num_scalar_prefetch=2,
        grid=(n // DISP_T,),
        in_specs=[pl.BlockSpec((DISP_T * TOK_SUB, LANES), lambda i, d, p: (i, 0))],
        out_specs=pl.BlockSpec(memory_space=pl.ANY),
        scratch_shapes=[pltpu.VMEM((TOK_SUB, LANES), F32), pltpu.SemaphoreType.DMA],
    )
    return pl.pallas_call(
        _dispatch_kernel,
        grid_spec=grid_spec,
        out_shape=jax.ShapeDtypeStruct((cap * TOK_SUB, LANES), F32),
        compiler_params=pltpu.CompilerParams(dimension_semantics=("arbitrary",)),
        name="dispatch",
    )(dest_flat, pad_tbl, hp)


def _expert_kernel(be_ref, nxt_ref, nu_ref, xs_ref, w1_hbm, b1_ref, w2_hbm, b2_ref, ys_ref,
                   w1f_scr, w2f_scr, w1b_scr, w2b_scr, sems):
    b = pl.program_id(0)
    used = b < nu_ref[0]
    new_expert = jnp.logical_or(b == 0, be_ref[b] != be_ref[jnp.maximum(b - 1, 0)])

    def fetch(e):
        return (pltpu.make_async_copy(w1_hbm.at[e], w1f_scr, sems.at[0]),
                pltpu.make_async_copy(w2_hbm.at[e], w2f_scr, sems.at[1]))

    @pl.when(jnp.logical_and(used, b == 0))
    def _():
        for c in fetch(be_ref[0]):
            c.start()

    @pl.when(jnp.logical_and(used, new_expert))
    def _():
        for c in fetch(be_ref[b]):
            c.wait()

        def cast(rb, carry):
            rows = pl.ds(pl.multiple_of(rb * CAST_ROWS, CAST_ROWS), CAST_ROWS)
            w1b_scr[rows, :] = w1f_scr[rows, :].astype(BF16)
            w2b_scr[rows, :] = w2f_scr[rows, :].astype(BF16)
            return carry

        lax.fori_loop(0, D_MODEL // CAST_ROWS, cast, 0)

        @pl.when(nxt_ref[b] >= 0)
        def _():
            for c in fetch(nxt_ref[b]):
                c.start()

    @pl.when(used)
    def _():
        for sub in range(MOE_BLOCK // MOE_SUB):
            x = _load_token_tiles(xs_ref, MOE_SUB, sub * MOE_SUB).astype(BF16)
            hid = jnp.dot(x, w1b_scr[...], preferred_element_type=F32) + b1_ref[0]
            glu = jnp.minimum(hid[:, :D_FF_EXPERT], SWIGLU_LIMIT)
            lin = jnp.clip(hid[:, D_FF_EXPERT:], -SWIGLU_LIMIT, SWIGLU_LIMIT)
            act = glu * _sigmoid(SWIGLU_ALPHA * glu) * (lin + 1.0)
            y = jnp.dot(act.astype(BF16), w2b_scr[...], preferred_element_type=F32) + b2_ref[0]
            _store_token_tiles(ys_ref, y, MOE_SUB, sub * MOE_SUB)


def _experts(blk_expert, blk_next, n_used, xs, w1, b1, w2, b2):
    cap = xs.shape[0] // TOK_SUB
    n_blocks = cap // MOE_BLOCK
    grid_spec = pltpu.PrefetchScalarGridSpec(
        num_scalar_prefetch=3,
        grid=(n_blocks,),
        in_specs=[
            pl.BlockSpec((MOE_BLOCK * TOK_SUB, LANES), lambda b, be, nx, nu: (b, 0)),
            pl.BlockSpec(memory_space=pl.ANY),
            pl.BlockSpec((1, 1, 2 * D_FF_EXPERT), lambda b, be, nx, nu: (be[b], 0, 0)),
            pl.BlockSpec(memory_space=pl.ANY),
            pl.BlockSpec((1, 1, D_MODEL), lambda b, be, nx, nu: (be[b], 0, 0)),
        ],
        out_specs=pl.BlockSpec((MOE_BLOCK * TOK_SUB, LANES), lambda b, be, nx, nu: (b, 0)),
        scratch_shapes=[pltpu.VMEM((D_MODEL, 2 * D_FF_EXPERT), F32),
                        pltpu.VMEM((D_FF_EXPERT, D_MODEL), F32),
                        pltpu.VMEM((D_MODEL, 2 * D_FF_EXPERT), BF16),
                        pltpu.VMEM((D_FF_EXPERT, D_MODEL), BF16),
                        pltpu.SemaphoreType.DMA((2,))],
    )
    assert D_MODEL == D_FF_EXPERT
    return pl.pallas_call(
        _expert_kernel,
        grid_spec=grid_spec,
        out_shape=jax.ShapeDtypeStruct((cap * TOK_SUB, LANES), F32),
        compiler_params=pltpu.CompilerParams(
            dimension_semantics=("arbitrary",), vmem_limit_bytes=VMEM_LIMIT),
        name="experts",
    )(blk_expert, blk_next, n_used, xs, w1, b1, w2, b2)


def _combine_kernel(dest_ref, ys_hbm, x1_ref, gate_ref, p_ref, pg_ref, wg_ref, wp_ref, fg_ref,
                    out_ref, ybuf, sems, *, final):
    i = pl.program_id(0)
    nsteps = pl.num_programs(0)
    buf_tokens = TOP_K * COMB_T

    def issue_all(step, slot):
        tbl = step * buf_tokens

        def issue(tb, carry):
            for r in range(DMA_UNROLL):
                for k in range(TOP_K):
                    row = k * COMB_T + tb * DMA_UNROLL + r
                    pltpu.make_async_copy(
                        ys_hbm.at[_tile_rows(dest_ref[tbl + row]), :],
                        ybuf.at[_tile_rows(slot * buf_tokens + row), :],
                        sems.at[slot]).start(priority=(r * TOP_K + k) % 2)
            return carry

        lax.fori_loop(0, COMB_T // DMA_UNROLL, issue, 0)

    def drain_all(slot):
        def drain(tb, carry):
            for _ in range(DMA_UNROLL * TOP_K):
                pltpu.make_async_copy(ys_hbm.at[_tile_rows(0), :],
                                      ybuf.at[_tile_rows(slot * buf_tokens), :], sems.at[slot]).wait()
            return carry

        lax.fori_loop(0, COMB_T // DMA_UNROLL, drain, 0)

    @pl.when(i == 0)
    def _():
        issue_all(0, 0)

    for slot in range(2):
        @pl.when(jnp.logical_and(i % 2 == slot, i + 1 < nsteps))
        def _():
            issue_all(i + 1, 1 - slot)

    for slot in range(2):
        @pl.when(i % 2 == slot)
        def _():
            drain_all(slot)

            moe = gate_ref[:, 0:1] * _load_token_tiles(ybuf, COMB_T, slot * buf_tokens)
            for k in range(1, TOP_K):
                moe = moe + gate_ref[:, k:k + 1] * _load_token_tiles(
                    ybuf, COMB_T, slot * buf_tokens + k * COMB_T)
            x2 = x1_ref[...] + moe
            hp = _rms(x2, pg_ref[...]).astype(BF16)
            gate = _sigmoid(jnp.dot(hp, wg_ref[...], preferred_element_type=F32))
            emb = jnp.dot(p_ref[...].astype(BF16), wp_ref[...], preferred_element_type=F32)
            x3 = x2 + gate * emb
            if final:
                x3 = _rms(x3, fg_ref[...])
            out_ref[...] = x3


def _combine(dest_flat, ys, x1, gates_t, p2d, pg, wg, wp, fg, final):
    n = x1.shape[0]
    grid_spec = pltpu.PrefetchScalarGridSpec(
        num_scalar_prefetch=1,
        grid=(n // COMB_T,),
        in_specs=[
            pl.BlockSpec(memory_space=pl.ANY),
            pl.BlockSpec((COMB_T, D_MODEL), lambda i, d: (i, 0)),
            pl.BlockSpec((COMB_T, 8), lambda i, d: (i, 0)),
            pl.BlockSpec((COMB_T, PLE_DIM), lambda i, d: (i, 0)),
            pl.BlockSpec((1, D_MODEL), lambda i, d: (0, 0)),
            pl.BlockSpec((D_MODEL, D_MODEL), lambda i, d: (0, 0)),
            pl.BlockSpec((PLE_DIM, D_MODEL), lambda i, d: (0, 0)),
            pl.BlockSpec((1, D_MODEL), lambda i, d: (0, 0)),
        ],
        out_specs=pl.BlockSpec((COMB_T, D_MODEL), lambda i, d: (i, 0)),
        scratch_shapes=[pltpu.VMEM((2 * TOP_K * COMB_T * TOK_SUB, LANES), F32),
                        pltpu.SemaphoreType.DMA((2,))],
    )
    return pl.pallas_call(
        functools.partial(_combine_kernel, final=final),
        grid_spec=grid_spec,
        out_shape=jax.ShapeDtypeStruct((n, D_MODEL), F32),
        compiler_params=pltpu.CompilerParams(
            dimension_semantics=("arbitrary",), vmem_limit_bytes=VMEM_LIMIT),
        name="combine",
    )(dest_flat, ys, x1, gates_t, p2d, pg, wg, wp, fg)


def _layer(x2d, p2d, bsz, seq, mix_norm, w_in, sgu_ln_g, sgu_ln_b, sgu_w, sgu_b, conv_w, conv_b,
           dt_bias, a_log, d_skip, ssm_norm, w_branch_a, w_branch_b, w_out, ffn_norm, w_router,
           b_router, w1, b1, w2, b2, ple_norm, w_ple_gate, w_ple_proj, final_norm, final):
    n = x2d.shape[0]
    row = lambda a: a.reshape(1, -1).astype(F32)

    w_main = jnp.concatenate([w_in[:, :OFF_DT], w_in[:, OFF_GA:]], axis=1).astype(BF16)
    w_dt = jnp.pad(w_in[:, OFF_DT:OFF_GA], ((0, 0), (0, LANES - SSM_HEADS))).astype(BF16)
    pos = jnp.arange(SGU_LEN)
    allowed = (pos[None, :] // CHUNK) <= (pos[:, None] // CHUNK)
    wsgu = jnp.where(allowed[None], sgu_w, 0.0).astype(BF16)
    bsgu = jnp.repeat(sgu_b.T, SGU_HEAD_DIM, axis=1).astype(F32)
    pad_h = lambda a: jnp.pad(a.reshape(1, -1).astype(F32), ((0, 0), (0, LANES - SSM_HEADS)))
    dtb = pad_h(dt_bias)
    arow = pad_h(-jnp.exp(a_log.astype(F32)))
    dskip = jnp.repeat(d_skip.astype(F32), SSM_HEAD_DIM).reshape(1, -1)
    head_of_col = jnp.arange(SSM_INNER) // SSM_HEAD_DIM
    eexp = (jnp.arange(LANES)[:, None] == head_of_col[None, :]).astype(F32)
    t_out = jnp.arange((SSM_CONV - 1) * SSD_CHUNK)
    src = SSD_CHUNK + t_out % SSD_CHUNK - (SSM_CONV - 1) + t_out // SSD_CHUNK
    shift = (src[:, None] == jnp.arange(2 * SSD_CHUNK)[None, :]).astype(BF16)
    consts = [row(sgu_ln_g), row(sgu_ln_b), wsgu, bsgu, conv_w.astype(F32), row(conv_b), dtb, arow,
              dskip, row(ssm_norm), eexp, shift]

    proj, dt_raw = _in_proj(x2d, row(mix_norm), w_main, w_dt)
    x1 = _mixer(proj, dt_raw, x2d, bsz, seq, consts, w_branch_a.astype(BF16),
                w_branch_b.astype(BF16), w_out.astype(BF16))

    wr_pad = jnp.pad(w_router.astype(F32), ((0, 0), (0, LANES - N_EXPERTS)))
    hp, idx, gates, rank, cnt = _router(x1, row(ffn_norm), wr_pad, b_router.reshape(-1, 1).astype(F32))

    counts = cnt[:, 0]
    padded = (counts + MOE_BLOCK - 1) // MOE_BLOCK * MOE_BLOCK
    pend = jnp.cumsum(padded)
    pstart = pend - padded
    nk = n * TOP_K
    cap = (nk + MOE_BLOCK - 1) // MOE_BLOCK * MOE_BLOCK + N_EXPERTS * MOE_BLOCK
    n_blocks = cap // MOE_BLOCK
    ex = jnp.arange(N_EXPERTS, dtype=I32)
    dest = rank + jnp.sum(jnp.where(idx[..., None] == ex, pstart.astype(I32), 0), axis=-1)
    blk_start = jnp.arange(n_blocks, dtype=I32) * MOE_BLOCK
    blk_expert = jnp.minimum(jnp.sum((pend[None, :] <= blk_start[:, None]).astype(I32), axis=1),
                             N_EXPERTS - 1).astype(I32)
    n_used = (pend[-1] // MOE_BLOCK).astype(I32).reshape(1)
    live = jnp.where(counts > 0, ex, N_EXPERTS)
    later = jnp.concatenate([lax.cummin(live[::-1])[::-1][1:], jnp.full((1,), N_EXPERTS, I32)])
    blk_next = jnp.where(later < N_EXPERTS, later, -1).astype(I32)[blk_expert]

    dest_flat = dest.reshape(TOP_K, n // DISP_T, DISP_T).transpose(1, 0, 2).reshape(-1)
    pad_tbl = jnp.concatenate([pstart + counts, pend]).astype(I32)

    xs = _dispatch(dest_flat, pad_tbl, hp, cap)
    ys = _experts(blk_expert, blk_next, n_used, xs, w1.astype(F32), b1.reshape(N_EXPERTS, 1, -1).astype(F32),
                  w2.astype(F32), b2.reshape(N_EXPERTS, 1, -1).astype(F32))
    gates_t = jnp.pad(gates.T, ((0, 0), (0, 8 - TOP_K)))
    return _combine(dest_flat, ys, x1, gates_t, p2d, row(ple_norm), w_ple_gate.astype(BF16),
                    w_ple_proj.astype(BF16), row(final_norm), final)


def kernel(x, p, mix_norm, w_in, sgu_ln_g, sgu_ln_b, sgu_w, sgu_b, conv_w, conv_b, dt_bias, a_log,
           d_skip, ssm_norm, w_branch_a, w_branch_b, w_out, ffn_norm, w_router, b_router, w1, b1,
           w2, b2, ple_norm, w_ple_gate, w_ple_proj, final_norm):
    bsz, seq, d = x.shape
    depth = w_in.shape[0]
    assert d == D_MODEL and seq % max(MIX_T, IN_TM) == 0
    x2d = x.reshape(bsz * seq, d)
    for i in range(depth):
        x2d = _layer(x2d, p[i].reshape(bsz * seq, PLE_DIM), bsz, seq, mix_norm[i], w_in[i],
                     sgu_ln_g[i], sgu_ln_b[i], sgu_w[i], sgu_b[i], conv_w[i], conv_b[i], dt_bias[i],
                     a_log[i], d_skip[i], ssm_norm[i], w_branch_a[i], w_branch_b[i], w_out[i],
                     ffn_norm[i], w_router[i], b_router[i], w1[i], b1[i], w2[i], b2[i], ple_norm[i],
                     w_ple_gate[i], w_ple_proj[i], final_norm, final=(i == depth - 1))
    return x2d.reshape(bsz, seq, d)
```

```python
import functools
import math

import jax
import jax.numpy as jnp
from jax import lax
from jax.experimental import pallas as pl
from jax.experimental.pallas import tpu as pltpu

F32 = jnp.float32
BF16 = jnp.bfloat16
I32 = jnp.int32
U32 = jnp.uint32

D_MODEL = 1024
CHUNK = 64
PLE_DIM = 256
RMS_EPS = 1e-6
LN_EPS = 1e-5

SGU_HEADS = 8
SGU_HEAD_DIM = 128
SGU_WIDTH = SGU_HEADS * SGU_HEAD_DIM
SGU_LEN = 128

SSM_HEADS = 16
SSM_HEAD_DIM = 64
SSM_INNER = SSM_HEADS * SSM_HEAD_DIM
SSM_GROUPS = 4
SSM_STATE = 128
SSM_CONV = 4
SSD_CHUNK = 128
SSM_CONV_DIM = SSM_INNER + 2 * SSM_GROUPS * SSM_STATE

N_EXPERTS = 32
TOP_K = 4
D_FF_EXPERT = 1024
SWIGLU_LIMIT = 7.0
SWIGLU_ALPHA = 1.702
MOE_BLOCK = 512
MOE_SUB = 256

OFF_U = 0
OFF_V = OFF_U + SGU_WIDTH
OFF_Z = OFF_V + SGU_WIDTH
OFF_XBC = OFF_Z + SSM_INNER
OFF_DT = OFF_XBC + SSM_CONV_DIM
OFF_GA = OFF_DT + SSM_HEADS
OFF_GB = OFF_GA + D_MODEL
IN_PROJ_DIM = OFF_GB + D_MODEL

LANES = 128
PROJ_MAIN = IN_PROJ_DIM - SSM_HEADS
TOK_SUB = D_MODEL // LANES

VMEM_LIMIT = 56 * 1024 * 1024

IN_TM = 1024
IN_TN = 1024
MIX_T = 512
ROUTE_T = 512
DISP_T = 512
COMB_T = DISP_T
DMA_UNROLL = 8
CAST_ROWS = 128


def _sigmoid(x):
    return 1.0 / (1.0 + jnp.exp(-x))


def _gelu_exact(x):
    return 0.5 * x * (1.0 + lax.erf(x * (1.0 / math.sqrt(2.0))))


def _softplus(x):
    return jnp.maximum(x, 0.0) + jnp.log1p(jnp.exp(-jnp.abs(x)))


def _rms(x, g):
    ms = jnp.mean(x * x, axis=-1, keepdims=True)
    return x * lax.rsqrt(ms + RMS_EPS) * g


def _store_token_tiles(ref, val, rows, start=0):
    for j in range(TOK_SUB):
        ref[pl.ds(start * TOK_SUB + j, rows, stride=TOK_SUB), :] = val[:, j * LANES:(j + 1) * LANES]


def _load_token_tiles(ref, rows, start=0):
    return jnp.concatenate(
        [ref[pl.ds(start * TOK_SUB + j, rows, stride=TOK_SUB), :] for j in range(TOK_SUB)], axis=1)


def _inproj_kernel(x_ref, g_ref, w_ref, wdt_ref, proj_ref, dt_ref, h_scr):
    @pl.when(pl.program_id(1) == 0)
    def _():
        h = _rms(x_ref[...], g_ref[...]).astype(BF16)
        h_scr[...] = h
        dt_ref[...] = jnp.dot(h, wdt_ref[...], preferred_element_type=F32)

    proj_ref[...] = jnp.dot(h_scr[...], w_ref[...], preferred_element_type=F32).astype(BF16)


def _in_proj(x2d, g, w_main, w_dt):
    n = x2d.shape[0]
    grid = (n // IN_TM, PROJ_MAIN // IN_TN)
    return pl.pallas_call(
        _inproj_kernel,
        grid=grid,
        in_specs=[
            pl.BlockSpec((IN_TM, D_MODEL), lambda i, j: (i, 0)),
            pl.BlockSpec((1, D_MODEL), lambda i, j: (0, 0)),
            pl.BlockSpec((D_MODEL, IN_TN), lambda i, j: (0, j)),
            pl.BlockSpec((D_MODEL, LANES), lambda i, j: (0, 0)),
        ],
        out_specs=[
            pl.BlockSpec((IN_TM, IN_TN), lambda i, j: (i, j)),
            pl.BlockSpec((IN_TM, LANES), lambda i, j: (i, 0)),
        ],
        out_shape=[
            jax.ShapeDtypeStruct((n, PROJ_MAIN), BF16),
            jax.ShapeDtypeStruct((n, LANES), F32),
        ],
        scratch_shapes=[pltpu.VMEM((IN_TM, D_MODEL), BF16)],
        compiler_params=pltpu.CompilerParams(
            dimension_semantics=("arbitrary", "arbitrary"), vmem_limit_bytes=VMEM_LIMIT),
        name="in_proj",
    )(x2d, g, w_main, w_dt)


def _split3(a):
    a1 = a.astype(BF16)
    r1 = a - a1.astype(F32)
    a2 = r1.astype(BF16)
    a3 = (r1 - a2.astype(F32)).astype(BF16)
    return a1, a2, a3


def _dot3(parts, w):
    out = jnp.dot(parts[0], w, preferred_element_type=F32)
    for p in parts[1:]:
        out = out + jnp.dot(p, w, preferred_element_type=F32)
    return out


def _mixer_kernel(u_ref, v_ref, z_ref, xb0_ref, xb1_ref, dt_ref, ga_ref, gb_ref, x_ref,
                  lng_ref, lnb_ref, wsgu_ref, bsgu_ref, convw_ref, convb_ref, dtb_ref, acol_ref,
                  dskip_ref, normw_ref, eexp_ref, shift_ref, triu_ref, wa_ref, wb_ref, wo_ref,
                  x1_ref,
                  tail_scr, xc_scr, state_scr, ya_scr, yb_scr, yap_scr, ybp_scr, *, tps):
    nchunk = MIX_T // SSD_CHUNK
    s = pl.program_id(0)

    @pl.when(s % tps == 0)
    def _():
        tail_scr[...] = jnp.zeros_like(tail_scr)
        state_scr[...] = jnp.zeros_like(state_scr)

    @pl.when(s == 0)
    def _():
        yap_scr[...] = jnp.zeros_like(yap_scr)
        ybp_scr[...] = jnp.zeros_like(ybp_scr)

    for c in range(nchunk):
        r0 = c * SSD_CHUNK
        for hf, xb_ref in enumerate((xb0_ref, xb1_ref)):
            cols = slice(hf * 1024, (hf + 1) * 1024)
            cur = xb_ref[r0:r0 + SSD_CHUNK, :]
            prev = tail_scr[:, cols] if c == 0 else xb_ref[r0 - SSD_CHUNK:r0, :]
            sh = jnp.dot(shift_ref[...], jnp.concatenate([prev, cur], axis=0), preferred_element_type=F32)
            acc = convb_ref[:, cols] + convw_ref[SSM_CONV - 1:SSM_CONV, cols] * cur.astype(F32)
            for j in range(SSM_CONV - 1):
                acc = acc + convw_ref[j:j + 1, cols] * sh[j * SSD_CHUNK:(j + 1) * SSD_CHUNK, :]
            xc_scr[r0:r0 + SSD_CHUNK, cols] = acc * _sigmoid(acc)
    tail_scr[:, 0:1024] = xb0_ref[MIX_T - SSD_CHUNK:MIX_T, :]
    tail_scr[:, 1024:2048] = xb1_ref[MIX_T - SSD_CHUNK:MIX_T, :]

    row_i = lax.broadcasted_iota(I32, (SSD_CHUNK, SSD_CHUNK), 0)
    col_i = lax.broadcasted_iota(I32, (SSD_CHUNK, SSD_CHUNK), 1)
    tril = row_i >= col_i
    lane_hi = col_i >= SSM_HEAD_DIM
    eexp = eexp_ref[...]
    triu = triu_ref[...]

    for c in range(nchunk):
        rows = slice(c * SSD_CHUNK, (c + 1) * SSD_CHUNK)

        ug = _gelu_exact(u_ref[rows, :].astype(F32))
        vg = _gelu_exact(v_ref[rows, :].astype(F32))
        mu = jnp.mean(vg, axis=-1, keepdims=True)
        vc = vg - mu
        var = jnp.mean(vc * vc, axis=-1, keepdims=True)
        vn = (vc * lax.rsqrt(var + LN_EPS) * lng_ref[...] + lnb_ref[...]).astype(BF16)
        for g in range(SGU_HEADS):
            cols = slice(g * SGU_HEAD_DIM, (g + 1) * SGU_HEAD_DIM)
            mixed = jnp.dot(wsgu_ref[g], vn[:, cols], preferred_element_type=F32)
            ya_scr[rows, cols] = (ug[:, cols] * (mixed + bsgu_ref[:, cols])).astype(BF16)

        dt_t = _softplus(dt_ref[rows, :].T[0:SSM_HEADS, :] + dtb_ref[...])
        acs_t = _dot3(_split3(dt_t * acol_ref[...]), triu)
        w_t = jnp.exp(acs_t[:, SSD_CHUNK - 1:SSD_CHUNK] - acs_t) * dt_t
        acs = jnp.concatenate([acs_t, jnp.zeros((LANES - SSM_HEADS, SSD_CHUNK), F32)], axis=0).T
        aend = jnp.broadcast_to(acs[SSD_CHUNK - 1:SSD_CHUNK, :], (8, LANES))
        dec = jnp.exp(_dot3(_split3(aend), eexp)[0:1, :])

        for gi in range(SSM_GROUPS):
            bg = xc_scr[rows, SSM_INNER + gi * SSM_STATE:SSM_INNER + (gi + 1) * SSM_STATE]
            cg = xc_scr[rows, SSM_INNER + (SSM_GROUPS + gi) * SSM_STATE:
                        SSM_INNER + (SSM_GROUPS + gi + 1) * SSM_STATE]
            bg_t = bg.T
            cb = jnp.dot(cg.astype(BF16), bg_t.astype(BF16), preferred_element_type=F32)
            for pr in range(2):
                lb = gi * 2 + pr
                cols = slice(lb * LANES, (lb + 1) * LANES)
                xpair = xc_scr[rows, cols]
                hpair = state_scr[:, cols]
                acc = jnp.zeros((SSD_CHUNK, LANES), F32)
                st = jnp.zeros((SSM_STATE, LANES), F32)
                for hh in range(2):
                    h = lb * 2 + hh
                    lmask = lane_hi if hh == 1 else jnp.logical_not(lane_hi)
                    xm = jnp.where(lmask, xpair, 0.0).astype(BF16)
                    hm = jnp.where(lmask, hpair, 0.0).astype(BF16)
                    colb = acs[:, h:h + 1]
                    rowb = acs_t[h:h + 1, :]
                    decay = jnp.exp(jnp.where(tril, colb - rowb, -jnp.inf))
                    m_h = (cb * decay * dt_t[h:h + 1, :]).astype(BF16)
                    c_h = (cg * jnp.exp(colb)).astype(BF16)
                    acc = acc + jnp.dot(m_h, xm, preferred_element_type=F32)
                    acc = acc + jnp.dot(c_h, hm, preferred_element_type=F32)
                    btw = (bg_t * w_t[h:h + 1, :]).astype(BF16)
                    st = st + jnp.dot(btw, xm, preferred_element_type=F32)
                state_scr[:, cols] = hpair * dec[:, cols] + st
                y = acc + dskip_ref[:, cols] * xpair
                zz = z_ref[rows, cols].astype(F32)
                xc_scr[rows, cols] = y * (zz * _sigmoid(zz))
            gcols = slice(gi * 2 * LANES, (gi + 1) * 2 * LANES)
            yg = xc_scr[rows, gcols]
            ms = jnp.mean(yg * yg, axis=-1, keepdims=True)
            yb_scr[rows, gcols] = (yg * lax.rsqrt(ms + RMS_EPS) * normw_ref[:, gcols]).astype(BF16)

    ma = jnp.dot(yap_scr[...], wa_ref[...], preferred_element_type=F32)
    mb = jnp.dot(ybp_scr[...], wb_ref[...], preferred_element_type=F32)
    merged = (_sigmoid(ga_ref[...].astype(F32)) * ma + _sigmoid(gb_ref[...].astype(F32)) * mb)
    x1_ref[...] = x_ref[...] + jnp.dot(merged.astype(BF16), wo_ref[...], preferred_element_type=F32)
    yap_scr[...] = ya_scr[...]
    ybp_scr[...] = yb_scr[...]


def _mixer(proj, dt_raw, x2d, bsz, seq, consts, wa, wb, wo):
    n = x2d.shape[0]
    tps = seq // MIX_T
    ntiles = bsz * tps
    cur = lambda s: jnp.minimum(s, ntiles - 1)
    prev = lambda s: jnp.maximum(s - 1, 0)

    def pcol(k, tile):
        return pl.BlockSpec((MIX_T, 1024), lambda s, k=k: (tile(s), k))

    def full(a):
        nd = a.ndim
        return pl.BlockSpec(a.shape, lambda s, nd=nd: (0,) * nd)

    in_specs = [pcol(0, cur), pcol(1, cur), pcol(2, cur), pcol(3, cur), pcol(4, cur),
                pl.BlockSpec((MIX_T, LANES), lambda s: (cur(s), 0)),
                pcol(5, prev), pcol(6, prev),
                pl.BlockSpec((MIX_T, D_MODEL), lambda s: (prev(s), 0))]
    in_specs += [full(a) for a in consts] + [full(wa), full(wb), full(wo)]
    return pl.pallas_call(
        functools.partial(_mixer_kernel, tps=tps),
        grid=(ntiles + 1,),
        in_specs=in_specs,
        out_specs=pl.BlockSpec((MIX_T, D_MODEL), lambda s: (prev(s), 0)),
        out_shape=jax.ShapeDtypeStruct((n, D_MODEL), F32),
        scratch_shapes=[
            pltpu.VMEM((SSD_CHUNK, SSM_CONV_DIM), BF16),
            pltpu.VMEM((MIX_T, SSM_CONV_DIM), F32),
            pltpu.VMEM((SSM_STATE, SSM_INNER), F32),
            pltpu.VMEM((MIX_T, SGU_WIDTH), BF16),
            pltpu.VMEM((MIX_T, SSM_INNER), BF16),
            pltpu.VMEM((MIX_T, SGU_WIDTH), BF16),
            pltpu.VMEM((MIX_T, SSM_INNER), BF16),
        ],
        compiler_params=pltpu.CompilerParams(
            dimension_semantics=("arbitrary",), vmem_limit_bytes=VMEM_LIMIT),
        name="mixer",
    )(proj, proj, proj, proj, proj, dt_raw, proj, proj, x2d, *consts, wa, wb, wo)


def _router_kernel(x_ref, g_ref, wr_ref, br_ref, hp_ref, idx_ref, gate_ref, rank_ref, cnt_ref, carry_scr):
    @pl.when(pl.program_id(0) == 0)
    def _():
        carry_scr[...] = jnp.zeros_like(carry_scr)

    h = _rms(x_ref[...], g_ref[...])
    _store_token_tiles(hp_ref, h, ROUTE_T)

    logits = jnp.dot(h, wr_ref[...], preferred_element_type=F32, precision=lax.Precision.HIGHEST)
    lt = logits.T[0:N_EXPERTS, :] + br_ref[...]
    eidx = lax.broadcasted_iota(I32, (N_EXPERTS, ROUTE_T), 0).astype(F32)
    vals = lt
    sel_any = jnp.zeros((N_EXPERTS, ROUTE_T), F32)
    sels, tops = [], []
    for k in range(TOP_K):
        m = jnp.max(vals, axis=0, keepdims=True)
        first = jnp.min(jnp.where(vals == m, eidx, float(N_EXPERTS)), axis=0, keepdims=True)
        sel = eidx == first
        vals = jnp.where(sel, -jnp.inf, vals)
        sel_f = sel.astype(F32)
        sel_any = sel_any + sel_f
        sels.append(sel_f)
        tops.append(m)
        idx_ref[k:k + 1, :] = first.astype(I32)
    es = [jnp.exp(t - tops[0]) for t in tops]
    denom = es[0] + es[1] + es[2] + es[3]
    for k in range(TOP_K):
        gate_ref[k:k + 1, :] = es[k] / denom

    r_i = lax.broadcasted_iota(I32, (ROUTE_T, ROUTE_T), 0)
    c_i = lax.broadcasted_iota(I32, (ROUTE_T, ROUTE_T), 1)
    upper = (r_i < c_i).astype(BF16)
    excl = jnp.dot(sel_any.astype(BF16), upper, preferred_element_type=F32) + carry_scr[:, 0:1]
    for k in range(TOP_K):
        rank_ref[k:k + 1, :] = jnp.sum(sels[k] * excl, axis=0, keepdims=True).astype(I32)
    new_carry = carry_scr[...] + jnp.sum(sel_any, axis=1, keepdims=True)
    carry_scr[...] = new_carry
    cnt_ref[...] = new_carry.astype(I32)


def _router(x1, g, wr_pad, br_col):
    n = x1.shape[0]
    return pl.pallas_call(
        _router_kernel,
        grid=(n // ROUTE_T,),
        in_specs=[
            pl.BlockSpec((ROUTE_T, D_MODEL), lambda i: (i, 0)),
            pl.BlockSpec((1, D_MODEL), lambda i: (0, 0)),
            pl.BlockSpec((D_MODEL, LANES), lambda i: (0, 0)),
            pl.BlockSpec((N_EXPERTS, 1), lambda i: (0, 0)),
        ],
        out_specs=[
            pl.BlockSpec((ROUTE_T * TOK_SUB, LANES), lambda i: (i, 0)),
            pl.BlockSpec((TOP_K, ROUTE_T), lambda i: (0, i)),
            pl.BlockSpec((TOP_K, ROUTE_T), lambda i: (0, i)),
            pl.BlockSpec((TOP_K, ROUTE_T), lambda i: (0, i)),
            pl.BlockSpec((N_EXPERTS, LANES), lambda i: (0, 0)),
        ],
        out_shape=[
            jax.ShapeDtypeStruct((n * TOK_SUB, LANES), F32),
            jax.ShapeDtypeStruct((TOP_K, n), I32),
            jax.ShapeDtypeStruct((TOP_K, n), F32),
            jax.ShapeDtypeStruct((TOP_K, n), I32),
            jax.ShapeDtypeStruct((N_EXPERTS, LANES), I32),
        ],
        scratch_shapes=[pltpu.VMEM((N_EXPERTS, LANES), F32)],
        compiler_params=pltpu.CompilerParams(
            dimension_semantics=("arbitrary",), vmem_limit_bytes=VMEM_LIMIT),
        name="router",
    )(x1, g, wr_pad, br_col)


def _tile_rows(idx):
    return pl.ds(pl.multiple_of(idx * TOK_SUB, TOK_SUB), TOK_SUB)


def _dispatch_kernel(dest_ref, pad_ref, hp_ref, xs_hbm, zero_scr, sem_rows):
    i = pl.program_id(0)
    tbl = i * (TOP_K * DISP_T)

    def zero_copy(s):
        return pltpu.make_async_copy(zero_scr, xs_hbm.at[_tile_rows(s), :], sem_rows)

    @pl.when(i == 0)
    def _():
        zero_scr[...] = jnp.zeros_like(zero_scr)
        for e in range(N_EXPERTS):
            lax.fori_loop(pad_ref[e], pad_ref[N_EXPERTS + e], lambda s, c: (zero_copy(s).start(), c)[1], 0)
        for e in range(N_EXPERTS):
            lax.fori_loop(pad_ref[e], pad_ref[N_EXPERTS + e], lambda s, c: (zero_copy(s).wait(), c)[1], 0)

    def tile_copy(t, k):
        d = dest_ref[tbl + k * DISP_T + t]
        return pltpu.make_async_copy(hp_ref.at[_tile_rows(t), :], xs_hbm.at[_tile_rows(d), :], sem_rows)

    def issue(tb, carry):
        for r in range(DMA_UNROLL):
            for k in range(TOP_K):
                tile_copy(tb * DMA_UNROLL + r, k).start(priority=(r * TOP_K + k) % 2)
        return carry

    lax.fori_loop(0, DISP_T // DMA_UNROLL, issue, 0)

    def drain(tb, carry):
        for _ in range(DMA_UNROLL * TOP_K):
            zero_copy(0).wait()
        return carry

    lax.fori_loop(0, DISP_T // DMA_UNROLL, drain, 0)


def _dispatch(dest_flat, pad_tbl, hp, cap):
    n = hp.shape[0] // TOK_SUB
    grid_spec = pltpu.PrefetchScalarGridSpec(
        num_scalar_prefetch=2,
        grid=(n // DISP_T,),
        in_specs=[pl.BlockSpec((DISP_T * TOK_SUB, LANES), lambda i, d, p: (i, 0))],
        out_specs=pl.BlockSpec(memory_space=pl.ANY),
        scratch_shapes=[pltpu.VMEM((TOK_SUB, LANES), F32), pltpu.SemaphoreType.DMA],
    )
    return pl.pallas_call(
        _dispatch_kernel,
        grid_spec=grid_spec,
        out_shape=jax.ShapeDtypeStruct((cap * TOK_SUB, LANES), F32),
        compiler_params=pltpu.CompilerParams(dimension_semantics=("arbitrary",)),
        name="dispatch",
    )(dest_flat, pad_tbl, hp)


def _expert_kernel(be_ref, nxt_ref, nu_ref, xs_ref, w1_hbm, b1_ref, w2_hbm, b2_ref, ys_ref,
                   w1f_scr, w2f_scr, w1b_scr, w2b_scr, sems):
    b = pl.program_id(0)
    used = b < nu_ref[0]
    new_expert = jnp.logical_or(b == 0, be_ref[b] != be_ref[jnp.maximum(b - 1, 0)])

    def fetch(e):
        return (pltpu.make_async_copy(w1_hbm.at[e], w1f_scr, sems.at[0]),
                pltpu.make_async_copy(w2_hbm.at[e], w2f_scr, sems.at[1]))

    @pl.when(jnp.logical_and(used, b == 0))
    def _():
        for c in fetch(be_ref[0]):
            c.start()

    @pl.when(jnp.logical_and(used, new_expert))
    def _():
        for c in fetch(be_ref[b]):
            c.wait()

        def cast(rb, carry):
            rows = pl.ds(pl.multiple_of(rb * CAST_ROWS, CAST_ROWS), CAST_ROWS)
            w1b_scr[rows, :] = w1f_scr[rows, :].astype(BF16)
            w2b_scr[rows, :] = w2f_scr[rows, :].astype(BF16)
            return carry

        lax.fori_loop(0, D_MODEL // CAST_ROWS, cast, 0)

        @pl.when(nxt_ref[b] >= 0)
        def _():
            for c in fetch(nxt_ref[b]):
                c.start()

    @pl.when(used)
    def _():
        for sub in range(MOE_BLOCK // MOE_SUB):
            x = _load_token_tiles(xs_ref, MOE_SUB, sub * MOE_SUB).astype(BF16)
            hid = jnp.dot(x, w1b_scr[...], preferred_element_type=F32) + b1_ref[0]
            glu = jnp.minimum(hid[:, :D_FF_EXPERT], SWIGLU_LIMIT)
            lin = jnp.clip(hid[:, D_FF_EXPERT:], -SWIGLU_LIMIT, SWIGLU_LIMIT)
            act = glu * _sigmoid(SWIGLU_ALPHA * glu) * (lin + 1.0)
            y = jnp.dot(act.astype(BF16), w2b_scr[...], preferred_element_type=F32) + b2_ref[0]
            _store_token_tiles(ys_ref, y, MOE_SUB, sub * MOE_SUB)


def _experts(blk_expert, blk_next, n_used, xs, w1, b1, w2, b2):
    cap = xs.shape[0] // TOK_SUB
    n_blocks = cap // MOE_BLOCK
    grid_spec = pltpu.PrefetchScalarGridSpec(
        num_scalar_prefetch=3,
        grid=(n_blocks,),
        in_specs=[
            pl.BlockSpec((MOE_BLOCK * TOK_SUB, LANES), lambda b, be, nx, nu: (b, 0)),
            pl.BlockSpec(memory_space=pl.ANY),
            pl.BlockSpec((1, 1, 2 * D_FF_EXPERT), lambda b, be, nx, nu: (be[b], 0, 0)),
            pl.BlockSpec(memory_space=pl.ANY),
            pl.BlockSpec((1, 1, D_MODEL), lambda b, be, nx, nu: (be[b], 0, 0)),
        ],
        out_specs=pl.BlockSpec((MOE_BLOCK * TOK_SUB, LANES), lambda b, be, nx, nu: (b, 0)),
        scratch_shapes=[pltpu.VMEM((D_MODEL, 2 * D_FF_EXPERT), F32),
                        pltpu.VMEM((D_FF_EXPERT, D_MODEL), F32),
                        pltpu.VMEM((D_MODEL, 2 * D_FF_EXPERT), BF16),
                        pltpu.VMEM((D_FF_EXPERT, D_MODEL), BF16),
                        pltpu.SemaphoreType.DMA((2,))],
    )
    assert D_MODEL == D_FF_EXPERT
    return pl.pallas_call(
        _expert_kernel,
        grid_spec=grid_spec,
        out_shape=jax.ShapeDtypeStruct((cap * TOK_SUB, LANES), F32),
        compiler_params=pltpu.CompilerParams(
            dimension_semantics=("arbitrary",), vmem_limit_bytes=VMEM_LIMIT),
        name="experts",
    )(blk_expert, blk_next, n_used, xs, w1, b1, w2, b2)


def _combine_kernel(dest_ref, ys_hbm, x1_ref, gate_ref, p_ref, pg_ref, wg_ref, wp_ref, fg_ref,
                    out_ref, ybuf, sems, *, final):
    i = pl.program_id(0)
    nsteps = pl.num_programs(0)
    buf_tokens = TOP_K * COMB_T

    def issue_all(step, slot):
        tbl = step * buf_tokens

        def issue(tb, carry):
            for r in range(DMA_UNROLL):
                for k in range(TOP_K):
                    row = k * COMB_T + tb * DMA_UNROLL + r
                    pltpu.make_async_copy(
                        ys_hbm.at[_tile_rows(dest_ref[tbl + row]), :],
                        ybuf.at[_tile_rows(slot * buf_tokens + row), :],
                        sems.at[slot]).start(priority=(r * TOP_K + k) % 2)
            return carry

        lax.fori_loop(0, COMB_T // DMA_UNROLL, issue, 0)

    def drain_all(slot):
        def drain(tb, carry):
            for _ in range(DMA_UNROLL * TOP_K):
                pltpu.make_async_copy(ys_hbm.at[_tile_rows(0), :],
                                      ybuf.at[_tile_rows(slot * buf_tokens), :], sems.at[slot]).wait()
            return carry

        lax.fori_loop(0, COMB_T // DMA_UNROLL, drain, 0)

    @pl.when(i == 0)
    def _():
        issue_all(0, 0)

    for slot in range(2):
        @pl.when(jnp.logical_and(i % 2 == slot, i + 1 < nsteps))
        def _():
            issue_all(i + 1, 1 - slot)

    for slot in range(2):
        @pl.when(i % 2 == slot)
        def _():
            drain_all(slot)

            moe = gate_ref[:, 0:1] * _load_token_tiles(ybuf, COMB_T, slot * buf_tokens)
            for k in range(1, TOP_K):
                moe = moe + gate_ref[:, k:k + 1] * _load_token_tiles(
                    ybuf, COMB_T, slot * buf_tokens + k * COMB_T)
            x2 = x1_ref[...] + moe
            hp = _rms(x2, pg_ref[...]).astype(BF16)
            gate = _sigmoid(jnp.dot(hp, wg_ref[...], preferred_element_type=F32))
            emb = jnp.dot(p_ref[...].astype(BF16), wp_ref[...], preferred_element_type=F32)
            x3 = x2 + gate * emb
            if final:
                x3 = _rms(x3, fg_ref[...])
            out_ref[...] = x3


def _combine(dest_flat, ys, x1, gates_t, p2d, pg, wg, wp, fg, final):
    n = x1.shape[0]
    grid_spec = pltpu.PrefetchScalarGridSpec(
        num_scalar_prefetch=1,
        grid=(n // COMB_T,),
        in_specs=[
            pl.BlockSpec(memory_space=pl.ANY),
            pl.BlockSpec((COMB_T, D_MODEL), lambda i, d: (i, 0)),
            pl.BlockSpec((COMB_T, 8), lambda i, d: (i, 0)),
            pl.BlockSpec((COMB_T, PLE_DIM), lambda i, d: (i, 0)),
            pl.BlockSpec((1, D_MODEL), lambda i, d: (0, 0)),
            pl.BlockSpec((D_MODEL, D_MODEL), lambda i, d: (0, 0)),
            pl.BlockSpec((PLE_DIM, D_MODEL), lambda i, d: (0, 0)),
            pl.BlockSpec((1, D_MODEL), lambda i, d: (0, 0)),
        ],
        out_specs=pl.BlockSpec((COMB_T, D_MODEL), lambda i, d: (i, 0)),
        scratch_shapes=[pltpu.VMEM((2 * TOP_K * COMB_T * TOK_SUB, LANES), F32),
                        pltpu.SemaphoreType.DMA((2,))],
    )
    return pl.pallas_call(
        functools.partial(_combine_kernel, final=final),
        grid_spec=grid_spec,
        out_shape=jax.ShapeDtypeStruct((n, D_MODEL), F32),
        compiler_params=pltpu.CompilerParams(
            dimension_semantics=("arbitrary",), vmem_limit_bytes=VMEM_LIMIT),
        name="combine",
    )(dest_flat, ys, x1, gates_t, p2d, pg, wg, wp, fg)


def _layer(x2d, p2d, bsz, seq, mix_norm, w_in, sgu_ln_g, sgu_ln_b, sgu_w, sgu_b, conv_w, conv_b,
           dt_bias, a_log, d_skip, ssm_norm, w_branch_a, w_branch_b, w_out, ffn_norm, w_router,
           b_router, w1, b1, w2, b2, ple_norm, w_ple_gate, w_ple_proj, final_norm, final):
    n = x2d.shape[0]
    row = lambda a: a.reshape(1, -1).astype(F32)

    w_main = jnp.concatenate([w_in[:, :OFF_DT], w_in[:, OFF_GA:]], axis=1).astype(BF16)
    w_dt = jnp.pad(w_in[:, OFF_DT:OFF_GA], ((0, 0), (0, LANES - SSM_HEADS))).astype(BF16)
    pos = jnp.arange(SGU_LEN)
    allowed = (pos[None, :] // CHUNK) <= (pos[:, None] // CHUNK)
    wsgu = jnp.where(allowed[None], sgu_w, 0.0).astype(BF16)
    bsgu = jnp.repeat(sgu_b.T, SGU_HEAD_DIM, axis=1).astype(F32)
    dtb = dt_bias.reshape(-1, 1).astype(F32)
    acol = -jnp.exp(a_log.astype(F32)).reshape(-1, 1)
    dskip = jnp.repeat(d_skip.astype(F32), SSM_HEAD_DIM).reshape(1, -1)
    head_of_col = jnp.arange(SSM_INNER) // SSM_HEAD_DIM
    eexp = (jnp.arange(LANES)[:, None] == head_of_col[None, :]).astype(BF16)
    triu = (jnp.arange(SSD_CHUNK)[:, None] <= jnp.arange(SSD_CHUNK)[None, :]).astype(BF16)
    t_out = jnp.arange((SSM_CONV - 1) * SSD_CHUNK)
    src = SSD_CHUNK + t_out % SSD_CHUNK - (SSM_CONV - 1) + t_out // SSD_CHUNK
    shift = (src[:, None] == jnp.arange(2 * SSD_CHUNK)[None, :]).astype(BF16)
    consts = [row(sgu_ln_g), row(sgu_ln_b), wsgu, bsgu, conv_w.astype(F32), row(conv_b), dtb, acol,
              dskip, row(ssm_norm), eexp, shift, triu]

    proj, dt_raw = _in_proj(x2d, row(mix_norm), w_main, w_dt)
    x1 = _mixer(proj, dt_raw, x2d, bsz, seq, consts, w_branch_a.astype(BF16),
                w_branch_b.astype(BF16), w_out.astype(BF16))

    wr_pad = jnp.pad(w_router.astype(F32), ((0, 0), (0, LANES - N_EXPERTS)))
    hp, idx, gates, rank, cnt = _router(x1, row(ffn_norm), wr_pad, b_router.reshape(-1, 1).astype(F32))

    counts = cnt[:, 0]
    padded = (counts + MOE_BLOCK - 1) // MOE_BLOCK * MOE_BLOCK
    pend = jnp.cumsum(padded)
    pstart = pend - padded
    nk = n * TOP_K
    cap = (nk + MOE_BLOCK - 1) // MOE_BLOCK * MOE_BLOCK + N_EXPERTS * MOE_BLOCK
    n_blocks = cap // MOE_BLOCK
    ex = jnp.arange(N_EXPERTS, dtype=I32)
    dest = rank + jnp.sum(jnp.where(idx[..., None] == ex, pstart.astype(I32), 0), axis=-1)
    blk_start = jnp.arange(n_blocks, dtype=I32) * MOE_BLOCK
    blk_expert = jnp.minimum(jnp.sum((pend[None, :] <= blk_start[:, None]).astype(I32), axis=1),
                             N_EXPERTS - 1).astype(I32)
    n_used = (pend[-1] // MOE_BLOCK).astype(I32).reshape(1)
    live = jnp.where(counts > 0, ex, N_EXPERTS)
    later = jnp.concatenate([lax.cummin(live[::-1])[::-1][1:], jnp.full((1,), N_EXPERTS, I32)])
    blk_next = jnp.where(later < N_EXPERTS, later, -1).astype(I32)[blk_expert]

    dest_flat = dest.reshape(TOP_K, n // DISP_T, DISP_T).transpose(1, 0, 2).reshape(-1)
    pad_tbl = jnp.concatenate([pstart + counts, pend]).astype(I32)

    xs = _dispatch(dest_flat, pad_tbl, hp, cap)
    ys = _experts(blk_expert, blk_next, n_used, xs, w1.astype(F32), b1.reshape(N_EXPERTS, 1, -1).astype(F32),
                  w2.astype(F32), b2.reshape(N_EXPERTS, 1, -1).astype(F32))
    gates_t = jnp.pad(gates.T, ((0, 0), (0, 8 - TOP_K)))
    return _combine(dest_flat, ys, x1, gates_t, p2d, row(ple_norm), w_ple_gate.astype(BF16),
                    w_ple_proj.astype(BF16), row(final_norm), final)


def kernel(x, p, mix_norm, w_in, sgu_ln_g, sgu_ln_b, sgu_w, sgu_b, conv_w, conv_b, dt_bias, a_log,
           d_skip, ssm_norm, w_branch_a, w_branch_b, w_out, ffn_norm, w_router, b_router, w1, b1,
           w2, b2, ple_norm, w_ple_gate, w_ple_proj, final_norm):
    bsz, seq, d = x.shape
    depth = w_in.shape[0]
    assert d == D_MODEL and seq % max(MIX_T, IN_TM) == 0
    x2d = x.reshape(bsz * seq, d)
    for i in range(depth):
        x2d = _layer(x2d, p[i].reshape(bsz * seq, PLE_DIM), bsz, seq, mix_norm[i], w_in[i],
                     sgu_ln_g[i], sgu_ln_b[i], sgu_w[i], sgu_b[i], conv_w[i], conv_b[i], dt_bias[i],
                     a_log[i], d_skip[i], ssm_norm[i], w_branch_a[i], w_branch_b[i], w_out[i],
                     ffn_norm[i], w_router[i], b_router[i], w1[i], b1[i], w2[i], b2[i], ple_norm[i],
                     w_ple_gate[i], w_ple_proj[i], final_norm, final=(i == depth - 1))
    return x2d.reshape(bsz, seq, d)
```

```python
import functools
import math

import jax
import jax.numpy as jnp
from jax import lax
from jax.experimental import pallas as pl
from jax.experimental.pallas import tpu as pltpu

F32 = jnp.float32
BF16 = jnp.bfloat16
I32 = jnp.int32
U32 = jnp.uint32

D_MODEL = 1024
CHUNK = 64
PLE_DIM = 256
RMS_EPS = 1e-6
LN_EPS = 1e-5

SGU_HEADS = 8
SGU_HEAD_DIM = 128
SGU_WIDTH = SGU_HEADS * SGU_HEAD_DIM
SGU_LEN = 128

SSM_HEADS = 16
SSM_HEAD_DIM = 64
SSM_INNER = SSM_HEADS * SSM_HEAD_DIM
SSM_GROUPS = 4
SSM_STATE = 128
SSM_CONV = 4
SSD_CHUNK = 128
SSM_CONV_DIM = SSM_INNER + 2 * SSM_GROUPS * SSM_STATE

N_EXPERTS = 32
TOP_K = 4
D_FF_EXPERT = 1024
SWIGLU_LIMIT = 7.0
SWIGLU_ALPHA = 1.702
MOE_BLOCK = 512
MOE_SUB = 256

OFF_U = 0
OFF_V = OFF_U + SGU_WIDTH
OFF_Z = OFF_V + SGU_WIDTH
OFF_XBC = OFF_Z + SSM_INNER
OFF_DT = OFF_XBC + SSM_CONV_DIM
OFF_GA = OFF_DT + SSM_HEADS
OFF_GB = OFF_GA + D_MODEL
IN_PROJ_DIM = OFF_GB + D_MODEL

LANES = 128
PROJ_MAIN = IN_PROJ_DIM - SSM_HEADS
TOK_SUB = D_MODEL // LANES

VMEM_LIMIT = 56 * 1024 * 1024

IN_TM = 1024
IN_TN = 1792
MIX_T = 512
ROUTE_T = 512
DISP_T = 512
COMB_T = DISP_T
DMA_UNROLL = 8
ZERO_RUN = 64
CAST_ROWS = 128


def _sigmoid(x):
    return 1.0 / (1.0 + jnp.exp(-x))


def _gelu_exact(x):
    return 0.5 * x * (1.0 + lax.erf(x * (1.0 / math.sqrt(2.0))))


def _softplus(x):
    return jnp.maximum(x, 0.0) + jnp.log1p(jnp.exp(-jnp.abs(x)))


def _rms(x, g):
    ms = jnp.mean(x * x, axis=-1, keepdims=True)
    return x * lax.rsqrt(ms + RMS_EPS) * g


def _store_token_tiles(ref, val, rows, start=0):
    for j in range(TOK_SUB):
        ref[pl.ds(start * TOK_SUB + j, rows, stride=TOK_SUB), :] = val[:, j * LANES:(j + 1) * LANES]


def _load_token_tiles(ref, rows, start=0):
    return jnp.concatenate(
        [ref[pl.ds(start * TOK_SUB + j, rows, stride=TOK_SUB), :] for j in range(TOK_SUB)], axis=1)


def _inproj_kernel(x_ref, g_ref, w_ref, wdt_ref, proj_ref, dt_ref, h_scr):
    @pl.when(pl.program_id(1) == 0)
    def _():
        h = _rms(x_ref[...], g_ref[...]).astype(BF16)
        h_scr[...] = h
        dt_ref[...] = jnp.dot(h, wdt_ref[...], preferred_element_type=F32)

    proj_ref[...] = jnp.dot(h_scr[...], w_ref[...], preferred_element_type=F32).astype(BF16)


def _in_proj(x2d, g, w_main, w_dt):
    n = x2d.shape[0]
    grid = (n // IN_TM, PROJ_MAIN // IN_TN)
    return pl.pallas_call(
        _inproj_kernel,
        grid=grid,
        in_specs=[
            pl.BlockSpec((IN_TM, D_MODEL), lambda i, j: (i, 0)),
            pl.BlockSpec((1, D_MODEL), lambda i, j: (0, 0)),
            pl.BlockSpec((D_MODEL, IN_TN), lambda i, j: (0, j)),
            pl.BlockSpec((D_MODEL, LANES), lambda i, j: (0, 0)),
        ],
        out_specs=[
            pl.BlockSpec((IN_TM, IN_TN), lambda i, j: (i, j)),
            pl.BlockSpec((IN_TM, LANES), lambda i, j: (i, 0)),
        ],
        out_shape=[
            jax.ShapeDtypeStruct((n, PROJ_MAIN), BF16),
            jax.ShapeDtypeStruct((n, LANES), F32),
        ],
        scratch_shapes=[pltpu.VMEM((IN_TM, D_MODEL), BF16)],
        compiler_params=pltpu.CompilerParams(
            dimension_semantics=("arbitrary", "arbitrary"), vmem_limit_bytes=VMEM_LIMIT),
        name="in_proj",
    )(x2d, g, w_main, w_dt)


def _split3(a):
    a1 = a.astype(BF16)
    r1 = a - a1.astype(F32)
    a2 = r1.astype(BF16)
    a3 = (r1 - a2.astype(F32)).astype(BF16)
    return a1, a2, a3


def _dot3(parts, w):
    out = jnp.dot(parts[0], w, preferred_element_type=F32)
    for p in parts[1:]:
        out = out + jnp.dot(p, w, preferred_element_type=F32)
    return out


def _mixer_kernel(u_ref, v_ref, z_ref, xb0_ref, xb1_ref, dt_ref, ga_ref, gb_ref, x_ref,
                  lng_ref, lnb_ref, wsgu_ref, bsgu_ref, convw_ref, convb_ref, dtb_ref, acol_ref,
                  dskip_ref, normw_ref, eexp_ref, shift_ref, triu_ref, wa_ref, wb_ref, wo_ref,
                  x1_ref,
                  tail_scr, xc_scr, state_scr, ya_scr, yb_scr, yap_scr, ybp_scr, *, tps):
    nchunk = MIX_T // SSD_CHUNK
    s = pl.program_id(0)

    @pl.when(s % tps == 0)
    def _():
        tail_scr[...] = jnp.zeros_like(tail_scr)
        state_scr[...] = jnp.zeros_like(state_scr)

    @pl.when(s == 0)
    def _():
        yap_scr[...] = jnp.zeros_like(yap_scr)
        ybp_scr[...] = jnp.zeros_like(ybp_scr)

    for c in range(nchunk):
        r0 = c * SSD_CHUNK
        for hf, xb_ref in enumerate((xb0_ref, xb1_ref)):
            cols = slice(hf * 1024, (hf + 1) * 1024)
            cur = xb_ref[r0:r0 + SSD_CHUNK, :]
            prev = tail_scr[:, cols] if c == 0 else xb_ref[r0 - SSD_CHUNK:r0, :]
            sh = jnp.dot(shift_ref[...], jnp.concatenate([prev, cur], axis=0), preferred_element_type=F32)
            acc = convb_ref[:, cols] + convw_ref[SSM_CONV - 1:SSM_CONV, cols] * cur.astype(F32)
            for j in range(SSM_CONV - 1):
                acc = acc + convw_ref[j:j + 1, cols] * sh[j * SSD_CHUNK:(j + 1) * SSD_CHUNK, :]
            xc_scr[r0:r0 + SSD_CHUNK, cols] = acc * _sigmoid(acc)
    tail_scr[:, 0:1024] = xb0_ref[MIX_T - SSD_CHUNK:MIX_T, :]
    tail_scr[:, 1024:2048] = xb1_ref[MIX_T - SSD_CHUNK:MIX_T, :]

    row_i = lax.broadcasted_iota(I32, (SSD_CHUNK, SSD_CHUNK), 0)
    col_i = lax.broadcasted_iota(I32, (SSD_CHUNK, SSD_CHUNK), 1)
    tril = row_i >= col_i
    lane_hi = col_i >= SSM_HEAD_DIM
    eexp = eexp_ref[...]
    triu = triu_ref[...]

    for c in range(nchunk):
        rows = slice(c * SSD_CHUNK, (c + 1) * SSD_CHUNK)

        ug = _gelu_exact(u_ref[rows, :].astype(F32))
        vg = _gelu_exact(v_ref[rows, :].astype(F32))
        mu = jnp.mean(vg, axis=-1, keepdims=True)
        vc = vg - mu
        var = jnp.mean(vc * vc, axis=-1, keepdims=True)
        vn = (vc * lax.rsqrt(var + LN_EPS) * lng_ref[...] + lnb_ref[...]).astype(BF16)
        for g in range(SGU_HEADS):
            cols = slice(g * SGU_HEAD_DIM, (g + 1) * SGU_HEAD_DIM)
            mixed = jnp.dot(wsgu_ref[g], vn[:, cols], preferred_element_type=F32)
            ya_scr[rows, cols] = (ug[:, cols] * (mixed + bsgu_ref[:, cols])).astype(BF16)

        dt_t = _softplus(dt_ref[rows, :].T[0:SSM_HEADS, :] + dtb_ref[...])
        acs_t = _dot3(_split3(dt_t * acol_ref[...]), triu)
        w_t = jnp.exp(acs_t[:, SSD_CHUNK - 1:SSD_CHUNK] - acs_t) * dt_t
        acs = jnp.concatenate([acs_t, jnp.zeros((LANES - SSM_HEADS, SSD_CHUNK), F32)], axis=0).T
        aend = jnp.broadcast_to(acs[SSD_CHUNK - 1:SSD_CHUNK, :], (8, LANES))
        dec = jnp.exp(_dot3(_split3(aend), eexp)[0:1, :])

        for gi in range(SSM_GROUPS):
            bg = xc_scr[rows, SSM_INNER + gi * SSM_STATE:SSM_INNER + (gi + 1) * SSM_STATE]
            cg = xc_scr[rows, SSM_INNER + (SSM_GROUPS + gi) * SSM_STATE:
                        SSM_INNER + (SSM_GROUPS + gi + 1) * SSM_STATE]
            bg_t = bg.T
            cb = jnp.dot(cg.astype(BF16), bg_t.astype(BF16), preferred_element_type=F32)
            for pr in range(2):
                lb = gi * 2 + pr
                cols = slice(lb * LANES, (lb + 1) * LANES)
                xpair = xc_scr[rows, cols]
                hpair = state_scr[:, cols]
                acc = jnp.zeros((SSD_CHUNK, LANES), F32)
                st = jnp.zeros((SSM_STATE, LANES), F32)
                for hh in range(2):
                    h = lb * 2 + hh
                    lmask = lane_hi if hh == 1 else jnp.logical_not(lane_hi)
                    xm = jnp.where(lmask, xpair, 0.0).astype(BF16)
                    hm = jnp.where(lmask, hpair, 0.0).astype(BF16)
                    colb = acs[:, h:h + 1]
                    rowb = acs_t[h:h + 1, :]
                    decay = jnp.exp(jnp.where(tril, colb - rowb, -jnp.inf))
                    m_h = (cb * decay * dt_t[h:h + 1, :]).astype(BF16)
                    c_h = (cg * jnp.exp(colb)).astype(BF16)
                    acc = acc + jnp.dot(m_h, xm, preferred_element_type=F32)
                    acc = acc + jnp.dot(c_h, hm, preferred_element_type=F32)
                    btw = (bg_t * w_t[h:h + 1, :]).astype(BF16)
                    st = st + jnp.dot(btw, xm, preferred_element_type=F32)
                state_scr[:, cols] = hpair * dec[:, cols] + st
                y = acc + dskip_ref[:, cols] * xpair
                zz = z_ref[rows, cols].astype(F32)
                xc_scr[rows, cols] = y * (zz * _sigmoid(zz))
            gcols = slice(gi * 2 * LANES, (gi + 1) * 2 * LANES)
            yg = xc_scr[rows, gcols]
            ms = jnp.mean(yg * yg, axis=-1, keepdims=True)
            yb_scr[rows, gcols] = (yg * lax.rsqrt(ms + RMS_EPS) * normw_ref[:, gcols]).astype(BF16)

    ma = jnp.dot(yap_scr[...], wa_ref[...], preferred_element_type=F32)
    mb = jnp.dot(ybp_scr[...], wb_ref[...], preferred_element_type=F32)
    merged = (_sigmoid(ga_ref[...].astype(F32)) * ma + _sigmoid(gb_ref[...].astype(F32)) * mb)
    x1_ref[...] = x_ref[...] + jnp.dot(merged.astype(BF16), wo_ref[...], preferred_element_type=F32)
    yap_scr[...] = ya_scr[...]
    ybp_scr[...] = yb_scr[...]


def _mixer(proj, dt_raw, x2d, bsz, seq, consts, wa, wb, wo):
    n = x2d.shape[0]
    tps = seq // MIX_T
    ntiles = bsz * tps
    cur = lambda s: jnp.minimum(s, ntiles - 1)
    prev = lambda s: jnp.maximum(s - 1, 0)

    def pcol(k, tile):
        return pl.BlockSpec((MIX_T, 1024), lambda s, k=k: (tile(s), k))

    def full(a):
        nd = a.ndim
        return pl.BlockSpec(a.shape, lambda s, nd=nd: (0,) * nd)

    in_specs = [pcol(0, cur), pcol(1, cur), pcol(2, cur), pcol(3, cur), pcol(4, cur),
                pl.BlockSpec((MIX_T, LANES), lambda s: (cur(s), 0)),
                pcol(5, prev), pcol(6, prev),
                pl.BlockSpec((MIX_T, D_MODEL), lambda s: (prev(s), 0))]
    in_specs += [full(a) for a in consts] + [full(wa), full(wb), full(wo)]
    return pl.pallas_call(
        functools.partial(_mixer_kernel, tps=tps),
        grid=(ntiles + 1,),
        in_specs=in_specs,
        out_specs=pl.BlockSpec((MIX_T, D_MODEL), lambda s: (prev(s), 0)),
        out_shape=jax.ShapeDtypeStruct((n, D_MODEL), F32),
        scratch_shapes=[
            pltpu.VMEM((SSD_CHUNK, SSM_CONV_DIM), BF16),
            pltpu.VMEM((MIX_T, SSM_CONV_DIM), F32),
            pltpu.VMEM((SSM_STATE, SSM_INNER), F32),
            pltpu.VMEM((MIX_T, SGU_WIDTH), BF16),
            pltpu.VMEM((MIX_T, SSM_INNER), BF16),
            pltpu.VMEM((MIX_T, SGU_WIDTH), BF16),
            pltpu.VMEM((MIX_T, SSM_INNER), BF16),
        ],
        compiler_params=pltpu.CompilerParams(
            dimension_semantics=("arbitrary",), vmem_limit_bytes=VMEM_LIMIT),
        name="mixer",
    )(proj, proj, proj, proj, proj, dt_raw, proj, proj, x2d, *consts, wa, wb, wo)


def _router_kernel(x_ref, g_ref, wr_ref, br_ref, upper_ref, hp_ref, idx_ref, gate_ref, rank_ref, cnt_ref,
                   carry_scr):
    @pl.when(pl.program_id(0) == 0)
    def _():
        carry_scr[...] = jnp.zeros_like(carry_scr)

    h = _rms(x_ref[...], g_ref[...])
    _store_token_tiles(hp_ref, h, ROUTE_T)

    h_hi = h.astype(BF16)
    h_lo = (h - h_hi.astype(F32)).astype(BF16)
    logits = (jnp.dot(h_hi, wr_ref[0], preferred_element_type=F32)
              + jnp.dot(h_hi, wr_ref[1], preferred_element_type=F32)
              + jnp.dot(h_lo, wr_ref[0], preferred_element_type=F32))
    lt = logits.T[0:N_EXPERTS, :] + br_ref[...]
    eidx = lax.broadcasted_iota(I32, (N_EXPERTS, ROUTE_T), 0).astype(F32)
    vals = lt
    sel_any = jnp.zeros((N_EXPERTS, ROUTE_T), F32)
    sels, tops = [], []
    for k in range(TOP_K):
        m = jnp.max(vals, axis=0, keepdims=True)
        first = jnp.min(jnp.where(vals == m, eidx, float(N_EXPERTS)), axis=0, keepdims=True)
        sel = eidx == first
        vals = jnp.where(sel, -jnp.inf, vals)
        sel_f = sel.astype(F32)
        sel_any = sel_any + sel_f
        sels.append(sel_f)
        tops.append(m)
        idx_ref[k:k + 1, :] = first.astype(I32)
    es = [jnp.exp(t - tops[0]) for t in tops]
    denom = es[0] + es[1] + es[2] + es[3]
    for k in range(TOP_K):
        gate_ref[k:k + 1, :] = es[k] / denom

    excl = jnp.dot(sel_any.astype(BF16), upper_ref[...], preferred_element_type=F32) + carry_scr[:, 0:1]
    for k in range(TOP_K):
        rank_ref[k:k + 1, :] = jnp.sum(sels[k] * excl, axis=0, keepdims=True).astype(I32)
    new_carry = carry_scr[...] + jnp.sum(sel_any, axis=1, keepdims=True)
    carry_scr[...] = new_carry
    cnt_ref[...] = new_carry.astype(I32)


def _router(x1, g, wr_pad, br_col, upper):
    n = x1.shape[0]
    return pl.pallas_call(
        _router_kernel,
        grid=(n // ROUTE_T,),
        in_specs=[
            pl.BlockSpec((ROUTE_T, D_MODEL), lambda i: (i, 0)),
            pl.BlockSpec((1, D_MODEL), lambda i: (0, 0)),
            pl.BlockSpec((2, D_MODEL, LANES), lambda i: (0, 0, 0)),
            pl.BlockSpec((N_EXPERTS, 1), lambda i: (0, 0)),
            pl.BlockSpec((ROUTE_T, ROUTE_T), lambda i: (0, 0)),
        ],
        out_specs=[
            pl.BlockSpec((ROUTE_T * TOK_SUB, LANES), lambda i: (i, 0)),
            pl.BlockSpec((TOP_K, ROUTE_T), lambda i: (0, i)),
            pl.BlockSpec((TOP_K, ROUTE_T), lambda i: (0, i)),
            pl.BlockSpec((TOP_K, ROUTE_T), lambda i: (0, i)),
            pl.BlockSpec((N_EXPERTS, LANES), lambda i: (0, 0)),
        ],
        out_shape=[
            jax.ShapeDtypeStruct((n * TOK_SUB, LANES), F32),
            jax.ShapeDtypeStruct((TOP_K, n), I32),
            jax.ShapeDtypeStruct((TOP_K, n), F32),
            jax.ShapeDtypeStruct((TOP_K, n), I32),
            jax.ShapeDtypeStruct((N_EXPERTS, LANES), I32),
        ],
        scratch_shapes=[pltpu.VMEM((N_EXPERTS, LANES), F32)],
        compiler_params=pltpu.CompilerParams(
            dimension_semantics=("arbitrary",), vmem_limit_bytes=VMEM_LIMIT),
        name="router",
    )(x1, g, wr_pad, br_col, upper)


def _tile_rows(idx):
    return pl.ds(pl.multiple_of(idx * TOK_SUB, TOK_SUB), TOK_SUB)


def _dispatch_kernel(dest_ref, pad_ref, hp_ref, xs_hbm, zero_scr, sem_rows):
    i = pl.program_id(0)
    tbl = i * (TOP_K * DISP_T)

    def zero_copy(s):
        return pltpu.make_async_copy(zero_scr.at[0:TOK_SUB, :], xs_hbm.at[_tile_rows(s), :], sem_rows)

    def zero_run(s):
        rows = pl.ds(pl.multiple_of(s * TOK_SUB, TOK_SUB), ZERO_RUN * TOK_SUB)
        return pltpu.make_async_copy(zero_scr, xs_hbm.at[rows, :], sem_rows)

    def zero_pads(act_run, act_one):
        for e in range(N_EXPERTS):
            lo, hi = pad_ref[e], pad_ref[N_EXPERTS + e]
            nrun = (hi - lo) // ZERO_RUN
            lax.fori_loop(0, nrun, lambda r, c: (act_run(zero_run(lo + r * ZERO_RUN)), c)[1], 0)
            lax.fori_loop(lo + nrun * ZERO_RUN, hi, lambda s, c: (act_one(zero_copy(s)), c)[1], 0)

    @pl.when(i == 0)
    def _():
        zero_scr[...] = jnp.zeros_like(zero_scr)
        zero_pads(lambda d: d.start(), lambda d: d.start())
        zero_pads(lambda d: d.wait(), lambda d: d.wait())

    def tile_copy(t, k):
        d = dest_ref[tbl + k * DISP_T + t]
        return pltpu.make_async_copy(hp_ref.at[_tile_rows(t), :], xs_hbm.at[_tile_rows(d), :], sem_rows)

    def issue(tb, carry):
        for r in range(DMA_UNROLL):
            for k in range(TOP_K):
                tile_copy(tb * DMA_UNROLL + r, k).start(priority=(r * TOP_K + k) % 2)
        return carry

    lax.fori_loop(0, DISP_T // DMA_UNROLL, issue, 0)

    def drain(tb, carry):
        for _ in range(DMA_UNROLL * TOP_K):
            zero_copy(0).wait()
        return carry

    lax.fori_loop(0, DISP_T // DMA_UNROLL, drain, 0)


def _dispatch(dest_flat, pad_tbl, hp, cap):
    n = hp.shape[0] // TOK_SUB
    grid_spec = pltpu.PrefetchScalarGridSpec(
        num_scalar_prefetch=2,
        grid=(n // DISP_T,),
        in_specs=[pl.BlockSpec((DISP_T * TOK_SUB, LANES), lambda i, d, p: (i, 0))],
        out_specs=pl.BlockSpec(memory_space=pl.ANY),
        scratch_shapes=[pltpu.VMEM((ZERO_RUN * TOK_SUB, LANES), F32), pltpu.SemaphoreType.DMA],
    )
    return pl.pallas_call(
        _dispatch_kernel,
        grid_spec=grid_spec,
        out_shape=jax.ShapeDtypeStruct((cap * TOK_SUB, LANES), F32),
        compiler_params=pltpu.CompilerParams(dimension_semantics=("arbitrary",)),
        name="dispatch",
    )(dest_flat, pad_tbl, hp)


def _expert_kernel(be_ref, nxt_ref, nu_ref, xs_ref, w1_hbm, b1_ref, w2_hbm, b2_ref, ys_ref,
                   w1f_scr, w2f_scr, w1b_scr, w2b_scr, sems):
    b = pl.program_id(0)
    used = b < nu_ref[0]
    new_expert = jnp.logical_or(b == 0, be_ref[b] != be_ref[jnp.maximum(b - 1, 0)])

    def fetch(e):
        return (pltpu.make_async_copy(w1_hbm.at[e], w1f_scr, sems.at[0]),
                pltpu.make_async_copy(w2_hbm.at[e], w2f_scr, sems.at[1]))

    @pl.when(jnp.logical_and(used, b == 0))
    def _():
        for c in fetch(be_ref[0]):
            c.start()

    @pl.when(jnp.logical_and(used, new_expert))
    def _():
        for c in fetch(be_ref[b]):
            c.wait()

        def cast(rb, carry):
            rows = pl.ds(pl.multiple_of(rb * CAST_ROWS, CAST_ROWS), CAST_ROWS)
            w1b_scr[rows, :] = w1f_scr[rows, :].astype(BF16)
            w2b_scr[rows, :] = w2f_scr[rows, :].astype(BF16)
            return carry

        lax.fori_loop(0, D_MODEL // CAST_ROWS, cast, 0)

        @pl.when(nxt_ref[b] >= 0)
        def _():
            for c in fetch(nxt_ref[b]):
                c.start()

    @pl.when(used)
    def _():
        for sub in range(MOE_BLOCK // MOE_SUB):
            x = _load_token_tiles(xs_ref, MOE_SUB, sub * MOE_SUB).astype(BF16)
            hid = jnp.dot(x, w1b_scr[...], preferred_element_type=F32) + b1_ref[0]
            glu = jnp.minimum(hid[:, :D_FF_EXPERT], SWIGLU_LIMIT)
            lin = jnp.clip(hid[:, D_FF_EXPERT:], -SWIGLU_LIMIT, SWIGLU_LIMIT)
            act = glu * _sigmoid(SWIGLU_ALPHA * glu) * (lin + 1.0)
            y = jnp.dot(act.astype(BF16), w2b_scr[...], preferred_element_type=F32) + b2_ref[0]
            _store_token_tiles(ys_ref, y, MOE_SUB, sub * MOE_SUB)


def _experts(blk_expert, blk_next, n_used, xs, w1, b1, w2, b2):
    cap = xs.shape[0] // TOK_SUB
    n_blocks = cap // MOE_BLOCK
    grid_spec = pltpu.PrefetchScalarGridSpec(
        num_scalar_prefetch=3,
        grid=(n_blocks,),
        in_specs=[
            pl.BlockSpec((MOE_BLOCK * TOK_SUB, LANES), lambda b, be, nx, nu: (b, 0)),
            pl.BlockSpec(memory_space=pl.ANY),
            pl.BlockSpec((1, 1, 2 * D_FF_EXPERT), lambda b, be, nx, nu: (be[b], 0, 0)),
            pl.BlockSpec(memory_space=pl.ANY),
            pl.BlockSpec((1, 1, D_MODEL), lambda b, be, nx, nu: (be[b], 0, 0)),
        ],
        out_specs=pl.BlockSpec((MOE_BLOCK * TOK_SUB, LANES), lambda b, be, nx, nu: (b, 0)),
        scratch_shapes=[pltpu.VMEM((D_MODEL, 2 * D_FF_EXPERT), F32),
                        pltpu.VMEM((D_FF_EXPERT, D_MODEL), F32),
                        pltpu.VMEM((D_MODEL, 2 * D_FF_EXPERT), BF16),
                        pltpu.VMEM((D_FF_EXPERT, D_MODEL), BF16),
                        pltpu.SemaphoreType.DMA((2,))],
    )
    assert D_MODEL == D_FF_EXPERT
    return pl.pallas_call(
        _expert_kernel,
        grid_spec=grid_spec,
        out_shape=jax.ShapeDtypeStruct((cap * TOK_SUB, LANES), F32),
        compiler_params=pltpu.CompilerParams(
            dimension_semantics=("arbitrary",), vmem_limit_bytes=VMEM_LIMIT),
        name="experts",
    )(blk_expert, blk_next, n_used, xs, w1, b1, w2, b2)


def _combine_kernel(dest_ref, ys_hbm, x1_ref, gate_ref, p_ref, pg_ref, wg_ref, wp_ref, fg_ref,
                    out_ref, ybuf, sems, *, final):
    i = pl.program_id(0)
    nsteps = pl.num_programs(0)
    buf_tokens = TOP_K * COMB_T

    def issue_all(step, slot):
        tbl = step * buf_tokens

        def issue(tb, carry):
            for r in range(DMA_UNROLL):
                for k in range(TOP_K):
                    row = k * COMB_T + tb * DMA_UNROLL + r
                    pltpu.make_async_copy(
                        ys_hbm.at[_tile_rows(dest_ref[tbl + row]), :],
                        ybuf.at[_tile_rows(slot * buf_tokens + row), :],
                        sems.at[slot]).start(priority=(r * TOP_K + k) % 2)
            return carry

        lax.fori_loop(0, COMB_T // DMA_UNROLL, issue, 0)

    def drain_all(slot):
        def drain(tb, carry):
            for _ in range(DMA_UNROLL * TOP_K):
                pltpu.make_async_copy(ys_hbm.at[_tile_rows(0), :],
                                      ybuf.at[_tile_rows(slot * buf_tokens), :], sems.at[slot]).wait()
            return carry

        lax.fori_loop(0, COMB_T // DMA_UNROLL, drain, 0)

    @pl.when(i == 0)
    def _():
        issue_all(0, 0)

    for slot in range(2):
        @pl.when(jnp.logical_and(i % 2 == slot, i + 1 < nsteps))
        def _():
            issue_all(i + 1, 1 - slot)

    for slot in range(2):
        @pl.when(i % 2 == slot)
        def _():
            drain_all(slot)

            moe = gate_ref[:, 0:1] * _load_token_tiles(ybuf, COMB_T, slot * buf_tokens)
            for k in range(1, TOP_K):
                moe = moe + gate_ref[:, k:k + 1] * _load_token_tiles(
                    ybuf, COMB_T, slot * buf_tokens + k * COMB_T)
            x2 = x1_ref[...] + moe
            hp = _rms(x2, pg_ref[...]).astype(BF16)
            gate = _sigmoid(jnp.dot(hp, wg_ref[...], preferred_element_type=F32))
            emb = jnp.dot(p_ref[...].astype(BF16), wp_ref[...], preferred_element_type=F32)
            x3 = x2 + gate * emb
            if final:
                x3 = _rms(x3, fg_ref[...])
            out_ref[...] = x3


def _combine(dest_flat, ys, x1, gates_t, p2d, pg, wg, wp, fg, final):
    n = x1.shape[0]
    grid_spec = pltpu.PrefetchScalarGridSpec(
        num_scalar_prefetch=1,
        grid=(n // COMB_T,),
        in_specs=[
            pl.BlockSpec(memory_space=pl.ANY),
            pl.BlockSpec((COMB_T, D_MODEL), lambda i, d: (i, 0)),
            pl.BlockSpec((COMB_T, 8), lambda i, d: (i, 0)),
            pl.BlockSpec((COMB_T, PLE_DIM), lambda i, d: (i, 0)),
            pl.BlockSpec((1, D_MODEL), lambda i, d: (0, 0)),
            pl.BlockSpec((D_MODEL, D_MODEL), lambda i, d: (0, 0)),
            pl.BlockSpec((PLE_DIM, D_MODEL), lambda i, d: (0, 0)),
            pl.BlockSpec((1, D_MODEL), lambda i, d: (0, 0)),
        ],
        out_specs=pl.BlockSpec((COMB_T, D_MODEL), lambda i, d: (i, 0)),
        scratch_shapes=[pltpu.VMEM((2 * TOP_K * COMB_T * TOK_SUB, LANES), F32),
                        pltpu.SemaphoreType.DMA((2,))],
    )
    return pl.pallas_call(
        functools.partial(_combine_kernel, final=final),
        grid_spec=grid_spec,
        out_shape=jax.ShapeDtypeStruct((n, D_MODEL), F32),
        compiler_params=pltpu.CompilerParams(
            dimension_semantics=("arbitrary",), vmem_limit_bytes=VMEM_LIMIT),
        name="combine",
    )(dest_flat, ys, x1, gates_t, p2d, pg, wg, wp, fg)


def _layer(x2d, p2d, bsz, seq, mix_norm, w_in, sgu_ln_g, sgu_ln_b, sgu_w, sgu_b, conv_w, conv_b,
           dt_bias, a_log, d_skip, ssm_norm, w_branch_a, w_branch_b, w_out, ffn_norm, w_router,
           b_router, w1, b1, w2, b2, ple_norm, w_ple_gate, w_ple_proj, final_norm, final):
    n = x2d.shape[0]
    row = lambda a: a.reshape(1, -1).astype(F32)

    w_main = jnp.concatenate([w_in[:, :OFF_DT], w_in[:, OFF_GA:]], axis=1).astype(BF16)
    w_dt = jnp.pad(w_in[:, OFF_DT:OFF_GA], ((0, 0), (0, LANES - SSM_HEADS))).astype(BF16)
    pos = jnp.arange(SGU_LEN)
    allowed = (pos[None, :] // CHUNK) <= (pos[:, None] // CHUNK)
    wsgu = jnp.where(allowed[None], sgu_w, 0.0).astype(BF16)
    bsgu = jnp.repeat(sgu_b.T, SGU_HEAD_DIM, axis=1).astype(F32)
    dtb = dt_bias.reshape(-1, 1).astype(F32)
    acol = -jnp.exp(a_log.astype(F32)).reshape(-1, 1)
    dskip = jnp.repeat(d_skip.astype(F32), SSM_HEAD_DIM).reshape(1, -1)
    head_of_col = jnp.arange(SSM_INNER) // SSM_HEAD_DIM
    eexp = (jnp.arange(LANES)[:, None] == head_of_col[None, :]).astype(BF16)
    triu = (jnp.arange(SSD_CHUNK)[:, None] <= jnp.arange(SSD_CHUNK)[None, :]).astype(BF16)
    t_out = jnp.arange((SSM_CONV - 1) * SSD_CHUNK)
    src = SSD_CHUNK + t_out % SSD_CHUNK - (SSM_CONV - 1) + t_out // SSD_CHUNK
    shift = (src[:, None] == jnp.arange(2 * SSD_CHUNK)[None, :]).astype(BF16)
    consts = [row(sgu_ln_g), row(sgu_ln_b), wsgu, bsgu, conv_w.astype(F32), row(conv_b), dtb, acol,
              dskip, row(ssm_norm), eexp, shift, triu]

    proj, dt_raw = _in_proj(x2d, row(mix_norm), w_main, w_dt)
    x1 = _mixer(proj, dt_raw, x2d, bsz, seq, consts, w_branch_a.astype(BF16),
                w_branch_b.astype(BF16), w_out.astype(BF16))

    wr_f = jnp.pad(w_router.astype(F32), ((0, 0), (0, LANES - N_EXPERTS)))
    wr_hi = wr_f.astype(BF16)
    wr_pad = jnp.stack([wr_hi, (wr_f - wr_hi.astype(F32)).astype(BF16)])
    upper = (jnp.arange(ROUTE_T)[:, None] < jnp.arange(ROUTE_T)[None, :]).astype(BF16)
    hp, idx, gates, rank, cnt = _router(x1, row(ffn_norm), wr_pad, b_router.reshape(-1, 1).astype(F32),
                                        upper)

    counts = cnt[:, 0]
    padded = (counts + MOE_BLOCK - 1) // MOE_BLOCK * MOE_BLOCK
    pend = jnp.cumsum(padded)
    pstart = pend - padded
    nk = n * TOP_K
    cap = (nk + MOE_BLOCK - 1) // MOE_BLOCK * MOE_BLOCK + N_EXPERTS * MOE_BLOCK
    n_blocks = cap // MOE_BLOCK
    ex = jnp.arange(N_EXPERTS, dtype=I32)
    dest = rank + jnp.sum(jnp.where(idx[..., None] == ex, pstart.astype(I32), 0), axis=-1)
    blk_start = jnp.arange(n_blocks, dtype=I32) * MOE_BLOCK
    blk_expert = jnp.minimum(jnp.sum((pend[None, :] <= blk_start[:, None]).astype(I32), axis=1),
                             N_EXPERTS - 1).astype(I32)
    n_used = (pend[-1] // MOE_BLOCK).astype(I32).reshape(1)
    live = jnp.where(counts > 0, ex, N_EXPERTS)
    later = jnp.concatenate([lax.cummin(live[::-1])[::-1][1:], jnp.full((1,), N_EXPERTS, I32)])
    blk_next = jnp.where(later < N_EXPERTS, later, -1).astype(I32)[blk_expert]

    dest_flat = dest.reshape(TOP_K, n // DISP_T, DISP_T).transpose(1, 0, 2).reshape(-1)
    pad_tbl = jnp.concatenate([pstart + counts, pend]).astype(I32)

    xs = _dispatch(dest_flat, pad_tbl, hp, cap)
    ys = _experts(blk_expert, blk_next, n_used, xs, w1.astype(F32), b1.reshape(N_EXPERTS, 1, -1).astype(F32),
                  w2.astype(F32), b2.reshape(N_EXPERTS, 1, -1).astype(F32))
    gates_t = jnp.pad(gates.T, ((0, 0), (0, 8 - TOP_K)))
    return _combine(dest_flat, ys, x1, gates_t, p2d, row(ple_norm), w_ple_gate.astype(BF16),
                    w_ple_proj.astype(BF16), row(final_norm), final)


def kernel(x, p, mix_norm, w_in, sgu_ln_g, sgu_ln_b, sgu_w, sgu_b, conv_w, conv_b, dt_bias, a_log,
           d_skip, ssm_norm, w_branch_a, w_branch_b, w_out, ffn_norm, w_router, b_router, w1, b1,
           w2, b2, ple_norm, w_ple_gate, w_ple_proj, final_norm):
    bsz, seq, d = x.shape
    depth = w_in.shape[0]
    assert d == D_MODEL and seq % max(MIX_T, IN_TM) == 0
    x2d = x.reshape(bsz * seq, d)
    for i in range(depth):
        x2d = _layer(x2d, p[i].reshape(bsz * seq, PLE_DIM), bsz, seq, mix_norm[i], w_in[i],
                     sgu_ln_g[i], sgu_ln_b[i], sgu_w[i], sgu_b[i], conv_w[i], conv_b[i], dt_bias[i],
                     a_log[i], d_skip[i], ssm_norm[i], w_branch_a[i], w_branch_b[i], w_out[i],
                     ffn_norm[i], w_router[i], b_router[i], w1[i], b1[i], w2[i], b2[i], ple_norm[i],
                     w_ple_gate[i], w_ple_proj[i], final_norm, final=(i == depth - 1))
    return x2d.reshape(bsz, seq, d)
```

```python
import functools
import math

import jax
import jax.numpy as jnp
from jax import lax
from jax.experimental import pallas as pl
from jax.experimental.pallas import tpu as pltpu

F32 = jnp.float32
BF16 = jnp.bfloat16
I32 = jnp.int32
U32 = jnp.uint32

D_MODEL = 1024
CHUNK = 64
PLE_DIM = 256
RMS_EPS = 1e-6
LN_EPS = 1e-5

SGU_HEADS = 8
SGU_HEAD_DIM = 128
SGU_WIDTH = SGU_HEADS * SGU_HEAD_DIM
SGU_LEN = 128

SSM_HEADS = 16
SSM_HEAD_DIM = 64
SSM_INNER = SSM_HEADS * SSM_HEAD_DIM
SSM_GROUPS = 4
SSM_STATE = 128
SSM_CONV = 4
SSD_CHUNK = 128
SSM_CONV_DIM = SSM_INNER + 2 * SSM_GROUPS * SSM_STATE

N_EXPERTS = 32
TOP_K = 4
D_FF_EXPERT = 1024
SWIGLU_LIMIT = 7.0
SWIGLU_ALPHA = 1.702
MOE_BLOCK = 512
MOE_SUB = 256

OFF_U = 0
OFF_V = OFF_U + SGU_WIDTH
OFF_Z = OFF_V + SGU_WIDTH
OFF_XBC = OFF_Z + SSM_INNER
OFF_DT = OFF_XBC + SSM_CONV_DIM
OFF_GA = OFF_DT + SSM_HEADS
OFF_GB = OFF_GA + D_MODEL
IN_PROJ_DIM = OFF_GB + D_MODEL

LANES = 128
PROJ_MAIN = IN_PROJ_DIM - SSM_HEADS
TOK_SUB = D_MODEL // LANES

VMEM_LIMIT = 56 * 1024 * 1024

IN_TM = 512
IN_TN = 1792
MIX_T = 512
ROUTE_T = 512
DISP_T = 512
COMB_T = DISP_T
DMA_UNROLL = 8
ZERO_RUN = 64
CAST_ROWS = 128


def _sigmoid(x):
    return 1.0 / (1.0 + jnp.exp(-x))


def _gelu_exact(x):
    return 0.5 * x * (1.0 + lax.erf(x * (1.0 / math.sqrt(2.0))))


def _softplus(x):
    return jnp.maximum(x, 0.0) + jnp.log1p(jnp.exp(-jnp.abs(x)))


def _rms(x, g):
    ms = jnp.mean(x * x, axis=-1, keepdims=True)
    return x * lax.rsqrt(ms + RMS_EPS) * g


def _store_token_tiles(ref, val, rows, start=0):
    for j in range(TOK_SUB):
        ref[pl.ds(start * TOK_SUB + j, rows, stride=TOK_SUB), :] = val[:, j * LANES:(j + 1) * LANES]


def _load_token_tiles(ref, rows, start=0):
    return jnp.concatenate(
        [ref[pl.ds(start * TOK_SUB + j, rows, stride=TOK_SUB), :] for j in range(TOK_SUB)], axis=1)


def _inproj_kernel(x_ref, g_ref, w_ref, wdt_ref, proj_ref, dt_ref, h_scr):
    @pl.when(pl.program_id(0) == 0)
    def _():
        h_scr[...] = jnp.zeros_like(h_scr)

    h = h_scr[...]
    dt_ref[...] = jnp.dot(h, wdt_ref[...], preferred_element_type=F32)
    for c in range(PROJ_MAIN // IN_TN):
        cols = slice(c * IN_TN, (c + 1) * IN_TN)
        proj_ref[:, cols] = jnp.dot(h, w_ref[:, cols], preferred_element_type=F32).astype(BF16)
    h_scr[...] = _rms(x_ref[...], g_ref[...]).astype(BF16)


def _in_proj(x2d, g, w_main, w_dt):
    n = x2d.shape[0]
    ntiles = n // IN_TM
    return pl.pallas_call(
        _inproj_kernel,
        grid=(ntiles + 1,),
        in_specs=[
            pl.BlockSpec((IN_TM, D_MODEL), lambda s: (jnp.minimum(s, ntiles - 1), 0)),
            pl.BlockSpec((1, D_MODEL), lambda s: (0, 0)),
            pl.BlockSpec((D_MODEL, PROJ_MAIN), lambda s: (0, 0)),
            pl.BlockSpec((D_MODEL, LANES), lambda s: (0, 0)),
        ],
        out_specs=[
            pl.BlockSpec((IN_TM, PROJ_MAIN), lambda s: (jnp.maximum(s - 1, 0), 0)),
            pl.BlockSpec((IN_TM, LANES), lambda s: (jnp.maximum(s - 1, 0), 0)),
        ],
        out_shape=[
            jax.ShapeDtypeStruct((n, PROJ_MAIN), BF16),
            jax.ShapeDtypeStruct((n, LANES), F32),
        ],
        scratch_shapes=[pltpu.VMEM((IN_TM, D_MODEL), BF16)],
        compiler_params=pltpu.CompilerParams(
            dimension_semantics=("arbitrary",), vmem_limit_bytes=VMEM_LIMIT),
        name="in_proj",
    )(x2d, g, w_main, w_dt)


def _split3(a):
    a1 = a.astype(BF16)
    r1 = a - a1.astype(F32)
    a2 = r1.astype(BF16)
    a3 = (r1 - a2.astype(F32)).astype(BF16)
    return a1, a2, a3


def _dot3(parts, w):
    out = jnp.dot(parts[0], w, preferred_element_type=F32)
    for p in parts[1:]:
        out = out + jnp.dot(p, w, preferred_element_type=F32)
    return out


def _mixer_kernel(u_ref, v_ref, z_ref, xb0_ref, xb1_ref, dt_ref, ga_ref, gb_ref, x_ref,
                  lng_ref, lnb_ref, wsgu_ref, bsgu_ref, convw_ref, convb_ref, dtb_ref, acol_ref,
                  dskip_ref, normw_ref, eexp_ref, shift_ref, triu_ref, wa_ref, wb_ref, wo_ref,
                  x1_ref,
                  tail_scr, xc_scr, state_scr, ya_scr, yb_scr, yap_scr, ybp_scr, *, tps):
    nchunk = MIX_T // SSD_CHUNK
    s = pl.program_id(0)

    @pl.when(s % tps == 0)
    def _():
        tail_scr[...] = jnp.zeros_like(tail_scr)
        state_scr[...] = jnp.zeros_like(state_scr)

    @pl.when(s == 0)
    def _():
        yap_scr[...] = jnp.zeros_like(yap_scr)
        ybp_scr[...] = jnp.zeros_like(ybp_scr)

    for c in range(nchunk):
        r0 = c * SSD_CHUNK
        for hf, xb_ref in enumerate((xb0_ref, xb1_ref)):
            cols = slice(hf * 1024, (hf + 1) * 1024)
            cur = xb_ref[r0:r0 + SSD_CHUNK, :]
            prev = tail_scr[:, cols] if c == 0 else xb_ref[r0 - SSD_CHUNK:r0, :]
            sh = jnp.dot(shift_ref[...], jnp.concatenate([prev, cur], axis=0), preferred_element_type=F32)
            acc = convb_ref[:, cols] + convw_ref[SSM_CONV - 1:SSM_CONV, cols] * cur.astype(F32)
            for j in range(SSM_CONV - 1):
                acc = acc + convw_ref[j:j + 1, cols] * sh[j * SSD_CHUNK:(j + 1) * SSD_CHUNK, :]
            xc_scr[r0:r0 + SSD_CHUNK, cols] = acc * _sigmoid(acc)
    tail_scr[:, 0:1024] = xb0_ref[MIX_T - SSD_CHUNK:MIX_T, :]
    tail_scr[:, 1024:2048] = xb1_ref[MIX_T - SSD_CHUNK:MIX_T, :]

    row_i = lax.broadcasted_iota(I32, (SSD_CHUNK, SSD_CHUNK), 0)
    col_i = lax.broadcasted_iota(I32, (SSD_CHUNK, SSD_CHUNK), 1)
    tril = row_i >= col_i
    lane_hi = col_i >= SSM_HEAD_DIM
    eexp = eexp_ref[...]
    triu = triu_ref[...]

    for c in range(nchunk):
        rows = slice(c * SSD_CHUNK, (c + 1) * SSD_CHUNK)

        ug = _gelu_exact(u_ref[rows, :].astype(F32))
        vg = _gelu_exact(v_ref[rows, :].astype(F32))
        mu = jnp.mean(vg, axis=-1, keepdims=True)
        vc = vg - mu
        var = jnp.mean(vc * vc, axis=-1, keepdims=True)
        vn = (vc * lax.rsqrt(var + LN_EPS) * lng_ref[...] + lnb_ref[...]).astype(BF16)
        for g in range(SGU_HEADS):
            cols = slice(g * SGU_HEAD_DIM, (g + 1) * SGU_HEAD_DIM)
            mixed = jnp.dot(wsgu_ref[g], vn[:, cols], preferred_element_type=F32)
            ya_scr[rows, cols] = (ug[:, cols] * (mixed + bsgu_ref[:, cols])).astype(BF16)

        dt_t = _softplus(dt_ref[rows, :].T[0:SSM_HEADS, :] + dtb_ref[...])
        acs_t = _dot3(_split3(dt_t * acol_ref[...]), triu)
        w_t = jnp.exp(acs_t[:, SSD_CHUNK - 1:SSD_CHUNK] - acs_t) * dt_t
        acs = jnp.concatenate([acs_t, jnp.zeros((LANES - SSM_HEADS, SSD_CHUNK), F32)], axis=0).T
        aend = jnp.broadcast_to(acs[SSD_CHUNK - 1:SSD_CHUNK, :], (8, LANES))
        dec = jnp.exp(_dot3(_split3(aend), eexp)[0:1, :])

        for gi in range(SSM_GROUPS):
            bg = xc_scr[rows, SSM_INNER + gi * SSM_STATE:SSM_INNER + (gi + 1) * SSM_STATE]
            cg = xc_scr[rows, SSM_INNER + (SSM_GROUPS + gi) * SSM_STATE:
                        SSM_INNER + (SSM_GROUPS + gi + 1) * SSM_STATE]
            bg_t = bg.T
            cb = jnp.dot(cg.astype(BF16), bg_t.astype(BF16), preferred_element_type=F32)
            for pr in range(2):
                lb = gi * 2 + pr
                cols = slice(lb * LANES, (lb + 1) * LANES)
                xpair = xc_scr[rows, cols]
                hpair = state_scr[:, cols]
                acc = jnp.zeros((SSD_CHUNK, LANES), F32)
                st = jnp.zeros((SSM_STATE, LANES), F32)
                for hh in range(2):
                    h = lb * 2 + hh
                    lmask = lane_hi if hh == 1 else jnp.logical_not(lane_hi)
                    xm = jnp.where(lmask, xpair, 0.0).astype(BF16)
                    hm = jnp.where(lmask, hpair, 0.0).astype(BF16)
                    colb = acs[:, h:h + 1]
                    rowb = acs_t[h:h + 1, :]
                    decay = jnp.exp(jnp.where(tril, colb - rowb, -jnp.inf))
                    m_h = (cb * decay * dt_t[h:h + 1, :]).astype(BF16)
                    c_h = (cg * jnp.exp(colb)).astype(BF16)
                    acc = acc + jnp.dot(m_h, xm, preferred_element_type=F32)
                    acc = acc + jnp.dot(c_h, hm, preferred_element_type=F32)
                    btw = (bg_t * w_t[h:h + 1, :]).astype(BF16)
                    st = st + jnp.dot(btw, xm, preferred_element_type=F32)
                state_scr[:, cols] = hpair * dec[:, cols] + st
                y = acc + dskip_ref[:, cols] * xpair
                zz = z_ref[rows, cols].astype(F32)
                xc_scr[rows, cols] = y * (zz * _sigmoid(zz))
            gcols = slice(gi * 2 * LANES, (gi + 1) * 2 * LANES)
            yg = xc_scr[rows, gcols]
            ms = jnp.mean(yg * yg, axis=-1, keepdims=True)
            yb_scr[rows, gcols] = (yg * lax.rsqrt(ms + RMS_EPS) * normw_ref[:, gcols]).astype(BF16)

    ma = jnp.dot(yap_scr[...], wa_ref[...], preferred_element_type=F32)
    mb = jnp.dot(ybp_scr[...], wb_ref[...], preferred_element_type=F32)
    merged = (_sigmoid(ga_ref[...].astype(F32)) * ma + _sigmoid(gb_ref[...].astype(F32)) * mb)
    x1_ref[...] = x_ref[...] + jnp.dot(merged.astype(BF16), wo_ref[...], preferred_element_type=F32)
    yap_scr[...] = ya_scr[...]
    ybp_scr[...] = yb_scr[...]


def _mixer(proj, dt_raw, x2d, bsz, seq, consts, wa, wb, wo):
    n = x2d.shape[0]
    tps = seq // MIX_T
    ntiles = bsz * tps
    cur = lambda s: jnp.minimum(s, ntiles - 1)
    prev = lambda s: jnp.maximum(s - 1, 0)

    def pcol(k, tile):
        return pl.BlockSpec((MIX_T, 1024), lambda s, k=k: (tile(s), k))

    def full(a):
        nd = a.ndim
        return pl.BlockSpec(a.shape, lambda s, nd=nd: (0,) * nd)

    in_specs = [pcol(0, cur), pcol(1, cur), pcol(2, cur), pcol(3, cur), pcol(4, cur),
                pl.BlockSpec((MIX_T, LANES), lambda s: (cur(s), 0)),
                pcol(5, prev), pcol(6, prev),
                pl.BlockSpec((MIX_T, D_MODEL), lambda s: (prev(s), 0))]
    in_specs += [full(a) for a in consts] + [full(wa), full(wb), full(wo)]
    return pl.pallas_call(
        functools.partial(_mixer_kernel, tps=tps),
        grid=(ntiles + 1,),
        in_specs=in_specs,
        out_specs=pl.BlockSpec((MIX_T, D_MODEL), lambda s: (prev(s), 0)),
        out_shape=jax.ShapeDtypeStruct((n, D_MODEL), F32),
        scratch_shapes=[
            pltpu.VMEM((SSD_CHUNK, SSM_CONV_DIM), BF16),
            pltpu.VMEM((MIX_T, SSM_CONV_DIM), F32),
            pltpu.VMEM((SSM_STATE, SSM_INNER), F32),
            pltpu.VMEM((MIX_T, SGU_WIDTH), BF16),
            pltpu.VMEM((MIX_T, SSM_INNER), BF16),
            pltpu.VMEM((MIX_T, SGU_WIDTH), BF16),
            pltpu.VMEM((MIX_T, SSM_INNER), BF16),
        ],
        compiler_params=pltpu.CompilerParams(
            dimension_semantics=("arbitrary",), vmem_limit_bytes=VMEM_LIMIT),
        name="mixer",
    )(proj, proj, proj, proj, proj, dt_raw, proj, proj, x2d, *consts, wa, wb, wo)


def _router_kernel(x_ref, g_ref, wr_ref, br_ref, upper_ref, hp_ref, idx_ref, gate_ref, rank_ref, cnt_ref,
                   carry_scr):
    @pl.when(pl.program_id(0) == 0)
    def _():
        carry_scr[...] = jnp.zeros_like(carry_scr)

    h = _rms(x_ref[...], g_ref[...])
    _store_token_tiles(hp_ref, h, ROUTE_T)

    h_hi = h.astype(BF16)
    h_lo = (h - h_hi.astype(F32)).astype(BF16)
    logits = (jnp.dot(h_hi, wr_ref[0], preferred_element_type=F32)
              + jnp.dot(h_hi, wr_ref[1], preferred_element_type=F32)
              + jnp.dot(h_lo, wr_ref[0], preferred_element_type=F32))
    lt = logits.T[0:N_EXPERTS, :] + br_ref[...]
    eidx = lax.broadcasted_iota(I32, (N_EXPERTS, ROUTE_T), 0).astype(F32)
    vals = lt
    sel_any = jnp.zeros((N_EXPERTS, ROUTE_T), F32)
    sels, tops = [], []
    for k in range(TOP_K):
        m = jnp.max(vals, axis=0, keepdims=True)
        first = jnp.min(jnp.where(vals == m, eidx, float(N_EXPERTS)), axis=0, keepdims=True)
        sel = eidx == first
        vals = jnp.where(sel, -jnp.inf, vals)
        sel_f = sel.astype(F32)
        sel_any = sel_any + sel_f
        sels.append(sel_f)
        tops.append(m)
        idx_ref[k:k + 1, :] = first.astype(I32)
    es = [jnp.exp(t - tops[0]) for t in tops]
    denom = es[0] + es[1] + es[2] + es[3]
    for k in range(TOP_K):
        gate_ref[k:k + 1, :] = es[k] / denom

    excl = jnp.dot(sel_any.astype(BF16), upper_ref[...], preferred_element_type=F32) + carry_scr[:, 0:1]
    for k in range(TOP_K):
        rank_ref[k:k + 1, :] = jnp.sum(sels[k] * excl, axis=0, keepdims=True).astype(I32)
    new_carry = carry_scr[...] + jnp.sum(sel_any, axis=1, keepdims=True)
    carry_scr[...] = new_carry
    cnt_ref[...] = new_carry.astype(I32)


def _router(x1, g, wr_pad, br_col, upper):
    n = x1.shape[0]
    return pl.pallas_call(
        _router_kernel,
        grid=(n // ROUTE_T,),
        in_specs=[
            pl.BlockSpec((ROUTE_T, D_MODEL), lambda i: (i, 0)),
            pl.BlockSpec((1, D_MODEL), lambda i: (0, 0)),
            pl.BlockSpec((2, D_MODEL, LANES), lambda i: (0, 0, 0)),
            pl.BlockSpec((N_EXPERTS, 1), lambda i: (0, 0)),
            pl.BlockSpec((ROUTE_T, ROUTE_T), lambda i: (0, 0)),
        ],
        out_specs=[
            pl.BlockSpec((ROUTE_T * TOK_SUB, LANES), lambda i: (i, 0)),
            pl.BlockSpec((TOP_K, ROUTE_T), lambda i: (0, i)),
            pl.BlockSpec((TOP_K, ROUTE_T), lambda i: (0, i)),
            pl.BlockSpec((TOP_K, ROUTE_T), lambda i: (0, i)),
            pl.BlockSpec((N_EXPERTS, LANES), lambda i: (0, 0)),
        ],
        out_shape=[
            jax.ShapeDtypeStruct((n * TOK_SUB, LANES), F32),
            jax.ShapeDtypeStruct((TOP_K, n), I32),
            jax.ShapeDtypeStruct((TOP_K, n), F32),
            jax.ShapeDtypeStruct((TOP_K, n), I32),
            jax.ShapeDtypeStruct((N_EXPERTS, LANES), I32),
        ],
        scratch_shapes=[pltpu.VMEM((N_EXPERTS, LANES), F32)],
        compiler_params=pltpu.CompilerParams(
            dimension_semantics=("arbitrary",), vmem_limit_bytes=VMEM_LIMIT),
        name="router",
    )(x1, g, wr_pad, br_col, upper)


def _tile_rows(idx):
    return pl.ds(pl.multiple_of(idx * TOK_SUB, TOK_SUB), TOK_SUB)


def _dispatch_kernel(dest_ref, pad_ref, hp_ref, xs_hbm, zero_scr, sem_rows):
    i = pl.program_id(0)
    tbl = i * (TOP_K * DISP_T)

    def zero_copy(s):
        return pltpu.make_async_copy(zero_scr.at[0:TOK_SUB, :], xs_hbm.at[_tile_rows(s), :], sem_rows)

    def zero_run(s):
        rows = pl.ds(pl.multiple_of(s * TOK_SUB, TOK_SUB), ZERO_RUN * TOK_SUB)
        return pltpu.make_async_copy(zero_scr, xs_hbm.at[rows, :], sem_rows)

    def zero_pads(act_run, act_one):
        for e in range(N_EXPERTS):
            lo, hi = pad_ref[e], pad_ref[N_EXPERTS + e]
            nrun = (hi - lo) // ZERO_RUN
            lax.fori_loop(0, nrun, lambda r, c: (act_run(zero_run(lo + r * ZERO_RUN)), c)[1], 0)
            lax.fori_loop(lo + nrun * ZERO_RUN, hi, lambda s, c: (act_one(zero_copy(s)), c)[1], 0)

    @pl.when(i == 0)
    def _():
        zero_scr[...] = jnp.zeros_like(zero_scr)
        zero_pads(lambda d: d.start(), lambda d: d.start())
        zero_pads(lambda d: d.wait(), lambda d: d.wait())

    def tile_copy(t, k):
        d = dest_ref[tbl + k * DISP_T + t]
        return pltpu.make_async_copy(hp_ref.at[_tile_rows(t), :], xs_hbm.at[_tile_rows(d), :], sem_rows)

    def issue(tb, carry):
        for r in range(DMA_UNROLL):
            for k in range(TOP_K):
                tile_copy(tb * DMA_UNROLL + r, k).start(priority=(r * TOP_K + k) % 2)
        return carry

    lax.fori_loop(0, DISP_T // DMA_UNROLL, issue, 0)

    def drain(tb, carry):
        for _ in range(DMA_UNROLL * TOP_K):
            zero_copy(0).wait()
        return carry

    lax.fori_loop(0, DISP_T // DMA_UNROLL, drain, 0)


def _dispatch(dest_flat, pad_tbl, hp, cap):
    n = hp.shape[0] // TOK_SUB
    grid_spec = pltpu.PrefetchScalarGridSpec(
        num_scalar_prefetch=2,
        grid=(n // DISP_T,),
        in_specs=[pl.BlockSpec((DISP_T * TOK_SUB, LANES), lambda i, d, p: (i, 0))],
        out_specs=pl.BlockSpec(memory_space=pl.ANY),
        scratch_shapes=[pltpu.VMEM((ZERO_RUN * TOK_SUB, LANES), F32), pltpu.SemaphoreType.DMA],
    )
    return pl.pallas_call(
        _dispatch_kernel,
        grid_spec=grid_spec,
        out_shape=jax.ShapeDtypeStruct((cap * TOK_SUB, LANES), F32),
        compiler_params=pltpu.CompilerParams(dimension_semantics=("arbitrary",)),
        name="dispatch",
    )(dest_flat, pad_tbl, hp)


def _expert_kernel(be_ref, nxt_ref, nu_ref, xs_ref, w1_hbm, b1_ref, w2_hbm, b2_ref, ys_ref,
                   w1f_scr, w2f_scr, w1b_scr, w2b_scr, sems):
    b = pl.program_id(0)
    used = b < nu_ref[0]
    new_expert = jnp.logical_or(b == 0, be_ref[b] != be_ref[jnp.maximum(b - 1, 0)])

    def fetch(e):
        return (pltpu.make_async_copy(w1_hbm.at[e], w1f_scr, sems.at[0]),
                pltpu.make_async_copy(w2_hbm.at[e], w2f_scr, sems.at[1]))

    @pl.when(jnp.logical_and(used, b == 0))
    def _():
        for c in fetch(be_ref[0]):
            c.start()

    @pl.when(jnp.logical_and(used, new_expert))
    def _():
        for c in fetch(be_ref[b]):
            c.wait()

        def cast(rb, carry):
            rows = pl.ds(pl.multiple_of(rb * CAST_ROWS, CAST_ROWS), CAST_ROWS)
            w1b_scr[rows, :] = w1f_scr[rows, :].astype(BF16)
            w2b_scr[rows, :] = w2f_scr[rows, :].astype(BF16)
            return carry

        lax.fori_loop(0, D_MODEL // CAST_ROWS, cast, 0)

        @pl.when(nxt_ref[b] >= 0)
        def _():
            for c in fetch(nxt_ref[b]):
                c.start()

    @pl.when(used)
    def _():
        for sub in range(MOE_BLOCK // MOE_SUB):
            x = _load_token_tiles(xs_ref, MOE_SUB, sub * MOE_SUB).astype(BF16)
            hid = jnp.dot(x, w1b_scr[...], preferred_element_type=F32) + b1_ref[0]
            glu = jnp.minimum(hid[:, :D_FF_EXPERT], SWIGLU_LIMIT)
            lin = jnp.clip(hid[:, D_FF_EXPERT:], -SWIGLU_LIMIT, SWIGLU_LIMIT)
            act = glu * _sigmoid(SWIGLU_ALPHA * glu) * (lin + 1.0)
            y = jnp.dot(act.astype(BF16), w2b_scr[...], preferred_element_type=F32) + b2_ref[0]
            _store_token_tiles(ys_ref, y, MOE_SUB, sub * MOE_SUB)


def _experts(blk_expert, blk_next, n_used, xs, w1, b1, w2, b2):
    cap = xs.shape[0] // TOK_SUB
    n_blocks = cap // MOE_BLOCK
    grid_spec = pltpu.PrefetchScalarGridSpec(
        num_scalar_prefetch=3,
        grid=(n_blocks,),
        in_specs=[
            pl.BlockSpec((MOE_BLOCK * TOK_SUB, LANES), lambda b, be, nx, nu: (b, 0)),
            pl.BlockSpec(memory_space=pl.ANY),
            pl.BlockSpec((1, 1, 2 * D_FF_EXPERT), lambda b, be, nx, nu: (be[b], 0, 0)),
            pl.BlockSpec(memory_space=pl.ANY),
            pl.BlockSpec((1, 1, D_MODEL), lambda b, be, nx, nu: (be[b], 0, 0)),
        ],
        out_specs=pl.BlockSpec((MOE_BLOCK * TOK_SUB, LANES), lambda b, be, nx, nu: (b, 0)),
        scratch_shapes=[pltpu.VMEM((D_MODEL, 2 * D_FF_EXPERT), F32),
                        pltpu.VMEM((D_FF_EXPERT, D_MODEL), F32),
                        pltpu.VMEM((D_MODEL, 2 * D_FF_EXPERT), BF16),
                        pltpu.VMEM((D_FF_EXPERT, D_MODEL), BF16),
                        pltpu.SemaphoreType.DMA((2,))],
    )
    assert D_MODEL == D_FF_EXPERT
    return pl.pallas_call(
        _expert_kernel,
        grid_spec=grid_spec,
        out_shape=jax.ShapeDtypeStruct((cap * TOK_SUB, LANES), F32),
        compiler_params=pltpu.CompilerParams(
            dimension_semantics=("arbitrary",), vmem_limit_bytes=VMEM_LIMIT),
        name="experts",
    )(blk_expert, blk_next, n_used, xs, w1, b1, w2, b2)


def _combine_kernel(dest_ref, ys_hbm, x1_ref, gate_ref, p_ref, pg_ref, wg_ref, wp_ref, fg_ref,
                    out_ref, ybuf, sems, *, final):
    i = pl.program_id(0)
    nsteps = pl.num_programs(0)
    buf_tokens = TOP_K * COMB_T

    def issue_all(step, slot):
        tbl = step * buf_tokens

        def issue(tb, carry):
            for r in range(DMA_UNROLL):
                for k in range(TOP_K):
                    row = k * COMB_T + tb * DMA_UNROLL + r
                    pltpu.make_async_copy(
                        ys_hbm.at[_tile_rows(dest_ref[tbl + row]), :],
                        ybuf.at[_tile_rows(slot * buf_tokens + row), :],
                        sems.at[slot]).start(priority=(r * TOP_K + k) % 2)
            return carry

        lax.fori_loop(0, COMB_T // DMA_UNROLL, issue, 0)

    def drain_all(slot):
        def drain(tb, carry):
            for _ in range(DMA_UNROLL * TOP_K):
                pltpu.make_async_copy(ys_hbm.at[_tile_rows(0), :],
                                      ybuf.at[_tile_rows(slot * buf_tokens), :], sems.at[slot]).wait()
            return carry

        lax.fori_loop(0, COMB_T // DMA_UNROLL, drain, 0)

    @pl.when(i == 0)
    def _():
        issue_all(0, 0)

    for slot in range(2):
        @pl.when(jnp.logical_and(i % 2 == slot, i + 1 < nsteps))
        def _():
            issue_all(i + 1, 1 - slot)

    for slot in range(2):
        @pl.when(i % 2 == slot)
        def _():
            drain_all(slot)

            moe = gate_ref[:, 0:1] * _load_token_tiles(ybuf, COMB_T, slot * buf_tokens)
            for k in range(1, TOP_K):
                moe = moe + gate_ref[:, k:k + 1] * _load_token_tiles(
                    ybuf, COMB_T, slot * buf_tokens + k * COMB_T)
            x2 = x1_ref[...] + moe
            hp = _rms(x2, pg_ref[...]).astype(BF16)
            gate = _sigmoid(jnp.dot(hp, wg_ref[...], preferred_element_type=F32))
            emb = jnp.dot(p_ref[...].astype(BF16), wp_ref[...], preferred_element_type=F32)
            x3 = x2 + gate * emb
            if final:
                x3 = _rms(x3, fg_ref[...])
            out_ref[...] = x3


def _combine(dest_flat, ys, x1, gates_t, p2d, pg, wg, wp, fg, final):
    n = x1.shape[0]
    grid_spec = pltpu.PrefetchScalarGridSpec(
        num_scalar_prefetch=1,
        grid=(n // COMB_T,),
        in_specs=[
            pl.BlockSpec(memory_space=pl.ANY),
            pl.BlockSpec((COMB_T, D_MODEL), lambda i, d: (i, 0)),
            pl.BlockSpec((COMB_T, 8), lambda i, d: (i, 0)),
            pl.BlockSpec((COMB_T, PLE_DIM), lambda i, d: (i, 0)),
            pl.BlockSpec((1, D_MODEL), lambda i, d: (0, 0)),
            pl.BlockSpec((D_MODEL, D_MODEL), lambda i, d: (0, 0)),
            pl.BlockSpec((PLE_DIM, D_MODEL), lambda i, d: (0, 0)),
            pl.BlockSpec((1, D_MODEL), lambda i, d: (0, 0)),
        ],
        out_specs=pl.BlockSpec((COMB_T, D_MODEL), lambda i, d: (i, 0)),
        scratch_shapes=[pltpu.VMEM((2 * TOP_K * COMB_T * TOK_SUB, LANES), F32),
                        pltpu.SemaphoreType.DMA((2,))],
    )
    return pl.pallas_call(
        functools.partial(_combine_kernel, final=final),
        grid_spec=grid_spec,
        out_shape=jax.ShapeDtypeStruct((n, D_MODEL), F32),
        compiler_params=pltpu.CompilerParams(
            dimension_semantics=("arbitrary",), vmem_limit_bytes=VMEM_LIMIT),
        name="combine",
    )(dest_flat, ys, x1, gates_t, p2d, pg, wg, wp, fg)


def _layer(x2d, p2d, bsz, seq, mix_norm, w_in, sgu_ln_g, sgu_ln_b, sgu_w, sgu_b, conv_w, conv_b,
           dt_bias, a_log, d_skip, ssm_norm, w_branch_a, w_branch_b, w_out, ffn_norm, w_router,
           b_router, w1, b1, w2, b2, ple_norm, w_ple_gate, w_ple_proj, final_norm, final):
    n = x2d.shape[0]
    row = lambda a: a.reshape(1, -1).astype(F32)

    w_main = jnp.concatenate([w_in[:, :OFF_DT], w_in[:, OFF_GA:]], axis=1).astype(BF16)
    w_dt = jnp.pad(w_in[:, OFF_DT:OFF_GA], ((0, 0), (0, LANES - SSM_HEADS))).astype(BF16)
    pos = jnp.arange(SGU_LEN)
    allowed = (pos[None, :] // CHUNK) <= (pos[:, None] // CHUNK)
    wsgu = jnp.where(allowed[None], sgu_w, 0.0).astype(BF16)
    bsgu = jnp.repeat(sgu_b.T, SGU_HEAD_DIM, axis=1).astype(F32)
    dtb = dt_bias.reshape(-1, 1).astype(F32)
    acol = -jnp.exp(a_log.astype(F32)).reshape(-1, 1)
    dskip = jnp.repeat(d_skip.astype(F32), SSM_HEAD_DIM).reshape(1, -1)
    head_of_col = jnp.arange(SSM_INNER) // SSM_HEAD_DIM
    eexp = (jnp.arange(LANES)[:, None] == head_of_col[None, :]).astype(BF16)
    triu = (jnp.arange(SSD_CHUNK)[:, None] <= jnp.arange(SSD_CHUNK)[None, :]).astype(BF16)
    t_out = jnp.arange((SSM_CONV - 1) * SSD_CHUNK)
    src = SSD_CHUNK + t_out % SSD_CHUNK - (SSM_CONV - 1) + t_out // SSD_CHUNK
    shift = (src[:, None] == jnp.arange(2 * SSD_CHUNK)[None, :]).astype(BF16)
    consts = [row(sgu_ln_g), row(sgu_ln_b), wsgu, bsgu, conv_w.astype(F32), row(conv_b), dtb, acol,
              dskip, row(ssm_norm), eexp, shift, triu]

    proj, dt_raw = _in_proj(x2d, row(mix_norm), w_main, w_dt)
    x1 = _mixer(proj, dt_raw, x2d, bsz, seq, consts, w_branch_a.astype(BF16),
                w_branch_b.astype(BF16), w_out.astype(BF16))

    wr_f = jnp.pad(w_router.astype(F32), ((0, 0), (0, LANES - N_EXPERTS)))
    wr_hi = wr_f.astype(BF16)
    wr_pad = jnp.stack([wr_hi, (wr_f - wr_hi.astype(F32)).astype(BF16)])
    upper = (jnp.arange(ROUTE_T)[:, None] < jnp.arange(ROUTE_T)[None, :]).astype(BF16)
    hp, idx, gates, rank, cnt = _router(x1, row(ffn_norm), wr_pad, b_router.reshape(-1, 1).astype(F32),
                                        upper)

    counts = cnt[:, 0]
    padded = (counts + MOE_BLOCK - 1) // MOE_BLOCK * MOE_BLOCK
    pend = jnp.cumsum(padded)
    pstart = pend - padded
    nk = n * TOP_K
    cap = (nk + MOE_BLOCK - 1) // MOE_BLOCK * MOE_BLOCK + N_EXPERTS * MOE_BLOCK
    n_blocks = cap // MOE_BLOCK
    ex = jnp.arange(N_EXPERTS, dtype=I32)
    dest = rank + jnp.sum(jnp.where(idx[..., None] == ex, pstart.astype(I32), 0), axis=-1)
    blk_start = jnp.arange(n_blocks, dtype=I32) * MOE_BLOCK
    blk_expert = jnp.minimum(jnp.sum((pend[None, :] <= blk_start[:, None]).astype(I32), axis=1),
                             N_EXPERTS - 1).astype(I32)
    n_used = (pend[-1] // MOE_BLOCK).astype(I32).reshape(1)
    live = jnp.where(counts > 0, ex, N_EXPERTS)
    later = jnp.concatenate([lax.cummin(live[::-1])[::-1][1:], jnp.full((1,), N_EXPERTS, I32)])
    blk_next = jnp.where(later < N_EXPERTS, later, -1).astype(I32)[blk_expert]

    dest_flat = dest.reshape(TOP_K, n // DISP_T, DISP_T).transpose(1, 0, 2).reshape(-1)
    pad_tbl = jnp.concatenate([pstart + counts, pend]).astype(I32)

    xs = _dispatch(dest_flat, pad_tbl, hp, cap)
    ys = _experts(blk_expert, blk_next, n_used, xs, w1.astype(F32), b1.reshape(N_EXPERTS, 1, -1).astype(F32),
                  w2.astype(F32), b2.reshape(N_EXPERTS, 1, -1).astype(F32))
    gates_t = jnp.pad(gates.T, ((0, 0), (0, 8 - TOP_K)))
    return _combine(dest_flat, ys, x1, gates_t, p2d, row(ple_norm), w_ple_gate.astype(BF16),
                    w_ple_proj.astype(BF16), row(final_norm), final)


def kernel(x, p, mix_norm, w_in, sgu_ln_g, sgu_ln_b, sgu_w, sgu_b, conv_w, conv_b, dt_bias, a_log,
           d_skip, ssm_norm, w_branch_a, w_branch_b, w_out, ffn_norm, w_router, b_router, w1, b1,
           w2, b2, ple_norm, w_ple_gate, w_ple_proj, final_norm):
    bsz, seq, d = x.shape
    depth = w_in.shape[0]
    assert d == D_MODEL and seq % max(MIX_T, IN_TM) == 0
    x2d = x.reshape(bsz * seq, d)
    for i in range(depth):
        x2d = _layer(x2d, p[i].reshape(bsz * seq, PLE_DIM), bsz, seq, mix_norm[i], w_in[i],
                     sgu_ln_g[i], sgu_ln_b[i], sgu_w[i], sgu_b[i], conv_w[i], conv_b[i], dt_bias[i],
                     a_log[i], d_skip[i], ssm_norm[i], w_branch_a[i], w_branch_b[i], w_out[i],
                     ffn_norm[i], w_router[i], b_router[i], w1[i], b1[i], w2[i], b2[i], ple_norm[i],
                     w_ple_gate[i], w_ple_proj[i], final_norm, final=(i == depth - 1))
    return x2d.reshape(bsz, seq, d)
```

```python
import functools
import math

import jax
import jax.numpy as jnp
from jax import lax
from jax.experimental import pallas as pl
from jax.experimental.pallas import tpu as pltpu

F32 = jnp.float32
BF16 = jnp.bfloat16
I32 = jnp.int32

D_MODEL = 1024
CHUNK = 64
PLE_DIM = 256
RMS_EPS = 1e-6
LN_EPS = 1e-5

SGU_HEADS = 8
SGU_HEAD_DIM = 128
SGU_WIDTH = SGU_HEADS * SGU_HEAD_DIM
SGU_LEN = 128

SSM_HEADS = 16
SSM_HEAD_DIM = 64
SSM_INNER = SSM_HEADS * SSM_HEAD_DIM
SSM_GROUPS = 4
SSM_STATE = 128
SSM_CONV = 4
SSD_CHUNK = 128
SSM_CONV_DIM = SSM_INNER + 2 * SSM_GROUPS * SSM_STATE

N_EXPERTS = 32
TOP_K = 4
D_FF_EXPERT = 1024
SWIGLU_LIMIT = 7.0
SWIGLU_ALPHA = 1.702
MOE_BLOCK = 512
MOE_SUB = 256

OFF_U = 0
OFF_V = OFF_U + SGU_WIDTH
OFF_Z = OFF_V + SGU_WIDTH
OFF_XBC = OFF_Z + SSM_INNER
OFF_DT = OFF_XBC + SSM_CONV_DIM
OFF_GA = OFF_DT + SSM_HEADS
OFF_GB = OFF_GA + D_MODEL
IN_PROJ_DIM = OFF_GB + D_MODEL

LANES = 128
SUBLANES = 8
PROJ_MAIN = IN_PROJ_DIM - SSM_HEADS
PROJ_TILE = 1024
TOK_SUB = D_MODEL // LANES

VMEM_LIMIT = 56 * 1024 * 1024

IN_TM = 512
IN_TN = 1792
MIX_T = 512
ROUTE_T = 512
DISP_T = 512
COMB_T = DISP_T
COMB_SUB = 128
DMA_UNROLL = 8
ZERO_RUN = 64
CAST_ROWS = 128


def _sigmoid(x):
    return 1.0 / (1.0 + jnp.exp(-x))


def _gelu_exact(x):
    return 0.5 * x * (1.0 + lax.erf(x * (1.0 / math.sqrt(2.0))))


def _softplus(x):
    return jnp.maximum(x, 0.0) + jnp.log1p(jnp.exp(-jnp.abs(x)))


def _rms(x, g):
    ms = jnp.mean(x * x, axis=-1, keepdims=True)
    return x * lax.rsqrt(ms + RMS_EPS) * g


def _store_token_tiles(ref, val, rows, start=0):
    for j in range(TOK_SUB):
        ref[pl.ds(start * TOK_SUB + j, rows, stride=TOK_SUB), :] = val[:, j * LANES:(j + 1) * LANES]


def _load_token_tiles(ref, rows, start=0):
    return jnp.concatenate(
        [ref[pl.ds(start * TOK_SUB + j, rows, stride=TOK_SUB), :] for j in range(TOK_SUB)], axis=1)


def _tile_rows(idx):
    return pl.ds(pl.multiple_of(idx * TOK_SUB, TOK_SUB), TOK_SUB)


def _inproj_kernel(x_ref, g_ref, w_ref, wdt_ref, proj_ref, dt_ref, h_scr):
    @pl.when(pl.program_id(0) == 0)
    def _():
        h_scr[...] = jnp.zeros_like(h_scr)

    h = h_scr[...]
    dt_ref[...] = jnp.dot(h, wdt_ref[...], preferred_element_type=F32)
    for c in range(PROJ_MAIN // IN_TN):
        cols = slice(c * IN_TN, (c + 1) * IN_TN)
        proj_ref[:, cols] = jnp.dot(h, w_ref[:, cols], preferred_element_type=F32).astype(BF16)
    h_scr[...] = _rms(x_ref[...], g_ref[...]).astype(BF16)


def _in_proj(x2d, g, w_main, w_dt):
    n = x2d.shape[0]
    ntiles = n // IN_TM
    return pl.pallas_call(
        _inproj_kernel,
        grid=(ntiles + 1,),
        in_specs=[
            pl.BlockSpec((IN_TM, D_MODEL), lambda s: (jnp.minimum(s, ntiles - 1), 0)),
            pl.BlockSpec((1, D_MODEL), lambda s: (0, 0)),
            pl.BlockSpec((D_MODEL, PROJ_MAIN), lambda s: (0, 0)),
            pl.BlockSpec((D_MODEL, LANES), lambda s: (0, 0)),
        ],
        out_specs=[
            pl.BlockSpec((IN_TM, PROJ_MAIN), lambda s: (jnp.maximum(s - 1, 0), 0)),
            pl.BlockSpec((IN_TM, LANES), lambda s: (jnp.maximum(s - 1, 0), 0)),
        ],
        out_shape=[
            jax.ShapeDtypeStruct((n, PROJ_MAIN), BF16),
            jax.ShapeDtypeStruct((n, LANES), F32),
        ],
        scratch_shapes=[pltpu.VMEM((IN_TM, D_MODEL), BF16)],
        compiler_params=pltpu.CompilerParams(
            dimension_semantics=("arbitrary",), vmem_limit_bytes=VMEM_LIMIT),
        name="in_proj",
    )(x2d, g, w_main, w_dt)


def _split3(a):
    a1 = a.astype(BF16)
    r1 = a - a1.astype(F32)
    a2 = r1.astype(BF16)
    a3 = (r1 - a2.astype(F32)).astype(BF16)
    return a1, a2, a3


def _dot3(parts, w):
    out = jnp.dot(parts[0], w, preferred_element_type=F32)
    for p in parts[1:]:
        out = out + jnp.dot(p, w, preferred_element_type=F32)
    return out


def _mixer_kernel(u_ref, v_ref, z_ref, xb0_ref, xb1_ref, dt_ref, ga_ref, gb_ref, x_ref,
                  lng_ref, lnb_ref, wsgu_ref, bsgu_ref, convw_ref, convb_ref, dtb_ref, acol_ref,
                  dskip_ref, normw_ref, eexp_ref, shift_ref, triu_ref, wa_ref, wb_ref, wo_ref,
                  x1_ref,
                  tail_scr, xc_scr, state_scr, ya_scr, yb_scr, yap_scr, ybp_scr, *, tps):
    nchunk = MIX_T // SSD_CHUNK
    s = pl.program_id(0)

    @pl.when(s % tps == 0)
    def _():
        tail_scr[...] = jnp.zeros_like(tail_scr)
        state_scr[...] = jnp.zeros_like(state_scr)

    @pl.when(s == 0)
    def _():
        yap_scr[...] = jnp.zeros_like(yap_scr)
        ybp_scr[...] = jnp.zeros_like(ybp_scr)

    xb_refs = (xb0_ref, xb1_ref)
    for c in range(nchunk):
        r0 = c * SSD_CHUNK
        for hf, xb_ref in enumerate(xb_refs):
            cols = slice(hf * PROJ_TILE, (hf + 1) * PROJ_TILE)
            cur = xb_ref[r0:r0 + SSD_CHUNK, :]
            prev = tail_scr[:, cols] if c == 0 else xb_ref[r0 - SSD_CHUNK:r0, :]
            sh = jnp.dot(shift_ref[...], jnp.concatenate([prev, cur], axis=0), preferred_element_type=F32)
            acc = convb_ref[:, cols] + convw_ref[SSM_CONV - 1:SSM_CONV, cols] * cur.astype(F32)
            for j in range(SSM_CONV - 1):
                acc = acc + convw_ref[j:j + 1, cols] * sh[j * SSD_CHUNK:(j + 1) * SSD_CHUNK, :]
            xc_scr[r0:r0 + SSD_CHUNK, cols] = acc * _sigmoid(acc)
    for hf, xb_ref in enumerate(xb_refs):
        tail_scr[:, hf * PROJ_TILE:(hf + 1) * PROJ_TILE] = xb_ref[MIX_T - SSD_CHUNK:MIX_T, :]

    row_i = lax.broadcasted_iota(I32, (SSD_CHUNK, SSD_CHUNK), 0)
    col_i = lax.broadcasted_iota(I32, (SSD_CHUNK, SSD_CHUNK), 1)
    tril = row_i >= col_i
    lane_hi = col_i >= SSM_HEAD_DIM
    eexp = eexp_ref[...]
    triu = triu_ref[...]

    for c in range(nchunk):
        rows = slice(c * SSD_CHUNK, (c + 1) * SSD_CHUNK)

        ug = _gelu_exact(u_ref[rows, :].astype(F32))
        vg = _gelu_exact(v_ref[rows, :].astype(F32))
        mu = jnp.mean(vg, axis=-1, keepdims=True)
        vc = vg - mu
        var = jnp.mean(vc * vc, axis=-1, keepdims=True)
        vn = (vc * lax.rsqrt(var + LN_EPS) * lng_ref[...] + lnb_ref[...]).astype(BF16)
        for g in range(SGU_HEADS):
            cols = slice(g * SGU_HEAD_DIM, (g + 1) * SGU_HEAD_DIM)
            mixed = jnp.dot(wsgu_ref[g], vn[:, cols], preferred_element_type=F32)
            ya_scr[rows, cols] = (ug[:, cols] * (mixed + bsgu_ref[:, cols])).astype(BF16)

        dt_t = _softplus(dt_ref[rows, :].T[0:SSM_HEADS, :] + dtb_ref[...])
        acs_t = _dot3(_split3(dt_t * acol_ref[...]), triu)
        w_t = jnp.exp(acs_t[:, SSD_CHUNK - 1:SSD_CHUNK] - acs_t) * dt_t
        acs = jnp.concatenate([acs_t, jnp.zeros((LANES - SSM_HEADS, SSD_CHUNK), F32)], axis=0).T
        aend = jnp.broadcast_to(acs[SSD_CHUNK - 1:SSD_CHUNK, :], (SUBLANES, LANES))
        dec = jnp.exp(_dot3(_split3(aend), eexp)[0:1, :])

        for gi in range(SSM_GROUPS):
            bg = xc_scr[rows, SSM_INNER + gi * SSM_STATE:SSM_INNER + (gi + 1) * SSM_STATE]
            cg = xc_scr[rows, SSM_INNER + (SSM_GROUPS + gi) * SSM_STATE:
                        SSM_INNER + (SSM_GROUPS + gi + 1) * SSM_STATE]
            bg_t = bg.T
            cb = jnp.dot(cg.astype(BF16), bg_t.astype(BF16), preferred_element_type=F32)
            for pr in range(2):
                lb = gi * 2 + pr
                cols = slice(lb * LANES, (lb + 1) * LANES)
                xpair = xc_scr[rows, cols]
                hpair = state_scr[:, cols]
                acc = jnp.zeros((SSD_CHUNK, LANES), F32)
                st = jnp.zeros((SSM_STATE, LANES), F32)
                for hh in range(2):
                    h = lb * 2 + hh
                    lmask = lane_hi if hh == 1 else jnp.logical_not(lane_hi)
                    xm = jnp.where(lmask, xpair, 0.0).astype(BF16)
                    hm = jnp.where(lmask, hpair, 0.0).astype(BF16)
                    colb = acs[:, h:h + 1]
                    rowb = acs_t[h:h + 1, :]
                    decay = jnp.exp(jnp.where(tril, colb - rowb, -jnp.inf))
                    m_h = (cb * decay * dt_t[h:h + 1, :]).astype(BF16)
                    c_h = (cg * jnp.exp(colb)).astype(BF16)
                    acc = acc + jnp.dot(m_h, xm, preferred_element_type=F32)
                    acc = acc + jnp.dot(c_h, hm, preferred_element_type=F32)
                    btw = (bg_t * w_t[h:h + 1, :]).astype(BF16)
                    st = st + jnp.dot(btw, xm, preferred_element_type=F32)
                state_scr[:, cols] = hpair * dec[:, cols] + st
                y = acc + dskip_ref[:, cols] * xpair
                zz = z_ref[rows, cols].astype(F32)
                xc_scr[rows, cols] = y * (zz * _sigmoid(zz))
            gcols = slice(gi * 2 * LANES, (gi + 1) * 2 * LANES)
            yg = xc_scr[rows, gcols]
            ms = jnp.mean(yg * yg, axis=-1, keepdims=True)
            yb_scr[rows, gcols] = (yg * lax.rsqrt(ms + RMS_EPS) * normw_ref[:, gcols]).astype(BF16)

    ma = jnp.dot(yap_scr[...], wa_ref[...], preferred_element_type=F32)
    mb = jnp.dot(ybp_scr[...], wb_ref[...], preferred_element_type=F32)
    merged = (_sigmoid(ga_ref[...].astype(F32)) * ma + _sigmoid(gb_ref[...].astype(F32)) * mb)
    x1_ref[...] = x_ref[...] + jnp.dot(merged.astype(BF16), wo_ref[...], preferred_element_type=F32)
    yap_scr[...] = ya_scr[...]
    ybp_scr[...] = yb_scr[...]


def _mixer(proj, dt_raw, x2d, bsz, seq, consts, wa, wb, wo):
    n = x2d.shape[0]
    tps = seq // MIX_T
    ntiles = bsz * tps
    cur = lambda s: jnp.minimum(s, ntiles - 1)
    prev = lambda s: jnp.maximum(s - 1, 0)

    def pcol(off, tile):
        return pl.BlockSpec((MIX_T, PROJ_TILE), lambda s: (tile(s), off // PROJ_TILE))

    def full(a):
        nd = a.ndim
        return pl.BlockSpec(a.shape, lambda s, nd=nd: (0,) * nd)

    in_specs = [pcol(OFF_U, cur), pcol(OFF_V, cur), pcol(OFF_Z, cur),
                pcol(OFF_XBC, cur), pcol(OFF_XBC + PROJ_TILE, cur),
                pl.BlockSpec((MIX_T, LANES), lambda s: (cur(s), 0)),
                pcol(OFF_GA - SSM_HEADS, prev), pcol(OFF_GB - SSM_HEADS, prev),
                pl.BlockSpec((MIX_T, D_MODEL), lambda s: (prev(s), 0))]
    in_specs += [full(a) for a in consts] + [full(wa), full(wb), full(wo)]
    return pl.pallas_call(
        functools.partial(_mixer_kernel, tps=tps),
        grid=(ntiles + 1,),
        in_specs=in_specs,
        out_specs=pl.BlockSpec((MIX_T, D_MODEL), lambda s: (prev(s), 0)),
        out_shape=jax.ShapeDtypeStruct((n, D_MODEL), F32),
        scratch_shapes=[
            pltpu.VMEM((SSD_CHUNK, SSM_CONV_DIM), BF16),
            pltpu.VMEM((MIX_T, SSM_CONV_DIM), F32),
            pltpu.VMEM((SSM_STATE, SSM_INNER), F32),
            pltpu.VMEM((MIX_T, SGU_WIDTH), BF16),
            pltpu.VMEM((MIX_T, SSM_INNER), BF16),
            pltpu.VMEM((MIX_T, SGU_WIDTH), BF16),
            pltpu.VMEM((MIX_T, SSM_INNER), BF16),
        ],
        compiler_params=pltpu.CompilerParams(
            dimension_semantics=("arbitrary",), vmem_limit_bytes=VMEM_LIMIT),
        name="mixer",
    )(proj, proj, proj, proj, proj, dt_raw, proj, proj, x2d, *consts, wa, wb, wo)


def _router_kernel(x_ref, g_ref, wr_ref, br_ref, upper_ref, hp_ref, idx_ref, gate_ref, rank_ref, cnt_ref,
                   carry_scr):
    @pl.when(pl.program_id(0) == 0)
    def _():
        carry_scr[...] = jnp.zeros_like(carry_scr)

    h = _rms(x_ref[...], g_ref[...])
    _store_token_tiles(hp_ref, h, ROUTE_T)

    h_hi = h.astype(BF16)
    h_lo = (h - h_hi.astype(F32)).astype(BF16)
    logits = (jnp.dot(h_hi, wr_ref[0], preferred_element_type=F32)
              + jnp.dot(h_hi, wr_ref[1], preferred_element_type=F32)
              + jnp.dot(h_lo, wr_ref[0], preferred_element_type=F32))
    lt = logits.T[0:N_EXPERTS, :] + br_ref[...]
    eidx = lax.broadcasted_iota(I32, (N_EXPERTS, ROUTE_T), 0).astype(F32)
    vals = lt
    sel_any = jnp.zeros((N_EXPERTS, ROUTE_T), F32)
    sels, tops = [], []
    for k in range(TOP_K):
        m = jnp.max(vals, axis=0, keepdims=True)
        first = jnp.min(jnp.where(vals == m, eidx, float(N_EXPERTS)), axis=0, keepdims=True)
        sel = eidx == first
        vals = jnp.where(sel, -jnp.inf, vals)
        sel_f = sel.astype(F32)
        sel_any = sel_any + sel_f
        sels.append(sel_f)
        tops.append(m)
        idx_ref[k:k + 1, :] = first.astype(I32)
    es = [jnp.exp(t - tops[0]) for t in tops]
    denom = functools.reduce(lambda a, b: a + b, es)
    for k in range(TOP_K):
        gate_ref[k:k + 1, :] = es[k] / denom
    gate_ref[TOP_K:, :] = jnp.zeros((SUBLANES - TOP_K, ROUTE_T), F32)

    excl = jnp.dot(sel_any.astype(BF16), upper_ref[...], preferred_element_type=F32) + carry_scr[:, 0:1]
    for k in range(TOP_K):
        rank_ref[k:k + 1, :] = jnp.sum(sels[k] * excl, axis=0, keepdims=True).astype(I32)
    new_carry = carry_scr[...] + jnp.sum(sel_any, axis=1, keepdims=True)
    carry_scr[...] = new_carry
    cnt_ref[...] = new_carry.astype(I32)


def _router(x1, g, wr_pad, br_col, upper):
    n = x1.shape[0]
    per_k = pl.BlockSpec((TOP_K, ROUTE_T), lambda i: (0, i))
    return pl.pallas_call(
        _router_kernel,
        grid=(n // ROUTE_T,),
        in_specs=[
            pl.BlockSpec((ROUTE_T, D_MODEL), lambda i: (i, 0)),
            pl.BlockSpec((1, D_MODEL), lambda i: (0, 0)),
            pl.BlockSpec((2, D_MODEL, LANES), lambda i: (0, 0, 0)),
            pl.BlockSpec((N_EXPERTS, 1), lambda i: (0, 0)),
            pl.BlockSpec((ROUTE_T, ROUTE_T), lambda i: (0, 0)),
        ],
        out_specs=[
            pl.BlockSpec((ROUTE_T * TOK_SUB, LANES), lambda i: (i, 0)),
            per_k,
            pl.BlockSpec((SUBLANES, ROUTE_T), lambda i: (0, i)),
            per_k,
            pl.BlockSpec((N_EXPERTS, LANES), lambda i: (0, 0)),
        ],
        out_shape=[
            jax.ShapeDtypeStruct((n * TOK_SUB, LANES), F32),
            jax.ShapeDtypeStruct((TOP_K, n), I32),
            jax.ShapeDtypeStruct((SUBLANES, n), F32),
            jax.ShapeDtypeStruct((TOP_K, n), I32),
            jax.ShapeDtypeStruct((N_EXPERTS, LANES), I32),
        ],
        scratch_shapes=[pltpu.VMEM((N_EXPERTS, LANES), F32)],
        compiler_params=pltpu.CompilerParams(
            dimension_semantics=("arbitrary",), vmem_limit_bytes=VMEM_LIMIT),
        name="router",
    )(x1, g, wr_pad, br_col, upper)


def _dispatch_kernel(dest_ref, pad_ref, hp_ref, xs_hbm, zero_scr, sem_rows):
    i = pl.program_id(0)
    tbl = i * (TOP_K * DISP_T)

    def zero_copy(s):
        return pltpu.make_async_copy(zero_scr.at[0:TOK_SUB, :], xs_hbm.at[_tile_rows(s), :], sem_rows)

    def zero_run(s):
        rows = pl.ds(pl.multiple_of(s * TOK_SUB, TOK_SUB), ZERO_RUN * TOK_SUB)
        return pltpu.make_async_copy(zero_scr, xs_hbm.at[rows, :], sem_rows)

    def zero_pads(act):
        for e in range(N_EXPERTS):
            lo, hi = pad_ref[e], pad_ref[N_EXPERTS + e]
            nrun = (hi - lo) // ZERO_RUN
            lax.fori_loop(0, nrun, lambda r, c: (act(zero_run(lo + r * ZERO_RUN)), c)[1], 0)
            lax.fori_loop(lo + nrun * ZERO_RUN, hi, lambda s, c: (act(zero_copy(s)), c)[1], 0)

    @pl.when(i == 0)
    def _():
        zero_scr[...] = jnp.zeros_like(zero_scr)
        zero_pads(lambda d: d.start())
        zero_pads(lambda d: d.wait())

    def tile_copy(t, k):
        d = dest_ref[tbl + k * DISP_T + t]
        return pltpu.make_async_copy(hp_ref.at[_tile_rows(t), :], xs_hbm.at[_tile_rows(d), :], sem_rows)

    def issue(tb, carry):
        for r in range(DMA_UNROLL):
            for k in range(TOP_K):
                tile_copy(tb * DMA_UNROLL + r, k).start(priority=(r * TOP_K + k) % 2)
        return carry

    lax.fori_loop(0, DISP_T // DMA_UNROLL, issue, 0)

    def drain(tb, carry):
        for _ in range(DMA_UNROLL * TOP_K):
            zero_copy(0).wait()
        return carry

    lax.fori_loop(0, DISP_T // DMA_UNROLL, drain, 0)


def _dispatch(dest_flat, pad_tbl, hp, cap):
    n = hp.shape[0] // TOK_SUB
    grid_spec = pltpu.PrefetchScalarGridSpec(
        num_scalar_prefetch=2,
        grid=(n // DISP_T,),
        in_specs=[pl.BlockSpec((DISP_T * TOK_SUB, LANES), lambda i, d, p: (i, 0))],
        out_specs=pl.BlockSpec(memory_space=pl.ANY),
        scratch_shapes=[pltpu.VMEM((ZERO_RUN * TOK_SUB, LANES), F32), pltpu.SemaphoreType.DMA],
    )
    return pl.pallas_call(
        _dispatch_kernel,
        grid_spec=grid_spec,
        out_shape=jax.ShapeDtypeStruct((cap * TOK_SUB, LANES), F32),
        compiler_params=pltpu.CompilerParams(dimension_semantics=("arbitrary",)),
        name="dispatch",
    )(dest_flat, pad_tbl, hp)


def _expert_kernel(be_ref, nxt_ref, nv_ref, xs_ref, w1_hbm, b1_ref, w2_hbm, b2_ref, ys_ref,
                   w1f_scr, w2f_scr, w1b_scr, w2b_scr, sems):
    b = pl.program_id(0)
    valid = nv_ref[b]
    used = valid > 0
    new_expert = jnp.logical_or(b == 0, be_ref[b] != be_ref[jnp.maximum(b - 1, 0)])

    def fetch(e):
        return (pltpu.make_async_copy(w1_hbm.at[e], w1f_scr, sems.at[0]),
                pltpu.make_async_copy(w2_hbm.at[e], w2f_scr, sems.at[1]))

    @pl.when(jnp.logical_and(used, b == 0))
    def _():
        for c in fetch(be_ref[0]):
            c.start()

    @pl.when(jnp.logical_and(used, new_expert))
    def _():
        for c in fetch(be_ref[b]):
            c.wait()

        def cast(rb, carry):
            rows = pl.ds(pl.multiple_of(rb * CAST_ROWS, CAST_ROWS), CAST_ROWS)
            w1b_scr[rows, :] = w1f_scr[rows, :].astype(BF16)
            w2b_scr[rows, :] = w2f_scr[rows, :].astype(BF16)
            return carry

        lax.fori_loop(0, D_MODEL // CAST_ROWS, cast, 0)

        @pl.when(nxt_ref[b] >= 0)
        def _():
            for c in fetch(nxt_ref[b]):
                c.start()

    def sub_block(sub):
        x = _load_token_tiles(xs_ref, MOE_SUB, sub * MOE_SUB).astype(BF16)
        hid = jnp.dot(x, w1b_scr[...], preferred_element_type=F32) + b1_ref[0]
        glu = jnp.minimum(hid[:, :D_FF_EXPERT], SWIGLU_LIMIT)
        lin = jnp.clip(hid[:, D_FF_EXPERT:], -SWIGLU_LIMIT, SWIGLU_LIMIT)
        act = glu * _sigmoid(SWIGLU_ALPHA * glu) * (lin + 1.0)
        y = jnp.dot(act.astype(BF16), w2b_scr[...], preferred_element_type=F32) + b2_ref[0]
        _store_token_tiles(ys_ref, y, MOE_SUB, sub * MOE_SUB)

    @pl.when(valid > MOE_SUB)
    def _():
        for sub in range(MOE_BLOCK // MOE_SUB):
            sub_block(sub)

    @pl.when(jnp.logical_and(used, valid <= MOE_SUB))
    def _():
        sub_block(0)


def _experts(blk_expert, blk_next, blk_valid, xs, w1, b1, w2, b2):
    cap = xs.shape[0] // TOK_SUB
    n_blocks = cap // MOE_BLOCK
    grid_spec = pltpu.PrefetchScalarGridSpec(
        num_scalar_prefetch=3,
        grid=(n_blocks,),
        in_specs=[
            pl.BlockSpec((MOE_BLOCK * TOK_SUB, LANES), lambda b, be, nx, nv: (b, 0)),
            pl.BlockSpec(memory_space=pl.ANY),
            pl.BlockSpec((1, 1, 2 * D_FF_EXPERT), lambda b, be, nx, nv: (be[b], 0, 0)),
            pl.BlockSpec(memory_space=pl.ANY),
            pl.BlockSpec((1, 1, D_MODEL), lambda b, be, nx, nv: (be[b], 0, 0)),
        ],
        out_specs=pl.BlockSpec((MOE_BLOCK * TOK_SUB, LANES), lambda b, be, nx, nv: (b, 0)),
        scratch_shapes=[pltpu.VMEM((D_MODEL, 2 * D_FF_EXPERT), F32),
                        pltpu.VMEM((D_FF_EXPERT, D_MODEL), F32),
                        pltpu.VMEM((D_MODEL, 2 * D_FF_EXPERT), BF16),
                        pltpu.VMEM((D_FF_EXPERT, D_MODEL), BF16),
                        pltpu.SemaphoreType.DMA((2,))],
    )
    assert D_MODEL == D_FF_EXPERT
    assert MOE_BLOCK == 2 * MOE_SUB
    return pl.pallas_call(
        _expert_kernel,
        grid_spec=grid_spec,
        out_shape=jax.ShapeDtypeStruct((cap * TOK_SUB, LANES), F32),
        compiler_params=pltpu.CompilerParams(
            dimension_semantics=("arbitrary",), vmem_limit_bytes=VMEM_LIMIT),
        name="experts",
    )(blk_expert, blk_next, blk_valid, xs, w1, b1, w2, b2)


def _combine_kernel(dest_ref, ys_hbm, x1_ref, gate_ref, p_ref, pg_ref, wg_ref, wp_ref, fg_ref,
                    out_ref, ybuf, sems, *, final):
    i = pl.program_id(0)
    nsteps = pl.num_programs(0)
    buf_tokens = TOP_K * COMB_T

    def issue_all(step, slot):
        tbl = step * buf_tokens

        def issue(tb, carry):
            for r in range(DMA_UNROLL):
                for k in range(TOP_K):
                    row = k * COMB_T + tb * DMA_UNROLL + r
                    pltpu.make_async_copy(
                        ys_hbm.at[_tile_rows(dest_ref[tbl + row]), :],
                        ybuf.at[_tile_rows(slot * buf_tokens + row), :],
                        sems.at[slot]).start(priority=(r * TOP_K + k) % 2)
            return carry

        lax.fori_loop(0, COMB_T // DMA_UNROLL, issue, 0)

    def drain_all(slot):
        def drain(tb, carry):
            for _ in range(DMA_UNROLL * TOP_K):
                pltpu.make_async_copy(ys_hbm.at[_tile_rows(0), :],
                                      ybuf.at[_tile_rows(slot * buf_tokens), :], sems.at[slot]).wait()
            return carry

        lax.fori_loop(0, COMB_T // DMA_UNROLL, drain, 0)

    @pl.when(i == 0)
    def _():
        issue_all(0, 0)

    for slot in range(2):
        @pl.when(jnp.logical_and(i % 2 == slot, i + 1 < nsteps))
        def _():
            issue_all(i + 1, 1 - slot)

    for slot in range(2):
        @pl.when(i % 2 == slot)
        def _():
            drain_all(slot)

            for sub in range(COMB_T // COMB_SUB):
                rows = slice(sub * COMB_SUB, (sub + 1) * COMB_SUB)
                x2 = x1_ref[rows, :]
                for k in range(TOP_K):
                    x2 = x2 + gate_ref[rows, k:k + 1] * _load_token_tiles(
                        ybuf, COMB_SUB, slot * buf_tokens + k * COMB_T + sub * COMB_SUB)
                hp = _rms(x2, pg_ref[...]).astype(BF16)
                gate = _sigmoid(jnp.dot(hp, wg_ref[...], preferred_element_type=F32))
                emb = jnp.dot(p_ref[rows, :].astype(BF16), wp_ref[...], preferred_element_type=F32)
                x3 = x2 + gate * emb
                if final:
                    x3 = _rms(x3, fg_ref[...])
                out_ref[rows, :] = x3


def _combine(dest_flat, ys, x1, gates_t, p2d, pg, wg, wp, fg, final):
    n = x1.shape[0]
    grid_spec = pltpu.PrefetchScalarGridSpec(
        num_scalar_prefetch=1,
        grid=(n // COMB_T,),
        in_specs=[
            pl.BlockSpec(memory_space=pl.ANY),
            pl.BlockSpec((COMB_T, D_MODEL), lambda i, d: (i, 0)),
            pl.BlockSpec((COMB_T, SUBLANES), lambda i, d: (i, 0)),
            pl.BlockSpec((COMB_T, PLE_DIM), lambda i, d: (i, 0)),
            pl.BlockSpec((1, D_MODEL), lambda i, d: (0, 0)),
            pl.BlockSpec((D_MODEL, D_MODEL), lambda i, d: (0, 0)),
            pl.BlockSpec((PLE_DIM, D_MODEL), lambda i, d: (0, 0)),
            pl.BlockSpec((1, D_MODEL), lambda i, d: (0, 0)),
        ],
        out_specs=pl.BlockSpec((COMB_T, D_MODEL), lambda i, d: (i, 0)),
        scratch_shapes=[pltpu.VMEM((2 * TOP_K * COMB_T * TOK_SUB, LANES), F32),
                        pltpu.SemaphoreType.DMA((2,))],
    )
    return pl.pallas_call(
        functools.partial(_combine_kernel, final=final),
        grid_spec=grid_spec,
        out_shape=jax.ShapeDtypeStruct((n, D_MODEL), F32),
        compiler_params=pltpu.CompilerParams(
            dimension_semantics=("arbitrary",), vmem_limit_bytes=VMEM_LIMIT),
        name="combine",
    )(dest_flat, ys, x1, gates_t, p2d, pg, wg, wp, fg)


def _layer(x2d, p2d, bsz, seq, mix_norm, w_in, sgu_ln_g, sgu_ln_b, sgu_w, sgu_b, conv_w, conv_b,
           dt_bias, a_log, d_skip, ssm_norm, w_branch_a, w_branch_b, w_out, ffn_norm, w_router,
           b_router, w1, b1, w2, b2, ple_norm, w_ple_gate, w_ple_proj, final_norm, final):
    n = x2d.shape[0]
    row = lambda a: a.reshape(1, -1).astype(F32)

    w_main = jnp.concatenate([w_in[:, :OFF_DT], w_in[:, OFF_GA:]], axis=1).astype(BF16)
    w_dt = jnp.pad(w_in[:, OFF_DT:OFF_GA], ((0, 0), (0, LANES - SSM_HEADS))).astype(BF16)
    pos = jnp.arange(SGU_LEN)
    allowed = (pos[None, :] // CHUNK) <= (pos[:, None] // CHUNK)
    wsgu = jnp.where(allowed[None], sgu_w, 0.0).astype(BF16)
    bsgu = jnp.repeat(sgu_b.T, SGU_HEAD_DIM, axis=1).astype(F32)
    dtb = dt_bias.reshape(-1, 1).astype(F32)
    acol = -jnp.exp(a_log.astype(F32)).reshape(-1, 1)
    dskip = jnp.repeat(d_skip.astype(F32), SSM_HEAD_DIM).reshape(1, -1)
    head_of_col = jnp.arange(SSM_INNER) // SSM_HEAD_DIM
    eexp = (jnp.arange(LANES)[:, None] == head_of_col[None, :]).astype(BF16)
    triu = (jnp.arange(SSD_CHUNK)[:, None] <= jnp.arange(SSD_CHUNK)[None, :]).astype(BF16)
    t_out = jnp.arange((SSM_CONV - 1) * SSD_CHUNK)
    src = SSD_CHUNK + t_out % SSD_CHUNK - (SSM_CONV - 1) + t_out // SSD_CHUNK
    shift = (src[:, None] == jnp.arange(2 * SSD_CHUNK)[None, :]).astype(BF16)
    consts = [row(sgu_ln_g), row(sgu_ln_b), wsgu, bsgu, conv_w.astype(F32), row(conv_b), dtb, acol,
              dskip, row(ssm_norm), eexp, shift, triu]

    proj, dt_raw = _in_proj(x2d, row(mix_norm), w_main, w_dt)
    x1 = _mixer(proj, dt_raw, x2d, bsz, seq, consts, w_branch_a.astype(BF16),
                w_branch_b.astype(BF16), w_out.astype(BF16))

    wr_f = jnp.pad(w_router.astype(F32), ((0, 0), (0, LANES - N_EXPERTS)))
    wr_hi = wr_f.astype(BF16)
    wr_pad = jnp.stack([wr_hi, (wr_f - wr_hi.astype(F32)).astype(BF16)])
    upper = (jnp.arange(ROUTE_T)[:, None] < jnp.arange(ROUTE_T)[None, :]).astype(BF16)
    hp, idx, gates, rank, cnt = _router(x1, row(ffn_norm), wr_pad, b_router.reshape(-1, 1).astype(F32),
                                        upper)

    counts = cnt[:, 0]
    padded = (counts + MOE_BLOCK - 1) // MOE_BLOCK * MOE_BLOCK
    pend = jnp.cumsum(padded)
    pstart = pend - padded
    nk = n * TOP_K
    cap = (nk + MOE_BLOCK - 1) // MOE_BLOCK * MOE_BLOCK + N_EXPERTS * MOE_BLOCK
    n_blocks = cap // MOE_BLOCK
    ex = jnp.arange(N_EXPERTS, dtype=I32)
    dest = rank + jnp.sum(jnp.where(idx[..., None] == ex, pstart.astype(I32), 0), axis=-1)
    blk_start = jnp.arange(n_blocks, dtype=I32) * MOE_BLOCK
    blk_expert = jnp.minimum(jnp.sum((pend[None, :] <= blk_start[:, None]).astype(I32), axis=1),
                             N_EXPERTS - 1).astype(I32)
    of_block = blk_expert[:, None] == ex[None, :]
    per_block = lambda v: jnp.sum(jnp.where(of_block, v[None, :].astype(I32), 0), axis=1)
    blk_valid = jnp.where(blk_start < pend[-1],
                          jnp.clip(per_block(pstart + counts) - blk_start, 0, MOE_BLOCK), 0).astype(I32)
    live = jnp.where(counts > 0, ex, N_EXPERTS)
    later = jnp.concatenate([lax.cummin(live[::-1])[::-1][1:], jnp.full((1,), N_EXPERTS, I32)])
    blk_next = per_block(jnp.where(later < N_EXPERTS, later, -1))

    dest_flat = dest.reshape(TOP_K, n // DISP_T, DISP_T).transpose(1, 0, 2).reshape(-1)
    used_end = pstart + (counts + MOE_SUB - 1) // MOE_SUB * MOE_SUB
    pad_tbl = jnp.concatenate([pstart + counts, used_end]).astype(I32)

    xs = _dispatch(dest_flat, pad_tbl, hp, cap)
    ys = _experts(blk_expert, blk_next, blk_valid, xs, w1.astype(F32),
                  b1.reshape(N_EXPERTS, 1, -1).astype(F32), w2.astype(F32),
                  b2.reshape(N_EXPERTS, 1, -1).astype(F32))
    return _combine(dest_flat, ys, x1, gates.T, p2d, row(ple_norm), w_ple_gate.astype(BF16),
                    w_ple_proj.astype(BF16), row(final_norm), final)


def kernel(x, p, mix_norm, w_in, sgu_ln_g, sgu_ln_b, sgu_w, sgu_b, conv_w, conv_b, dt_bias, a_log,
           d_skip, ssm_norm, w_branch_a, w_branch_b, w_out, ffn_norm, w_router, b_router, w1, b1,
           w2, b2, ple_norm, w_ple_gate, w_ple_proj, final_norm):
    bsz, seq, d = x.shape
    depth = w_in.shape[0]
    assert d == D_MODEL and seq % MIX_T == 0 and (bsz * seq) % max(IN_TM, ROUTE_T, DISP_T) == 0
    assert ROUTE_T == DISP_T == COMB_T
    x2d = x.reshape(bsz * seq, d)
    for i in range(depth):
        x2d = _layer(x2d, p[i].reshape(bsz * seq, PLE_DIM), bsz, seq, mix_norm[i], w_in[i],
                     sgu_ln_g[i], sgu_ln_b[i], sgu_w[i], sgu_b[i], conv_w[i], conv_b[i], dt_bias[i],
                     a_log[i], d_skip[i], ssm_norm[i], w_branch_a[i], w_branch_b[i], w_out[i],
                     ffn_norm[i], w_router[i], b_router[i], w1[i], b1[i], w2[i], b2[i], ple_norm[i],
                     w_ple_gate[i], w_ple_proj[i], final_norm, final=(i == depth - 1))
    return x2d.reshape(bsz, seq, d)
```

```python
import functools
import math

import jax
import jax.numpy as jnp
from jax import lax
from jax.experimental import pallas as pl
from jax.experimental.pallas import tpu as pltpu

F32 = jnp.float32
BF16 = jnp.bfloat16
I32 = jnp.int32

D_MODEL = 1024
CHUNK = 64
PLE_DIM = 256
RMS_EPS = 1e-6
LN_EPS = 1e-5

SGU_HEADS = 8
SGU_HEAD_DIM = 128
SGU_WIDTH = SGU_HEADS * SGU_HEAD_DIM
SGU_LEN = 128

SSM_HEADS = 16
SSM_HEAD_DIM = 64
SSM_INNER = SSM_HEADS * SSM_HEAD_DIM
SSM_GROUPS = 4
SSM_STATE = 128
SSM_CONV = 4
SSD_CHUNK = 128
SSM_CONV_DIM = SSM_INNER + 2 * SSM_GROUPS * SSM_STATE

N_EXPERTS = 32
TOP_K = 4
D_FF_EXPERT = 1024
SWIGLU_LIMIT = 7.0
SWIGLU_ALPHA = 1.702
MOE_BLOCK = 1024
MOE_SUB = 256

OFF_U = 0
OFF_V = OFF_U + SGU_WIDTH
OFF_Z = OFF_V + SGU_WIDTH
OFF_XBC = OFF_Z + SSM_INNER
OFF_DT = OFF_XBC + SSM_CONV_DIM
OFF_GA = OFF_DT + SSM_HEADS
OFF_GB = OFF_GA + D_MODEL
IN_PROJ_DIM = OFF_GB + D_MODEL

LANES = 128
SUBLANES = 8
PROJ_MAIN = IN_PROJ_DIM - SSM_HEADS
PROJ_TILE = 1024
TOK_SUB = D_MODEL // LANES

VMEM_LIMIT = 56 * 1024 * 1024

IN_TM = 512
IN_TN = 1792
MIX_T = 512
ROUTE_T = 512
DISP_T = 512
COMB_T = DISP_T
COMB_SUB = 128
DMA_UNROLL = 8
ZERO_RUN = 64
CAST_ROWS = 128


def _sigmoid(x):
    return 1.0 / (1.0 + jnp.exp(-x))


def _gelu_exact(x):
    return 0.5 * x * (1.0 + lax.erf(x * (1.0 / math.sqrt(2.0))))


def _softplus(x):
    return jnp.maximum(x, 0.0) + jnp.log1p(jnp.exp(-jnp.abs(x)))


def _rms(x, g):
    ms = jnp.mean(x * x, axis=-1, keepdims=True)
    return x * lax.rsqrt(ms + RMS_EPS) * g


def _store_token_tiles(ref, val, rows, start=0):
    for j in range(TOK_SUB):
        ref[pl.ds(start * TOK_SUB + j, rows, stride=TOK_SUB), :] = val[:, j * LANES:(j + 1) * LANES]


def _load_token_tiles(ref, rows, start=0):
    return jnp.concatenate(
        [ref[pl.ds(start * TOK_SUB + j, rows, stride=TOK_SUB), :] for j in range(TOK_SUB)], axis=1)


def _tile_rows(idx):
    return pl.ds(pl.multiple_of(idx * TOK_SUB, TOK_SUB), TOK_SUB)


def _inproj_kernel(x_ref, g_ref, w_ref, wdt_ref, proj_ref, dt_ref, h_scr):
    @pl.when(pl.program_id(0) == 0)
    def _():
        h_scr[...] = jnp.zeros_like(h_scr)

    h = h_scr[...]
    dt_ref[...] = jnp.dot(h, wdt_ref[...], preferred_element_type=F32)
    for c in range(PROJ_MAIN // IN_TN):
        cols = slice(c * IN_TN, (c + 1) * IN_TN)
        proj_ref[:, cols] = jnp.dot(h, w_ref[:, cols], preferred_element_type=F32).astype(BF16)
    h_scr[...] = _rms(x_ref[...], g_ref[...]).astype(BF16)


def _in_proj(x2d, g, w_main, w_dt):
    n = x2d.shape[0]
    ntiles = n // IN_TM
    return pl.pallas_call(
        _inproj_kernel,
        grid=(ntiles + 1,),
        in_specs=[
            pl.BlockSpec((IN_TM, D_MODEL), lambda s: (jnp.minimum(s, ntiles - 1), 0)),
            pl.BlockSpec((1, D_MODEL), lambda s: (0, 0)),
            pl.BlockSpec((D_MODEL, PROJ_MAIN), lambda s: (0, 0)),
            pl.BlockSpec((D_MODEL, LANES), lambda s: (0, 0)),
        ],
        out_specs=[
            pl.BlockSpec((IN_TM, PROJ_MAIN), lambda s: (jnp.maximum(s - 1, 0), 0)),
            pl.BlockSpec((IN_TM, LANES), lambda s: (jnp.maximum(s - 1, 0), 0)),
        ],
        out_shape=[
            jax.ShapeDtypeStruct((n, PROJ_MAIN), BF16),
            jax.ShapeDtypeStruct((n, LANES), F32),
        ],
        scratch_shapes=[pltpu.VMEM((IN_TM, D_MODEL), BF16)],
        compiler_params=pltpu.CompilerParams(
            dimension_semantics=("arbitrary",), vmem_limit_bytes=VMEM_LIMIT),
        name="in_proj",
    )(x2d, g, w_main, w_dt)


def _split3(a):
    a1 = a.astype(BF16)
    r1 = a - a1.astype(F32)
    a2 = r1.astype(BF16)
    a3 = (r1 - a2.astype(F32)).astype(BF16)
    return a1, a2, a3


def _dot3(parts, w):
    out = jnp.dot(parts[0], w, preferred_element_type=F32)
    for p in parts[1:]:
        out = out + jnp.dot(p, w, preferred_element_type=F32)
    return out


def _mixer_kernel(u_ref, v_ref, z_ref, xb0_ref, xb1_ref, dt_ref, ga_ref, gb_ref, x_ref,
                  lng_ref, lnb_ref, wsgu_ref, bsgu_ref, convw_ref, convb_ref, dtb_ref, acol_ref,
                  dskip_ref, normw_ref, eexp_ref, shift_ref, triu_ref, wa_ref, wb_ref, wo_ref,
                  x1_ref,
                  tail_scr, xc_scr, state_scr, ya_scr, yb_scr, yap_scr, ybp_scr, *, tps):
    nchunk = MIX_T // SSD_CHUNK
    s = pl.program_id(0)

    @pl.when(s % tps == 0)
    def _():
        tail_scr[...] = jnp.zeros_like(tail_scr)
        state_scr[...] = jnp.zeros_like(state_scr)

    @pl.when(s == 0)
    def _():
        yap_scr[...] = jnp.zeros_like(yap_scr)
        ybp_scr[...] = jnp.zeros_like(ybp_scr)

    xb_refs = (xb0_ref, xb1_ref)
    for c in range(nchunk):
        r0 = c * SSD_CHUNK
        for hf, xb_ref in enumerate(xb_refs):
            cols = slice(hf * PROJ_TILE, (hf + 1) * PROJ_TILE)
            cur = xb_ref[r0:r0 + SSD_CHUNK, :]
            prev = tail_scr[:, cols] if c == 0 else xb_ref[r0 - SSD_CHUNK:r0, :]
            sh = jnp.dot(shift_ref[...], jnp.concatenate([prev, cur], axis=0), preferred_element_type=F32)
            acc = convb_ref[:, cols] + convw_ref[SSM_CONV - 1:SSM_CONV, cols] * cur.astype(F32)
            for j in range(SSM_CONV - 1):
                acc = acc + convw_ref[j:j + 1, cols] * sh[j * SSD_CHUNK:(j + 1) * SSD_CHUNK, :]
            xc_scr[r0:r0 + SSD_CHUNK, cols] = acc * _sigmoid(acc)
    for hf, xb_ref in enumerate(xb_refs):
        tail_scr[:, hf * PROJ_TILE:(hf + 1) * PROJ_TILE] = xb_ref[MIX_T - SSD_CHUNK:MIX_T, :]

    row_i = lax.broadcasted_iota(I32, (SSD_CHUNK, SSD_CHUNK), 0)
    col_i = lax.broadcasted_iota(I32, (SSD_CHUNK, SSD_CHUNK), 1)
    tril = row_i >= col_i
    lane_hi = col_i >= SSM_HEAD_DIM
    eexp = eexp_ref[...]
    triu = triu_ref[...]

    for c in range(nchunk):
        rows = slice(c * SSD_CHUNK, (c + 1) * SSD_CHUNK)

        ug = _gelu_exact(u_ref[rows, :].astype(F32))
        vg = _gelu_exact(v_ref[rows, :].astype(F32))
        mu = jnp.mean(vg, axis=-1, keepdims=True)
        vc = vg - mu
        var = jnp.mean(vc * vc, axis=-1, keepdims=True)
        vn = (vc * lax.rsqrt(var + LN_EPS) * lng_ref[...] + lnb_ref[...]).astype(BF16)
        for g in range(SGU_HEADS):
            cols = slice(g * SGU_HEAD_DIM, (g + 1) * SGU_HEAD_DIM)
            mixed = jnp.dot(wsgu_ref[g], vn[:, cols], preferred_element_type=F32)
            ya_scr[rows, cols] = (ug[:, cols] * (mixed + bsgu_ref[:, cols])).astype(BF16)

        dt_t = _softplus(dt_ref[rows, :].T[0:SSM_HEADS, :] + dtb_ref[...])
        acs_t = _dot3(_split3(dt_t * acol_ref[...]), triu)
        w_t = jnp.exp(acs_t[:, SSD_CHUNK - 1:SSD_CHUNK] - acs_t) * dt_t
        acs = jnp.concatenate([acs_t, jnp.zeros((LANES - SSM_HEADS, SSD_CHUNK), F32)], axis=0).T
        aend = jnp.broadcast_to(acs[SSD_CHUNK - 1:SSD_CHUNK, :], (SUBLANES, LANES))
        dec = jnp.exp(_dot3(_split3(aend), eexp)[0:1, :])

        for gi in range(SSM_GROUPS):
            bg = xc_scr[rows, SSM_INNER + gi * SSM_STATE:SSM_INNER + (gi + 1) * SSM_STATE]
            cg = xc_scr[rows, SSM_INNER + (SSM_GROUPS + gi) * SSM_STATE:
                        SSM_INNER + (SSM_GROUPS + gi + 1) * SSM_STATE]
            bg_t = bg.T
            cb = jnp.dot(cg.astype(BF16), bg_t.astype(BF16), preferred_element_type=F32)
            for pr in range(2):
                lb = gi * 2 + pr
                cols = slice(lb * LANES, (lb + 1) * LANES)
                xpair = xc_scr[rows, cols]
                hpair = state_scr[:, cols]
                acc = jnp.zeros((SSD_CHUNK, LANES), F32)
                st = jnp.zeros((SSM_STATE, LANES), F32)
                for hh in range(2):
                    h = lb * 2 + hh
                    lmask = lane_hi if hh == 1 else jnp.logical_not(lane_hi)
                    xm = jnp.where(lmask, xpair, 0.0).astype(BF16)
                    hm = jnp.where(lmask, hpair, 0.0).astype(BF16)
                    colb = acs[:, h:h + 1]
                    rowb = acs_t[h:h + 1, :]
                    decay = jnp.exp(jnp.where(tril, colb - rowb, -jnp.inf))
                    m_h = (cb * decay * dt_t[h:h + 1, :]).astype(BF16)
                    c_h = (cg * jnp.exp(colb)).astype(BF16)
                    acc = acc + jnp.dot(m_h, xm, preferred_element_type=F32)
                    acc = acc + jnp.dot(c_h, hm, preferred_element_type=F32)
                    btw = (bg_t * w_t[h:h + 1, :]).astype(BF16)
                    st = st + jnp.dot(btw, xm, preferred_element_type=F32)
                state_scr[:, cols] = hpair * dec[:, cols] + st
                y = acc + dskip_ref[:, cols] * xpair
                zz = z_ref[rows, cols].astype(F32)
                xc_scr[rows, cols] = y * (zz * _sigmoid(zz))
            gcols = slice(gi * 2 * LANES, (gi + 1) * 2 * LANES)
            yg = xc_scr[rows, gcols]
            ms = jnp.mean(yg * yg, axis=-1, keepdims=True)
            yb_scr[rows, gcols] = (yg * lax.rsqrt(ms + RMS_EPS) * normw_ref[:, gcols]).astype(BF16)

    ma = jnp.dot(yap_scr[...], wa_ref[...], preferred_element_type=F32)
    mb = jnp.dot(ybp_scr[...], wb_ref[...], preferred_element_type=F32)
    merged = (_sigmoid(ga_ref[...].astype(F32)) * ma + _sigmoid(gb_ref[...].astype(F32)) * mb)
    x1_ref[...] = x_ref[...] + jnp.dot(merged.astype(BF16), wo_ref[...], preferred_element_type=F32)
    yap_scr[...] = ya_scr[...]
    ybp_scr[...] = yb_scr[...]


def _mixer(proj, dt_raw, x2d, bsz, seq, consts, wa, wb, wo):
    n = x2d.shape[0]
    tps = seq // MIX_T
    ntiles = bsz * tps
    cur = lambda s: jnp.minimum(s, ntiles - 1)
    prev = lambda s: jnp.maximum(s - 1, 0)

    def pcol(off, tile):
        return pl.BlockSpec((MIX_T, PROJ_TILE), lambda s: (tile(s), off // PROJ_TILE))

    def full(a):
        nd = a.ndim
        return pl.BlockSpec(a.shape, lambda s, nd=nd: (0,) * nd)

    in_specs = [pcol(OFF_U, cur), pcol(OFF_V, cur), pcol(OFF_Z, cur),
                pcol(OFF_XBC, cur), pcol(OFF_XBC + PROJ_TILE, cur),
                pl.BlockSpec((MIX_T, LANES), lambda s: (cur(s), 0)),
                pcol(OFF_GA - SSM_HEADS, prev), pcol(OFF_GB - SSM_HEADS, prev),
                pl.BlockSpec((MIX_T, D_MODEL), lambda s: (prev(s), 0))]
    in_specs += [full(a) for a in consts] + [full(wa), full(wb), full(wo)]
    return pl.pallas_call(
        functools.partial(_mixer_kernel, tps=tps),
        grid=(ntiles + 1,),
        in_specs=in_specs,
        out_specs=pl.BlockSpec((MIX_T, D_MODEL), lambda s: (prev(s), 0)),
        out_shape=jax.ShapeDtypeStruct((n, D_MODEL), F32),
        scratch_shapes=[
            pltpu.VMEM((SSD_CHUNK, SSM_CONV_DIM), BF16),
            pltpu.VMEM((MIX_T, SSM_CONV_DIM), F32),
            pltpu.VMEM((SSM_STATE, SSM_INNER), F32),
            pltpu.VMEM((MIX_T, SGU_WIDTH), BF16),
            pltpu.VMEM((MIX_T, SSM_INNER), BF16),
            pltpu.VMEM((MIX_T, SGU_WIDTH), BF16),
            pltpu.VMEM((MIX_T, SSM_INNER), BF16),
        ],
        compiler_params=pltpu.CompilerParams(
            dimension_semantics=("arbitrary",), vmem_limit_bytes=VMEM_LIMIT),
        name="mixer",
    )(proj, proj, proj, proj, proj, dt_raw, proj, proj, x2d, *consts, wa, wb, wo)


def _router_kernel(x_ref, g_ref, wr_ref, br_ref, upper_ref, hp_ref, idx_ref, gate_ref, rank_ref, cnt_ref,
                   carry_scr):
    @pl.when(pl.program_id(0) == 0)
    def _():
        carry_scr[...] = jnp.zeros_like(carry_scr)

    h = _rms(x_ref[...], g_ref[...])
    _store_token_tiles(hp_ref, h, ROUTE_T)

    h_hi = h.astype(BF16)
    h_lo = (h - h_hi.astype(F32)).astype(BF16)
    logits = (jnp.dot(h_hi, wr_ref[0], preferred_element_type=F32)
              + jnp.dot(h_hi, wr_ref[1], preferred_element_type=F32)
              + jnp.dot(h_lo, wr_ref[0], preferred_element_type=F32))
    lt = logits.T[0:N_EXPERTS, :] + br_ref[...]
    eidx = lax.broadcasted_iota(I32, (N_EXPERTS, ROUTE_T), 0).astype(F32)
    vals = lt
    sel_any = jnp.zeros((N_EXPERTS, ROUTE_T), F32)
    sels, tops = [], []
    for k in range(TOP_K):
        m = jnp.max(vals, axis=0, keepdims=True)
        first = jnp.min(jnp.where(vals == m, eidx, float(N_EXPERTS)), axis=0, keepdims=True)
        sel = eidx == first
        vals = jnp.where(sel, -jnp.inf, vals)
        sel_f = sel.astype(F32)
        sel_any = sel_any + sel_f
        sels.append(sel_f)
        tops.append(m)
        idx_ref[k:k + 1, :] = first.astype(I32)
    es = [jnp.exp(t - tops[0]) for t in tops]
    denom = functools.reduce(lambda a, b: a + b, es)
    for k in range(TOP_K):
        gate_ref[k:k + 1, :] = es[k] / denom
    gate_ref[TOP_K:, :] = jnp.zeros((SUBLANES - TOP_K, ROUTE_T), F32)

    excl = jnp.dot(sel_any.astype(BF16), upper_ref[...], preferred_element_type=F32) + carry_scr[:, 0:1]
    for k in range(TOP_K):
        rank_ref[k:k + 1, :] = jnp.sum(sels[k] * excl, axis=0, keepdims=True).astype(I32)
    new_carry = carry_scr[...] + jnp.sum(sel_any, axis=1, keepdims=True)
    carry_scr[...] = new_carry
    cnt_ref[...] = new_carry.astype(I32)


def _router(x1, g, wr_pad, br_col, upper):
    n = x1.shape[0]
    per_k = pl.BlockSpec((TOP_K, ROUTE_T), lambda i: (0, i))
    return pl.pallas_call(
        _router_kernel,
        grid=(n // ROUTE_T,),
        in_specs=[
            pl.BlockSpec((ROUTE_T, D_MODEL), lambda i: (i, 0)),
            pl.BlockSpec((1, D_MODEL), lambda i: (0, 0)),
            pl.BlockSpec((2, D_MODEL, LANES), lambda i: (0, 0, 0)),
            pl.BlockSpec((N_EXPERTS, 1), lambda i: (0, 0)),
            pl.BlockSpec((ROUTE_T, ROUTE_T), lambda i: (0, 0)),
        ],
        out_specs=[
            pl.BlockSpec((ROUTE_T * TOK_SUB, LANES), lambda i: (i, 0)),
            per_k,
            pl.BlockSpec((SUBLANES, ROUTE_T), lambda i: (0, i)),
            per_k,
            pl.BlockSpec((N_EXPERTS, LANES), lambda i: (0, 0)),
        ],
        out_shape=[
            jax.ShapeDtypeStruct((n * TOK_SUB, LANES), F32),
            jax.ShapeDtypeStruct((TOP_K, n), I32),
            jax.ShapeDtypeStruct((SUBLANES, n), F32),
            jax.ShapeDtypeStruct((TOP_K, n), I32),
            jax.ShapeDtypeStruct((N_EXPERTS, LANES), I32),
        ],
        scratch_shapes=[pltpu.VMEM((N_EXPERTS, LANES), F32)],
        compiler_params=pltpu.CompilerParams(
            dimension_semantics=("arbitrary",), vmem_limit_bytes=VMEM_LIMIT),
        name="router",
    )(x1, g, wr_pad, br_col, upper)


def _dispatch_kernel(dest_ref, pad_ref, hp_ref, xs_hbm, zero_scr, sem_rows):
    i = pl.program_id(0)
    tbl = i * (TOP_K * DISP_T)

    def zero_copy(s):
        return pltpu.make_async_copy(zero_scr.at[0:TOK_SUB, :], xs_hbm.at[_tile_rows(s), :], sem_rows)

    def zero_run(s):
        rows = pl.ds(pl.multiple_of(s * TOK_SUB, TOK_SUB), ZERO_RUN * TOK_SUB)
        return pltpu.make_async_copy(zero_scr, xs_hbm.at[rows, :], sem_rows)

    def zero_pads(act):
        for e in range(N_EXPERTS):
            lo, hi = pad_ref[e], pad_ref[N_EXPERTS + e]
            nrun = (hi - lo) // ZERO_RUN
            lax.fori_loop(0, nrun, lambda r, c: (act(zero_run(lo + r * ZERO_RUN)), c)[1], 0)
            lax.fori_loop(lo + nrun * ZERO_RUN, hi, lambda s, c: (act(zero_copy(s)), c)[1], 0)

    @pl.when(i == 0)
    def _():
        zero_scr[...] = jnp.zeros_like(zero_scr)
        zero_pads(lambda d: d.start())
        zero_pads(lambda d: d.wait())

    def tile_copy(t, k):
        d = dest_ref[tbl + k * DISP_T + t]
        return pltpu.make_async_copy(hp_ref.at[_tile_rows(t), :], xs_hbm.at[_tile_rows(d), :], sem_rows)

    def issue(tb, carry):
        for r in range(DMA_UNROLL):
            for k in range(TOP_K):
                tile_copy(tb * DMA_UNROLL + r, k).start(priority=(r * TOP_K + k) % 2)
        return carry

    lax.fori_loop(0, DISP_T // DMA_UNROLL, issue, 0)

    def drain(tb, carry):
        for _ in range(DMA_UNROLL * TOP_K):
            zero_copy(0).wait()
        return carry

    lax.fori_loop(0, DISP_T // DMA_UNROLL, drain, 0)


def _dispatch(dest_flat, pad_tbl, hp, cap):
    n = hp.shape[0] // TOK_SUB
    grid_spec = pltpu.PrefetchScalarGridSpec(
        num_scalar_prefetch=2,
        grid=(n // DISP_T,),
        in_specs=[pl.BlockSpec((DISP_T * TOK_SUB, LANES), lambda i, d, p: (i, 0))],
        out_specs=pl.BlockSpec(memory_space=pl.ANY),
        scratch_shapes=[pltpu.VMEM((ZERO_RUN * TOK_SUB, LANES), F32), pltpu.SemaphoreType.DMA],
    )
    return pl.pallas_call(
        _dispatch_kernel,
        grid_spec=grid_spec,
        out_shape=jax.ShapeDtypeStruct((cap * TOK_SUB, LANES), F32),
        compiler_params=pltpu.CompilerParams(dimension_semantics=("arbitrary",)),
        name="dispatch",
    )(dest_flat, pad_tbl, hp)


def _expert_kernel(be_ref, nxt_ref, nv_ref, xs_ref, w1_hbm, b1_ref, w2_hbm, b2_ref, ys_ref,
                   w1f_scr, w2f_scr, w1b_scr, w2b_scr, sems):
    b = pl.program_id(0)
    valid = nv_ref[b]
    used = valid > 0
    new_expert = jnp.logical_or(b == 0, be_ref[b] != be_ref[jnp.maximum(b - 1, 0)])

    def fetch(e):
        return (pltpu.make_async_copy(w1_hbm.at[e], w1f_scr, sems.at[0]),
                pltpu.make_async_copy(w2_hbm.at[e], w2f_scr, sems.at[1]))

    @pl.when(jnp.logical_and(used, b == 0))
    def _():
        for c in fetch(be_ref[0]):
            c.start()

    @pl.when(jnp.logical_and(used, new_expert))
    def _():
        for c in fetch(be_ref[b]):
            c.wait()

        def cast(rb, carry):
            rows = pl.ds(pl.multiple_of(rb * CAST_ROWS, CAST_ROWS), CAST_ROWS)
            w1b_scr[rows, :] = w1f_scr[rows, :].astype(BF16)
            w2b_scr[rows, :] = w2f_scr[rows, :].astype(BF16)
            return carry

        lax.fori_loop(0, D_MODEL // CAST_ROWS, cast, 0)

        @pl.when(nxt_ref[b] >= 0)
        def _():
            for c in fetch(nxt_ref[b]):
                c.start()

    def sub_block(sub):
        x = _load_token_tiles(xs_ref, MOE_SUB, sub * MOE_SUB).astype(BF16)
        hid = jnp.dot(x, w1b_scr[...], preferred_element_type=F32) + b1_ref[0]
        glu = jnp.minimum(hid[:, :D_FF_EXPERT], SWIGLU_LIMIT)
        lin = jnp.clip(hid[:, D_FF_EXPERT:], -SWIGLU_LIMIT, SWIGLU_LIMIT)
        act = glu * _sigmoid(SWIGLU_ALPHA * glu) * (lin + 1.0)
        y = jnp.dot(act.astype(BF16), w2b_scr[...], preferred_element_type=F32) + b2_ref[0]
        _store_token_tiles(ys_ref, y, MOE_SUB, sub * MOE_SUB)

    live_subs = (valid + MOE_SUB - 1) // MOE_SUB
    for nsub in range(1, MOE_BLOCK // MOE_SUB + 1):
        @pl.when(live_subs == nsub)
        def _(nsub=nsub):
            for sub in range(nsub):
                sub_block(sub)


def _experts(blk_expert, blk_next, blk_valid, xs, w1, b1, w2, b2):
    cap = xs.shape[0] // TOK_SUB
    n_blocks = cap // MOE_BLOCK
    grid_spec = pltpu.PrefetchScalarGridSpec(
        num_scalar_prefetch=3,
        grid=(n_blocks,),
        in_specs=[
            pl.BlockSpec((MOE_BLOCK * TOK_SUB, LANES), lambda b, be, nx, nv: (b, 0)),
            pl.BlockSpec(memory_space=pl.ANY),
            pl.BlockSpec((1, 1, 2 * D_FF_EXPERT), lambda b, be, nx, nv: (be[b], 0, 0)),
            pl.BlockSpec(memory_space=pl.ANY),
            pl.BlockSpec((1, 1, D_MODEL), lambda b, be, nx, nv: (be[b], 0, 0)),
        ],
        out_specs=pl.BlockSpec((MOE_BLOCK * TOK_SUB, LANES), lambda b, be, nx, nv: (b, 0)),
        scratch_shapes=[pltpu.VMEM((D_MODEL, 2 * D_FF_EXPERT), F32),
                        pltpu.VMEM((D_FF_EXPERT, D_MODEL), F32),
                        pltpu.VMEM((D_MODEL, 2 * D_FF_EXPERT), BF16),
                        pltpu.VMEM((D_FF_EXPERT, D_MODEL), BF16),
                        pltpu.SemaphoreType.DMA((2,))],
    )
    assert D_MODEL == D_FF_EXPERT
    return pl.pallas_call(
        _expert_kernel,
        grid_spec=grid_spec,
        out_shape=jax.ShapeDtypeStruct((cap * TOK_SUB, LANES), F32),
        compiler_params=pltpu.CompilerParams(
            dimension_semantics=("arbitrary",), vmem_limit_bytes=VMEM_LIMIT),
        name="experts",
    )(blk_expert, blk_next, blk_valid, xs, w1, b1, w2, b2)


def _combine_kernel(dest_ref, ys_hbm, x1_ref, gate_ref, p_ref, pg_ref, wg_ref, wp_ref, fg_ref,
                    out_ref, ybuf, sems, *, final):
    i = pl.program_id(0)
    nsteps = pl.num_programs(0)
    buf_tokens = TOP_K * COMB_T

    def issue_all(step, slot):
        tbl = step * buf_tokens

        def issue(tb, carry):
            for r in range(DMA_UNROLL):
                for k in range(TOP_K):
                    row = k * COMB_T + tb * DMA_UNROLL + r
                    pltpu.make_async_copy(
                        ys_hbm.at[_tile_rows(dest_ref[tbl + row]), :],
                        ybuf.at[_tile_rows(slot * buf_tokens + row), :],
                        sems.at[slot]).start(priority=(r * TOP_K + k) % 2)
            return carry

        lax.fori_loop(0, COMB_T // DMA_UNROLL, issue, 0)

    def drain_all(slot):
        def drain(tb, carry):
            for _ in range(DMA_UNROLL * TOP_K):
                pltpu.make_async_copy(ys_hbm.at[_tile_rows(0), :],
                                      ybuf.at[_tile_rows(slot * buf_tokens), :], sems.at[slot]).wait()
            return carry

        lax.fori_loop(0, COMB_T // DMA_UNROLL, drain, 0)

    @pl.when(i == 0)
    def _():
        issue_all(0, 0)

    for slot in range(2):
        @pl.when(jnp.logical_and(i % 2 == slot, i + 1 < nsteps))
        def _():
            issue_all(i + 1, 1 - slot)

    for slot in range(2):
        @pl.when(i % 2 == slot)
        def _():
            drain_all(slot)

            for sub in range(COMB_T // COMB_SUB):
                rows = slice(sub * COMB_SUB, (sub + 1) * COMB_SUB)
                x2 = x1_ref[rows, :]
                for k in range(TOP_K):
                    x2 = x2 + gate_ref[rows, k:k + 1] * _load_token_tiles(
                        ybuf, COMB_SUB, slot * buf_tokens + k * COMB_T + sub * COMB_SUB)
                hp = _rms(x2, pg_ref[...]).astype(BF16)
                gate = _sigmoid(jnp.dot(hp, wg_ref[...], preferred_element_type=F32))
                emb = jnp.dot(p_ref[rows, :].astype(BF16), wp_ref[...], preferred_element_type=F32)
                x3 = x2 + gate * emb
                if final:
                    x3 = _rms(x3, fg_ref[...])
                out_ref[rows, :] = x3


def _combine(dest_flat, ys, x1, gates_t, p2d, pg, wg, wp, fg, final):
    n = x1.shape[0]
    grid_spec = pltpu.PrefetchScalarGridSpec(
        num_scalar_prefetch=1,
        grid=(n // COMB_T,),
        in_specs=[
            pl.BlockSpec(memory_space=pl.ANY),
            pl.BlockSpec((COMB_T, D_MODEL), lambda i, d: (i, 0)),
            pl.BlockSpec((COMB_T, SUBLANES), lambda i, d: (i, 0)),
            pl.BlockSpec((COMB_T, PLE_DIM), lambda i, d: (i, 0)),
            pl.BlockSpec((1, D_MODEL), lambda i, d: (0, 0)),
            pl.BlockSpec((D_MODEL, D_MODEL), lambda i, d: (0, 0)),
            pl.BlockSpec((PLE_DIM, D_MODEL), lambda i, d: (0, 0)),
            pl.BlockSpec((1, D_MODEL), lambda i, d: (0, 0)),
        ],
        out_specs=pl.BlockSpec((COMB_T, D_MODEL), lambda i, d: (i, 0)),
        scratch_shapes=[pltpu.VMEM((2 * TOP_K * COMB_T * TOK_SUB, LANES), F32),
                        pltpu.SemaphoreType.DMA((2,))],
    )
    return pl.pallas_call(
        functools.partial(_combine_kernel, final=final),
        grid_spec=grid_spec,
        out_shape=jax.ShapeDtypeStruct((n, D_MODEL), F32),
        compiler_params=pltpu.CompilerParams(
            dimension_semantics=("arbitrary",), vmem_limit_bytes=VMEM_LIMIT),
        name="combine",
    )(dest_flat, ys, x1, gates_t, p2d, pg, wg, wp, fg)


def _layer(x2d, p2d, bsz, seq, mix_norm, w_in, sgu_ln_g, sgu_ln_b, sgu_w, sgu_b, conv_w, conv_b,
           dt_bias, a_log, d_skip, ssm_norm, w_branch_a, w_branch_b, w_out, ffn_norm, w_router,
           b_router, w1, b1, w2, b2, ple_norm, w_ple_gate, w_ple_proj, final_norm, final):
    n = x2d.shape[0]
    row = lambda a: a.reshape(1, -1).astype(F32)

    w_main = jnp.concatenate([w_in[:, :OFF_DT], w_in[:, OFF_GA:]], axis=1).astype(BF16)
    w_dt = jnp.pad(w_in[:, OFF_DT:OFF_GA], ((0, 0), (0, LANES - SSM_HEADS))).astype(BF16)
    pos = jnp.arange(SGU_LEN)
    allowed = (pos[None, :] // CHUNK) <= (pos[:, None] // CHUNK)
    wsgu = jnp.where(allowed[None], sgu_w, 0.0).astype(BF16)
    bsgu = jnp.repeat(sgu_b.T, SGU_HEAD_DIM, axis=1).astype(F32)
    dtb = dt_bias.reshape(-1, 1).astype(F32)
    acol = -jnp.exp(a_log.astype(F32)).reshape(-1, 1)
    dskip = jnp.repeat(d_skip.astype(F32), SSM_HEAD_DIM).reshape(1, -1)
    head_of_col = jnp.arange(SSM_INNER) // SSM_HEAD_DIM
    eexp = (jnp.arange(LANES)[:, None] == head_of_col[None, :]).astype(BF16)
    triu = (jnp.arange(SSD_CHUNK)[:, None] <= jnp.arange(SSD_CHUNK)[None, :]).astype(BF16)
    t_out = jnp.arange((SSM_CONV - 1) * SSD_CHUNK)
    src = SSD_CHUNK + t_out % SSD_CHUNK - (SSM_CONV - 1) + t_out // SSD_CHUNK
    shift = (src[:, None] == jnp.arange(2 * SSD_CHUNK)[None, :]).astype(BF16)
    consts = [row(sgu_ln_g), row(sgu_ln_b), wsgu, bsgu, conv_w.astype(F32), row(conv_b), dtb, acol,
              dskip, row(ssm_norm), eexp, shift, triu]

    proj, dt_raw = _in_proj(x2d, row(mix_norm), w_main, w_dt)
    x1 = _mixer(proj, dt_raw, x2d, bsz, seq, consts, w_branch_a.astype(BF16),
                w_branch_b.astype(BF16), w_out.astype(BF16))

    wr_f = jnp.pad(w_router.astype(F32), ((0, 0), (0, LANES - N_EXPERTS)))
    wr_hi = wr_f.astype(BF16)
    wr_pad = jnp.stack([wr_hi, (wr_f - wr_hi.astype(F32)).astype(BF16)])
    upper = (jnp.arange(ROUTE_T)[:, None] < jnp.arange(ROUTE_T)[None, :]).astype(BF16)
    hp, idx, gates, rank, cnt = _router(x1, row(ffn_norm), wr_pad, b_router.reshape(-1, 1).astype(F32),
                                        upper)

    counts = cnt[:, 0]
    padded = (counts + MOE_BLOCK - 1) // MOE_BLOCK * MOE_BLOCK
    pend = jnp.cumsum(padded)
    pstart = pend - padded
    nk = n * TOP_K
    cap = (nk + MOE_BLOCK - 1) // MOE_BLOCK * MOE_BLOCK + N_EXPERTS * MOE_BLOCK
    n_blocks = cap // MOE_BLOCK
    ex = jnp.arange(N_EXPERTS, dtype=I32)
    dest = rank + jnp.sum(jnp.where(idx[..., None] == ex, pstart.astype(I32), 0), axis=-1)
    blk_start = jnp.arange(n_blocks, dtype=I32) * MOE_BLOCK
    blk_expert = jnp.minimum(jnp.sum((pend[None, :] <= blk_start[:, None]).astype(I32), axis=1),
                             N_EXPERTS - 1).astype(I32)
    of_block = blk_expert[:, None] == ex[None, :]
    per_block = lambda v: jnp.sum(jnp.where(of_block, v[None, :].astype(I32), 0), axis=1)
    blk_valid = jnp.where(blk_start < pend[-1],
                          jnp.clip(per_block(pstart + counts) - blk_start, 0, MOE_BLOCK), 0).astype(I32)
    live = jnp.where(counts > 0, ex, N_EXPERTS)
    later = jnp.concatenate([lax.cummin(live[::-1])[::-1][1:], jnp.full((1,), N_EXPERTS, I32)])
    blk_next = per_block(jnp.where(later < N_EXPERTS, later, -1))

    dest_flat = dest.reshape(TOP_K, n // DISP_T, DISP_T).transpose(1, 0, 2).reshape(-1)
    used_end = pstart + (counts + MOE_SUB - 1) // MOE_SUB * MOE_SUB
    pad_tbl = jnp.concatenate([pstart + counts, used_end]).astype(I32)

    xs = _dispatch(dest_flat, pad_tbl, hp, cap)
    ys = _experts(blk_expert, blk_next, blk_valid, xs, w1.astype(F32),
                  b1.reshape(N_EXPERTS, 1, -1).astype(F32), w2.astype(F32),
                  b2.reshape(N_EXPERTS, 1, -1).astype(F32))
    return _combine(dest_flat, ys, x1, gates.T, p2d, row(ple_norm), w_ple_gate.astype(BF16),
                    w_ple_proj.astype(BF16), row(final_norm), final)


def kernel(x, p, mix_norm, w_in, sgu_ln_g, sgu_ln_b, sgu_w, sgu_b, conv_w, conv_b, dt_bias, a_log,
           d_skip, ssm_norm, w_branch_a, w_branch_b, w_out, ffn_norm, w_router, b_router, w1, b1,
           w2, b2, ple_norm, w_ple_gate, w_ple_proj, final_norm):
    bsz, seq, d = x.shape
    depth = w_in.shape[0]
    assert d == D_MODEL and seq % MIX_T == 0 and (bsz * seq) % max(IN_TM, ROUTE_T, DISP_T) == 0
    assert ROUTE_T == DISP_T == COMB_T
    x2d = x.reshape(bsz * seq, d)
    for i in range(depth):
        x2d = _layer(x2d, p[i].reshape(bsz * seq, PLE_DIM), bsz, seq, mix_norm[i], w_in[i],
                     sgu_ln_g[i], sgu_ln_b[i], sgu_w[i], sgu_b[i], conv_w[i], conv_b[i], dt_bias[i],
                     a_log[i], d_skip[i], ssm_norm[i], w_branch_a[i], w_branch_b[i], w_out[i],
                     ffn_norm[i], w_router[i], b_router[i], w1[i], b1[i], w2[i], b2[i], ple_norm[i],
                     w_ple_gate[i], w_ple_proj[i], final_norm, final=(i == depth - 1))
    return x2d.reshape(bsz, seq, d)
```

```python
import functools
import math

import jax
import jax.numpy as jnp
from jax import lax
from jax.experimental import pallas as pl
from jax.experimental.pallas import tpu as pltpu

F32 = jnp.float32
BF16 = jnp.bfloat16
I32 = jnp.int32

D_MODEL = 1024
CHUNK = 64
PLE_DIM = 256
RMS_EPS = 1e-6
LN_EPS = 1e-5

SGU_HEADS = 8
SGU_HEAD_DIM = 128
SGU_WIDTH = SGU_HEADS * SGU_HEAD_DIM
SGU_LEN = 128

SSM_HEADS = 16
SSM_HEAD_DIM = 64
SSM_INNER = SSM_HEADS * SSM_HEAD_DIM
SSM_GROUPS = 4
SSM_STATE = 128
SSM_CONV = 4
SSD_CHUNK = 128
SSM_CONV_DIM = SSM_INNER + 2 * SSM_GROUPS * SSM_STATE

N_EXPERTS = 32
TOP_K = 4
D_FF_EXPERT = 1024
SWIGLU_LIMIT = 7.0
SWIGLU_ALPHA = 1.702
MOE_BLOCK = 512
MOE_SUB = 256

OFF_U = 0
OFF_V = OFF_U + SGU_WIDTH
OFF_Z = OFF_V + SGU_WIDTH
OFF_XBC = OFF_Z + SSM_INNER
OFF_DT = OFF_XBC + SSM_CONV_DIM
OFF_GA = OFF_DT + SSM_HEADS
OFF_GB = OFF_GA + D_MODEL
IN_PROJ_DIM = OFF_GB + D_MODEL

LANES = 128
SUBLANES = 8
PROJ_MAIN = IN_PROJ_DIM - SSM_HEADS
PROJ_TILE = 1024
TOK_SUB = D_MODEL // LANES

VMEM_LIMIT = 56 * 1024 * 1024

IN_TM = 512
IN_TN = 1792
MIX_T = 512
ROUTE_T = 512
DISP_T = 512
COMB_T = DISP_T
COMB_SUB = 128
DMA_UNROLL = 8
ZERO_RUN = 64
CAST_ROWS = 128


def _sigmoid(x):
    return 1.0 / (1.0 + jnp.exp(-x))


def _gelu_exact(x):
    return 0.5 * x * (1.0 + lax.erf(x * (1.0 / math.sqrt(2.0))))


def _softplus(x):
    return jnp.maximum(x, 0.0) + jnp.log1p(jnp.exp(-jnp.abs(x)))


def _rms(x, g):
    ms = jnp.mean(x * x, axis=-1, keepdims=True)
    return x * lax.rsqrt(ms + RMS_EPS) * g


def _store_token_tiles(ref, val, rows, start=0):
    for j in range(TOK_SUB):
        ref[pl.ds(start * TOK_SUB + j, rows, stride=TOK_SUB), :] = val[:, j * LANES:(j + 1) * LANES]


def _load_token_tiles(ref, rows, start=0):
    return jnp.concatenate(
        [ref[pl.ds(start * TOK_SUB + j, rows, stride=TOK_SUB), :] for j in range(TOK_SUB)], axis=1)


def _tile_rows(idx):
    return pl.ds(pl.multiple_of(idx * TOK_SUB, TOK_SUB), TOK_SUB)


def _inproj_kernel(x_ref, g_ref, w_ref, wdt_ref, proj_ref, dt_ref, h_scr):
    @pl.when(pl.program_id(0) == 0)
    def _():
        h_scr[...] = jnp.zeros_like(h_scr)

    h = h_scr[...]
    dt_ref[...] = jnp.dot(h, wdt_ref[...], preferred_element_type=F32)
    for c in range(PROJ_MAIN // IN_TN):
        cols = slice(c * IN_TN, (c + 1) * IN_TN)
        proj_ref[:, cols] = jnp.dot(h, w_ref[:, cols], preferred_element_type=F32).astype(BF16)
    h_scr[...] = _rms(x_ref[...], g_ref[...]).astype(BF16)


def _in_proj(x2d, g, w_main, w_dt):
    n = x2d.shape[0]
    ntiles = n // IN_TM
    return pl.pallas_call(
        _inproj_kernel,
        grid=(ntiles + 1,),
        in_specs=[
            pl.BlockSpec((IN_TM, D_MODEL), lambda s: (jnp.minimum(s, ntiles - 1), 0)),
            pl.BlockSpec((1, D_MODEL), lambda s: (0, 0)),
            pl.BlockSpec((D_MODEL, PROJ_MAIN), lambda s: (0, 0)),
            pl.BlockSpec((D_MODEL, LANES), lambda s: (0, 0)),
        ],
        out_specs=[
            pl.BlockSpec((IN_TM, PROJ_MAIN), lambda s: (jnp.maximum(s - 1, 0), 0)),
            pl.BlockSpec((IN_TM, LANES), lambda s: (jnp.maximum(s - 1, 0), 0)),
        ],
        out_shape=[
            jax.ShapeDtypeStruct((n, PROJ_MAIN), BF16),
            jax.ShapeDtypeStruct((n, LANES), F32),
        ],
        scratch_shapes=[pltpu.VMEM((IN_TM, D_MODEL), BF16)],
        compiler_params=pltpu.CompilerParams(
            dimension_semantics=("arbitrary",), vmem_limit_bytes=VMEM_LIMIT),
        name="in_proj",
    )(x2d, g, w_main, w_dt)


def _split3(a):
    a1 = a.astype(BF16)
    r1 = a - a1.astype(F32)
    a2 = r1.astype(BF16)
    a3 = (r1 - a2.astype(F32)).astype(BF16)
    return a1, a2, a3


def _dot3(parts, w):
    out = jnp.dot(parts[0], w, preferred_element_type=F32)
    for p in parts[1:]:
        out = out + jnp.dot(p, w, preferred_element_type=F32)
    return out


def _mixer_kernel(u_ref, v_ref, z_ref, xb0_ref, xb1_ref, dt_ref, ga_ref, gb_ref, x_ref,
                  lng_ref, lnb_ref, wsgu_ref, bsgu_ref, convw_ref, convb_ref, dtb_ref, acol_ref,
                  dskip_ref, normw_ref, eexp_ref, shift_ref, triu_ref, wa_ref, wb_ref, wo_ref,
                  x1_ref,
                  tail_scr, xc_scr, state_scr, ya_scr, yb_scr, yap_scr, ybp_scr, *, tps):
    nchunk = MIX_T // SSD_CHUNK
    s = pl.program_id(0)

    @pl.when(s % tps == 0)
    def _():
        tail_scr[...] = jnp.zeros_like(tail_scr)
        state_scr[...] = jnp.zeros_like(state_scr)

    @pl.when(s == 0)
    def _():
        yap_scr[...] = jnp.zeros_like(yap_scr)
        ybp_scr[...] = jnp.zeros_like(ybp_scr)

    xb_refs = (xb0_ref, xb1_ref)
    for c in range(nchunk):
        r0 = c * SSD_CHUNK
        for hf, xb_ref in enumerate(xb_refs):
            cols = slice(hf * PROJ_TILE, (hf + 1) * PROJ_TILE)
            cur = xb_ref[r0:r0 + SSD_CHUNK, :]
            prev = tail_scr[:, cols] if c == 0 else xb_ref[r0 - SSD_CHUNK:r0, :]
            sh = jnp.dot(shift_ref[...], jnp.concatenate([prev, cur], axis=0), preferred_element_type=F32)
            acc = convb_ref[:, cols] + convw_ref[SSM_CONV - 1:SSM_CONV, cols] * cur.astype(F32)
            for j in range(SSM_CONV - 1):
                acc = acc + convw_ref[j:j + 1, cols] * sh[j * SSD_CHUNK:(j + 1) * SSD_CHUNK, :]
            xc_scr[r0:r0 + SSD_CHUNK, cols] = acc * _sigmoid(acc)
    for hf, xb_ref in enumerate(xb_refs):
        tail_scr[:, hf * PROJ_TILE:(hf + 1) * PROJ_TILE] = xb_ref[MIX_T - SSD_CHUNK:MIX_T, :]

    row_i = lax.broadcasted_iota(I32, (SSD_CHUNK, SSD_CHUNK), 0)
    col_i = lax.broadcasted_iota(I32, (SSD_CHUNK, SSD_CHUNK), 1)
    tril = row_i >= col_i
    lane_hi = col_i >= SSM_HEAD_DIM
    eexp = eexp_ref[...]
    triu = triu_ref[...]

    for c in range(nchunk):
        rows = slice(c * SSD_CHUNK, (c + 1) * SSD_CHUNK)

        ug = _gelu_exact(u_ref[rows, :].astype(F32))
        vg = _gelu_exact(v_ref[rows, :].astype(F32))
        mu = jnp.mean(vg, axis=-1, keepdims=True)
        vc = vg - mu
        var = jnp.mean(vc * vc, axis=-1, keepdims=True)
        vn = (vc * lax.rsqrt(var + LN_EPS) * lng_ref[...] + lnb_ref[...]).astype(BF16)
        for g in range(SGU_HEADS):
            cols = slice(g * SGU_HEAD_DIM, (g + 1) * SGU_HEAD_DIM)
            mixed = jnp.dot(wsgu_ref[g], vn[:, cols], preferred_element_type=F32)
            ya_scr[rows, cols] = (ug[:, cols] * (mixed + bsgu_ref[:, cols])).astype(BF16)

        dt_t = _softplus(dt_ref[rows, :].T[0:SSM_HEADS, :] + dtb_ref[...])
        acs_t = _dot3(_split3(dt_t * acol_ref[...]), triu)
        w_t = jnp.exp(acs_t[:, SSD_CHUNK - 1:SSD_CHUNK] - acs_t) * dt_t
        acs = jnp.concatenate([acs_t, jnp.zeros((LANES - SSM_HEADS, SSD_CHUNK), F32)], axis=0).T
        aend = jnp.broadcast_to(acs[SSD_CHUNK - 1:SSD_CHUNK, :], (SUBLANES, LANES))
        dec = jnp.exp(_dot3(_split3(aend), eexp)[0:1, :])

        for gi in range(SSM_GROUPS):
            bg = xc_scr[rows, SSM_INNER + gi * SSM_STATE:SSM_INNER + (gi + 1) * SSM_STATE]
            cg = xc_scr[rows, SSM_INNER + (SSM_GROUPS + gi) * SSM_STATE:
                        SSM_INNER + (SSM_GROUPS + gi + 1) * SSM_STATE]
            bg_t = bg.T
            cb = jnp.dot(cg.astype(BF16), bg_t.astype(BF16), preferred_element_type=F32)
            for pr in range(2):
                lb = gi * 2 + pr
                cols = slice(lb * LANES, (lb + 1) * LANES)
                xpair = xc_scr[rows, cols]
                hpair = state_scr[:, cols]
                acc = jnp.zeros((SSD_CHUNK, LANES), F32)
                st = jnp.zeros((SSM_STATE, LANES), F32)
                for hh in range(2):
                    h = lb * 2 + hh
                    lmask = lane_hi if hh == 1 else jnp.logical_not(lane_hi)
                    xm = jnp.where(lmask, xpair, 0.0).astype(BF16)
                    hm = jnp.where(lmask, hpair, 0.0).astype(BF16)
                    colb = acs[:, h:h + 1]
                    rowb = acs_t[h:h + 1, :]
                    decay = jnp.exp(jnp.where(tril, colb - rowb, -jnp.inf))
                    m_h = (cb * decay * dt_t[h:h + 1, :]).astype(BF16)
                    c_h = (cg * jnp.exp(colb)).astype(BF16)
                    acc = acc + jnp.dot(m_h, xm, preferred_element_type=F32)
                    acc = acc + jnp.dot(c_h, hm, preferred_element_type=F32)
                    btw = (bg_t * w_t[h:h + 1, :]).astype(BF16)
                    st = st + jnp.dot(btw, xm, preferred_element_type=F32)
                state_scr[:, cols] = hpair * dec[:, cols] + st
                y = acc + dskip_ref[:, cols] * xpair
                zz = z_ref[rows, cols].astype(F32)
                xc_scr[rows, cols] = y * (zz * _sigmoid(zz))
            gcols = slice(gi * 2 * LANES, (gi + 1) * 2 * LANES)
            yg = xc_scr[rows, gcols]
            ms = jnp.mean(yg * yg, axis=-1, keepdims=True)
            yb_scr[rows, gcols] = (yg * lax.rsqrt(ms + RMS_EPS) * normw_ref[:, gcols]).astype(BF16)

    ma = jnp.dot(yap_scr[...], wa_ref[...], preferred_element_type=F32)
    mb = jnp.dot(ybp_scr[...], wb_ref[...], preferred_element_type=F32)
    merged = (_sigmoid(ga_ref[...].astype(F32)) * ma + _sigmoid(gb_ref[...].astype(F32)) * mb)
    x1_ref[...] = x_ref[...] + jnp.dot(merged.astype(BF16), wo_ref[...], preferred_element_type=F32)
    yap_scr[...] = ya_scr[...]
    ybp_scr[...] = yb_scr[...]


def _mixer(proj, dt_raw, x2d, bsz, seq, consts, wa, wb, wo):
    n = x2d.shape[0]
    tps = seq // MIX_T
    ntiles = bsz * tps
    cur = lambda s: jnp.minimum(s, ntiles - 1)
    prev = lambda s: jnp.maximum(s - 1, 0)

    def pcol(off, tile):
        return pl.BlockSpec((MIX_T, PROJ_TILE), lambda s: (tile(s), off // PROJ_TILE))

    def full(a):
        nd = a.ndim
        return pl.BlockSpec(a.shape, lambda s, nd=nd: (0,) * nd)

    in_specs = [pcol(OFF_U, cur), pcol(OFF_V, cur), pcol(OFF_Z, cur),
                pcol(OFF_XBC, cur), pcol(OFF_XBC + PROJ_TILE, cur),
                pl.BlockSpec((MIX_T, LANES), lambda s: (cur(s), 0)),
                pcol(OFF_GA - SSM_HEADS, prev), pcol(OFF_GB - SSM_HEADS, prev),
                pl.BlockSpec((MIX_T, D_MODEL), lambda s: (prev(s), 0))]
    in_specs += [full(a) for a in consts] + [full(wa), full(wb), full(wo)]
    return pl.pallas_call(
        functools.partial(_mixer_kernel, tps=tps),
        grid=(ntiles + 1,),
        in_specs=in_specs,
        out_specs=pl.BlockSpec((MIX_T, D_MODEL), lambda s: (prev(s), 0)),
        out_shape=jax.ShapeDtypeStruct((n, D_MODEL), F32),
        scratch_shapes=[
            pltpu.VMEM((SSD_CHUNK, SSM_CONV_DIM), BF16),
            pltpu.VMEM((MIX_T, SSM_CONV_DIM), F32),
            pltpu.VMEM((SSM_STATE, SSM_INNER), F32),
            pltpu.VMEM((MIX_T, SGU_WIDTH), BF16),
            pltpu.VMEM((MIX_T, SSM_INNER), BF16),
            pltpu.VMEM((MIX_T, SGU_WIDTH), BF16),
            pltpu.VMEM((MIX_T, SSM_INNER), BF16),
        ],
        compiler_params=pltpu.CompilerParams(
            dimension_semantics=("arbitrary",), vmem_limit_bytes=VMEM_LIMIT),
        name="mixer",
    )(proj, proj, proj, proj, proj, dt_raw, proj, proj, x2d, *consts, wa, wb, wo)


def _router_kernel(x_ref, g_ref, wr_ref, br_ref, upper_ref, hp_ref, idx_ref, gate_ref, rank_ref, cnt_ref,
                   carry_scr):
    @pl.when(pl.program_id(0) == 0)
    def _():
        carry_scr[...] = jnp.zeros_like(carry_scr)

    h = _rms(x_ref[...], g_ref[...])
    _store_token_tiles(hp_ref, h, ROUTE_T)

    h_hi = h.astype(BF16)
    h_lo = (h - h_hi.astype(F32)).astype(BF16)
    logits = (jnp.dot(h_hi, wr_ref[0], preferred_element_type=F32)
              + jnp.dot(h_hi, wr_ref[1], preferred_element_type=F32)
              + jnp.dot(h_lo, wr_ref[0], preferred_element_type=F32))
    lt = logits.T[0:N_EXPERTS, :] + br_ref[...]
    eidx = lax.broadcasted_iota(I32, (N_EXPERTS, ROUTE_T), 0).astype(F32)
    vals = lt
    sel_any = jnp.zeros((N_EXPERTS, ROUTE_T), F32)
    sels, tops = [], []
    for k in range(TOP_K):
        m = jnp.max(vals, axis=0, keepdims=True)
        first = jnp.min(jnp.where(vals == m, eidx, float(N_EXPERTS)), axis=0, keepdims=True)
        sel = eidx == first
        vals = jnp.where(sel, -jnp.inf, vals)
        sel_f = sel.astype(F32)
        sel_any = sel_any + sel_f
        sels.append(sel_f)
        tops.append(m)
        idx_ref[k:k + 1, :] = first.astype(I32)
    es = [jnp.exp(t - tops[0]) for t in tops]
    denom = functools.reduce(lambda a, b: a + b, es)
    for k in range(TOP_K):
        gate_ref[k:k + 1, :] = es[k] / denom
    gate_ref[TOP_K:, :] = jnp.zeros((SUBLANES - TOP_K, ROUTE_T), F32)

    excl = jnp.dot(sel_any.astype(BF16), upper_ref[...], preferred_element_type=F32) + carry_scr[:, 0:1]
    for k in range(TOP_K):
        rank_ref[k:k + 1, :] = jnp.sum(sels[k] * excl, axis=0, keepdims=True).astype(I32)
    new_carry = carry_scr[...] + jnp.sum(sel_any, axis=1, keepdims=True)
    carry_scr[...] = new_carry
    cnt_ref[...] = new_carry.astype(I32)


def _router(x1, g, wr_pad, br_col, upper):
    n = x1.shape[0]
    per_k = pl.BlockSpec((TOP_K, ROUTE_T), lambda i: (0, i))
    return pl.pallas_call(
        _router_kernel,
        grid=(n // ROUTE_T,),
        in_specs=[
            pl.BlockSpec((ROUTE_T, D_MODEL), lambda i: (i, 0)),
            pl.BlockSpec((1, D_MODEL), lambda i: (0, 0)),
            pl.BlockSpec((2, D_MODEL, LANES), lambda i: (0, 0, 0)),
            pl.BlockSpec((N_EXPERTS, 1), lambda i: (0, 0)),
            pl.BlockSpec((ROUTE_T, ROUTE_T), lambda i: (0, 0)),
        ],
        out_specs=[
            pl.BlockSpec((ROUTE_T * TOK_SUB, LANES), lambda i: (i, 0)),
            per_k,
            pl.BlockSpec((SUBLANES, ROUTE_T), lambda i: (0, i)),
            per_k,
            pl.BlockSpec((N_EXPERTS, LANES), lambda i: (0, 0)),
        ],
        out_shape=[
            jax.ShapeDtypeStruct((n * TOK_SUB, LANES), F32),
            jax.ShapeDtypeStruct((TOP_K, n), I32),
            jax.ShapeDtypeStruct((SUBLANES, n), F32),
            jax.ShapeDtypeStruct((TOP_K, n), I32),
            jax.ShapeDtypeStruct((N_EXPERTS, LANES), I32),
        ],
        scratch_shapes=[pltpu.VMEM((N_EXPERTS, LANES), F32)],
        compiler_params=pltpu.CompilerParams(
            dimension_semantics=("arbitrary",), vmem_limit_bytes=VMEM_LIMIT),
        name="router",
    )(x1, g, wr_pad, br_col, upper)


def _dispatch_kernel(dest_ref, pad_ref, hp_ref, xs_hbm, zero_scr, sem_rows):
    i = pl.program_id(0)
    tbl = i * (TOP_K * DISP_T)

    def zero_copy(s):
        return pltpu.make_async_copy(zero_scr.at[0:TOK_SUB, :], xs_hbm.at[_tile_rows(s), :], sem_rows)

    def zero_run(s):
        rows = pl.ds(pl.multiple_of(s * TOK_SUB, TOK_SUB), ZERO_RUN * TOK_SUB)
        return pltpu.make_async_copy(zero_scr, xs_hbm.at[rows, :], sem_rows)

    def zero_pads(act):
        for e in range(N_EXPERTS):
            lo, hi = pad_ref[e], pad_ref[N_EXPERTS + e]
            nrun = (hi - lo) // ZERO_RUN
            lax.fori_loop(0, nrun, lambda r, c: (act(zero_run(lo + r * ZERO_RUN)), c)[1], 0)
            lax.fori_loop(lo + nrun * ZERO_RUN, hi, lambda s, c: (act(zero_copy(s)), c)[1], 0)

    @pl.when(i == 0)
    def _():
        zero_scr[...] = jnp.zeros_like(zero_scr)
        zero_pads(lambda d: d.start())
        zero_pads(lambda d: d.wait())

    def tile_copy(t, k):
        d = dest_ref[tbl + k * DISP_T + t]
        return pltpu.make_async_copy(hp_ref.at[_tile_rows(t), :], xs_hbm.at[_tile_rows(d), :], sem_rows)

    def issue(tb, carry):
        for r in range(DMA_UNROLL):
            for k in range(TOP_K):
                tile_copy(tb * DMA_UNROLL + r, k).start(priority=(r * TOP_K + k) % 2)
        return carry

    lax.fori_loop(0, DISP_T // DMA_UNROLL, issue, 0)

    def drain(tb, carry):
        for _ in range(DMA_UNROLL * TOP_K):
            zero_copy(0).wait()
        return carry

    lax.fori_loop(0, DISP_T // DMA_UNROLL, drain, 0)


def _dispatch(dest_flat, pad_tbl, hp, cap):
    n = hp.shape[0] // TOK_SUB
    grid_spec = pltpu.PrefetchScalarGridSpec(
        num_scalar_prefetch=2,
        grid=(n // DISP_T,),
        in_specs=[pl.BlockSpec((DISP_T * TOK_SUB, LANES), lambda i, d, p: (i, 0))],
        out_specs=pl.BlockSpec(memory_space=pl.ANY),
        scratch_shapes=[pltpu.VMEM((ZERO_RUN * TOK_SUB, LANES), F32), pltpu.SemaphoreType.DMA],
    )
    return pl.pallas_call(
        _dispatch_kernel,
        grid_spec=grid_spec,
        out_shape=jax.ShapeDtypeStruct((cap * TOK_SUB, LANES), F32),
        compiler_params=pltpu.CompilerParams(dimension_semantics=("arbitrary",)),
        name="dispatch",
    )(dest_flat, pad_tbl, hp)


def _expert_kernel(be_ref, nxt_ref, nv_ref, xs_ref, w1_hbm, b1_ref, w2_hbm, b2_ref, ys_ref,
                   w1f_scr, w2f_scr, w1b_scr, w2b_scr, sems):
    b = pl.program_id(0)
    valid = nv_ref[b]
    used = valid > 0
    new_expert = jnp.logical_or(b == 0, be_ref[b] != be_ref[jnp.maximum(b - 1, 0)])

    def fetch(e):
        return (pltpu.make_async_copy(w1_hbm.at[e], w1f_scr, sems.at[0]),
                pltpu.make_async_copy(w2_hbm.at[e], w2f_scr, sems.at[1]))

    @pl.when(jnp.logical_and(used, b == 0))
    def _():
        for c in fetch(be_ref[0]):
            c.start()

    @pl.when(jnp.logical_and(used, new_expert))
    def _():
        for c in fetch(be_ref[b]):
            c.wait()

        def cast(rb, carry):
            rows = pl.ds(pl.multiple_of(rb * CAST_ROWS, CAST_ROWS), CAST_ROWS)
            w1b_scr[rows, :] = w1f_scr[rows, :].astype(BF16)
            w2b_scr[rows, :] = w2f_scr[rows, :].astype(BF16)
            return carry

        lax.fori_loop(0, D_MODEL // CAST_ROWS, cast, 0)

        @pl.when(nxt_ref[b] >= 0)
        def _():
            for c in fetch(nxt_ref[b]):
                c.start()

    def sub_block(sub):
        x = _load_token_tiles(xs_ref, MOE_SUB, sub * MOE_SUB).astype(BF16)
        hid = jnp.dot(x, w1b_scr[...], preferred_element_type=F32) + b1_ref[0]
        glu = jnp.minimum(hid[:, :D_FF_EXPERT], SWIGLU_LIMIT)
        lin = jnp.clip(hid[:, D_FF_EXPERT:], -SWIGLU_LIMIT, SWIGLU_LIMIT)
        act = glu * _sigmoid(SWIGLU_ALPHA * glu) * (lin + 1.0)
        y = jnp.dot(act.astype(BF16), w2b_scr[...], preferred_element_type=F32) + b2_ref[0]
        _store_token_tiles(ys_ref, y, MOE_SUB, sub * MOE_SUB)

    @pl.when(valid > MOE_SUB)
    def _():
        x = _load_token_tiles(xs_ref, MOE_BLOCK, 0).astype(BF16)
        hid = jnp.dot(x, w1b_scr[...], preferred_element_type=F32) + b1_ref[0]
        glu = jnp.minimum(hid[:, :D_FF_EXPERT], SWIGLU_LIMIT)
        lin = jnp.clip(hid[:, D_FF_EXPERT:], -SWIGLU_LIMIT, SWIGLU_LIMIT)
        act = glu * _sigmoid(SWIGLU_ALPHA * glu) * (lin + 1.0)
        y = jnp.dot(act.astype(BF16), w2b_scr[...], preferred_element_type=F32) + b2_ref[0]
        _store_token_tiles(ys_ref, y, MOE_BLOCK, 0)

    @pl.when(jnp.logical_and(used, valid <= MOE_SUB))
    def _():
        sub_block(0)


def _experts(blk_expert, blk_next, blk_valid, xs, w1, b1, w2, b2):
    cap = xs.shape[0] // TOK_SUB
    n_blocks = cap // MOE_BLOCK
    grid_spec = pltpu.PrefetchScalarGridSpec(
        num_scalar_prefetch=3,
        grid=(n_blocks,),
        in_specs=[
            pl.BlockSpec((MOE_BLOCK * TOK_SUB, LANES), lambda b, be, nx, nv: (b, 0)),
            pl.BlockSpec(memory_space=pl.ANY),
            pl.BlockSpec((1, 1, 2 * D_FF_EXPERT), lambda b, be, nx, nv: (be[b], 0, 0)),
            pl.BlockSpec(memory_space=pl.ANY),
            pl.BlockSpec((1, 1, D_MODEL), lambda b, be, nx, nv: (be[b], 0, 0)),
        ],
        out_specs=pl.BlockSpec((MOE_BLOCK * TOK_SUB, LANES), lambda b, be, nx, nv: (b, 0)),
        scratch_shapes=[pltpu.VMEM((D_MODEL, 2 * D_FF_EXPERT), F32),
                        pltpu.VMEM((D_FF_EXPERT, D_MODEL), F32),
                        pltpu.VMEM((D_MODEL, 2 * D_FF_EXPERT), BF16),
                        pltpu.VMEM((D_FF_EXPERT, D_MODEL), BF16),
                        pltpu.SemaphoreType.DMA((2,))],
    )
    assert D_MODEL == D_FF_EXPERT
    assert MOE_BLOCK == 2 * MOE_SUB
    return pl.pallas_call(
        _expert_kernel,
        grid_spec=grid_spec,
        out_shape=jax.ShapeDtypeStruct((cap * TOK_SUB, LANES), F32),
        compiler_params=pltpu.CompilerParams(
            dimension_semantics=("arbitrary",), vmem_limit_bytes=VMEM_LIMIT),
        name="experts",
    )(blk_expert, blk_next, blk_valid, xs, w1, b1, w2, b2)


def _combine_kernel(dest_ref, ys_hbm, x1_ref, gate_ref, p_ref, pg_ref, wg_ref, wp_ref, fg_ref,
                    out_ref, ybuf, sems, *, final):
    i = pl.program_id(0)
    nsteps = pl.num_programs(0)
    buf_tokens = TOP_K * COMB_T

    def issue_all(step, slot):
        tbl = step * buf_tokens

        def issue(tb, carry):
            for r in range(DMA_UNROLL):
                for k in range(TOP_K):
                    row = k * COMB_T + tb * DMA_UNROLL + r
                    pltpu.make_async_copy(
                        ys_hbm.at[_tile_rows(dest_ref[tbl + row]), :],
                        ybuf.at[_tile_rows(slot * buf_tokens + row), :],
                        sems.at[slot]).start(priority=(r * TOP_K + k) % 2)
            return carry

        lax.fori_loop(0, COMB_T // DMA_UNROLL, issue, 0)

    def drain_all(slot):
        def drain(tb, carry):
            for _ in range(DMA_UNROLL * TOP_K):
                pltpu.make_async_copy(ys_hbm.at[_tile_rows(0), :],
                                      ybuf.at[_tile_rows(slot * buf_tokens), :], sems.at[slot]).wait()
            return carry

        lax.fori_loop(0, COMB_T // DMA_UNROLL, drain, 0)

    @pl.when(i == 0)
    def _():
        issue_all(0, 0)

    for slot in range(2):
        @pl.when(jnp.logical_and(i % 2 == slot, i + 1 < nsteps))
        def _():
            issue_all(i + 1, 1 - slot)

    for slot in range(2):
        @pl.when(i % 2 == slot)
        def _():
            drain_all(slot)

            for sub in range(COMB_T // COMB_SUB):
                rows = slice(sub * COMB_SUB, (sub + 1) * COMB_SUB)
                x2 = x1_ref[rows, :]
                for k in range(TOP_K):
                    x2 = x2 + gate_ref[rows, k:k + 1] * _load_token_tiles(
                        ybuf, COMB_SUB, slot * buf_tokens + k * COMB_T + sub * COMB_SUB)
                hp = _rms(x2, pg_ref[...]).astype(BF16)
                gate = _sigmoid(jnp.dot(hp, wg_ref[...], preferred_element_type=F32))
                emb = jnp.dot(p_ref[rows, :].astype(BF16), wp_ref[...], preferred_element_type=F32)
                x3 = x2 + gate * emb
                if final:
                    x3 = _rms(x3, fg_ref[...])
                out_ref[rows, :] = x3


def _combine(dest_flat, ys, x1, gates_t, p2d, pg, wg, wp, fg, final):
    n = x1.shape[0]
    grid_spec = pltpu.PrefetchScalarGridSpec(
        num_scalar_prefetch=1,
        grid=(n // COMB_T,),
        in_specs=[
            pl.BlockSpec(memory_space=pl.ANY),
            pl.BlockSpec((COMB_T, D_MODEL), lambda i, d: (i, 0)),
            pl.BlockSpec((COMB_T, SUBLANES), lambda i, d: (i, 0)),
            pl.BlockSpec((COMB_T, PLE_DIM), lambda i, d: (i, 0)),
            pl.BlockSpec((1, D_MODEL), lambda i, d: (0, 0)),
            pl.BlockSpec((D_MODEL, D_MODEL), lambda i, d: (0, 0)),
            pl.BlockSpec((PLE_DIM, D_MODEL), lambda i, d: (0, 0)),
            pl.BlockSpec((1, D_MODEL), lambda i, d: (0, 0)),
        ],
        out_specs=pl.BlockSpec((COMB_T, D_MODEL), lambda i, d: (i, 0)),
        scratch_shapes=[pltpu.VMEM((2 * TOP_K * COMB_T * TOK_SUB, LANES), F32),
                        pltpu.SemaphoreType.DMA((2,))],
    )
    return pl.pallas_call(
        functools.partial(_combine_kernel, final=final),
        grid_spec=grid_spec,
        out_shape=jax.ShapeDtypeStruct((n, D_MODEL), F32),
        compiler_params=pltpu.CompilerParams(
            dimension_semantics=("arbitrary",), vmem_limit_bytes=VMEM_LIMIT),
        name="combine",
    )(dest_flat, ys, x1, gates_t, p2d, pg, wg, wp, fg)


def _layer(x2d, p2d, bsz, seq, mix_norm, w_in, sgu_ln_g, sgu_ln_b, sgu_w, sgu_b, conv_w, conv_b,
           dt_bias, a_log, d_skip, ssm_norm, w_branch_a, w_branch_b, w_out, ffn_norm, w_router,
           b_router, w1, b1, w2, b2, ple_norm, w_ple_gate, w_ple_proj, final_norm, final):
    n = x2d.shape[0]
    row = lambda a: a.reshape(1, -1).astype(F32)

    w_main = jnp.concatenate([w_in[:, :OFF_DT], w_in[:, OFF_GA:]], axis=1).astype(BF16)
    w_dt = jnp.pad(w_in[:, OFF_DT:OFF_GA], ((0, 0), (0, LANES - SSM_HEADS))).astype(BF16)
    pos = jnp.arange(SGU_LEN)
    allowed = (pos[None, :] // CHUNK) <= (pos[:, None] // CHUNK)
    wsgu = jnp.where(allowed[None], sgu_w, 0.0).astype(BF16)
    bsgu = jnp.repeat(sgu_b.T, SGU_HEAD_DIM, axis=1).astype(F32)
    dtb = dt_bias.reshape(-1, 1).astype(F32)
    acol = -jnp.exp(a_log.astype(F32)).reshape(-1, 1)
    dskip = jnp.repeat(d_skip.astype(F32), SSM_HEAD_DIM).reshape(1, -1)
    head_of_col = jnp.arange(SSM_INNER) // SSM_HEAD_DIM
    eexp = (jnp.arange(LANES)[:, None] == head_of_col[None, :]).astype(BF16)
    triu = (jnp.arange(SSD_CHUNK)[:, None] <= jnp.arange(SSD_CHUNK)[None, :]).astype(BF16)
    t_out = jnp.arange((SSM_CONV - 1) * SSD_CHUNK)
    src = SSD_CHUNK + t_out % SSD_CHUNK - (SSM_CONV - 1) + t_out // SSD_CHUNK
    shift = (src[:, None] == jnp.arange(2 * SSD_CHUNK)[None, :]).astype(BF16)
    consts = [row(sgu_ln_g), row(sgu_ln_b), wsgu, bsgu, conv_w.astype(F32), row(conv_b), dtb, acol,
              dskip, row(ssm_norm), eexp, shift, triu]

    proj, dt_raw = _in_proj(x2d, row(mix_norm), w_main, w_dt)
    x1 = _mixer(proj, dt_raw, x2d, bsz, seq, consts, w_branch_a.astype(BF16),
                w_branch_b.astype(BF16), w_out.astype(BF16))

    wr_f = jnp.pad(w_router.astype(F32), ((0, 0), (0, LANES - N_EXPERTS)))
    wr_hi = wr_f.astype(BF16)
    wr_pad = jnp.stack([wr_hi, (wr_f - wr_hi.astype(F32)).astype(BF16)])
    upper = (jnp.arange(ROUTE_T)[:, None] < jnp.arange(ROUTE_T)[None, :]).astype(BF16)
    hp, idx, gates, rank, cnt = _router(x1, row(ffn_norm), wr_pad, b_router.reshape(-1, 1).astype(F32),
                                        upper)

    counts = cnt[:, 0]
    padded = (counts + MOE_BLOCK - 1) // MOE_BLOCK * MOE_BLOCK
    pend = jnp.cumsum(padded)
    pstart = pend - padded
    nk = n * TOP_K
    cap = (nk + MOE_BLOCK - 1) // MOE_BLOCK * MOE_BLOCK + N_EXPERTS * MOE_BLOCK
    n_blocks = cap // MOE_BLOCK
    ex = jnp.arange(N_EXPERTS, dtype=I32)
    dest = rank + jnp.sum(jnp.where(idx[..., None] == ex, pstart.astype(I32), 0), axis=-1)
    blk_start = jnp.arange(n_blocks, dtype=I32) * MOE_BLOCK
    blk_expert = jnp.minimum(jnp.sum((pend[None, :] <= blk_start[:, None]).astype(I32), axis=1),
                             N_EXPERTS - 1).astype(I32)
    of_block = blk_expert[:, None] == ex[None, :]
    per_block = lambda v: jnp.sum(jnp.where(of_block, v[None, :].astype(I32), 0), axis=1)
    blk_valid = jnp.where(blk_start < pend[-1],
                          jnp.clip(per_block(pstart + counts) - blk_start, 0, MOE_BLOCK), 0).astype(I32)
    live = jnp.where(counts > 0, ex, N_EXPERTS)
    later = jnp.concatenate([lax.cummin(live[::-1])[::-1][1:], jnp.full((1,), N_EXPERTS, I32)])
    blk_next = per_block(jnp.where(later < N_EXPERTS, later, -1))

    dest_flat = dest.reshape(TOP_K, n // DISP_T, DISP_T).transpose(1, 0, 2).reshape(-1)
    used_end = pstart + (counts + MOE_SUB - 1) // MOE_SUB * MOE_SUB
    pad_tbl = jnp.concatenate([pstart + counts, used_end]).astype(I32)

    xs = _dispatch(dest_flat, pad_tbl, hp, cap)
    ys = _experts(blk_expert, blk_next, blk_valid, xs, w1.astype(F32),
                  b1.reshape(N_EXPERTS, 1, -1).astype(F32), w2.astype(F32),
                  b2.reshape(N_EXPERTS, 1, -1).astype(F32))
    return _combine(dest_flat, ys, x1, gates.T, p2d, row(ple_norm), w_ple_gate.astype(BF16),
                    w_ple_proj.astype(BF16), row(final_norm), final)


def kernel(x, p, mix_norm, w_in, sgu_ln_g, sgu_ln_b, sgu_w, sgu_b, conv_w, conv_b, dt_bias, a_log,
           d_skip, ssm_norm, w_branch_a, w_branch_b, w_out, ffn_norm, w_router, b_router, w1, b1,
           w2, b2, ple_norm, w_ple_gate, w_ple_proj, final_norm):
    bsz, seq, d = x.shape
    depth = w_in.shape[0]
    assert d == D_MODEL and seq % MIX_T == 0 and (bsz * seq) % max(IN_TM, ROUTE_T, DISP_T) == 0
    assert ROUTE_T == DISP_T == COMB_T
    x2d = x.reshape(bsz * seq, d)
    for i in range(depth):
        x2d = _layer(x2d, p[i].reshape(bsz * seq, PLE_DIM), bsz, seq, mix_norm[i], w_in[i],
                     sgu_ln_g[i], sgu_ln_b[i], sgu_w[i], sgu_b[i], conv_w[i], conv_b[i], dt_bias[i],
                     a_log[i], d_skip[i], ssm_norm[i], w_branch_a[i], w_branch_b[i], w_out[i],
                     ffn_norm[i], w_router[i], b_router[i], w1[i], b1[i], w2[i], b2[i], ple_norm[i],
                     w_ple_gate[i], w_ple_proj[i], final_norm, final=(i == depth - 1))
    return x2d.reshape(bsz, seq, d)
```

```python
import functools
import math

import jax
import jax.numpy as jnp
from jax import lax
from jax.experimental import pallas as pl
from jax.experimental.pallas import tpu as pltpu

F32 = jnp.float32
BF16 = jnp.bfloat16
I32 = jnp.int32

D_MODEL = 1024
CHUNK = 64
PLE_DIM = 256
RMS_EPS = 1e-6
LN_EPS = 1e-5

SGU_HEADS = 8
SGU_HEAD_DIM = 128
SGU_WIDTH = SGU_HEADS * SGU_HEAD_DIM
SGU_LEN = 128

SSM_HEADS = 16
SSM_HEAD_DIM = 64
SSM_INNER = SSM_HEADS * SSM_HEAD_DIM
SSM_GROUPS = 4
SSM_STATE = 128
SSM_CONV = 4
SSD_CHUNK = 128
SSM_CONV_DIM = SSM_INNER + 2 * SSM_GROUPS * SSM_STATE

N_EXPERTS = 32
TOP_K = 4
D_FF_EXPERT = 1024
SWIGLU_LIMIT = 7.0
SWIGLU_ALPHA = 1.702
MOE_BLOCK = 512
MOE_SUB = 256

OFF_U = 0
OFF_V = OFF_U + SGU_WIDTH
OFF_Z = OFF_V + SGU_WIDTH
OFF_XBC = OFF_Z + SSM_INNER
OFF_DT = OFF_XBC + SSM_CONV_DIM
OFF_GA = OFF_DT + SSM_HEADS
OFF_GB = OFF_GA + D_MODEL
IN_PROJ_DIM = OFF_GB + D_MODEL

LANES = 128
SUBLANES = 8
PROJ_MAIN = IN_PROJ_DIM - SSM_HEADS
PROJ_TILE = 1024
TOK_SUB = D_MODEL // LANES

VMEM_LIMIT = 56 * 1024 * 1024

IN_TM = 512
IN_TN = 1792
MIX_T = 512
ROUTE_T = 512
DISP_T = 512
COMB_T = DISP_T
COMB_SUB = 128
DMA_UNROLL = 8
ZERO_RUN = 64
CAST_COLS = 256


def _sigmoid(x):
    return 1.0 / (1.0 + jnp.exp(-x))


def _gelu_exact(x):
    return 0.5 * x * (1.0 + lax.erf(x * (1.0 / math.sqrt(2.0))))


def _softplus(x):
    return jnp.maximum(x, 0.0) + jnp.log1p(jnp.exp(-jnp.abs(x)))


def _rms(x, g):
    ms = jnp.mean(x * x, axis=-1, keepdims=True)
    return x * lax.rsqrt(ms + RMS_EPS) * g


def _store_token_tiles(ref, val, rows, start=0):
    for j in range(TOK_SUB):
        ref[pl.ds(start * TOK_SUB + j, rows, stride=TOK_SUB), :] = val[:, j * LANES:(j + 1) * LANES]


def _load_token_tiles(ref, rows, start=0):
    return jnp.concatenate(
        [ref[pl.ds(start * TOK_SUB + j, rows, stride=TOK_SUB), :] for j in range(TOK_SUB)], axis=1)


def _tile_rows(idx):
    return pl.ds(pl.multiple_of(idx * TOK_SUB, TOK_SUB), TOK_SUB)


def _inproj_kernel(x_ref, g_ref, w_ref, wdt_ref, proj_ref, dt_ref, h_scr):
    @pl.when(pl.program_id(0) == 0)
    def _():
        h_scr[...] = jnp.zeros_like(h_scr)

    h = h_scr[...]
    dt_ref[...] = jnp.dot(h, wdt_ref[...], preferred_element_type=F32)
    for c in range(PROJ_MAIN // IN_TN):
        cols = slice(c * IN_TN, (c + 1) * IN_TN)
        proj_ref[:, cols] = jnp.dot(h, w_ref[:, cols], preferred_element_type=F32).astype(BF16)
    h_scr[...] = _rms(x_ref[...], g_ref[...]).astype(BF16)


def _in_proj(x2d, g, w_main, w_dt):
    n = x2d.shape[0]
    ntiles = n // IN_TM
    return pl.pallas_call(
        _inproj_kernel,
        grid=(ntiles + 1,),
        in_specs=[
            pl.BlockSpec((IN_TM, D_MODEL), lambda s: (jnp.minimum(s, ntiles - 1), 0)),
            pl.BlockSpec((1, D_MODEL), lambda s: (0, 0)),
            pl.BlockSpec((D_MODEL, PROJ_MAIN), lambda s: (0, 0)),
            pl.BlockSpec((D_MODEL, LANES), lambda s: (0, 0)),
        ],
        out_specs=[
            pl.BlockSpec((IN_TM, PROJ_MAIN), lambda s: (jnp.maximum(s - 1, 0), 0)),
            pl.BlockSpec((IN_TM, LANES), lambda s: (jnp.maximum(s - 1, 0), 0)),
        ],
        out_shape=[
            jax.ShapeDtypeStruct((n, PROJ_MAIN), BF16),
            jax.ShapeDtypeStruct((n, LANES), F32),
        ],
        scratch_shapes=[pltpu.VMEM((IN_TM, D_MODEL), BF16)],
        compiler_params=pltpu.CompilerParams(
            dimension_semantics=("arbitrary",), vmem_limit_bytes=VMEM_LIMIT),
        name="in_proj",
    )(x2d, g, w_main, w_dt)


def _split3(a):
    a1 = a.astype(BF16)
    r1 = a - a1.astype(F32)
    a2 = r1.astype(BF16)
    a3 = (r1 - a2.astype(F32)).astype(BF16)
    return a1, a2, a3


def _dot3(parts, w):
    out = jnp.dot(parts[0], w, preferred_element_type=F32)
    for p in parts[1:]:
        out = out + jnp.dot(p, w, preferred_element_type=F32)
    return out


def _mixer_kernel(u_ref, v_ref, z_ref, xb0_ref, xb1_ref, dt_ref, ga_ref, gb_ref, x_ref,
                  lng_ref, lnb_ref, wsgu_ref, bsgu_ref, convw_ref, convb_ref, dtb_ref, acol_ref,
                  dskip_ref, normw_ref, eexp_ref, shift_ref, triu_ref, wa_ref, wb_ref, wo_ref,
                  x1_ref,
                  tail_scr, xc_scr, state_scr, ya_scr, yb_scr, yap_scr, ybp_scr, *, tps):
    nchunk = MIX_T // SSD_CHUNK
    s = pl.program_id(0)

    @pl.when(s % tps == 0)
    def _():
        tail_scr[...] = jnp.zeros_like(tail_scr)
        state_scr[...] = jnp.zeros_like(state_scr)

    @pl.when(s == 0)
    def _():
        yap_scr[...] = jnp.zeros_like(yap_scr)
        ybp_scr[...] = jnp.zeros_like(ybp_scr)

    xb_refs = (xb0_ref, xb1_ref)
    for c in range(nchunk):
        r0 = c * SSD_CHUNK
        for hf, xb_ref in enumerate(xb_refs):
            cols = slice(hf * PROJ_TILE, (hf + 1) * PROJ_TILE)
            cur = xb_ref[r0:r0 + SSD_CHUNK, :]
            prev = tail_scr[:, cols] if c == 0 else xb_ref[r0 - SSD_CHUNK:r0, :]
            sh = jnp.dot(shift_ref[...], jnp.concatenate([prev, cur], axis=0), preferred_element_type=F32)
            acc = convb_ref[:, cols] + convw_ref[SSM_CONV - 1:SSM_CONV, cols] * cur.astype(F32)
            for j in range(SSM_CONV - 1):
                acc = acc + convw_ref[j:j + 1, cols] * sh[j * SSD_CHUNK:(j + 1) * SSD_CHUNK, :]
            xc_scr[r0:r0 + SSD_CHUNK, cols] = acc * _sigmoid(acc)
    for hf, xb_ref in enumerate(xb_refs):
        tail_scr[:, hf * PROJ_TILE:(hf + 1) * PROJ_TILE] = xb_ref[MIX_T - SSD_CHUNK:MIX_T, :]

    row_i = lax.broadcasted_iota(I32, (SSD_CHUNK, SSD_CHUNK), 0)
    col_i = lax.broadcasted_iota(I32, (SSD_CHUNK, SSD_CHUNK), 1)
    tril = row_i >= col_i
    lane_hi = col_i >= SSM_HEAD_DIM
    eexp = eexp_ref[...]
    triu = triu_ref[...]

    for c in range(nchunk):
        rows = slice(c * SSD_CHUNK, (c + 1) * SSD_CHUNK)

        ug = _gelu_exact(u_ref[rows, :].astype(F32))
        vg = _gelu_exact(v_ref[rows, :].astype(F32))
        mu = jnp.mean(vg, axis=-1, keepdims=True)
        vc = vg - mu
        var = jnp.mean(vc * vc, axis=-1, keepdims=True)
        vn = (vc * lax.rsqrt(var + LN_EPS) * lng_ref[...] + lnb_ref[...]).astype(BF16)
        for g in range(SGU_HEADS):
            cols = slice(g * SGU_HEAD_DIM, (g + 1) * SGU_HEAD_DIM)
            mixed = jnp.dot(wsgu_ref[g], vn[:, cols], preferred_element_type=F32)
            ya_scr[rows, cols] = (ug[:, cols] * (mixed + bsgu_ref[:, cols])).astype(BF16)

        dt_t = _softplus(dt_ref[rows, :].T[0:SSM_HEADS, :] + dtb_ref[...])
        acs_t = _dot3(_split3(dt_t * acol_ref[...]), triu)
        w_t = jnp.exp(acs_t[:, SSD_CHUNK - 1:SSD_CHUNK] - acs_t) * dt_t
        acs = jnp.concatenate([acs_t, jnp.zeros((LANES - SSM_HEADS, SSD_CHUNK), F32)], axis=0).T
        aend = jnp.broadcast_to(acs[SSD_CHUNK - 1:SSD_CHUNK, :], (SUBLANES, LANES))
        dec = jnp.exp(_dot3(_split3(aend), eexp)[0:1, :])

        for gi in range(SSM_GROUPS):
            bg = xc_scr[rows, SSM_INNER + gi * SSM_STATE:SSM_INNER + (gi + 1) * SSM_STATE]
            cg = xc_scr[rows, SSM_INNER + (SSM_GROUPS + gi) * SSM_STATE:
                        SSM_INNER + (SSM_GROUPS + gi + 1) * SSM_STATE]
            bg_t = bg.T
            cb = jnp.dot(cg.astype(BF16), bg_t.astype(BF16), preferred_element_type=F32)
            for pr in range(2):
                lb = gi * 2 + pr
                cols = slice(lb * LANES, (lb + 1) * LANES)
                xpair = xc_scr[rows, cols]
                hpair = state_scr[:, cols]
                acc = jnp.zeros((SSD_CHUNK, LANES), F32)
                st = jnp.zeros((SSM_STATE, LANES), F32)
                for hh in range(2):
                    h = lb * 2 + hh
                    lmask = lane_hi if hh == 1 else jnp.logical_not(lane_hi)
                    xm = jnp.where(lmask, xpair, 0.0).astype(BF16)
                    hm = jnp.where(lmask, hpair, 0.0).astype(BF16)
                    colb = acs[:, h:h + 1]
                    rowb = acs_t[h:h + 1, :]
                    decay = jnp.exp(jnp.where(tril, colb - rowb, -jnp.inf))
                    m_h = (cb * decay * dt_t[h:h + 1, :]).astype(BF16)
                    c_h = (cg * jnp.exp(colb)).astype(BF16)
                    acc = acc + jnp.dot(m_h, xm, preferred_element_type=F32)
                    acc = acc + jnp.dot(c_h, hm, preferred_element_type=F32)
                    btw = (bg_t * w_t[h:h + 1, :]).astype(BF16)
                    st = st + jnp.dot(btw, xm, preferred_element_type=F32)
                state_scr[:, cols] = hpair * dec[:, cols] + st
                y = acc + dskip_ref[:, cols] * xpair
                zz = z_ref[rows, cols].astype(F32)
                xc_scr[rows, cols] = y * (zz * _sigmoid(zz))
            gcols = slice(gi * 2 * LANES, (gi + 1) * 2 * LANES)
            yg = xc_scr[rows, gcols]
            ms = jnp.mean(yg * yg, axis=-1, keepdims=True)
            yb_scr[rows, gcols] = (yg * lax.rsqrt(ms + RMS_EPS) * normw_ref[:, gcols]).astype(BF16)

    ma = jnp.dot(yap_scr[...], wa_ref[...], preferred_element_type=F32)
    mb = jnp.dot(ybp_scr[...], wb_ref[...], preferred_element_type=F32)
    merged = (_sigmoid(ga_ref[...].astype(F32)) * ma + _sigmoid(gb_ref[...].astype(F32)) * mb)
    x1_ref[...] = x_ref[...] + jnp.dot(merged.astype(BF16), wo_ref[...], preferred_element_type=F32)
    yap_scr[...] = ya_scr[...]
    ybp_scr[...] = yb_scr[...]


def _mixer(proj, dt_raw, x2d, bsz, seq, consts, wa, wb, wo):
    n = x2d.shape[0]
    tps = seq // MIX_T
    ntiles = bsz * tps
    cur = lambda s: jnp.minimum(s, ntiles - 1)
    prev = lambda s: jnp.maximum(s - 1, 0)

    def pcol(off, tile):
        return pl.BlockSpec((MIX_T, PROJ_TILE), lambda s: (tile(s), off // PROJ_TILE))

    def full(a):
        nd = a.ndim
        return pl.BlockSpec(a.shape, lambda s, nd=nd: (0,) * nd)

    in_specs = [pcol(OFF_U, cur), pcol(OFF_V, cur), pcol(OFF_Z, cur),
                pcol(OFF_XBC, cur), pcol(OFF_XBC + PROJ_TILE, cur),
                pl.BlockSpec((MIX_T, LANES), lambda s: (cur(s), 0)),
                pcol(OFF_GA - SSM_HEADS, prev), pcol(OFF_GB - SSM_HEADS, prev),
                pl.BlockSpec((MIX_T, D_MODEL), lambda s: (prev(s), 0))]
    in_specs += [full(a) for a in consts] + [full(wa), full(wb), full(wo)]
    return pl.pallas_call(
        functools.partial(_mixer_kernel, tps=tps),
        grid=(ntiles + 1,),
        in_specs=in_specs,
        out_specs=pl.BlockSpec((MIX_T, D_MODEL), lambda s: (prev(s), 0)),
        out_shape=jax.ShapeDtypeStruct((n, D_MODEL), F32),
        scratch_shapes=[
            pltpu.VMEM((SSD_CHUNK, SSM_CONV_DIM), BF16),
            pltpu.VMEM((MIX_T, SSM_CONV_DIM), F32),
            pltpu.VMEM((SSM_STATE, SSM_INNER), F32),
            pltpu.VMEM((MIX_T, SGU_WIDTH), BF16),
            pltpu.VMEM((MIX_T, SSM_INNER), BF16),
            pltpu.VMEM((MIX_T, SGU_WIDTH), BF16),
            pltpu.VMEM((MIX_T, SSM_INNER), BF16),
        ],
        compiler_params=pltpu.CompilerParams(
            dimension_semantics=("arbitrary",), vmem_limit_bytes=VMEM_LIMIT),
        name="mixer",
    )(proj, proj, proj, proj, proj, dt_raw, proj, proj, x2d, *consts, wa, wb, wo)


def _router_kernel(x_ref, g_ref, wr_ref, br_ref, upper_ref, hp_ref, idx_ref, gate_ref, rank_ref, cnt_ref,
                   carry_scr):
    @pl.when(pl.program_id(0) == 0)
    def _():
        carry_scr[...] = jnp.zeros_like(carry_scr)

    h = _rms(x_ref[...], g_ref[...])
    _store_token_tiles(hp_ref, h, ROUTE_T)

    h_hi = h.astype(BF16)
    h_lo = (h - h_hi.astype(F32)).astype(BF16)
    logits = (jnp.dot(h_hi, wr_ref[0], preferred_element_type=F32)
              + jnp.dot(h_hi, wr_ref[1], preferred_element_type=F32)
              + jnp.dot(h_lo, wr_ref[0], preferred_element_type=F32))
    lt = logits.T[0:N_EXPERTS, :] + br_ref[...]
    eidx = lax.broadcasted_iota(I32, (N_EXPERTS, ROUTE_T), 0).astype(F32)
    vals = lt
    sel_any = jnp.zeros((N_EXPERTS, ROUTE_T), F32)
    sels, tops = [], []
    for k in range(TOP_K):
        m = jnp.max(vals, axis=0, keepdims=True)
        first = jnp.min(jnp.where(vals == m, eidx, float(N_EXPERTS)), axis=0, keepdims=True)
        sel = eidx == first
        vals = jnp.where(sel, -jnp.inf, vals)
        sel_f = sel.astype(F32)
        sel_any = sel_any + sel_f
        sels.append(sel_f)
        tops.append(m)
        idx_ref[k:k + 1, :] = first.astype(I32)
    es = [jnp.exp(t - tops[0]) for t in tops]
    denom = functools.reduce(lambda a, b: a + b, es)
    for k in range(TOP_K):
        gate_ref[k:k + 1, :] = es[k] / denom
    gate_ref[TOP_K:, :] = jnp.zeros((SUBLANES - TOP_K, ROUTE_T), F32)

    excl = jnp.dot(sel_any.astype(BF16), upper_ref[...], preferred_element_type=F32) + carry_scr[:, 0:1]
    for k in range(TOP_K):
        rank_ref[k:k + 1, :] = jnp.sum(sels[k] * excl, axis=0, keepdims=True).astype(I32)
    new_carry = carry_scr[...] + jnp.sum(sel_any, axis=1, keepdims=True)
    carry_scr[...] = new_carry
    cnt_ref[...] = new_carry.astype(I32)


def _router(x1, g, wr_pad, br_col, upper):
    n = x1.shape[0]
    per_k = pl.BlockSpec((TOP_K, ROUTE_T), lambda i: (0, i))
    return pl.pallas_call(
        _router_kernel,
        grid=(n // ROUTE_T,),
        in_specs=[
            pl.BlockSpec((ROUTE_T, D_MODEL), lambda i: (i, 0)),
            pl.BlockSpec((1, D_MODEL), lambda i: (0, 0)),
            pl.BlockSpec((2, D_MODEL, LANES), lambda i: (0, 0, 0)),
            pl.BlockSpec((N_EXPERTS, 1), lambda i: (0, 0)),
            pl.BlockSpec((ROUTE_T, ROUTE_T), lambda i: (0, 0)),
        ],
        out_specs=[
            pl.BlockSpec((ROUTE_T * TOK_SUB, LANES), lambda i: (i, 0)),
            per_k,
            pl.BlockSpec((SUBLANES, ROUTE_T), lambda i: (0, i)),
            per_k,
            pl.BlockSpec((N_EXPERTS, LANES), lambda i: (0, 0)),
        ],
        out_shape=[
            jax.ShapeDtypeStruct((n * TOK_SUB, LANES), F32),
            jax.ShapeDtypeStruct((TOP_K, n), I32),
            jax.ShapeDtypeStruct((SUBLANES, n), F32),
            jax.ShapeDtypeStruct((TOP_K, n), I32),
            jax.ShapeDtypeStruct((N_EXPERTS, LANES), I32),
        ],
        scratch_shapes=[pltpu.VMEM((N_EXPERTS, LANES), F32)],
        compiler_params=pltpu.CompilerParams(
            dimension_semantics=("arbitrary",), vmem_limit_bytes=VMEM_LIMIT),
        name="router",
    )(x1, g, wr_pad, br_col, upper)


def _dispatch_kernel(dest_ref, pad_ref, hp_ref, xs_hbm, zero_scr, sem_rows):
    i = pl.program_id(0)
    tbl = i * (TOP_K * DISP_T)

    def zero_copy(s):
        return pltpu.make_async_copy(zero_scr.at[0:TOK_SUB, :], xs_hbm.at[_tile_rows(s), :], sem_rows)

    def zero_run(s):
        rows = pl.ds(pl.multiple_of(s * TOK_SUB, TOK_SUB), ZERO_RUN * TOK_SUB)
        return pltpu.make_async_copy(zero_scr, xs_hbm.at[rows, :], sem_rows)

    def zero_pads(act):
        for e in range(N_EXPERTS):
            lo, hi = pad_ref[e], pad_ref[N_EXPERTS + e]
            nrun = (hi - lo) // ZERO_RUN
            lax.fori_loop(0, nrun, lambda r, c: (act(zero_run(lo + r * ZERO_RUN)), c)[1], 0)
            lax.fori_loop(lo + nrun * ZERO_RUN, hi, lambda s, c: (act(zero_copy(s)), c)[1], 0)

    @pl.when(i == 0)
    def _():
        zero_scr[...] = jnp.zeros_like(zero_scr)
        zero_pads(lambda d: d.start())
        zero_pads(lambda d: d.wait())

    def tile_copy(t, k):
        d = dest_ref[tbl + k * DISP_T + t]
        return pltpu.make_async_copy(hp_ref.at[_tile_rows(t), :], xs_hbm.at[_tile_rows(d), :], sem_rows)

    def issue(tb, carry):
        for r in range(DMA_UNROLL):
            for k in range(TOP_K):
                tile_copy(tb * DMA_UNROLL + r, k).start(priority=(r * TOP_K + k) % 2)
        return carry

    lax.fori_loop(0, DISP_T // DMA_UNROLL, issue, 0)

    def drain(tb, carry):
        for _ in range(DMA_UNROLL * TOP_K):
            zero_copy(0).wait()
        return carry

    lax.fori_loop(0, DISP_T // DMA_UNROLL, drain, 0)


def _dispatch(dest_flat, pad_tbl, hp, cap):
    n = hp.shape[0] // TOK_SUB
    grid_spec = pltpu.PrefetchScalarGridSpec(
        num_scalar_prefetch=2,
        grid=(n // DISP_T,),
        in_specs=[pl.BlockSpec((DISP_T * TOK_SUB, LANES), lambda i, d, p: (i, 0))],
        out_specs=pl.BlockSpec(memory_space=pl.ANY),
        scratch_shapes=[pltpu.VMEM((ZERO_RUN * TOK_SUB, LANES), F32), pltpu.SemaphoreType.DMA],
    )
    return pl.pallas_call(
        _dispatch_kernel,
        grid_spec=grid_spec,
        out_shape=jax.ShapeDtypeStruct((cap * TOK_SUB, LANES), F32),
        compiler_params=pltpu.CompilerParams(dimension_semantics=("arbitrary",)),
        name="dispatch",
    )(dest_flat, pad_tbl, hp)


def _expert_kernel(be_ref, nxt_ref, nv_ref, xs_ref, w1_hbm, b1_ref, w2_hbm, b2_ref, ys_ref,
                   w1f_scr, w2f_scr, w1b_scr, w2b_scr, sems):
    b = pl.program_id(0)
    valid = nv_ref[b]
    used = valid > 0
    new_expert = jnp.logical_or(b == 0, be_ref[b] != be_ref[jnp.maximum(b - 1, 0)])

    def fetch(e):
        return (pltpu.make_async_copy(w1_hbm.at[e], w1f_scr, sems.at[0]),
                pltpu.make_async_copy(w2_hbm.at[e], w2f_scr, sems.at[1]))

    @pl.when(jnp.logical_and(used, b == 0))
    def _():
        for c in fetch(be_ref[0]):
            c.start()

    fresh = jnp.logical_and(used, new_expert)

    @pl.when(fresh)
    def _():
        for c in fetch(be_ref[b]):
            c.wait()

    def expert_mlp(rows, cast):
        if cast:
            for c0 in range(0, 2 * D_FF_EXPERT, CAST_COLS):
                w1b_scr[:, c0:c0 + CAST_COLS] = w1f_scr[:, c0:c0 + CAST_COLS].astype(BF16)
            for c0 in range(0, D_MODEL, CAST_COLS):
                w2b_scr[:, c0:c0 + CAST_COLS] = w2f_scr[:, c0:c0 + CAST_COLS].astype(BF16)
        x = _load_token_tiles(xs_ref, rows).astype(BF16)
        hid = jnp.dot(x, w1b_scr[...], preferred_element_type=F32) + b1_ref[0]
        glu = jnp.minimum(hid[:, :D_FF_EXPERT], SWIGLU_LIMIT)
        lin = jnp.clip(hid[:, D_FF_EXPERT:], -SWIGLU_LIMIT, SWIGLU_LIMIT)
        act = glu * _sigmoid(SWIGLU_ALPHA * glu) * (lin + 1.0)
        y = jnp.dot(act.astype(BF16), w2b_scr[...], preferred_element_type=F32) + b2_ref[0]
        _store_token_tiles(ys_ref, y, rows)

    full = valid > MOE_SUB
    for cast in (True, False):
        first = fresh if cast else jnp.logical_and(used, jnp.logical_not(new_expert))

        @pl.when(jnp.logical_and(first, full))
        def _(cast=cast):
            expert_mlp(MOE_BLOCK, cast)

        @pl.when(jnp.logical_and(first, jnp.logical_not(full)))
        def _(cast=cast):
            expert_mlp(MOE_SUB, cast)

    @pl.when(jnp.logical_and(fresh, nxt_ref[b] >= 0))
    def _():
        for c in fetch(nxt_ref[b]):
            c.start()


def _experts(blk_expert, blk_next, blk_valid, xs, w1, b1, w2, b2):
    cap = xs.shape[0] // TOK_SUB
    n_blocks = cap // MOE_BLOCK
    grid_spec = pltpu.PrefetchScalarGridSpec(
        num_scalar_prefetch=3,
        grid=(n_blocks,),
        in_specs=[
            pl.BlockSpec((MOE_BLOCK * TOK_SUB, LANES), lambda b, be, nx, nv: (b, 0)),
            pl.BlockSpec(memory_space=pl.ANY),
            pl.BlockSpec((1, 1, 2 * D_FF_EXPERT), lambda b, be, nx, nv: (be[b], 0, 0)),
            pl.BlockSpec(memory_space=pl.ANY),
            pl.BlockSpec((1, 1, D_MODEL), lambda b, be, nx, nv: (be[b], 0, 0)),
        ],
        out_specs=pl.BlockSpec((MOE_BLOCK * TOK_SUB, LANES), lambda b, be, nx, nv: (b, 0)),
        scratch_shapes=[pltpu.VMEM((D_MODEL, 2 * D_FF_EXPERT), F32),
                        pltpu.VMEM((D_FF_EXPERT, D_MODEL), F32),
                        pltpu.VMEM((D_MODEL, 2 * D_FF_EXPERT), BF16),
                        pltpu.VMEM((D_FF_EXPERT, D_MODEL), BF16),
                        pltpu.SemaphoreType.DMA((2,))],
    )
    assert D_MODEL == D_FF_EXPERT
    assert MOE_BLOCK == 2 * MOE_SUB
    return pl.pallas_call(
        _expert_kernel,
        grid_spec=grid_spec,
        out_shape=jax.ShapeDtypeStruct((cap * TOK_SUB, LANES), F32),
        compiler_params=pltpu.CompilerParams(
            dimension_semantics=("arbitrary",), vmem_limit_bytes=VMEM_LIMIT),
        name="experts",
    )(blk_expert, blk_next, blk_valid, xs, w1, b1, w2, b2)


def _combine_kernel(dest_ref, ys_hbm, x1_ref, gate_ref, p_ref, pg_ref, wg_ref, wp_ref, fg_ref,
                    out_ref, ybuf, sems, *, final):
    i = pl.program_id(0)
    nsteps = pl.num_programs(0)
    buf_tokens = TOP_K * COMB_T

    def issue_all(step, slot):
        tbl = step * buf_tokens

        def issue(tb, carry):
            for r in range(DMA_UNROLL):
                for k in range(TOP_K):
                    row = k * COMB_T + tb * DMA_UNROLL + r
                    pltpu.make_async_copy(
                        ys_hbm.at[_tile_rows(dest_ref[tbl + row]), :],
                        ybuf.at[_tile_rows(slot * buf_tokens + row), :],
                        sems.at[slot]).start(priority=(r * TOP_K + k) % 2)
            return carry

        lax.fori_loop(0, COMB_T // DMA_UNROLL, issue, 0)

    def drain_all(slot):
        def drain(tb, carry):
            for _ in range(DMA_UNROLL * TOP_K):
                pltpu.make_async_copy(ys_hbm.at[_tile_rows(0), :],
                                      ybuf.at[_tile_rows(slot * buf_tokens), :], sems.at[slot]).wait()
            return carry

        lax.fori_loop(0, COMB_T // DMA_UNROLL, drain, 0)

    @pl.when(i == 0)
    def _():
        issue_all(0, 0)

    for slot in range(2):
        @pl.when(jnp.logical_and(i % 2 == slot, i + 1 < nsteps))
        def _():
            issue_all(i + 1, 1 - slot)

    for slot in range(2):
        @pl.when(i % 2 == slot)
        def _():
            drain_all(slot)

            for sub in range(COMB_T // COMB_SUB):
                rows = slice(sub * COMB_SUB, (sub + 1) * COMB_SUB)
                x2 = x1_ref[rows, :]
                for k in range(TOP_K):
                    x2 = x2 + gate_ref[rows, k:k + 1] * _load_token_tiles(
                        ybuf, COMB_SUB, slot * buf_tokens + k * COMB_T + sub * COMB_SUB)
                hp = _rms(x2, pg_ref[...]).astype(BF16)
                gate = _sigmoid(jnp.dot(hp, wg_ref[...], preferred_element_type=F32))
                emb = jnp.dot(p_ref[rows, :].astype(BF16), wp_ref[...], preferred_element_type=F32)
                x3 = x2 + gate * emb
                if final:
                    x3 = _rms(x3, fg_ref[...])
                out_ref[rows, :] = x3


def _combine(dest_flat, ys, x1, gates_t, p2d, pg, wg, wp, fg, final):
    n = x1.shape[0]
    grid_spec = pltpu.PrefetchScalarGridSpec(
        num_scalar_prefetch=1,
        grid=(n // COMB_T,),
        in_specs=[
            pl.BlockSpec(memory_space=pl.ANY),
            pl.BlockSpec((COMB_T, D_MODEL), lambda i, d: (i, 0)),
            pl.BlockSpec((COMB_T, SUBLANES), lambda i, d: (i, 0)),
            pl.BlockSpec((COMB_T, PLE_DIM), lambda i, d: (i, 0)),
            pl.BlockSpec((1, D_MODEL), lambda i, d: (0, 0)),
            pl.BlockSpec((D_MODEL, D_MODEL), lambda i, d: (0, 0)),
            pl.BlockSpec((PLE_DIM, D_MODEL), lambda i, d: (0, 0)),
            pl.BlockSpec((1, D_MODEL), lambda i, d: (0, 0)),
        ],
        out_specs=pl.BlockSpec((COMB_T, D_MODEL), lambda i, d: (i, 0)),
        scratch_shapes=[pltpu.VMEM((2 * TOP_K * COMB_T * TOK_SUB, LANES), F32),
                        pltpu.SemaphoreType.DMA((2,))],
    )
    return pl.pallas_call(
        functools.partial(_combine_kernel, final=final),
        grid_spec=grid_spec,
        out_shape=jax.ShapeDtypeStruct((n, D_MODEL), F32),
        compiler_params=pltpu.CompilerParams(
            dimension_semantics=("arbitrary",), vmem_limit_bytes=VMEM_LIMIT),
        name="combine",
    )(dest_flat, ys, x1, gates_t, p2d, pg, wg, wp, fg)


def _layer(x2d, p2d, bsz, seq, mix_norm, w_in, sgu_ln_g, sgu_ln_b, sgu_w, sgu_b, conv_w, conv_b,
           dt_bias, a_log, d_skip, ssm_norm, w_branch_a, w_branch_b, w_out, ffn_norm, w_router,
           b_router, w1, b1, w2, b2, ple_norm, w_ple_gate, w_ple_proj, final_norm, final):
    n = x2d.shape[0]
    row = lambda a: a.reshape(1, -1).astype(F32)

    w_main = jnp.concatenate([w_in[:, :OFF_DT], w_in[:, OFF_GA:]], axis=1).astype(BF16)
    w_dt = jnp.pad(w_in[:, OFF_DT:OFF_GA], ((0, 0), (0, LANES - SSM_HEADS))).astype(BF16)
    pos = jnp.arange(SGU_LEN)
    allowed = (pos[None, :] // CHUNK) <= (pos[:, None] // CHUNK)
    wsgu = jnp.where(allowed[None], sgu_w, 0.0).astype(BF16)
    bsgu = jnp.repeat(sgu_b.T, SGU_HEAD_DIM, axis=1).astype(F32)
    dtb = dt_bias.reshape(-1, 1).astype(F32)
    acol = -jnp.exp(a_log.astype(F32)).reshape(-1, 1)
    dskip = jnp.repeat(d_skip.astype(F32), SSM_HEAD_DIM).reshape(1, -1)
    head_of_col = jnp.arange(SSM_INNER) // SSM_HEAD_DIM
    eexp = (jnp.arange(LANES)[:, None] == head_of_col[None, :]).astype(BF16)
    triu = (jnp.arange(SSD_CHUNK)[:, None] <= jnp.arange(SSD_CHUNK)[None, :]).astype(BF16)
    t_out = jnp.arange((SSM_CONV - 1) * SSD_CHUNK)
    src = SSD_CHUNK + t_out % SSD_CHUNK - (SSM_CONV - 1) + t_out // SSD_CHUNK
    shift = (src[:, None] == jnp.arange(2 * SSD_CHUNK)[None, :]).astype(BF16)
    consts = [row(sgu_ln_g), row(sgu_ln_b), wsgu, bsgu, conv_w.astype(F32), row(conv_b), dtb, acol,
              dskip, row(ssm_norm), eexp, shift, triu]

    proj, dt_raw = _in_proj(x2d, row(mix_norm), w_main, w_dt)
    x1 = _mixer(proj, dt_raw, x2d, bsz, seq, consts, w_branch_a.astype(BF16),
                w_branch_b.astype(BF16), w_out.astype(BF16))

    wr_f = jnp.pad(w_router.astype(F32), ((0, 0), (0, LANES - N_EXPERTS)))
    wr_hi = wr_f.astype(BF16)
    wr_pad = jnp.stack([wr_hi, (wr_f - wr_hi.astype(F32)).astype(BF16)])
    upper = (jnp.arange(ROUTE_T)[:, None] < jnp.arange(ROUTE_T)[None, :]).astype(BF16)
    hp, idx, gates, rank, cnt = _router(x1, row(ffn_norm), wr_pad, b_router.reshape(-1, 1).astype(F32),
                                        upper)

    counts = cnt[:, 0]
    padded = (counts + MOE_BLOCK - 1) // MOE_BLOCK * MOE_BLOCK
    pend = jnp.cumsum(padded)
    pstart = pend - padded
    nk = n * TOP_K
    cap = (nk + MOE_BLOCK - 1) // MOE_BLOCK * MOE_BLOCK + N_EXPERTS * MOE_BLOCK
    n_blocks = cap // MOE_BLOCK
    ex = jnp.arange(N_EXPERTS, dtype=I32)
    dest = rank + jnp.sum(jnp.where(idx[..., None] == ex, pstart.astype(I32), 0), axis=-1)
    blk_start = jnp.arange(n_blocks, dtype=I32) * MOE_BLOCK
    blk_expert = jnp.minimum(jnp.sum((pend[None, :] <= blk_start[:, None]).astype(I32), axis=1),
                             N_EXPERTS - 1).astype(I32)
    of_block = blk_expert[:, None] == ex[None, :]
    per_block = lambda v: jnp.sum(jnp.where(of_block, v[None, :].astype(I32), 0), axis=1)
    blk_valid = jnp.where(blk_start < pend[-1],
                          jnp.clip(per_block(pstart + counts) - blk_start, 0, MOE_BLOCK), 0).astype(I32)
    live = jnp.where(counts > 0, ex, N_EXPERTS)
    later = jnp.concatenate([lax.cummin(live[::-1])[::-1][1:], jnp.full((1,), N_EXPERTS, I32)])
    blk_next = per_block(jnp.where(later < N_EXPERTS, later, -1))

    dest_flat = dest.reshape(TOP_K, n // DISP_T, DISP_T).transpose(1, 0, 2).reshape(-1)
    used_end = pstart + (counts + MOE_SUB - 1) // MOE_SUB * MOE_SUB
    pad_tbl = jnp.concatenate([pstart + counts, used_end]).astype(I32)

    xs = _dispatch(dest_flat, pad_tbl, hp, cap)
    ys = _experts(blk_expert, blk_next, blk_valid, xs, w1.astype(F32),
                  b1.reshape(N_EXPERTS, 1, -1).astype(F32), w2.astype(F32),
                  b2.reshape(N_EXPERTS, 1, -1).astype(F32))
    return _combine(dest_flat, ys, x1, gates.T, p2d, row(ple_norm), w_ple_gate.astype(BF16),
                    w_ple_proj.astype(BF16), row(final_norm), final)


def kernel(x, p, mix_norm, w_in, sgu_ln_g, sgu_ln_b, sgu_w, sgu_b, conv_w, conv_b, dt_bias, a_log,
           d_skip, ssm_norm, w_branch_a, w_branch_b, w_out, ffn_norm, w_router, b_router, w1, b1,
           w2, b2, ple_norm, w_ple_gate, w_ple_proj, final_norm):
    bsz, seq, d = x.shape
    depth = w_in.shape[0]
    assert d == D_MODEL and seq % MIX_T == 0 and (bsz * seq) % max(IN_TM, ROUTE_T, DISP_T) == 0
    assert ROUTE_T == DISP_T == COMB_T
    x2d = x.reshape(bsz * seq, d)
    for i in range(depth):
        x2d = _layer(x2d, p[i].reshape(bsz * seq, PLE_DIM), bsz, seq, mix_norm[i], w_in[i],
                     sgu_ln_g[i], sgu_ln_b[i], sgu_w[i], sgu_b[i], conv_w[i], conv_b[i], dt_bias[i],
                     a_log[i], d_skip[i], ssm_norm[i], w_branch_a[i], w_branch_b[i], w_out[i],
                     ffn_norm[i], w_router[i], b_router[i], w1[i], b1[i], w2[i], b2[i], ple_norm[i],
                     w_ple_gate[i], w_ple_proj[i], final_norm, final=(i == depth - 1))
    return x2d.reshape(bsz, seq, d)
```

```python
import functools
import math

import jax
import jax.numpy as jnp
from jax import lax
from jax.experimental import pallas as pl
from jax.experimental.pallas import tpu as pltpu

F32 = jnp.float32
BF16 = jnp.bfloat16
I32 = jnp.int32

D_MODEL = 1024
CHUNK = 64
PLE_DIM = 256
RMS_EPS = 1e-6
LN_EPS = 1e-5

SGU_HEADS = 8
SGU_HEAD_DIM = 128
SGU_WIDTH = SGU_HEADS * SGU_HEAD_DIM
SGU_LEN = 128

SSM_HEADS = 16
SSM_HEAD_DIM = 64
SSM_INNER = SSM_HEADS * SSM_HEAD_DIM
SSM_GROUPS = 4
SSM_STATE = 128
SSM_CONV = 4
SSD_CHUNK = 128
SSM_CONV_DIM = SSM_INNER + 2 * SSM_GROUPS * SSM_STATE

N_EXPERTS = 32
TOP_K = 4
D_FF_EXPERT = 1024
SWIGLU_LIMIT = 7.0
SWIGLU_ALPHA = 1.702
MOE_BLOCK = 512
MOE_SUB = 256

OFF_U = 0
OFF_V = OFF_U + SGU_WIDTH
OFF_Z = OFF_V + SGU_WIDTH
OFF_XBC = OFF_Z + SSM_INNER
OFF_DT = OFF_XBC + SSM_CONV_DIM
OFF_GA = OFF_DT + SSM_HEADS
OFF_GB = OFF_GA + D_MODEL
IN_PROJ_DIM = OFF_GB + D_MODEL

LANES = 128
SUBLANES = 8
PROJ_MAIN = IN_PROJ_DIM - SSM_HEADS
PROJ_TILE = 1024
TOK_SUB = D_MODEL // LANES

VMEM_LIMIT = 56 * 1024 * 1024

IN_TM = 512
IN_TN = 1792
MIX_T = 512
ROUTE_T = 512
DISP_T = 512
COMB_T = DISP_T
COMB_SUB = 128
DMA_UNROLL = 8
ZERO_RUN = 64
CAST_ROWS = 128


def _sigmoid(x):
    return 1.0 / (1.0 + jnp.exp(-x))


def _gelu_exact(x):
    return 0.5 * x * (1.0 + lax.erf(x * (1.0 / math.sqrt(2.0))))


def _softplus(x):
    return jnp.maximum(x, 0.0) + jnp.log1p(jnp.exp(-jnp.abs(x)))


def _rms(x, g):
    ms = jnp.mean(x * x, axis=-1, keepdims=True)
    return x * lax.rsqrt(ms + RMS_EPS) * g


def _store_token_tiles(ref, val, rows, start=0):
    for j in range(TOK_SUB):
        ref[pl.ds(start * TOK_SUB + j, rows, stride=TOK_SUB), :] = val[:, j * LANES:(j + 1) * LANES]


def _load_token_tiles(ref, rows, start=0):
    return jnp.concatenate(
        [ref[pl.ds(start * TOK_SUB + j, rows, stride=TOK_SUB), :] for j in range(TOK_SUB)], axis=1)


def _tile_rows(idx):
    return pl.ds(pl.multiple_of(idx * TOK_SUB, TOK_SUB), TOK_SUB)


def _inproj_kernel(x_ref, g_ref, w_ref, wdt_ref, proj_ref, dt_ref, h_scr):
    @pl.when(pl.program_id(0) == 0)
    def _():
        h_scr[...] = jnp.zeros_like(h_scr)

    h = h_scr[...]
    dt_ref[...] = jnp.dot(h, wdt_ref[...], preferred_element_type=F32)
    for c in range(PROJ_MAIN // IN_TN):
        cols = slice(c * IN_TN, (c + 1) * IN_TN)
        proj_ref[:, cols] = jnp.dot(h, w_ref[:, cols], preferred_element_type=F32).astype(BF16)
    h_scr[...] = _rms(x_ref[...], g_ref[...]).astype(BF16)


def _in_proj(x2d, g, w_main, w_dt):
    n = x2d.shape[0]
    ntiles = n // IN_TM
    return pl.pallas_call(
        _inproj_kernel,
        grid=(ntiles + 1,),
        in_specs=[
            pl.BlockSpec((IN_TM, D_MODEL), lambda s: (jnp.minimum(s, ntiles - 1), 0)),
            pl.BlockSpec((1, D_MODEL), lambda s: (0, 0)),
            pl.BlockSpec((D_MODEL, PROJ_MAIN), lambda s: (0, 0)),
            pl.BlockSpec((D_MODEL, LANES), lambda s: (0, 0)),
        ],
        out_specs=[
            pl.BlockSpec((IN_TM, PROJ_MAIN), lambda s: (jnp.maximum(s - 1, 0), 0)),
            pl.BlockSpec((IN_TM, LANES), lambda s: (jnp.maximum(s - 1, 0), 0)),
        ],
        out_shape=[
            jax.ShapeDtypeStruct((n, PROJ_MAIN), BF16),
            jax.ShapeDtypeStruct((n, LANES), F32),
        ],
        scratch_shapes=[pltpu.VMEM((IN_TM, D_MODEL), BF16)],
        compiler_params=pltpu.CompilerParams(
            dimension_semantics=("arbitrary",), vmem_limit_bytes=VMEM_LIMIT),
        name="in_proj",
    )(x2d, g, w_main, w_dt)


def _split3(a):
    a1 = a.astype(BF16)
    r1 = a - a1.astype(F32)
    a2 = r1.astype(BF16)
    a3 = (r1 - a2.astype(F32)).astype(BF16)
    return a1, a2, a3


def _dot3(parts, w):
    out = jnp.dot(parts[0], w, preferred_element_type=F32)
    for p in parts[1:]:
        out = out + jnp.dot(p, w, preferred_element_type=F32)
    return out


def _mixer_kernel(u_ref, v_ref, z_ref, xb0_ref, xb1_ref, dt_ref, ga_ref, gb_ref, x_ref,
                  lng_ref, lnb_ref, wsgu_ref, bsgu_ref, convw_ref, convb_ref, dtb_ref, acol_ref,
                  dskip_ref, normw_ref, eexp_ref, shift_ref, triu_ref, wa_ref, wb_ref, wo_ref,
                  x1_ref,
                  tail_scr, xc_scr, state_scr, ya_scr, yb_scr, yap_scr, ybp_scr, *, tps):
    nchunk = MIX_T // SSD_CHUNK
    s = pl.program_id(0)

    @pl.when(s % tps == 0)
    def _():
        tail_scr[...] = jnp.zeros_like(tail_scr)
        state_scr[...] = jnp.zeros_like(state_scr)

    @pl.when(s == 0)
    def _():
        yap_scr[...] = jnp.zeros_like(yap_scr)
        ybp_scr[...] = jnp.zeros_like(ybp_scr)

    xb_refs = (xb0_ref, xb1_ref)
    for c in range(nchunk):
        r0 = c * SSD_CHUNK
        for hf, xb_ref in enumerate(xb_refs):
            cols = slice(hf * PROJ_TILE, (hf + 1) * PROJ_TILE)
            cur = xb_ref[r0:r0 + SSD_CHUNK, :]
            prev = tail_scr[:, cols] if c == 0 else xb_ref[r0 - SSD_CHUNK:r0, :]
            sh = jnp.dot(shift_ref[...], jnp.concatenate([prev, cur], axis=0), preferred_element_type=F32)
            acc = convb_ref[:, cols] + convw_ref[SSM_CONV - 1:SSM_CONV, cols] * cur.astype(F32)
            for j in range(SSM_CONV - 1):
                acc = acc + convw_ref[j:j + 1, cols] * sh[j * SSD_CHUNK:(j + 1) * SSD_CHUNK, :]
            xc_scr[r0:r0 + SSD_CHUNK, cols] = acc * _sigmoid(acc)
    for hf, xb_ref in enumerate(xb_refs):
        tail_scr[:, hf * PROJ_TILE:(hf + 1) * PROJ_TILE] = xb_ref[MIX_T - SSD_CHUNK:MIX_T, :]

    row_i = lax.broadcasted_iota(I32, (SSD_CHUNK, SSD_CHUNK), 0)
    col_i = lax.broadcasted_iota(I32, (SSD_CHUNK, SSD_CHUNK), 1)
    tril = row_i >= col_i
    lane_hi = col_i >= SSM_HEAD_DIM
    eexp = eexp_ref[...]
    triu = triu_ref[...]

    for c in range(nchunk):
        rows = slice(c * SSD_CHUNK, (c + 1) * SSD_CHUNK)

        ug = _gelu_exact(u_ref[rows, :].astype(F32))
        vg = _gelu_exact(v_ref[rows, :].astype(F32))
        mu = jnp.mean(vg, axis=-1, keepdims=True)
        vc = vg - mu
        var = jnp.mean(vc * vc, axis=-1, keepdims=True)
        vn = (vc * lax.rsqrt(var + LN_EPS) * lng_ref[...] + lnb_ref[...]).astype(BF16)
        for g in range(SGU_HEADS):
            cols = slice(g * SGU_HEAD_DIM, (g + 1) * SGU_HEAD_DIM)
            mixed = jnp.dot(wsgu_ref[g], vn[:, cols], preferred_element_type=F32)
            ya_scr[rows, cols] = (ug[:, cols] * (mixed + bsgu_ref[:, cols])).astype(BF16)

        dt_t = _softplus(dt_ref[rows, :].T[0:SSM_HEADS, :] + dtb_ref[...])
        acs_t = _dot3(_split3(dt_t * acol_ref[...]), triu)
        w_t = jnp.exp(acs_t[:, SSD_CHUNK - 1:SSD_CHUNK] - acs_t) * dt_t
        acs = jnp.concatenate([acs_t, jnp.zeros((LANES - SSM_HEADS, SSD_CHUNK), F32)], axis=0).T
        aend = jnp.broadcast_to(acs[SSD_CHUNK - 1:SSD_CHUNK, :], (SUBLANES, LANES))
        dec = jnp.exp(_dot3(_split3(aend), eexp)[0:1, :])

        for gi in range(SSM_GROUPS):
            bg = xc_scr[rows, SSM_INNER + gi * SSM_STATE:SSM_INNER + (gi + 1) * SSM_STATE]
            cg = xc_scr[rows, SSM_INNER + (SSM_GROUPS + gi) * SSM_STATE:
                        SSM_INNER + (SSM_GROUPS + gi + 1) * SSM_STATE]
            bg_t = bg.T
            cb = jnp.dot(cg.astype(BF16), bg_t.astype(BF16), preferred_element_type=F32)
            for pr in range(2):
                lb = gi * 2 + pr
                cols = slice(lb * LANES, (lb + 1) * LANES)
                xpair = xc_scr[rows, cols]
                hpair = state_scr[:, cols]
                acc = jnp.zeros((SSD_CHUNK, LANES), F32)
                st = jnp.zeros((SSM_STATE, LANES), F32)
                for hh in range(2):
                    h = lb * 2 + hh
                    lmask = lane_hi if hh == 1 else jnp.logical_not(lane_hi)
                    xm = jnp.where(lmask, xpair, 0.0).astype(BF16)
                    hm = jnp.where(lmask, hpair, 0.0).astype(BF16)
                    colb = acs[:, h:h + 1]
                    rowb = acs_t[h:h + 1, :]
                    decay = jnp.exp(jnp.where(tril, colb - rowb, -jnp.inf))
                    m_h = (cb * decay * dt_t[h:h + 1, :]).astype(BF16)
                    c_h = (cg * jnp.exp(colb)).astype(BF16)
                    acc = acc + jnp.dot(m_h, xm, preferred_element_type=F32)
                    acc = acc + jnp.dot(c_h, hm, preferred_element_type=F32)
                    btw = (bg_t * w_t[h:h + 1, :]).astype(BF16)
                    st = st + jnp.dot(btw, xm, preferred_element_type=F32)
                state_scr[:, cols] = hpair * dec[:, cols] + st
                y = acc + dskip_ref[:, cols] * xpair
                zz = z_ref[rows, cols].astype(F32)
                xc_scr[rows, cols] = y * (zz * _sigmoid(zz))
            gcols = slice(gi * 2 * LANES, (gi + 1) * 2 * LANES)
            yg = xc_scr[rows, gcols]
            ms = jnp.mean(yg * yg, axis=-1, keepdims=True)
            yb_scr[rows, gcols] = (yg * lax.rsqrt(ms + RMS_EPS) * normw_ref[:, gcols]).astype(BF16)

    ma = jnp.dot(yap_scr[...], wa_ref[...], preferred_element_type=F32)
    mb = jnp.dot(ybp_scr[...], wb_ref[...], preferred_element_type=F32)
    merged = (_sigmoid(ga_ref[...].astype(F32)) * ma + _sigmoid(gb_ref[...].astype(F32)) * mb)
    x1_ref[...] = x_ref[...] + jnp.dot(merged.astype(BF16), wo_ref[...], preferred_element_type=F32)
    yap_scr[...] = ya_scr[...]
    ybp_scr[...] = yb_scr[...]


def _mixer(proj, dt_raw, x2d, bsz, seq, consts, wa, wb, wo):
    n = x2d.shape[0]
    tps = seq // MIX_T
    ntiles = bsz * tps
    cur = lambda s: jnp.minimum(s, ntiles - 1)
    prev = lambda s: jnp.maximum(s - 1, 0)

    def pcol(off, tile):
        return pl.BlockSpec((MIX_T, PROJ_TILE), lambda s: (tile(s), off // PROJ_TILE))

    def full(a):
        nd = a.ndim
        return pl.BlockSpec(a.shape, lambda s, nd=nd: (0,) * nd)

    in_specs = [pcol(OFF_U, cur), pcol(OFF_V, cur), pcol(OFF_Z, cur),
                pcol(OFF_XBC, cur), pcol(OFF_XBC + PROJ_TILE, cur),
                pl.BlockSpec((MIX_T, LANES), lambda s: (cur(s), 0)),
                pcol(OFF_GA - SSM_HEADS, prev), pcol(OFF_GB - SSM_HEADS, prev),
                pl.BlockSpec((MIX_T, D_MODEL), lambda s: (prev(s), 0))]
    in_specs += [full(a) for a in consts] + [full(wa), full(wb), full(wo)]
    return pl.pallas_call(
        functools.partial(_mixer_kernel, tps=tps),
        grid=(ntiles + 1,),
        in_specs=in_specs,
        out_specs=pl.BlockSpec((MIX_T, D_MODEL), lambda s: (prev(s), 0)),
        out_shape=jax.ShapeDtypeStruct((n, D_MODEL), F32),
        scratch_shapes=[
            pltpu.VMEM((SSD_CHUNK, SSM_CONV_DIM), BF16),
            pltpu.VMEM((MIX_T, SSM_CONV_DIM), F32),
            pltpu.VMEM((SSM_STATE, SSM_INNER), F32),
            pltpu.VMEM((MIX_T, SGU_WIDTH), BF16),
            pltpu.VMEM((MIX_T, SSM_INNER), BF16),
            pltpu.VMEM((MIX_T, SGU_WIDTH), BF16),
            pltpu.VMEM((MIX_T, SSM_INNER), BF16),
        ],
        compiler_params=pltpu.CompilerParams(
            dimension_semantics=("arbitrary",), vmem_limit_bytes=VMEM_LIMIT),
        name="mixer",
    )(proj, proj, proj, proj, proj, dt_raw, proj, proj, x2d, *consts, wa, wb, wo)


def _router_kernel(x_ref, g_ref, wr_ref, br_ref, upper_ref, hp_ref, idx_ref, gate_ref, rank_ref, cnt_ref,
                   carry_scr):
    @pl.when(pl.program_id(0) == 0)
    def _():
        carry_scr[...] = jnp.zeros_like(carry_scr)

    h = _rms(x_ref[...], g_ref[...])
    _store_token_tiles(hp_ref, h, ROUTE_T)

    h_hi = h.astype(BF16)
    h_lo = (h - h_hi.astype(F32)).astype(BF16)
    logits = (jnp.dot(h_hi, wr_ref[0], preferred_element_type=F32)
              + jnp.dot(h_hi, wr_ref[1], preferred_element_type=F32)
              + jnp.dot(h_lo, wr_ref[0], preferred_element_type=F32))
    lt = logits.T[0:N_EXPERTS, :] + br_ref[...]
    eidx = lax.broadcasted_iota(I32, (N_EXPERTS, ROUTE_T), 0).astype(F32)
    vals = lt
    sel_any = jnp.zeros((N_EXPERTS, ROUTE_T), F32)
    sels, tops = [], []
    for k in range(TOP_K):
        m = jnp.max(vals, axis=0, keepdims=True)
        first = jnp.min(jnp.where(vals == m, eidx, float(N_EXPERTS)), axis=0, keepdims=True)
        sel = eidx == first
        vals = jnp.where(sel, -jnp.inf, vals)
        sel_f = sel.astype(F32)
        sel_any = sel_any + sel_f
        sels.append(sel_f)
        tops.append(m)
        idx_ref[k:k + 1, :] = first.astype(I32)
    es = [jnp.exp(t - tops[0]) for t in tops]
    denom = functools.reduce(lambda a, b: a + b, es)
    for k in range(TOP_K):
        gate_ref[k:k + 1, :] = es[k] / denom
    gate_ref[TOP_K:, :] = jnp.zeros((SUBLANES - TOP_K, ROUTE_T), F32)

    excl = jnp.dot(sel_any.astype(BF16), upper_ref[...], preferred_element_type=F32) + carry_scr[:, 0:1]
    for k in range(TOP_K):
        rank_ref[k:k + 1, :] = jnp.sum(sels[k] * excl, axis=0, keepdims=True).astype(I32)
    new_carry = carry_scr[...] + jnp.sum(sel_any, axis=1, keepdims=True)
    carry_scr[...] = new_carry
    cnt_ref[...] = new_carry.astype(I32)


def _router(x1, g, wr_pad, br_col, upper):
    n = x1.shape[0]
    per_k = pl.BlockSpec((TOP_K, ROUTE_T), lambda i: (0, i))
    return pl.pallas_call(
        _router_kernel,
        grid=(n // ROUTE_T,),
        in_specs=[
            pl.BlockSpec((ROUTE_T, D_MODEL), lambda i: (i, 0)),
            pl.BlockSpec((1, D_MODEL), lambda i: (0, 0)),
            pl.BlockSpec((2, D_MODEL, LANES), lambda i: (0, 0, 0)),
            pl.BlockSpec((N_EXPERTS, 1), lambda i: (0, 0)),
            pl.BlockSpec((ROUTE_T, ROUTE_T), lambda i: (0, 0)),
        ],
        out_specs=[
            pl.BlockSpec((ROUTE_T * TOK_SUB, LANES), lambda i: (i, 0)),
            per_k,
            pl.BlockSpec((SUBLANES, ROUTE_T), lambda i: (0, i)),
            per_k,
            pl.BlockSpec((N_EXPERTS, LANES), lambda i: (0, 0)),
        ],
        out_shape=[
            jax.ShapeDtypeStruct((n * TOK_SUB, LANES), F32),
            jax.ShapeDtypeStruct((TOP_K, n), I32),
            jax.ShapeDtypeStruct((SUBLANES, n), F32),
            jax.ShapeDtypeStruct((TOP_K, n), I32),
            jax.ShapeDtypeStruct((N_EXPERTS, LANES), I32),
        ],
        scratch_shapes=[pltpu.VMEM((N_EXPERTS, LANES), F32)],
        compiler_params=pltpu.CompilerParams(
            dimension_semantics=("arbitrary",), vmem_limit_bytes=VMEM_LIMIT),
        name="router",
    )(x1, g, wr_pad, br_col, upper)


def _dispatch_kernel(dest_ref, pad_ref, hp_ref, xs_hbm, zero_scr, sem_rows):
    i = pl.program_id(0)
    tbl = i * (TOP_K * DISP_T)

    def zero_copy(s):
        return pltpu.make_async_copy(zero_scr.at[0:TOK_SUB, :], xs_hbm.at[_tile_rows(s), :], sem_rows)

    def zero_run(s):
        rows = pl.ds(pl.multiple_of(s * TOK_SUB, TOK_SUB), ZERO_RUN * TOK_SUB)
        return pltpu.make_async_copy(zero_scr, xs_hbm.at[rows, :], sem_rows)

    def zero_pads(act):
        for e in range(N_EXPERTS):
            lo, hi = pad_ref[e], pad_ref[N_EXPERTS + e]
            nrun = (hi - lo) // ZERO_RUN
            lax.fori_loop(0, nrun, lambda r, c: (act(zero_run(lo + r * ZERO_RUN)), c)[1], 0)
            lax.fori_loop(lo + nrun * ZERO_RUN, hi, lambda s, c: (act(zero_copy(s)), c)[1], 0)

    @pl.when(i == 0)
    def _():
        zero_scr[...] = jnp.zeros_like(zero_scr)
        zero_pads(lambda d: d.start())
        zero_pads(lambda d: d.wait())

    def tile_copy(t, k):
        d = dest_ref[tbl + k * DISP_T + t]
        return pltpu.make_async_copy(hp_ref.at[_tile_rows(t), :], xs_hbm.at[_tile_rows(d), :], sem_rows)

    def issue(tb, carry):
        for r in range(DMA_UNROLL):
            for k in range(TOP_K):
                tile_copy(tb * DMA_UNROLL + r, k).start(priority=(r * TOP_K + k) % 2)
        return carry

    lax.fori_loop(0, DISP_T // DMA_UNROLL, issue, 0)

    def drain(tb, carry):
        for _ in range(DMA_UNROLL * TOP_K):
            zero_copy(0).wait()
        return carry

    lax.fori_loop(0, DISP_T // DMA_UNROLL, drain, 0)


def _dispatch(dest_flat, pad_tbl, hp, cap):
    n = hp.shape[0] // TOK_SUB
    grid_spec = pltpu.PrefetchScalarGridSpec(
        num_scalar_prefetch=2,
        grid=(n // DISP_T,),
        in_specs=[pl.BlockSpec((DISP_T * TOK_SUB, LANES), lambda i, d, p: (i, 0))],
        out_specs=pl.BlockSpec(memory_space=pl.ANY),
        scratch_shapes=[pltpu.VMEM((ZERO_RUN * TOK_SUB, LANES), F32), pltpu.SemaphoreType.DMA],
    )
    return pl.pallas_call(
        _dispatch_kernel,
        grid_spec=grid_spec,
        out_shape=jax.ShapeDtypeStruct((cap * TOK_SUB, LANES), F32),
        compiler_params=pltpu.CompilerParams(dimension_semantics=("arbitrary",)),
        name="dispatch",
    )(dest_flat, pad_tbl, hp)


def _expert_kernel(be_ref, nxt_ref, nv_ref, xs_ref, w1_hbm, b1_ref, w2_hbm, b2_ref, ys_ref,
                   w1f_scr, w2f_scr, w1b_scr, w2b_scr, sems):
    b = pl.program_id(0)
    valid = nv_ref[b]
    used = valid > 0
    new_expert = jnp.logical_or(b == 0, be_ref[b] != be_ref[jnp.maximum(b - 1, 0)])

    def fetch(e):
        return (pltpu.make_async_copy(w1_hbm.at[e], w1f_scr, sems.at[0]),
                pltpu.make_async_copy(w2_hbm.at[e], w2f_scr, sems.at[1]))

    @pl.when(jnp.logical_and(used, b == 0))
    def _():
        for c in fetch(be_ref[0]):
            c.start()

    @pl.when(jnp.logical_and(used, new_expert))
    def _():
        for c in fetch(be_ref[b]):
            c.wait()

        def cast(rb, carry):
            rows = pl.ds(pl.multiple_of(rb * CAST_ROWS, CAST_ROWS), CAST_ROWS)
            w1b_scr[rows, :] = w1f_scr[rows, :].astype(BF16)
            w2b_scr[rows, :] = w2f_scr[rows, :].astype(BF16)
            return carry

        lax.fori_loop(0, D_MODEL // CAST_ROWS, cast, 0)

        @pl.when(nxt_ref[b] >= 0)
        def _():
            for c in fetch(nxt_ref[b]):
                c.start()

    def expert_mlp(rows):
        x = _load_token_tiles(xs_ref, rows).astype(BF16)
        hid = jnp.dot(x, w1b_scr[...], preferred_element_type=F32) + b1_ref[0]
        glu = jnp.minimum(hid[:, :D_FF_EXPERT], SWIGLU_LIMIT)
        lin = jnp.clip(hid[:, D_FF_EXPERT:], -SWIGLU_LIMIT, SWIGLU_LIMIT)
        act = glu * _sigmoid(SWIGLU_ALPHA * glu) * (lin + 1.0)
        y = jnp.dot(act.astype(BF16), w2b_scr[...], preferred_element_type=F32) + b2_ref[0]
        _store_token_tiles(ys_ref, y, rows)

    @pl.when(valid > MOE_SUB)
    def _():
        expert_mlp(MOE_BLOCK)

    @pl.when(jnp.logical_and(used, valid <= MOE_SUB))
    def _():
        expert_mlp(MOE_SUB)


def _experts(blk_expert, blk_next, blk_valid, xs, w1, b1, w2, b2):
    cap = xs.shape[0] // TOK_SUB
    n_blocks = cap // MOE_BLOCK
    grid_spec = pltpu.PrefetchScalarGridSpec(
        num_scalar_prefetch=3,
        grid=(n_blocks,),
        in_specs=[
            pl.BlockSpec((MOE_BLOCK * TOK_SUB, LANES), lambda b, be, nx, nv: (b, 0)),
            pl.BlockSpec(memory_space=pl.ANY),
            pl.BlockSpec((1, 1, 2 * D_FF_EXPERT), lambda b, be, nx, nv: (be[b], 0, 0)),
            pl.BlockSpec(memory_space=pl.ANY),
            pl.BlockSpec((1, 1, D_MODEL), lambda b, be, nx, nv: (be[b], 0, 0)),
        ],
        out_specs=pl.BlockSpec((MOE_BLOCK * TOK_SUB, LANES), lambda b, be, nx, nv: (b, 0)),
        scratch_shapes=[pltpu.VMEM((D_MODEL, 2 * D_FF_EXPERT), F32),
                        pltpu.VMEM((D_FF_EXPERT, D_MODEL), F32),
                        pltpu.VMEM((D_MODEL, 2 * D_FF_EXPERT), BF16),
                        pltpu.VMEM((D_FF_EXPERT, D_MODEL), BF16),
                        pltpu.SemaphoreType.DMA((2,))],
    )
    assert D_MODEL == D_FF_EXPERT
    assert MOE_BLOCK == 2 * MOE_SUB
    return pl.pallas_call(
        _expert_kernel,
        grid_spec=grid_spec,
        out_shape=jax.ShapeDtypeStruct((cap * TOK_SUB, LANES), F32),
        compiler_params=pltpu.CompilerParams(
            dimension_semantics=("arbitrary",), vmem_limit_bytes=VMEM_LIMIT),
        name="experts",
    )(blk_expert, blk_next, blk_valid, xs, w1, b1, w2, b2)


def _combine_kernel(dest_ref, ys_hbm, x1_ref, gate_ref, p_ref, pg_ref, wg_ref, wp_ref, fg_ref,
                    out_ref, ybuf, sems, *, final):
    i = pl.program_id(0)
    nsteps = pl.num_programs(0)
    buf_tokens = TOP_K * COMB_T

    def issue_all(step, slot):
        tbl = step * buf_tokens

        def issue(tb, carry):
            for r in range(DMA_UNROLL):
                for k in range(TOP_K):
                    row = k * COMB_T + tb * DMA_UNROLL + r
                    pltpu.make_async_copy(
                        ys_hbm.at[_tile_rows(dest_ref[tbl + row]), :],
                        ybuf.at[_tile_rows(slot * buf_tokens + row), :],
                        sems.at[slot]).start(priority=(r * TOP_K + k) % 2)
            return carry

        lax.fori_loop(0, COMB_T // DMA_UNROLL, issue, 0)

    def drain_all(slot):
        def drain(tb, carry):
            for _ in range(DMA_UNROLL * TOP_K):
                pltpu.make_async_copy(ys_hbm.at[_tile_rows(0), :],
                                      ybuf.at[_tile_rows(slot * buf_tokens), :], sems.at[slot]).wait()
            return carry

        lax.fori_loop(0, COMB_T // DMA_UNROLL, drain, 0)

    @pl.when(i == 0)
    def _():
        issue_all(0, 0)

    for slot in range(2):
        @pl.when(jnp.logical_and(i % 2 == slot, i + 1 < nsteps))
        def _():
            issue_all(i + 1, 1 - slot)

    for slot in range(2):
        @pl.when(i % 2 == slot)
        def _():
            drain_all(slot)

            for sub in range(COMB_T // COMB_SUB):
                rows = slice(sub * COMB_SUB, (sub + 1) * COMB_SUB)
                x2 = x1_ref[rows, :]
                for k in range(TOP_K):
                    x2 = x2 + gate_ref[rows, k:k + 1] * _load_token_tiles(
                        ybuf, COMB_SUB, slot * buf_tokens + k * COMB_T + sub * COMB_SUB)
                hp = _rms(x2, pg_ref[...]).astype(BF16)
                gate = _sigmoid(jnp.dot(hp, wg_ref[...], preferred_element_type=F32))
                emb = jnp.dot(p_ref[rows, :].astype(BF16), wp_ref[...], preferred_element_type=F32)
                x3 = x2 + gate * emb
                if final:
                    x3 = _rms(x3, fg_ref[...])
                out_ref[rows, :] = x3


def _combine(dest_flat, ys, x1, gates_t, p2d, pg, wg, wp, fg, final):
    n = x1.shape[0]
    grid_spec = pltpu.PrefetchScalarGridSpec(
        num_scalar_prefetch=1,
        grid=(n // COMB_T,),
        in_specs=[
            pl.BlockSpec(memory_space=pl.ANY),
            pl.BlockSpec((COMB_T, D_MODEL), lambda i, d: (i, 0)),
            pl.BlockSpec((COMB_T, SUBLANES), lambda i, d: (i, 0)),
            pl.BlockSpec((COMB_T, PLE_DIM), lambda i, d: (i, 0)),
            pl.BlockSpec((1, D_MODEL), lambda i, d: (0, 0)),
            pl.BlockSpec((D_MODEL, D_MODEL), lambda i, d: (0, 0)),
            pl.BlockSpec((PLE_DIM, D_MODEL), lambda i, d: (0, 0)),
            pl.BlockSpec((1, D_MODEL), lambda i, d: (0, 0)),
        ],
        out_specs=pl.BlockSpec((COMB_T, D_MODEL), lambda i, d: (i, 0)),
        scratch_shapes=[pltpu.VMEM((2 * TOP_K * COMB_T * TOK_SUB, LANES), F32),
                        pltpu.SemaphoreType.DMA((2,))],
    )
    return pl.pallas_call(
        functools.partial(_combine_kernel, final=final),
        grid_spec=grid_spec,
        out_shape=jax.ShapeDtypeStruct((n, D_MODEL), F32),
        compiler_params=pltpu.CompilerParams(
            dimension_semantics=("arbitrary",), vmem_limit_bytes=VMEM_LIMIT),
        name="combine",
    )(dest_flat, ys, x1, gates_t, p2d, pg, wg, wp, fg)


def _layer(x2d, p2d, bsz, seq, mix_norm, w_in, sgu_ln_g, sgu_ln_b, sgu_w, sgu_b, conv_w, conv_b,
           dt_bias, a_log, d_skip, ssm_norm, w_branch_a, w_branch_b, w_out, ffn_norm, w_router,
           b_router, w1, b1, w2, b2, ple_norm, w_ple_gate, w_ple_proj, final_norm, final):
    n = x2d.shape[0]
    row = lambda a: a.reshape(1, -1).astype(F32)

    w_main = jnp.concatenate([w_in[:, :OFF_DT], w_in[:, OFF_GA:]], axis=1).astype(BF16)
    w_dt = jnp.pad(w_in[:, OFF_DT:OFF_GA], ((0, 0), (0, LANES - SSM_HEADS))).astype(BF16)
    pos = jnp.arange(SGU_LEN)
    allowed = (pos[None, :] // CHUNK) <= (pos[:, None] // CHUNK)
    wsgu = jnp.where(allowed[None], sgu_w, 0.0).astype(BF16)
    bsgu = jnp.repeat(sgu_b.T, SGU_HEAD_DIM, axis=1).astype(F32)
    dtb = dt_bias.reshape(-1, 1).astype(F32)
    acol = -jnp.exp(a_log.astype(F32)).reshape(-1, 1)
    dskip = jnp.repeat(d_skip.astype(F32), SSM_HEAD_DIM).reshape(1, -1)
    head_of_col = jnp.arange(SSM_INNER) // SSM_HEAD_DIM
    eexp = (jnp.arange(LANES)[:, None] == head_of_col[None, :]).astype(BF16)
    triu = (jnp.arange(SSD_CHUNK)[:, None] <= jnp.arange(SSD_CHUNK)[None, :]).astype(BF16)
    t_out = jnp.arange((SSM_CONV - 1) * SSD_CHUNK)
    src = SSD_CHUNK + t_out % SSD_CHUNK - (SSM_CONV - 1) + t_out // SSD_CHUNK
    shift = (src[:, None] == jnp.arange(2 * SSD_CHUNK)[None, :]).astype(BF16)
    consts = [row(sgu_ln_g), row(sgu_ln_b), wsgu, bsgu, conv_w.astype(F32), row(conv_b), dtb, acol,
              dskip, row(ssm_norm), eexp, shift, triu]

    proj, dt_raw = _in_proj(x2d, row(mix_norm), w_main, w_dt)
    x1 = _mixer(proj, dt_raw, x2d, bsz, seq, consts, w_branch_a.astype(BF16),
                w_branch_b.astype(BF16), w_out.astype(BF16))

    wr_f = jnp.pad(w_router.astype(F32), ((0, 0), (0, LANES - N_EXPERTS)))
    wr_hi = wr_f.astype(BF16)
    wr_pad = jnp.stack([wr_hi, (wr_f - wr_hi.astype(F32)).astype(BF16)])
    upper = (jnp.arange(ROUTE_T)[:, None] < jnp.arange(ROUTE_T)[None, :]).astype(BF16)
    hp, idx, gates, rank, cnt = _router(x1, row(ffn_norm), wr_pad, b_router.reshape(-1, 1).astype(F32),
                                        upper)

    counts = cnt[:, 0]
    padded = (counts + MOE_BLOCK - 1) // MOE_BLOCK * MOE_BLOCK
    pend = jnp.cumsum(padded)
    pstart = pend - padded
    nk = n * TOP_K
    cap = (nk + MOE_BLOCK - 1) // MOE_BLOCK * MOE_BLOCK + N_EXPERTS * MOE_BLOCK
    n_blocks = cap // MOE_BLOCK
    ex = jnp.arange(N_EXPERTS, dtype=I32)
    dest = rank + jnp.sum(jnp.where(idx[..., None] == ex, pstart.astype(I32), 0), axis=-1)
    blk_start = jnp.arange(n_blocks, dtype=I32) * MOE_BLOCK
    blk_expert = jnp.minimum(jnp.sum((pend[None, :] <= blk_start[:, None]).astype(I32), axis=1),
                             N_EXPERTS - 1).astype(I32)
    of_block = blk_expert[:, None] == ex[None, :]
    per_block = lambda v: jnp.sum(jnp.where(of_block, v[None, :].astype(I32), 0), axis=1)
    blk_valid = jnp.where(blk_start < pend[-1],
                          jnp.clip(per_block(pstart + counts) - blk_start, 0, MOE_BLOCK), 0).astype(I32)
    live = jnp.where(counts > 0, ex, N_EXPERTS)
    later = jnp.concatenate([lax.cummin(live[::-1])[::-1][1:], jnp.full((1,), N_EXPERTS, I32)])
    blk_next = per_block(jnp.where(later < N_EXPERTS, later, -1))

    dest_flat = dest.reshape(TOP_K, n // DISP_T, DISP_T).transpose(1, 0, 2).reshape(-1)
    used_end = pstart + (counts + MOE_SUB - 1) // MOE_SUB * MOE_SUB
    pad_tbl = jnp.concatenate([pstart + counts, used_end]).astype(I32)

    xs = _dispatch(dest_flat, pad_tbl, hp, cap)
    ys = _experts(blk_expert, blk_next, blk_valid, xs, w1.astype(F32),
                  b1.reshape(N_EXPERTS, 1, -1).astype(F32), w2.astype(F32),
                  b2.reshape(N_EXPERTS, 1, -1).astype(F32))
    return _combine(dest_flat, ys, x1, gates.T, p2d, row(ple_norm), w_ple_gate.astype(BF16),
                    w_ple_proj.astype(BF16), row(final_norm), final)


def kernel(x, p, mix_norm, w_in, sgu_ln_g, sgu_ln_b, sgu_w, sgu_b, conv_w, conv_b, dt_bias, a_log,
           d_skip, ssm_norm, w_branch_a, w_branch_b, w_out, ffn_norm, w_router, b_router, w1, b1,
           w2, b2, ple_norm, w_ple_gate, w_ple_proj, final_norm):
    bsz, seq, d = x.shape
    depth = w_in.shape[0]
    assert d == D_MODEL and seq % MIX_T == 0 and (bsz * seq) % max(IN_TM, ROUTE_T, DISP_T) == 0
    assert ROUTE_T == DISP_T == COMB_T
    x2d = x.reshape(bsz * seq, d)
    for i in range(depth):
        x2d = _layer(x2d, p[i].reshape(bsz * seq, PLE_DIM), bsz, seq, mix_norm[i], w_in[i],
                     sgu_ln_g[i], sgu_ln_b[i], sgu_w[i], sgu_b[i], conv_w[i], conv_b[i], dt_bias[i],
                     a_log[i], d_skip[i], ssm_norm[i], w_branch_a[i], w_branch_b[i], w_out[i],
                     ffn_norm[i], w_router[i], b_router[i], w1[i], b1[i], w2[i], b2[i], ple_norm[i],
                     w_ple_gate[i], w_ple_proj[i], final_norm, final=(i == depth - 1))
    return x2d.reshape(bsz, seq, d)
```

```python
import functools
import math

import jax
import jax.numpy as jnp
from jax import lax
from jax.experimental import pallas as pl
from jax.experimental.pallas import tpu as pltpu

F32 = jnp.float32
BF16 = jnp.bfloat16
I32 = jnp.int32

D_MODEL = 1024
CHUNK = 64
PLE_DIM = 256
RMS_EPS = 1e-6
LN_EPS = 1e-5

SGU_HEADS = 8
SGU_HEAD_DIM = 128
SGU_WIDTH = SGU_HEADS * SGU_HEAD_DIM
SGU_LEN = 128

SSM_HEADS = 16
SSM_HEAD_DIM = 64
SSM_INNER = SSM_HEADS * SSM_HEAD_DIM
SSM_GROUPS = 4
SSM_STATE = 128
SSM_CONV = 4
SSD_CHUNK = 128
SSM_CONV_DIM = SSM_INNER + 2 * SSM_GROUPS * SSM_STATE

N_EXPERTS = 32
TOP_K = 4
D_FF_EXPERT = 1024
SWIGLU_LIMIT = 7.0
SWIGLU_ALPHA = 1.702
MOE_BLOCK = 512
MOE_SUB = 256

OFF_U = 0
OFF_V = OFF_U + SGU_WIDTH
OFF_Z = OFF_V + SGU_WIDTH
OFF_XBC = OFF_Z + SSM_INNER
OFF_DT = OFF_XBC + SSM_CONV_DIM
OFF_GA = OFF_DT + SSM_HEADS
OFF_GB = OFF_GA + D_MODEL
IN_PROJ_DIM = OFF_GB + D_MODEL

LANES = 128
SUBLANES = 8
PROJ_MAIN = IN_PROJ_DIM - SSM_HEADS
PROJ_TILE = 1024
TOK_SUB = D_MODEL // LANES

VMEM_LIMIT = 56 * 1024 * 1024

IN_TM = 512
IN_TN = 1792
MIX_T = 512
ROUTE_T = 512
DISP_T = 512
COMB_T = DISP_T
COMB_SUB = 128
DMA_UNROLL = 8
ZERO_RUN = 64
CAST_ROWS = 128


def _sigmoid(x):
    return 1.0 / (1.0 + jnp.exp(-x))


def _gelu_exact(x):
    return 0.5 * x * (1.0 + lax.erf(x * (1.0 / math.sqrt(2.0))))


def _softplus(x):
    return jnp.maximum(x, 0.0) + jnp.log1p(jnp.exp(-jnp.abs(x)))


def _rms(x, g):
    ms = jnp.mean(x * x, axis=-1, keepdims=True)
    return x * lax.rsqrt(ms + RMS_EPS) * g


def _store_token_tiles(ref, val, rows, start=0):
    for j in range(TOK_SUB):
        ref[pl.ds(start * TOK_SUB + j, rows, stride=TOK_SUB), :] = val[:, j * LANES:(j + 1) * LANES]


def _load_token_tiles(ref, rows, start=0):
    return jnp.concatenate(
        [ref[pl.ds(start * TOK_SUB + j, rows, stride=TOK_SUB), :] for j in range(TOK_SUB)], axis=1)


def _tile_rows(idx):
    return pl.ds(pl.multiple_of(idx * TOK_SUB, TOK_SUB), TOK_SUB)


def _inproj_kernel(x_ref, g_ref, w_ref, wdt_ref, proj_ref, dt_ref, h_scr):
    @pl.when(pl.program_id(0) == 0)
    def _():
        h_scr[...] = jnp.zeros_like(h_scr)

    h = h_scr[...]
    dt_ref[...] = jnp.dot(h, wdt_ref[...], preferred_element_type=F32)
    for c in range(PROJ_MAIN // IN_TN):
        cols = slice(c * IN_TN, (c + 1) * IN_TN)
        proj_ref[:, cols] = jnp.dot(h, w_ref[:, cols], preferred_element_type=F32).astype(BF16)
    h_scr[...] = _rms(x_ref[...], g_ref[...]).astype(BF16)


def _in_proj(x2d, g, w_main, w_dt):
    n = x2d.shape[0]
    ntiles = n // IN_TM
    return pl.pallas_call(
        _inproj_kernel,
        grid=(ntiles + 1,),
        in_specs=[
            pl.BlockSpec((IN_TM, D_MODEL), lambda s: (jnp.minimum(s, ntiles - 1), 0)),
            pl.BlockSpec((1, D_MODEL), lambda s: (0, 0)),
            pl.BlockSpec((D_MODEL, PROJ_MAIN), lambda s: (0, 0)),
            pl.BlockSpec((D_MODEL, LANES), lambda s: (0, 0)),
        ],
        out_specs=[
            pl.BlockSpec((IN_TM, PROJ_MAIN), lambda s: (jnp.maximum(s - 1, 0), 0)),
            pl.BlockSpec((IN_TM, LANES), lambda s: (jnp.maximum(s - 1, 0), 0)),
        ],
        out_shape=[
            jax.ShapeDtypeStruct((n, PROJ_MAIN), BF16),
            jax.ShapeDtypeStruct((n, LANES), F32),
        ],
        scratch_shapes=[pltpu.VMEM((IN_TM, D_MODEL), BF16)],
        compiler_params=pltpu.CompilerParams(
            dimension_semantics=("arbitrary",), vmem_limit_bytes=VMEM_LIMIT),
        name="in_proj",
    )(x2d, g, w_main, w_dt)


def _split3(a):
    a1 = a.astype(BF16)
    r1 = a - a1.astype(F32)
    a2 = r1.astype(BF16)
    a3 = (r1 - a2.astype(F32)).astype(BF16)
    return a1, a2, a3


def _dot3(parts, w):
    out = jnp.dot(parts[0], w, preferred_element_type=F32)
    for p in parts[1:]:
        out = out + jnp.dot(p, w, preferred_element_type=F32)
    return out


def _mixer_kernel(u_ref, v_ref, z_ref, xb0_ref, xb1_ref, dt_ref, ga_ref, gb_ref, x_ref,
                  lng_ref, lnb_ref, wsgu_ref, bsgu_ref, convw_ref, convb_ref, dtb_ref, acol_ref,
                  dskip_ref, normw_ref, eexp_ref, shift_ref, triu_ref, wa_ref, wb_ref, wo_ref,
                  x1_ref,
                  tail_scr, xc_scr, state_scr, ya_scr, yb_scr, yap_scr, ybp_scr, *, tps):
    nchunk = MIX_T // SSD_CHUNK
    s = pl.program_id(0)

    @pl.when(s % tps == 0)
    def _():
        tail_scr[...] = jnp.zeros_like(tail_scr)
        state_scr[...] = jnp.zeros_like(state_scr)

    @pl.when(s == 0)
    def _():
        yap_scr[...] = jnp.zeros_like(yap_scr)
        ybp_scr[...] = jnp.zeros_like(ybp_scr)

    xb_refs = (xb0_ref, xb1_ref)
    for c in range(nchunk):
        r0 = c * SSD_CHUNK
        for hf, xb_ref in enumerate(xb_refs):
            cols = slice(hf * PROJ_TILE, (hf + 1) * PROJ_TILE)
            cur = xb_ref[r0:r0 + SSD_CHUNK, :]
            prev = tail_scr[:, cols] if c == 0 else xb_ref[r0 - SSD_CHUNK:r0, :]
            sh = jnp.dot(shift_ref[...], jnp.concatenate([prev, cur], axis=0), preferred_element_type=F32)
            acc = convb_ref[:, cols] + convw_ref[SSM_CONV - 1:SSM_CONV, cols] * cur.astype(F32)
            for j in range(SSM_CONV - 1):
                acc = acc + convw_ref[j:j + 1, cols] * sh[j * SSD_CHUNK:(j + 1) * SSD_CHUNK, :]
            xc_scr[r0:r0 + SSD_CHUNK, cols] = acc * _sigmoid(acc)
    for hf, xb_ref in enumerate(xb_refs):
        tail_scr[:, hf * PROJ_TILE:(hf + 1) * PROJ_TILE] = xb_ref[MIX_T - SSD_CHUNK:MIX_T, :]

    row_i = lax.broadcasted_iota(I32, (SSD_CHUNK, SSD_CHUNK), 0)
    col_i = lax.broadcasted_iota(I32, (SSD_CHUNK, SSD_CHUNK), 1)
    tril = row_i >= col_i
    lane_hi = col_i >= SSM_HEAD_DIM
    eexp = eexp_ref[...]
    triu = triu_ref[...]

    for c in range(nchunk):
        rows = slice(c * SSD_CHUNK, (c + 1) * SSD_CHUNK)

        ug = _gelu_exact(u_ref[rows, :].astype(F32))
        vg = _gelu_exact(v_ref[rows, :].astype(F32))
        mu = jnp.mean(vg, axis=-1, keepdims=True)
        vc = vg - mu
        var = jnp.mean(vc * vc, axis=-1, keepdims=True)
        vn = (vc * lax.rsqrt(var + LN_EPS) * lng_ref[...] + lnb_ref[...]).astype(BF16)
        for g in range(SGU_HEADS):
            cols = slice(g * SGU_HEAD_DIM, (g + 1) * SGU_HEAD_DIM)
            mixed = jnp.dot(wsgu_ref[g], vn[:, cols], preferred_element_type=F32)
            ya_scr[rows, cols] = (ug[:, cols] * (mixed + bsgu_ref[:, cols])).astype(BF16)

        dt_t = _softplus(dt_ref[rows, :].T[0:SSM_HEADS, :] + dtb_ref[...])
        acs_t = _dot3(_split3(dt_t * acol_ref[...]), triu)
        w_t = jnp.exp(acs_t[:, SSD_CHUNK - 1:SSD_CHUNK] - acs_t) * dt_t
        acs = jnp.concatenate([acs_t, jnp.zeros((LANES - SSM_HEADS, SSD_CHUNK), F32)], axis=0).T
        aend = jnp.broadcast_to(acs[SSD_CHUNK - 1:SSD_CHUNK, :], (SUBLANES, LANES))
        dec = jnp.exp(_dot3(_split3(aend), eexp)[0:1, :])

        for gi in range(SSM_GROUPS):
            bg = xc_scr[rows, SSM_INNER + gi * SSM_STATE:SSM_INNER + (gi + 1) * SSM_STATE]
            cg = xc_scr[rows, SSM_INNER + (SSM_GROUPS + gi) * SSM_STATE:
                        SSM_INNER + (SSM_GROUPS + gi + 1) * SSM_STATE]
            bg_t = bg.T
            cb = jnp.dot(cg.astype(BF16), bg_t.astype(BF16), preferred_element_type=F32)
            for pr in range(2):
                lb = gi * 2 + pr
                cols = slice(lb * LANES, (lb + 1) * LANES)
                xpair = xc_scr[rows, cols]
                hpair = state_scr[:, cols]
                acc = jnp.zeros((SSD_CHUNK, LANES), F32)
                st = jnp.zeros((SSM_STATE, LANES), F32)
                for hh in range(2):
                    h = lb * 2 + hh
                    lmask = lane_hi if hh == 1 else jnp.logical_not(lane_hi)
                    xm = jnp.where(lmask, xpair, 0.0).astype(BF16)
                    hm = jnp.where(lmask, hpair, 0.0).astype(BF16)
                    colb = acs[:, h:h + 1]
                    rowb = acs_t[h:h + 1, :]
                    decay = jnp.exp(jnp.where(tril, colb - rowb, -jnp.inf))
                    m_h = (cb * decay * dt_t[h:h + 1, :]).astype(BF16)
                    c_h = (cg * jnp.exp(colb)).astype(BF16)
                    acc = acc + jnp.dot(m_h, xm, preferred_element_type=F32)
                    acc = acc + jnp.dot(c_h, hm, preferred_element_type=F32)
                    btw = (bg_t * w_t[h:h + 1, :]).astype(BF16)
                    st = st + jnp.dot(btw, xm, preferred_element_type=F32)
                state_scr[:, cols] = hpair * dec[:, cols] + st
                y = acc + dskip_ref[:, cols] * xpair
                zz = z_ref[rows, cols].astype(F32)
                xc_scr[rows, cols] = y * (zz * _sigmoid(zz))
            gcols = slice(gi * 2 * LANES, (gi + 1) * 2 * LANES)
            yg = xc_scr[rows, gcols]
            ms = jnp.mean(yg * yg, axis=-1, keepdims=True)
            yb_scr[rows, gcols] = (yg * lax.rsqrt(ms + RMS_EPS) * normw_ref[:, gcols]).astype(BF16)

    ma = jnp.dot(yap_scr[...], wa_ref[...], preferred_element_type=F32)
    mb = jnp.dot(ybp_scr[...], wb_ref[...], preferred_element_type=F32)
    merged = (_sigmoid(ga_ref[...].astype(F32)) * ma + _sigmoid(gb_ref[...].astype(F32)) * mb)
    x1_ref[...] = x_ref[...] + jnp.dot(merged.astype(BF16), wo_ref[...], preferred_element_type=F32)
    yap_scr[...] = ya_scr[...]
    ybp_scr[...] = yb_scr[...]


def _mixer(proj, dt_raw, x2d, bsz, seq, consts, wa, wb, wo):
    n = x2d.shape[0]
    tps = seq // MIX_T
    ntiles = bsz * tps
    cur = lambda s: jnp.minimum(s, ntiles - 1)
    prev = lambda s: jnp.maximum(s - 1, 0)

    def pcol(off, tile):
        return pl.BlockSpec((MIX_T, PROJ_TILE), lambda s: (tile(s), off // PROJ_TILE))

    def full(a):
        nd = a.ndim
        return pl.BlockSpec(a.shape, lambda s, nd=nd: (0,) * nd)

    in_specs = [pcol(OFF_U, cur), pcol(OFF_V, cur), pcol(OFF_Z, cur),
                pcol(OFF_XBC, cur), pcol(OFF_XBC + PROJ_TILE, cur),
                pl.BlockSpec((MIX_T, LANES), lambda s: (cur(s), 0)),
                pcol(OFF_GA - SSM_HEADS, prev), pcol(OFF_GB - SSM_HEADS, prev),
                pl.BlockSpec((MIX_T, D_MODEL), lambda s: (prev(s), 0))]
    in_specs += [full(a) for a in consts] + [full(wa), full(wb), full(wo)]
    return pl.pallas_call(
        functools.partial(_mixer_kernel, tps=tps),
        grid=(ntiles + 1,),
        in_specs=in_specs,
        out_specs=pl.BlockSpec((MIX_T, D_MODEL), lambda s: (prev(s), 0)),
        out_shape=jax.ShapeDtypeStruct((n, D_MODEL), F32),
        scratch_shapes=[
            pltpu.VMEM((SSD_CHUNK, SSM_CONV_DIM), BF16),
            pltpu.VMEM((MIX_T, SSM_CONV_DIM), F32),
            pltpu.VMEM((SSM_STATE, SSM_INNER), F32),
            pltpu.VMEM((MIX_T, SGU_WIDTH), BF16),
            pltpu.VMEM((MIX_T, SSM_INNER), BF16),
            pltpu.VMEM((MIX_T, SGU_WIDTH), BF16),
            pltpu.VMEM((MIX_T, SSM_INNER), BF16),
        ],
        compiler_params=pltpu.CompilerParams(
            dimension_semantics=("arbitrary",), vmem_limit_bytes=VMEM_LIMIT),
        name="mixer",
    )(proj, proj, proj, proj, proj, dt_raw, proj, proj, x2d, *consts, wa, wb, wo)


def _router_kernel(x_ref, g_ref, wr_ref, br_ref, upper_ref, hp_ref, idx_ref, gate_ref, rank_ref, cnt_ref,
                   carry_scr):
    @pl.when(pl.program_id(0) == 0)
    def _():
        carry_scr[...] = jnp.zeros_like(carry_scr)

    h = _rms(x_ref[...], g_ref[...])
    _store_token_tiles(hp_ref, h, ROUTE_T)

    h_hi = h.astype(BF16)
    h_lo = (h - h_hi.astype(F32)).astype(BF16)
    logits = (jnp.dot(h_hi, wr_ref[0], preferred_element_type=F32)
              + jnp.dot(h_hi, wr_ref[1], preferred_element_type=F32)
              + jnp.dot(h_lo, wr_ref[0], preferred_element_type=F32))
    lt = logits.T[0:N_EXPERTS, :] + br_ref[...]
    eidx = lax.broadcasted_iota(I32, (N_EXPERTS, ROUTE_T), 0).astype(F32)
    vals = lt
    sel_any = jnp.zeros((N_EXPERTS, ROUTE_T), F32)
    sels, tops = [], []
    for k in range(TOP_K):
        m = jnp.max(vals, axis=0, keepdims=True)
        first = jnp.min(jnp.where(vals == m, eidx, float(N_EXPERTS)), axis=0, keepdims=True)
        sel = eidx == first
        vals = jnp.where(sel, -jnp.inf, vals)
        sel_f = sel.astype(F32)
        sel_any = sel_any + sel_f
        sels.append(sel_f)
        tops.append(m)
        idx_ref[k:k + 1, :] = first.astype(I32)
    es = [jnp.exp(t - tops[0]) for t in tops]
    denom = functools.reduce(lambda a, b: a + b, es)
    for k in range(TOP_K):
        gate_ref[k:k + 1, :] = es[k] / denom
    gate_ref[TOP_K:, :] = jnp.zeros((SUBLANES - TOP_K, ROUTE_T), F32)

    excl = jnp.dot(sel_any.astype(BF16), upper_ref[...], preferred_element_type=F32) + carry_scr[:, 0:1]
    for k in range(TOP_K):
        rank_ref[k:k + 1, :] = jnp.sum(sels[k] * excl, axis=0, keepdims=True).astype(I32)
    new_carry = carry_scr[...] + jnp.sum(sel_any, axis=1, keepdims=True)
    carry_scr[...] = new_carry
    cnt_ref[...] = new_carry.astype(I32)


def _router(x1, g, wr_pad, br_col, upper):
    n = x1.shape[0]
    per_k = pl.BlockSpec((TOP_K, ROUTE_T), lambda i: (0, i))
    return pl.pallas_call(
        _router_kernel,
        grid=(n // ROUTE_T,),
        in_specs=[
            pl.BlockSpec((ROUTE_T, D_MODEL), lambda i: (i, 0)),
            pl.BlockSpec((1, D_MODEL), lambda i: (0, 0)),
            pl.BlockSpec((2, D_MODEL, LANES), lambda i: (0, 0, 0)),
            pl.BlockSpec((N_EXPERTS, 1), lambda i: (0, 0)),
            pl.BlockSpec((ROUTE_T, ROUTE_T), lambda i: (0, 0)),
        ],
        out_specs=[
            pl.BlockSpec((ROUTE_T * TOK_SUB, LANES), lambda i: (i, 0)),
            per_k,
            pl.BlockSpec((SUBLANES, ROUTE_T), lambda i: (0, i)),
            per_k,
            pl.BlockSpec((N_EXPERTS, LANES), lambda i: (0, 0)),
        ],
        out_shape=[
            jax.ShapeDtypeStruct((n * TOK_SUB, LANES), F32),
            jax.ShapeDtypeStruct((TOP_K, n), I32),
            jax.ShapeDtypeStruct((SUBLANES, n), F32),
            jax.ShapeDtypeStruct((TOP_K, n), I32),
            jax.ShapeDtypeStruct((N_EXPERTS, LANES), I32),
        ],
        scratch_shapes=[pltpu.VMEM((N_EXPERTS, LANES), F32)],
        compiler_params=pltpu.CompilerParams(
            dimension_semantics=("arbitrary",), vmem_limit_bytes=VMEM_LIMIT),
        name="router",
    )(x1, g, wr_pad, br_col, upper)


def _dispatch_kernel(dest_ref, pad_ref, hp_ref, xs_hbm, zero_scr, sem_rows):
    i = pl.program_id(0)
    tbl = i * (TOP_K * DISP_T)

    def zero_copy(s):
        return pltpu.make_async_copy(zero_scr.at[0:TOK_SUB, :], xs_hbm.at[_tile_rows(s), :], sem_rows)

    def zero_run(s):
        rows = pl.ds(pl.multiple_of(s * TOK_SUB, TOK_SUB), ZERO_RUN * TOK_SUB)
        return pltpu.make_async_copy(zero_scr, xs_hbm.at[rows, :], sem_rows)

    def zero_pads(act):
        for e in range(N_EXPERTS):
            lo, hi = pad_ref[e], pad_ref[N_EXPERTS + e]
            nrun = (hi - lo) // ZERO_RUN
            lax.fori_loop(0, nrun, lambda r, c: (act(zero_run(lo + r * ZERO_RUN)), c)[1], 0)
            lax.fori_loop(lo + nrun * ZERO_RUN, hi, lambda s, c: (act(zero_copy(s)), c)[1], 0)

    @pl.when(i == 0)
    def _():
        zero_scr[...] = jnp.zeros_like(zero_scr)
        zero_pads(lambda d: d.start())
        zero_pads(lambda d: d.wait())

    def tile_copy(t, k):
        d = dest_ref[tbl + t * TOP_K + k]
        return pltpu.make_async_copy(hp_ref.at[_tile_rows(t), :], xs_hbm.at[_tile_rows(d), :], sem_rows)

    def issue(tb, carry):
        for r in range(DMA_UNROLL):
            for k in range(TOP_K):
                tile_copy(tb * DMA_UNROLL + r, k).start(priority=(r * TOP_K + k) % 2)
        return carry

    lax.fori_loop(0, DISP_T // DMA_UNROLL, issue, 0)

    def drain(tb, carry):
        for _ in range(DMA_UNROLL * TOP_K):
            zero_copy(0).wait()
        return carry

    lax.fori_loop(0, DISP_T // DMA_UNROLL, drain, 0)


def _dispatch(dest_flat, pad_tbl, hp, cap):
    n = hp.shape[0] // TOK_SUB
    grid_spec = pltpu.PrefetchScalarGridSpec(
        num_scalar_prefetch=2,
        grid=(n // DISP_T,),
        in_specs=[pl.BlockSpec((DISP_T * TOK_SUB, LANES), lambda i, d, p: (i, 0))],
        out_specs=pl.BlockSpec(memory_space=pl.ANY),
        scratch_shapes=[pltpu.VMEM((ZERO_RUN * TOK_SUB, LANES), F32), pltpu.SemaphoreType.DMA],
    )
    return pl.pallas_call(
        _dispatch_kernel,
        grid_spec=grid_spec,
        out_shape=jax.ShapeDtypeStruct((cap * TOK_SUB, LANES), F32),
        compiler_params=pltpu.CompilerParams(dimension_semantics=("arbitrary",)),
        name="dispatch",
    )(dest_flat, pad_tbl, hp)


def _expert_kernel(be_ref, nxt_ref, nv_ref, xs_ref, w1_hbm, b1_ref, w2_hbm, b2_ref, ys_ref,
                   w1f_scr, w2f_scr, w1b_scr, w2b_scr, sems):
    b = pl.program_id(0)
    valid = nv_ref[b]
    used = valid > 0
    new_expert = jnp.logical_or(b == 0, be_ref[b] != be_ref[jnp.maximum(b - 1, 0)])

    def fetch(e):
        return (pltpu.make_async_copy(w1_hbm.at[e], w1f_scr, sems.at[0]),
                pltpu.make_async_copy(w2_hbm.at[e], w2f_scr, sems.at[1]))

    @pl.when(jnp.logical_and(used, b == 0))
    def _():
        for c in fetch(be_ref[0]):
            c.start()

    @pl.when(jnp.logical_and(used, new_expert))
    def _():
        for c in fetch(be_ref[b]):
            c.wait()

        def cast(rb, carry):
            rows = pl.ds(pl.multiple_of(rb * CAST_ROWS, CAST_ROWS), CAST_ROWS)
            w1b_scr[rows, :] = w1f_scr[rows, :].astype(BF16)
            w2b_scr[rows, :] = w2f_scr[rows, :].astype(BF16)
            return carry

        lax.fori_loop(0, D_MODEL // CAST_ROWS, cast, 0)

        @pl.when(nxt_ref[b] >= 0)
        def _():
            for c in fetch(nxt_ref[b]):
                c.start()

    def expert_mlp(rows):
        x = _load_token_tiles(xs_ref, rows).astype(BF16)
        hid = jnp.dot(x, w1b_scr[...], preferred_element_type=F32) + b1_ref[0]
        glu = jnp.minimum(hid[:, :D_FF_EXPERT], SWIGLU_LIMIT)
        lin = jnp.clip(hid[:, D_FF_EXPERT:], -SWIGLU_LIMIT, SWIGLU_LIMIT)
        act = glu * _sigmoid(SWIGLU_ALPHA * glu) * (lin + 1.0)
        y = jnp.dot(act.astype(BF16), w2b_scr[...], preferred_element_type=F32) + b2_ref[0]
        _store_token_tiles(ys_ref, y, rows)

    @pl.when(valid > MOE_SUB)
    def _():
        expert_mlp(MOE_BLOCK)

    @pl.when(jnp.logical_and(used, valid <= MOE_SUB))
    def _():
        expert_mlp(MOE_SUB)


def _experts(blk_expert, blk_next, blk_valid, xs, w1, b1, w2, b2):
    cap = xs.shape[0] // TOK_SUB
    n_blocks = cap // MOE_BLOCK
    grid_spec = pltpu.PrefetchScalarGridSpec(
        num_scalar_prefetch=3,
        grid=(n_blocks,),
        in_specs=[
            pl.BlockSpec((MOE_BLOCK * TOK_SUB, LANES), lambda b, be, nx, nv: (b, 0)),
            pl.BlockSpec(memory_space=pl.ANY),
            pl.BlockSpec((1, 1, 2 * D_FF_EXPERT), lambda b, be, nx, nv: (be[b], 0, 0)),
            pl.BlockSpec(memory_space=pl.ANY),
            pl.BlockSpec((1, 1, D_MODEL), lambda b, be, nx, nv: (be[b], 0, 0)),
        ],
        out_specs=pl.BlockSpec((MOE_BLOCK * TOK_SUB, LANES), lambda b, be, nx, nv: (b, 0)),
        scratch_shapes=[pltpu.VMEM((D_MODEL, 2 * D_FF_EXPERT), F32),
                        pltpu.VMEM((D_FF_EXPERT, D_MODEL), F32),
                        pltpu.VMEM((D_MODEL, 2 * D_FF_EXPERT), BF16),
                        pltpu.VMEM((D_FF_EXPERT, D_MODEL), BF16),
                        pltpu.SemaphoreType.DMA((2,))],
    )
    assert D_MODEL == D_FF_EXPERT
    assert MOE_BLOCK == 2 * MOE_SUB
    return pl.pallas_call(
        _expert_kernel,
        grid_spec=grid_spec,
        out_shape=jax.ShapeDtypeStruct((cap * TOK_SUB, LANES), F32),
        compiler_params=pltpu.CompilerParams(
            dimension_semantics=("arbitrary",), vmem_limit_bytes=VMEM_LIMIT),
        name="experts",
    )(blk_expert, blk_next, blk_valid, xs, w1, b1, w2, b2)


def _combine_kernel(dest_ref, ys_hbm, x1_ref, gate_ref, p_ref, pg_ref, wg_ref, wp_ref, fg_ref,
                    out_ref, ybuf, sems, *, final):
    i = pl.program_id(0)
    nsteps = pl.num_programs(0)
    buf_tokens = TOP_K * COMB_T

    def issue_part(step, slot, part, parts):
        tbl = step * buf_tokens
        per_part = COMB_T // DMA_UNROLL // parts

        def issue(tb, carry):
            for r in range(DMA_UNROLL):
                for k in range(TOP_K):
                    t = tb * DMA_UNROLL + r
                    pltpu.make_async_copy(
                        ys_hbm.at[_tile_rows(dest_ref[tbl + t * TOP_K + k]), :],
                        ybuf.at[_tile_rows(slot * buf_tokens + k * COMB_T + t), :],
                        sems.at[slot]).start(priority=(r * TOP_K + k) % 2)
            return carry

        last = jnp.where(step < nsteps, (part + 1) * per_part, part * per_part)
        lax.fori_loop(part * per_part, last, issue, 0)

    def drain_all(slot):
        def drain(tb, carry):
            for _ in range(DMA_UNROLL * TOP_K):
                pltpu.make_async_copy(ys_hbm.at[_tile_rows(0), :],
                                      ybuf.at[_tile_rows(slot * buf_tokens), :], sems.at[slot]).wait()
            return carry

        lax.fori_loop(0, COMB_T // DMA_UNROLL, drain, 0)

    @pl.when(i == 0)
    def _():
        issue_part(0, 0, 0, 1)

    nsub = COMB_T // COMB_SUB
    for slot in range(2):
        @pl.when(i % 2 == slot)
        def _():
            drain_all(slot)

            for sub in range(nsub):
                issue_part(i + 1, 1 - slot, sub, nsub)
                rows = slice(sub * COMB_SUB, (sub + 1) * COMB_SUB)
                x2 = x1_ref[rows, :]
                for k in range(TOP_K):
                    x2 = x2 + gate_ref[rows, k:k + 1] * _load_token_tiles(
                        ybuf, COMB_SUB, slot * buf_tokens + k * COMB_T + sub * COMB_SUB)
                hp = _rms(x2, pg_ref[...]).astype(BF16)
                gate = _sigmoid(jnp.dot(hp, wg_ref[...], preferred_element_type=F32))
                emb = jnp.dot(p_ref[rows, :].astype(BF16), wp_ref[...], preferred_element_type=F32)
                x3 = x2 + gate * emb
                if final:
                    x3 = _rms(x3, fg_ref[...])
                out_ref[rows, :] = x3


def _combine(dest_flat, ys, x1, gates_t, p2d, pg, wg, wp, fg, final):
    n = x1.shape[0]
    grid_spec = pltpu.PrefetchScalarGridSpec(
        num_scalar_prefetch=1,
        grid=(n // COMB_T,),
        in_specs=[
            pl.BlockSpec(memory_space=pl.ANY),
            pl.BlockSpec((COMB_T, D_MODEL), lambda i, d: (i, 0)),
            pl.BlockSpec((COMB_T, SUBLANES), lambda i, d: (i, 0)),
            pl.BlockSpec((COMB_T, PLE_DIM), lambda i, d: (i, 0)),
            pl.BlockSpec((1, D_MODEL), lambda i, d: (0, 0)),
            pl.BlockSpec((D_MODEL, D_MODEL), lambda i, d: (0, 0)),
            pl.BlockSpec((PLE_DIM, D_MODEL), lambda i, d: (0, 0)),
            pl.BlockSpec((1, D_MODEL), lambda i, d: (0, 0)),
        ],
        out_specs=pl.BlockSpec((COMB_T, D_MODEL), lambda i, d: (i, 0)),
        scratch_shapes=[pltpu.VMEM((2 * TOP_K * COMB_T * TOK_SUB, LANES), F32),
                        pltpu.SemaphoreType.DMA((2,))],
    )
    return pl.pallas_call(
        functools.partial(_combine_kernel, final=final),
        grid_spec=grid_spec,
        out_shape=jax.ShapeDtypeStruct((n, D_MODEL), F32),
        compiler_params=pltpu.CompilerParams(
            dimension_semantics=("arbitrary",), vmem_limit_bytes=VMEM_LIMIT),
        name="combine",
    )(dest_flat, ys, x1, gates_t, p2d, pg, wg, wp, fg)


def _layer(x2d, p2d, bsz, seq, mix_norm, w_in, sgu_ln_g, sgu_ln_b, sgu_w, sgu_b, conv_w, conv_b,
           dt_bias, a_log, d_skip, ssm_norm, w_branch_a, w_branch_b, w_out, ffn_norm, w_router,
           b_router, w1, b1, w2, b2, ple_norm, w_ple_gate, w_ple_proj, final_norm, final):
    n = x2d.shape[0]
    row = lambda a: a.reshape(1, -1).astype(F32)

    w_main = jnp.concatenate([w_in[:, :OFF_DT], w_in[:, OFF_GA:]], axis=1).astype(BF16)
    w_dt = jnp.pad(w_in[:, OFF_DT:OFF_GA], ((0, 0), (0, LANES - SSM_HEADS))).astype(BF16)
    pos = jnp.arange(SGU_LEN)
    allowed = (pos[None, :] // CHUNK) <= (pos[:, None] // CHUNK)
    wsgu = jnp.where(allowed[None], sgu_w, 0.0).astype(BF16)
    bsgu = jnp.repeat(sgu_b.T, SGU_HEAD_DIM, axis=1).astype(F32)
    dtb = dt_bias.reshape(-1, 1).astype(F32)
    acol = -jnp.exp(a_log.astype(F32)).reshape(-1, 1)
    dskip = jnp.repeat(d_skip.astype(F32), SSM_HEAD_DIM).reshape(1, -1)
    head_of_col = jnp.arange(SSM_INNER) // SSM_HEAD_DIM
    eexp = (jnp.arange(LANES)[:, None] == head_of_col[None, :]).astype(BF16)
    triu = (jnp.arange(SSD_CHUNK)[:, None] <= jnp.arange(SSD_CHUNK)[None, :]).astype(BF16)
    t_out = jnp.arange((SSM_CONV - 1) * SSD_CHUNK)
    src = SSD_CHUNK + t_out % SSD_CHUNK - (SSM_CONV - 1) + t_out // SSD_CHUNK
    shift = (src[:, None] == jnp.arange(2 * SSD_CHUNK)[None, :]).astype(BF16)
    consts = [row(sgu_ln_g), row(sgu_ln_b), wsgu, bsgu, conv_w.astype(F32), row(conv_b), dtb, acol,
              dskip, row(ssm_norm), eexp, shift, triu]

    proj, dt_raw = _in_proj(x2d, row(mix_norm), w_main, w_dt)
    x1 = _mixer(proj, dt_raw, x2d, bsz, seq, consts, w_branch_a.astype(BF16),
                w_branch_b.astype(BF16), w_out.astype(BF16))

    wr_f = jnp.pad(w_router.astype(F32), ((0, 0), (0, LANES - N_EXPERTS)))
    wr_hi = wr_f.astype(BF16)
    wr_pad = jnp.stack([wr_hi, (wr_f - wr_hi.astype(F32)).astype(BF16)])
    upper = (jnp.arange(ROUTE_T)[:, None] < jnp.arange(ROUTE_T)[None, :]).astype(BF16)
    hp, idx, gates, rank, cnt = _router(x1, row(ffn_norm), wr_pad, b_router.reshape(-1, 1).astype(F32),
                                        upper)

    counts = cnt[:, 0]
    padded = (counts + MOE_BLOCK - 1) // MOE_BLOCK * MOE_BLOCK
    pend = jnp.cumsum(padded)
    pstart = pend - padded
    nk = n * TOP_K
    cap = (nk + MOE_BLOCK - 1) // MOE_BLOCK * MOE_BLOCK + N_EXPERTS * MOE_BLOCK
    n_blocks = cap // MOE_BLOCK
    ex = jnp.arange(N_EXPERTS, dtype=I32)
    dest = rank + jnp.sum(jnp.where(idx[..., None] == ex, pstart.astype(I32), 0), axis=-1)
    blk_start = jnp.arange(n_blocks, dtype=I32) * MOE_BLOCK
    blk_expert = jnp.minimum(jnp.sum((pend[None, :] <= blk_start[:, None]).astype(I32), axis=1),
                             N_EXPERTS - 1).astype(I32)
    of_block = blk_expert[:, None] == ex[None, :]
    per_block = lambda v: jnp.sum(jnp.where(of_block, v[None, :].astype(I32), 0), axis=1)
    blk_valid = jnp.where(blk_start < pend[-1],
                          jnp.clip(per_block(pstart + counts) - blk_start, 0, MOE_BLOCK), 0).astype(I32)
    live = jnp.where(counts > 0, ex, N_EXPERTS)
    later = jnp.concatenate([lax.cummin(live[::-1])[::-1][1:], jnp.full((1,), N_EXPERTS, I32)])
    blk_next = per_block(jnp.where(later < N_EXPERTS, later, -1))

    dest_flat = dest.T.reshape(-1)
    used_end = pstart + (counts + MOE_SUB - 1) // MOE_SUB * MOE_SUB
    pad_tbl = jnp.concatenate([pstart + counts, used_end]).astype(I32)

    xs = _dispatch(dest_flat, pad_tbl, hp, cap)
    ys = _experts(blk_expert, blk_next, blk_valid, xs, w1.astype(F32),
                  b1.reshape(N_EXPERTS, 1, -1).astype(F32), w2.astype(F32),
                  b2.reshape(N_EXPERTS, 1, -1).astype(F32))
    return _combine(dest_flat, ys, x1, gates.T, p2d, row(ple_norm), w_ple_gate.astype(BF16),
                    w_ple_proj.astype(BF16), row(final_norm), final)


def kernel(x, p, mix_norm, w_in, sgu_ln_g, sgu_ln_b, sgu_w, sgu_b, conv_w, conv_b, dt_bias, a_log,
           d_skip, ssm_norm, w_branch_a, w_branch_b, w_out, ffn_norm, w_router, b_router, w1, b1,
           w2, b2, ple_norm, w_ple_gate, w_ple_proj, final_norm):
    bsz, seq, d = x.shape
    depth = w_in.shape[0]
    assert d == D_MODEL and seq % MIX_T == 0 and (bsz * seq) % max(IN_TM, ROUTE_T, DISP_T) == 0
    assert ROUTE_T == DISP_T == COMB_T
    x2d = x.reshape(bsz * seq, d)
    for i in range(depth):
        x2d = _layer(x2d, p[i].reshape(bsz * seq, PLE_DIM), bsz, seq, mix_norm[i], w_in[i],
                     sgu_ln_g[i], sgu_ln_b[i], sgu_w[i], sgu_b[i], conv_w[i], conv_b[i], dt_bias[i],
                     a_log[i], d_skip[i], ssm_norm[i], w_branch_a[i], w_branch_b[i], w_out[i],
                     ffn_norm[i], w_router[i], b_router[i], w1[i], b1[i], w2[i], b2[i], ple_norm[i],
                     w_ple_gate[i], w_ple_proj[i], final_norm, final=(i == depth - 1))
    return x2d.reshape(bsz, seq, d)
```

```python
import functools
import math

import jax
import jax.numpy as jnp
from jax import lax
from jax.experimental import pallas as pl
from jax.experimental.pallas import tpu as pltpu

F32 = jnp.float32
BF16 = jnp.bfloat16
I32 = jnp.int32

D_MODEL = 1024
CHUNK = 64
PLE_DIM = 256
RMS_EPS = 1e-6
LN_EPS = 1e-5

SGU_HEADS = 8
SGU_HEAD_DIM = 128
SGU_WIDTH = SGU_HEADS * SGU_HEAD_DIM
SGU_LEN = 128

SSM_HEADS = 16
SSM_HEAD_DIM = 64
SSM_INNER = SSM_HEADS * SSM_HEAD_DIM
SSM_GROUPS = 4
SSM_STATE = 128
SSM_CONV = 4
SSD_CHUNK = 128
SSM_CONV_DIM = SSM_INNER + 2 * SSM_GROUPS * SSM_STATE

N_EXPERTS = 32
TOP_K = 4
D_FF_EXPERT = 1024
SWIGLU_LIMIT = 7.0
SWIGLU_ALPHA = 1.702
MOE_BLOCK = 512
MOE_SUB = 256

OFF_U = 0
OFF_V = OFF_U + SGU_WIDTH
OFF_Z = OFF_V + SGU_WIDTH
OFF_XBC = OFF_Z + SSM_INNER
OFF_DT = OFF_XBC + SSM_CONV_DIM
OFF_GA = OFF_DT + SSM_HEADS
OFF_GB = OFF_GA + D_MODEL
IN_PROJ_DIM = OFF_GB + D_MODEL

LANES = 128
SUBLANES = 8
PROJ_MAIN = IN_PROJ_DIM - SSM_HEADS
PROJ_TILE = 1024
TOK_SUB = D_MODEL // LANES

VMEM_LIMIT = 56 * 1024 * 1024

IN_TM = 512
IN_TN = 1792
MIX_T = 512
ROUTE_T = 512
DISP_T = 512
COMB_T = DISP_T
COMB_SUB = 128
ISSUE_PARTS = 2
DMA_UNROLL = 8
ZERO_RUN = 64
CAST_ROWS = 128


def _sigmoid(x):
    return 1.0 / (1.0 + jnp.exp(-x))


def _gelu_exact(x):
    return 0.5 * x * (1.0 + lax.erf(x * (1.0 / math.sqrt(2.0))))


def _softplus(x):
    return jnp.maximum(x, 0.0) + jnp.log1p(jnp.exp(-jnp.abs(x)))


def _rms(x, g):
    ms = jnp.mean(x * x, axis=-1, keepdims=True)
    return x * lax.rsqrt(ms + RMS_EPS) * g


def _store_token_tiles(ref, val, rows, start=0):
    for j in range(TOK_SUB):
        ref[pl.ds(start * TOK_SUB + j, rows, stride=TOK_SUB), :] = val[:, j * LANES:(j + 1) * LANES]


def _load_token_tiles(ref, rows, start=0):
    return jnp.concatenate(
        [ref[pl.ds(start * TOK_SUB + j, rows, stride=TOK_SUB), :] for j in range(TOK_SUB)], axis=1)


def _tile_rows(idx):
    return pl.ds(pl.multiple_of(idx * TOK_SUB, TOK_SUB), TOK_SUB)


def _inproj_kernel(x_ref, g_ref, w_ref, wdt_ref, proj_ref, dt_ref, h_scr):
    @pl.when(pl.program_id(0) == 0)
    def _():
        h_scr[...] = jnp.zeros_like(h_scr)

    h = h_scr[...]
    dt_ref[...] = jnp.dot(h, wdt_ref[...], preferred_element_type=F32)
    for c in range(PROJ_MAIN // IN_TN):
        cols = slice(c * IN_TN, (c + 1) * IN_TN)
        proj_ref[:, cols] = jnp.dot(h, w_ref[:, cols], preferred_element_type=F32).astype(BF16)
    h_scr[...] = _rms(x_ref[...], g_ref[...]).astype(BF16)


def _in_proj(x2d, g, w_main, w_dt):
    n = x2d.shape[0]
    ntiles = n // IN_TM
    return pl.pallas_call(
        _inproj_kernel,
        grid=(ntiles + 1,),
        in_specs=[
            pl.BlockSpec((IN_TM, D_MODEL), lambda s: (jnp.minimum(s, ntiles - 1), 0)),
            pl.BlockSpec((1, D_MODEL), lambda s: (0, 0)),
            pl.BlockSpec((D_MODEL, PROJ_MAIN), lambda s: (0, 0)),
            pl.BlockSpec((D_MODEL, LANES), lambda s: (0, 0)),
        ],
        out_specs=[
            pl.BlockSpec((IN_TM, PROJ_MAIN), lambda s: (jnp.maximum(s - 1, 0), 0)),
            pl.BlockSpec((IN_TM, LANES), lambda s: (jnp.maximum(s - 1, 0), 0)),
        ],
        out_shape=[
            jax.ShapeDtypeStruct((n, PROJ_MAIN), BF16),
            jax.ShapeDtypeStruct((n, LANES), F32),
        ],
        scratch_shapes=[pltpu.VMEM((IN_TM, D_MODEL), BF16)],
        compiler_params=pltpu.CompilerParams(
            dimension_semantics=("arbitrary",), vmem_limit_bytes=VMEM_LIMIT),
        name="in_proj",
    )(x2d, g, w_main, w_dt)


def _split3(a):
    a1 = a.astype(BF16)
    r1 = a - a1.astype(F32)
    a2 = r1.astype(BF16)
    a3 = (r1 - a2.astype(F32)).astype(BF16)
    return a1, a2, a3


def _dot3(parts, w):
    out = jnp.dot(parts[0], w, preferred_element_type=F32)
    for p in parts[1:]:
        out = out + jnp.dot(p, w, preferred_element_type=F32)
    return out


def _mixer_kernel(u_ref, v_ref, z_ref, xb0_ref, xb1_ref, dt_ref, ga_ref, gb_ref, x_ref,
                  lng_ref, lnb_ref, wsgu_ref, bsgu_ref, convw_ref, convb_ref, dtb_ref, acol_ref,
                  dskip_ref, normw_ref, eexp_ref, shift_ref, triu_ref, wa_ref, wb_ref, wo_ref,
                  x1_ref,
                  tail_scr, xc_scr, state_scr, ya_scr, yb_scr, yap_scr, ybp_scr, *, tps):
    nchunk = MIX_T // SSD_CHUNK
    s = pl.program_id(0)

    @pl.when(s % tps == 0)
    def _():
        tail_scr[...] = jnp.zeros_like(tail_scr)
        state_scr[...] = jnp.zeros_like(state_scr)

    @pl.when(s == 0)
    def _():
        yap_scr[...] = jnp.zeros_like(yap_scr)
        ybp_scr[...] = jnp.zeros_like(ybp_scr)

    xb_refs = (xb0_ref, xb1_ref)
    for c in range(nchunk):
        r0 = c * SSD_CHUNK
        for hf, xb_ref in enumerate(xb_refs):
            cols = slice(hf * PROJ_TILE, (hf + 1) * PROJ_TILE)
            cur = xb_ref[r0:r0 + SSD_CHUNK, :]
            prev = tail_scr[:, cols] if c == 0 else xb_ref[r0 - SSD_CHUNK:r0, :]
            sh = jnp.dot(shift_ref[...], jnp.concatenate([prev, cur], axis=0), preferred_element_type=F32)
            acc = convb_ref[:, cols] + convw_ref[SSM_CONV - 1:SSM_CONV, cols] * cur.astype(F32)
            for j in range(SSM_CONV - 1):
                acc = acc + convw_ref[j:j + 1, cols] * sh[j * SSD_CHUNK:(j + 1) * SSD_CHUNK, :]
            xc_scr[r0:r0 + SSD_CHUNK, cols] = acc * _sigmoid(acc)
    for hf, xb_ref in enumerate(xb_refs):
        tail_scr[:, hf * PROJ_TILE:(hf + 1) * PROJ_TILE] = xb_ref[MIX_T - SSD_CHUNK:MIX_T, :]

    row_i = lax.broadcasted_iota(I32, (SSD_CHUNK, SSD_CHUNK), 0)
    col_i = lax.broadcasted_iota(I32, (SSD_CHUNK, SSD_CHUNK), 1)
    tril = row_i >= col_i
    lane_hi = col_i >= SSM_HEAD_DIM
    eexp = eexp_ref[...]
    triu = triu_ref[...]

    for c in range(nchunk):
        rows = slice(c * SSD_CHUNK, (c + 1) * SSD_CHUNK)

        ug = _gelu_exact(u_ref[rows, :].astype(F32))
        vg = _gelu_exact(v_ref[rows, :].astype(F32))
        mu = jnp.mean(vg, axis=-1, keepdims=True)
        vc = vg - mu
        var = jnp.mean(vc * vc, axis=-1, keepdims=True)
        vn = (vc * lax.rsqrt(var + LN_EPS) * lng_ref[...] + lnb_ref[...]).astype(BF16)
        for g in range(SGU_HEADS):
            cols = slice(g * SGU_HEAD_DIM, (g + 1) * SGU_HEAD_DIM)
            mixed = jnp.dot(wsgu_ref[g], vn[:, cols], preferred_element_type=F32)
            ya_scr[rows, cols] = (ug[:, cols] * (mixed + bsgu_ref[:, cols])).astype(BF16)

        dt_t = _softplus(dt_ref[rows, :].T[0:SSM_HEADS, :] + dtb_ref[...])
        acs_t = _dot3(_split3(dt_t * acol_ref[...]), triu)
        w_t = jnp.exp(acs_t[:, SSD_CHUNK - 1:SSD_CHUNK] - acs_t) * dt_t
        acs = jnp.concatenate([acs_t, jnp.zeros((LANES - SSM_HEADS, SSD_CHUNK), F32)], axis=0).T
        aend = jnp.broadcast_to(acs[SSD_CHUNK - 1:SSD_CHUNK, :], (SUBLANES, LANES))
        dec = jnp.exp(_dot3(_split3(aend), eexp)[0:1, :])

        for gi in range(SSM_GROUPS):
            bg = xc_scr[rows, SSM_INNER + gi * SSM_STATE:SSM_INNER + (gi + 1) * SSM_STATE]
            cg = xc_scr[rows, SSM_INNER + (SSM_GROUPS + gi) * SSM_STATE:
                        SSM_INNER + (SSM_GROUPS + gi + 1) * SSM_STATE]
            bg_t = bg.T
            cb = jnp.dot(cg.astype(BF16), bg_t.astype(BF16), preferred_element_type=F32)
            for pr in range(2):
                lb = gi * 2 + pr
                cols = slice(lb * LANES, (lb + 1) * LANES)
                xpair = xc_scr[rows, cols]
                hpair = state_scr[:, cols]
                acc = jnp.zeros((SSD_CHUNK, LANES), F32)
                st = jnp.zeros((SSM_STATE, LANES), F32)
                for hh in range(2):
                    h = lb * 2 + hh
                    lmask = lane_hi if hh == 1 else jnp.logical_not(lane_hi)
                    xm = jnp.where(lmask, xpair, 0.0).astype(BF16)
                    hm = jnp.where(lmask, hpair, 0.0).astype(BF16)
                    colb = acs[:, h:h + 1]
                    rowb = acs_t[h:h + 1, :]
                    decay = jnp.exp(jnp.where(tril, colb - rowb, -jnp.inf))
                    m_h = (cb * decay * dt_t[h:h + 1, :]).astype(BF16)
                    c_h = (cg * jnp.exp(colb)).astype(BF16)
                    acc = acc + jnp.dot(m_h, xm, preferred_element_type=F32)
                    acc = acc + jnp.dot(c_h, hm, preferred_element_type=F32)
                    btw = (bg_t * w_t[h:h + 1, :]).astype(BF16)
                    st = st + jnp.dot(btw, xm, preferred_element_type=F32)
                state_scr[:, cols] = hpair * dec[:, cols] + st
                y = acc + dskip_ref[:, cols] * xpair
                zz = z_ref[rows, cols].astype(F32)
                xc_scr[rows, cols] = y * (zz * _sigmoid(zz))
            gcols = slice(gi * 2 * LANES, (gi + 1) * 2 * LANES)
            yg = xc_scr[rows, gcols]
            ms = jnp.mean(yg * yg, axis=-1, keepdims=True)
            yb_scr[rows, gcols] = (yg * lax.rsqrt(ms + RMS_EPS) * normw_ref[:, gcols]).astype(BF16)

    ma = jnp.dot(yap_scr[...], wa_ref[...], preferred_element_type=F32)
    mb = jnp.dot(ybp_scr[...], wb_ref[...], preferred_element_type=F32)
    merged = (_sigmoid(ga_ref[...].astype(F32)) * ma + _sigmoid(gb_ref[...].astype(F32)) * mb)
    x1_ref[...] = x_ref[...] + jnp.dot(merged.astype(BF16), wo_ref[...], preferred_element_type=F32)
    yap_scr[...] = ya_scr[...]
    ybp_scr[...] = yb_scr[...]


def _mixer(proj, dt_raw, x2d, bsz, seq, consts, wa, wb, wo):
    n = x2d.shape[0]
    tps = seq // MIX_T
    ntiles = bsz * tps
    cur = lambda s: jnp.minimum(s, ntiles - 1)
    prev = lambda s: jnp.maximum(s - 1, 0)

    def pcol(off, tile):
        return pl.BlockSpec((MIX_T, PROJ_TILE), lambda s: (tile(s), off // PROJ_TILE))

    def full(a):
        nd = a.ndim
        return pl.BlockSpec(a.shape, lambda s, nd=nd: (0,) * nd)

    in_specs = [pcol(OFF_U, cur), pcol(OFF_V, cur), pcol(OFF_Z, cur),
                pcol(OFF_XBC, cur), pcol(OFF_XBC + PROJ_TILE, cur),
                pl.BlockSpec((MIX_T, LANES), lambda s: (cur(s), 0)),
                pcol(OFF_GA - SSM_HEADS, prev), pcol(OFF_GB - SSM_HEADS, prev),
                pl.BlockSpec((MIX_T, D_MODEL), lambda s: (prev(s), 0))]
    in_specs += [full(a) for a in consts] + [full(wa), full(wb), full(wo)]
    return pl.pallas_call(
        functools.partial(_mixer_kernel, tps=tps),
        grid=(ntiles + 1,),
        in_specs=in_specs,
        out_specs=pl.BlockSpec((MIX_T, D_MODEL), lambda s: (prev(s), 0)),
        out_shape=jax.ShapeDtypeStruct((n, D_MODEL), F32),
        scratch_shapes=[
            pltpu.VMEM((SSD_CHUNK, SSM_CONV_DIM), BF16),
            pltpu.VMEM((MIX_T, SSM_CONV_DIM), F32),
            pltpu.VMEM((SSM_STATE, SSM_INNER), F32),
            pltpu.VMEM((MIX_T, SGU_WIDTH), BF16),
            pltpu.VMEM((MIX_T, SSM_INNER), BF16),
            pltpu.VMEM((MIX_T, SGU_WIDTH), BF16),
            pltpu.VMEM((MIX_T, SSM_INNER), BF16),
        ],
        compiler_params=pltpu.CompilerParams(
            dimension_semantics=("arbitrary",), vmem_limit_bytes=VMEM_LIMIT),
        name="mixer",
    )(proj, proj, proj, proj, proj, dt_raw, proj, proj, x2d, *consts, wa, wb, wo)


def _router_kernel(x_ref, g_ref, wr_ref, br_ref, upper_ref, hp_ref, idx_ref, gate_ref, rank_ref, cnt_ref,
                   carry_scr):
    @pl.when(pl.program_id(0) == 0)
    def _():
        carry_scr[...] = jnp.zeros_like(carry_scr)

    h = _rms(x_ref[...], g_ref[...])
    _store_token_tiles(hp_ref, h, ROUTE_T)

    h_hi = h.astype(BF16)
    h_lo = (h - h_hi.astype(F32)).astype(BF16)
    logits = (jnp.dot(h_hi, wr_ref[0], preferred_element_type=F32)
              + jnp.dot(h_hi, wr_ref[1], preferred_element_type=F32)
              + jnp.dot(h_lo, wr_ref[0], preferred_element_type=F32))
    lt = logits.T[0:N_EXPERTS, :] + br_ref[...]
    eidx = lax.broadcasted_iota(I32, (N_EXPERTS, ROUTE_T), 0).astype(F32)
    vals = lt
    sel_any = jnp.zeros((N_EXPERTS, ROUTE_T), F32)
    sels, tops = [], []
    for k in range(TOP_K):
        m = jnp.max(vals, axis=0, keepdims=True)
        first = jnp.min(jnp.where(vals == m, eidx, float(N_EXPERTS)), axis=0, keepdims=True)
        sel = eidx == first
        vals = jnp.where(sel, -jnp.inf, vals)
        sel_f = sel.astype(F32)
        sel_any = sel_any + sel_f
        sels.append(sel_f)
        tops.append(m)
        idx_ref[k:k + 1, :] = first.astype(I32)
    es = [jnp.exp(t - tops[0]) for t in tops]
    denom = functools.reduce(lambda a, b: a + b, es)
    for k in range(TOP_K):
        gate_ref[k:k + 1, :] = es[k] / denom
    gate_ref[TOP_K:, :] = jnp.zeros((SUBLANES - TOP_K, ROUTE_T), F32)

    excl = jnp.dot(sel_any.astype(BF16), upper_ref[...], preferred_element_type=F32) + carry_scr[:, 0:1]
    for k in range(TOP_K):
        rank_ref[k:k + 1, :] = jnp.sum(sels[k] * excl, axis=0, keepdims=True).astype(I32)
    new_carry = carry_scr[...] + jnp.sum(sel_any, axis=1, keepdims=True)
    carry_scr[...] = new_carry
    cnt_ref[...] = new_carry.astype(I32)


def _router(x1, g, wr_pad, br_col, upper):
    n = x1.shape[0]
    per_k = pl.BlockSpec((TOP_K, ROUTE_T), lambda i: (0, i))
    return pl.pallas_call(
        _router_kernel,
        grid=(n // ROUTE_T,),
        in_specs=[
            pl.BlockSpec((ROUTE_T, D_MODEL), lambda i: (i, 0)),
            pl.BlockSpec((1, D_MODEL), lambda i: (0, 0)),
            pl.BlockSpec((2, D_MODEL, LANES), lambda i: (0, 0, 0)),
            pl.BlockSpec((N_EXPERTS, 1), lambda i: (0, 0)),
            pl.BlockSpec((ROUTE_T, ROUTE_T), lambda i: (0, 0)),
        ],
        out_specs=[
            pl.BlockSpec((ROUTE_T * TOK_SUB, LANES), lambda i: (i, 0)),
            per_k,
            pl.BlockSpec((SUBLANES, ROUTE_T), lambda i: (0, i)),
            per_k,
            pl.BlockSpec((N_EXPERTS, LANES), lambda i: (0, 0)),
        ],
        out_shape=[
            jax.ShapeDtypeStruct((n * TOK_SUB, LANES), F32),
            jax.ShapeDtypeStruct((TOP_K, n), I32),
            jax.ShapeDtypeStruct((SUBLANES, n), F32),
            jax.ShapeDtypeStruct((TOP_K, n), I32),
            jax.ShapeDtypeStruct((N_EXPERTS, LANES), I32),
        ],
        scratch_shapes=[pltpu.VMEM((N_EXPERTS, LANES), F32)],
        compiler_params=pltpu.CompilerParams(
            dimension_semantics=("arbitrary",), vmem_limit_bytes=VMEM_LIMIT),
        name="router",
    )(x1, g, wr_pad, br_col, upper)


def _dispatch_kernel(dest_ref, pad_ref, hp_ref, xs_hbm, zero_scr, sem_rows):
    i = pl.program_id(0)
    tbl = i * (TOP_K * DISP_T)

    def zero_copy(s):
        return pltpu.make_async_copy(zero_scr.at[0:TOK_SUB, :], xs_hbm.at[_tile_rows(s), :], sem_rows)

    def zero_run(s):
        rows = pl.ds(pl.multiple_of(s * TOK_SUB, TOK_SUB), ZERO_RUN * TOK_SUB)
        return pltpu.make_async_copy(zero_scr, xs_hbm.at[rows, :], sem_rows)

    def zero_pads(act):
        for e in range(N_EXPERTS):
            lo, hi = pad_ref[e], pad_ref[N_EXPERTS + e]
            nrun = (hi - lo) // ZERO_RUN
            lax.fori_loop(0, nrun, lambda r, c: (act(zero_run(lo + r * ZERO_RUN)), c)[1], 0)
            lax.fori_loop(lo + nrun * ZERO_RUN, hi, lambda s, c: (act(zero_copy(s)), c)[1], 0)

    @pl.when(i == 0)
    def _():
        zero_scr[...] = jnp.zeros_like(zero_scr)
        zero_pads(lambda d: d.start())
        zero_pads(lambda d: d.wait())

    def tile_copy(t, k):
        d = dest_ref[tbl + t * TOP_K + k]
        return pltpu.make_async_copy(hp_ref.at[_tile_rows(t), :], xs_hbm.at[_tile_rows(d), :], sem_rows)

    def issue(tb, carry):
        for r in range(DMA_UNROLL):
            for k in range(TOP_K):
                tile_copy(tb * DMA_UNROLL + r, k).start(priority=(r * TOP_K + k) % 2)
        return carry

    lax.fori_loop(0, DISP_T // DMA_UNROLL, issue, 0)

    def drain(tb, carry):
        for _ in range(DMA_UNROLL * TOP_K):
            zero_copy(0).wait()
        return carry

    lax.fori_loop(0, DISP_T // DMA_UNROLL, drain, 0)


def _dispatch(dest_flat, pad_tbl, hp, cap):
    n = hp.shape[0] // TOK_SUB
    grid_spec = pltpu.PrefetchScalarGridSpec(
        num_scalar_prefetch=2,
        grid=(n // DISP_T,),
        in_specs=[pl.BlockSpec((DISP_T * TOK_SUB, LANES), lambda i, d, p: (i, 0))],
        out_specs=pl.BlockSpec(memory_space=pl.ANY),
        scratch_shapes=[pltpu.VMEM((ZERO_RUN * TOK_SUB, LANES), F32), pltpu.SemaphoreType.DMA],
    )
    return pl.pallas_call(
        _dispatch_kernel,
        grid_spec=grid_spec,
        out_shape=jax.ShapeDtypeStruct((cap * TOK_SUB, LANES), F32),
        compiler_params=pltpu.CompilerParams(dimension_semantics=("arbitrary",)),
        name="dispatch",
    )(dest_flat, pad_tbl, hp)


def _expert_kernel(be_ref, nxt_ref, nv_ref, xs_ref, w1_hbm, b1_ref, w2_hbm, b2_ref, ys_ref,
                   w1f_scr, w2f_scr, w1b_scr, w2b_scr, sems):
    b = pl.program_id(0)
    valid = nv_ref[b]
    used = valid > 0
    new_expert = jnp.logical_or(b == 0, be_ref[b] != be_ref[jnp.maximum(b - 1, 0)])

    def fetch(e):
        return (pltpu.make_async_copy(w1_hbm.at[e], w1f_scr, sems.at[0]),
                pltpu.make_async_copy(w2_hbm.at[e], w2f_scr, sems.at[1]))

    @pl.when(jnp.logical_and(used, b == 0))
    def _():
        for c in fetch(be_ref[0]):
            c.start()

    @pl.when(jnp.logical_and(used, new_expert))
    def _():
        for c in fetch(be_ref[b]):
            c.wait()

        def cast(rb, carry):
            rows = pl.ds(pl.multiple_of(rb * CAST_ROWS, CAST_ROWS), CAST_ROWS)
            w1b_scr[rows, :] = w1f_scr[rows, :].astype(BF16)
            w2b_scr[rows, :] = w2f_scr[rows, :].astype(BF16)
            return carry

        lax.fori_loop(0, D_MODEL // CAST_ROWS, cast, 0)

        @pl.when(nxt_ref[b] >= 0)
        def _():
            for c in fetch(nxt_ref[b]):
                c.start()

    def expert_mlp(rows):
        x = _load_token_tiles(xs_ref, rows).astype(BF16)
        hid = jnp.dot(x, w1b_scr[...], preferred_element_type=F32) + b1_ref[0]
        glu = jnp.minimum(hid[:, :D_FF_EXPERT], SWIGLU_LIMIT)
        lin = jnp.clip(hid[:, D_FF_EXPERT:], -SWIGLU_LIMIT, SWIGLU_LIMIT)
        act = glu * _sigmoid(SWIGLU_ALPHA * glu) * (lin + 1.0)
        y = jnp.dot(act.astype(BF16), w2b_scr[...], preferred_element_type=F32) + b2_ref[0]
        _store_token_tiles(ys_ref, y, rows)

    @pl.when(valid > MOE_SUB)
    def _():
        expert_mlp(MOE_BLOCK)

    @pl.when(jnp.logical_and(used, valid <= MOE_SUB))
    def _():
        expert_mlp(MOE_SUB)


def _experts(blk_expert, blk_next, blk_valid, xs, w1, b1, w2, b2):
    cap = xs.shape[0] // TOK_SUB
    n_blocks = cap // MOE_BLOCK
    grid_spec = pltpu.PrefetchScalarGridSpec(
        num_scalar_prefetch=3,
        grid=(n_blocks,),
        in_specs=[
            pl.BlockSpec((MOE_BLOCK * TOK_SUB, LANES), lambda b, be, nx, nv: (b, 0)),
            pl.BlockSpec(memory_space=pl.ANY),
            pl.BlockSpec((1, 1, 2 * D_FF_EXPERT), lambda b, be, nx, nv: (be[b], 0, 0)),
            pl.BlockSpec(memory_space=pl.ANY),
            pl.BlockSpec((1, 1, D_MODEL), lambda b, be, nx, nv: (be[b], 0, 0)),
        ],
        out_specs=pl.BlockSpec((MOE_BLOCK * TOK_SUB, LANES), lambda b, be, nx, nv: (b, 0)),
        scratch_shapes=[pltpu.VMEM((D_MODEL, 2 * D_FF_EXPERT), F32),
                        pltpu.VMEM((D_FF_EXPERT, D_MODEL), F32),
                        pltpu.VMEM((D_MODEL, 2 * D_FF_EXPERT), BF16),
                        pltpu.VMEM((D_FF_EXPERT, D_MODEL), BF16),
                        pltpu.SemaphoreType.DMA((2,))],
    )
    assert D_MODEL == D_FF_EXPERT
    assert MOE_BLOCK == 2 * MOE_SUB
    return pl.pallas_call(
        _expert_kernel,
        grid_spec=grid_spec,
        out_shape=jax.ShapeDtypeStruct((cap * TOK_SUB, LANES), F32),
        compiler_params=pltpu.CompilerParams(
            dimension_semantics=("arbitrary",), vmem_limit_bytes=VMEM_LIMIT),
        name="experts",
    )(blk_expert, blk_next, blk_valid, xs, w1, b1, w2, b2)


def _combine_kernel(dest_ref, ys_hbm, x1_ref, gate_ref, p_ref, pg_ref, wg_ref, wp_ref, fg_ref,
                    out_ref, ybuf, sems, *, final):
    i = pl.program_id(0)
    nsteps = pl.num_programs(0)
    buf_tokens = TOP_K * COMB_T

    def issue_part(step, slot, part, parts):
        tbl = step * buf_tokens
        per_part = COMB_T // DMA_UNROLL // parts

        def issue(tb, carry):
            for r in range(DMA_UNROLL):
                for k in range(TOP_K):
                    t = tb * DMA_UNROLL + r
                    pltpu.make_async_copy(
                        ys_hbm.at[_tile_rows(dest_ref[tbl + t * TOP_K + k]), :],
                        ybuf.at[_tile_rows(slot * buf_tokens + k * COMB_T + t), :],
                        sems.at[slot]).start(priority=(r * TOP_K + k) % 2)
            return carry

        last = jnp.where(step < nsteps, (part + 1) * per_part, part * per_part)
        lax.fori_loop(part * per_part, last, issue, 0)

    def drain_all(slot):
        def drain(tb, carry):
            for _ in range(DMA_UNROLL * TOP_K):
                pltpu.make_async_copy(ys_hbm.at[_tile_rows(0), :],
                                      ybuf.at[_tile_rows(slot * buf_tokens), :], sems.at[slot]).wait()
            return carry

        lax.fori_loop(0, COMB_T // DMA_UNROLL, drain, 0)

    @pl.when(i == 0)
    def _():
        issue_part(0, 0, 0, 1)

    nsub = COMB_T // COMB_SUB
    for slot in range(2):
        @pl.when(i % 2 == slot)
        def _():
            drain_all(slot)

            for sub in range(nsub):
                if sub % (nsub // ISSUE_PARTS) == 0:
                    issue_part(i + 1, 1 - slot, sub // (nsub // ISSUE_PARTS), ISSUE_PARTS)
                rows = slice(sub * COMB_SUB, (sub + 1) * COMB_SUB)
                x2 = x1_ref[rows, :]
                for k in range(TOP_K):
                    x2 = x2 + gate_ref[rows, k:k + 1] * _load_token_tiles(
                        ybuf, COMB_SUB, slot * buf_tokens + k * COMB_T + sub * COMB_SUB)
                hp = _rms(x2, pg_ref[...]).astype(BF16)
                gate = _sigmoid(jnp.dot(hp, wg_ref[...], preferred_element_type=F32))
                emb = jnp.dot(p_ref[rows, :].astype(BF16), wp_ref[...], preferred_element_type=F32)
                x3 = x2 + gate * emb
                if final:
                    x3 = _rms(x3, fg_ref[...])
                out_ref[rows, :] = x3


def _combine(dest_flat, ys, x1, gates_t, p2d, pg, wg, wp, fg, final):
    n = x1.shape[0]
    grid_spec = pltpu.PrefetchScalarGridSpec(
        num_scalar_prefetch=1,
        grid=(n // COMB_T,),
        in_specs=[
            pl.BlockSpec(memory_space=pl.ANY),
            pl.BlockSpec((COMB_T, D_MODEL), lambda i, d: (i, 0)),
            pl.BlockSpec((COMB_T, SUBLANES), lambda i, d: (i, 0)),
            pl.BlockSpec((COMB_T, PLE_DIM), lambda i, d: (i, 0)),
            pl.BlockSpec((1, D_MODEL), lambda i, d: (0, 0)),
            pl.BlockSpec((D_MODEL, D_MODEL), lambda i, d: (0, 0)),
            pl.BlockSpec((PLE_DIM, D_MODEL), lambda i, d: (0, 0)),
            pl.BlockSpec((1, D_MODEL), lambda i, d: (0, 0)),
        ],
        out_specs=pl.BlockSpec((COMB_T, D_MODEL), lambda i, d: (i, 0)),
        scratch_shapes=[pltpu.VMEM((2 * TOP_K * COMB_T * TOK_SUB, LANES), F32),
                        pltpu.SemaphoreType.DMA((2,))],
    )
    return pl.pallas_call(
        functools.partial(_combine_kernel, final=final),
        grid_spec=grid_spec,
        out_shape=jax.ShapeDtypeStruct((n, D_MODEL), F32),
        compiler_params=pltpu.CompilerParams(
            dimension_semantics=("arbitrary",), vmem_limit_bytes=VMEM_LIMIT),
        name="combine",
    )(dest_flat, ys, x1, gates_t, p2d, pg, wg, wp, fg)


def _layer(x2d, p2d, bsz, seq, mix_norm, w_in, sgu_ln_g, sgu_ln_b, sgu_w, sgu_b, conv_w, conv_b,
           dt_bias, a_log, d_skip, ssm_norm, w_branch_a, w_branch_b, w_out, ffn_norm, w_router,
           b_router, w1, b1, w2, b2, ple_norm, w_ple_gate, w_ple_proj, final_norm, final):
    n = x2d.shape[0]
    row = lambda a: a.reshape(1, -1).astype(F32)

    w_main = jnp.concatenate([w_in[:, :OFF_DT], w_in[:, OFF_GA:]], axis=1).astype(BF16)
    w_dt = jnp.pad(w_in[:, OFF_DT:OFF_GA], ((0, 0), (0, LANES - SSM_HEADS))).astype(BF16)
    pos = jnp.arange(SGU_LEN)
    allowed = (pos[None, :] // CHUNK) <= (pos[:, None] // CHUNK)
    wsgu = jnp.where(allowed[None], sgu_w, 0.0).astype(BF16)
    bsgu = jnp.repeat(sgu_b.T, SGU_HEAD_DIM, axis=1).astype(F32)
    dtb = dt_bias.reshape(-1, 1).astype(F32)
    acol = -jnp.exp(a_log.astype(F32)).reshape(-1, 1)
    dskip = jnp.repeat(d_skip.astype(F32), SSM_HEAD_DIM).reshape(1, -1)
    head_of_col = jnp.arange(SSM_INNER) // SSM_HEAD_DIM
    eexp = (jnp.arange(LANES)[:, None] == head_of_col[None, :]).astype(BF16)
    triu = (jnp.arange(SSD_CHUNK)[:, None] <= jnp.arange(SSD_CHUNK)[None, :]).astype(BF16)
    t_out = jnp.arange((SSM_CONV - 1) * SSD_CHUNK)
    src = SSD_CHUNK + t_out % SSD_CHUNK - (SSM_CONV - 1) + t_out // SSD_CHUNK
    shift = (src[:, None] == jnp.arange(2 * SSD_CHUNK)[None, :]).astype(BF16)
    consts = [row(sgu_ln_g), row(sgu_ln_b), wsgu, bsgu, conv_w.astype(F32), row(conv_b), dtb, acol,
              dskip, row(ssm_norm), eexp, shift, triu]

    proj, dt_raw = _in_proj(x2d, row(mix_norm), w_main, w_dt)
    x1 = _mixer(proj, dt_raw, x2d, bsz, seq, consts, w_branch_a.astype(BF16),
                w_branch_b.astype(BF16), w_out.astype(BF16))

    wr_f = jnp.pad(w_router.astype(F32), ((0, 0), (0, LANES - N_EXPERTS)))
    wr_hi = wr_f.astype(BF16)
    wr_pad = jnp.stack([wr_hi, (wr_f - wr_hi.astype(F32)).astype(BF16)])
    upper = (jnp.arange(ROUTE_T)[:, None] < jnp.arange(ROUTE_T)[None, :]).astype(BF16)
    hp, idx, gates, rank, cnt = _router(x1, row(ffn_norm), wr_pad, b_router.reshape(-1, 1).astype(F32),
                                        upper)

    counts = cnt[:, 0]
    padded = (counts + MOE_BLOCK - 1) // MOE_BLOCK * MOE_BLOCK
    pend = jnp.cumsum(padded)
    pstart = pend - padded
    nk = n * TOP_K
    cap = (nk + MOE_BLOCK - 1) // MOE_BLOCK * MOE_BLOCK + N_EXPERTS * MOE_BLOCK
    n_blocks = cap // MOE_BLOCK
    ex = jnp.arange(N_EXPERTS, dtype=I32)
    dest = rank + jnp.sum(jnp.where(idx[..., None] == ex, pstart.astype(I32), 0), axis=-1)
    blk_start = jnp.arange(n_blocks, dtype=I32) * MOE_BLOCK
    blk_expert = jnp.minimum(jnp.sum((pend[None, :] <= blk_start[:, None]).astype(I32), axis=1),
                             N_EXPERTS - 1).astype(I32)
    of_block = blk_expert[:, None] == ex[None, :]
    per_block = lambda v: jnp.sum(jnp.where(of_block, v[None, :].astype(I32), 0), axis=1)
    blk_valid = jnp.where(blk_start < pend[-1],
                          jnp.clip(per_block(pstart + counts) - blk_start, 0, MOE_BLOCK), 0).astype(I32)
    live = jnp.where(counts > 0, ex, N_EXPERTS)
    later = jnp.concatenate([lax.cummin(live[::-1])[::-1][1:], jnp.full((1,), N_EXPERTS, I32)])
    blk_next = per_block(jnp.where(later < N_EXPERTS, later, -1))

    dest_flat = dest.T.reshape(-1)
    used_end = pstart + (counts + MOE_SUB - 1) // MOE_SUB * MOE_SUB
    pad_tbl = jnp.concatenate([pstart + counts, used_end]).astype(I32)

    xs = _dispatch(dest_flat, pad_tbl, hp, cap)
    ys = _experts(blk_expert, blk_next, blk_valid, xs, w1.astype(F32),
                  b1.reshape(N_EXPERTS, 1, -1).astype(F32), w2.astype(F32),
                  b2.reshape(N_EXPERTS, 1, -1).astype(F32))
    return _combine(dest_flat, ys, x1, gates.T, p2d, row(ple_norm), w_ple_gate.astype(BF16),
                    w_ple_proj.astype(BF16), row(final_norm), final)


def kernel(x, p, mix_norm, w_in, sgu_ln_g, sgu_ln_b, sgu_w, sgu_b, conv_w, conv_b, dt_bias, a_log,
           d_skip, ssm_norm, w_branch_a, w_branch_b, w_out, ffn_norm, w_router, b_router, w1, b1,
           w2, b2, ple_norm, w_ple_gate, w_ple_proj, final_norm):
    bsz, seq, d = x.shape
    depth = w_in.shape[0]
    assert d == D_MODEL and seq % MIX_T == 0 and (bsz * seq) % max(IN_TM, ROUTE_T, DISP_T) == 0
    assert ROUTE_T == DISP_T == COMB_T
    x2d = x.reshape(bsz * seq, d)
    for i in range(depth):
        x2d = _layer(x2d, p[i].reshape(bsz * seq, PLE_DIM), bsz, seq, mix_norm[i], w_in[i],
                     sgu_ln_g[i], sgu_ln_b[i], sgu_w[i], sgu_b[i], conv_w[i], conv_b[i], dt_bias[i],
                     a_log[i], d_skip[i], ssm_norm[i], w_branch_a[i], w_branch_b[i], w_out[i],
                     ffn_norm[i], w_router[i], b_router[i], w1[i], b1[i], w2[i], b2[i], ple_norm[i],
                     w_ple_gate[i], w_ple_proj[i], final_norm, final=(i == depth - 1))
    return x2d.reshape(bsz, seq, d)
```

```python
import functools
import math

import jax
import jax.numpy as jnp
from jax import lax
from jax.experimental import pallas as pl
from jax.experimental.pallas import tpu as pltpu

F32 = jnp.float32
BF16 = jnp.bfloat16
I32 = jnp.int32

D_MODEL = 1024
CHUNK = 64
PLE_DIM = 256
RMS_EPS = 1e-6
LN_EPS = 1e-5

SGU_HEADS = 8
SGU_HEAD_DIM = 128
SGU_WIDTH = SGU_HEADS * SGU_HEAD_DIM
SGU_LEN = 128

SSM_HEADS = 16
SSM_HEAD_DIM = 64
SSM_INNER = SSM_HEADS * SSM_HEAD_DIM
SSM_GROUPS = 4
SSM_STATE = 128
SSM_CONV = 4
SSD_CHUNK = 128
SSM_CONV_DIM = SSM_INNER + 2 * SSM_GROUPS * SSM_STATE

N_EXPERTS = 32
TOP_K = 4
D_FF_EXPERT = 1024
SWIGLU_LIMIT = 7.0
SWIGLU_ALPHA = 1.702
MOE_BLOCK = 512
MOE_SUB = 256

OFF_U = 0
OFF_V = OFF_U + SGU_WIDTH
OFF_Z = OFF_V + SGU_WIDTH
OFF_XBC = OFF_Z + SSM_INNER
OFF_DT = OFF_XBC + SSM_CONV_DIM
OFF_GA = OFF_DT + SSM_HEADS
OFF_GB = OFF_GA + D_MODEL
IN_PROJ_DIM = OFF_GB + D_MODEL

LANES = 128
SUBLANES = 8
PROJ_MAIN = IN_PROJ_DIM - SSM_HEADS
PROJ_TILE = 1024
TOK_SUB = D_MODEL // LANES

VMEM_LIMIT = 56 * 1024 * 1024

IN_TM = 512
IN_TN = 1792
MIX_T = 512
ROUTE_T = 1024
DISP_T = 512
COMB_T = DISP_T
COMB_SUB = 128
ISSUE_PARTS = 2
DMA_UNROLL = 8
ZERO_RUN = 64
CAST_ROWS = 128


def _sigmoid(x):
    return 1.0 / (1.0 + jnp.exp(-x))


def _gelu_exact(x):
    return 0.5 * x * (1.0 + lax.erf(x * (1.0 / math.sqrt(2.0))))


def _softplus(x):
    return jnp.maximum(x, 0.0) + jnp.log1p(jnp.exp(-jnp.abs(x)))


def _rms(x, g):
    ms = jnp.mean(x * x, axis=-1, keepdims=True)
    return x * lax.rsqrt(ms + RMS_EPS) * g


def _store_token_tiles(ref, val, rows, start=0):
    for j in range(TOK_SUB):
        ref[pl.ds(start * TOK_SUB + j, rows, stride=TOK_SUB), :] = val[:, j * LANES:(j + 1) * LANES]


def _load_token_tiles(ref, rows, start=0):
    return jnp.concatenate(
        [ref[pl.ds(start * TOK_SUB + j, rows, stride=TOK_SUB), :] for j in range(TOK_SUB)], axis=1)


def _tile_rows(idx):
    return pl.ds(pl.multiple_of(idx * TOK_SUB, TOK_SUB), TOK_SUB)


def _inproj_kernel(x_ref, g_ref, w_ref, wdt_ref, proj_ref, dt_ref, h_scr):
    @pl.when(pl.program_id(0) == 0)
    def _():
        h_scr[...] = jnp.zeros_like(h_scr)

    h = h_scr[...]
    dt_ref[...] = jnp.dot(h, wdt_ref[...], preferred_element_type=F32)
    for c in range(PROJ_MAIN // IN_TN):
        cols = slice(c * IN_TN, (c + 1) * IN_TN)
        proj_ref[:, cols] = jnp.dot(h, w_ref[:, cols], preferred_element_type=F32).astype(BF16)
    h_scr[...] = _rms(x_ref[...], g_ref[...]).astype(BF16)


def _in_proj(x2d, g, w_main, w_dt):
    n = x2d.shape[0]
    ntiles = n // IN_TM
    return pl.pallas_call(
        _inproj_kernel,
        grid=(ntiles + 1,),
        in_specs=[
            pl.BlockSpec((IN_TM, D_MODEL), lambda s: (jnp.minimum(s, ntiles - 1), 0)),
            pl.BlockSpec((1, D_MODEL), lambda s: (0, 0)),
            pl.BlockSpec((D_MODEL, PROJ_MAIN), lambda s: (0, 0)),
            pl.BlockSpec((D_MODEL, LANES), lambda s: (0, 0)),
        ],
        out_specs=[
            pl.BlockSpec((IN_TM, PROJ_MAIN), lambda s: (jnp.maximum(s - 1, 0), 0)),
            pl.BlockSpec((IN_TM, LANES), lambda s: (jnp.maximum(s - 1, 0), 0)),
        ],
        out_shape=[
            jax.ShapeDtypeStruct((n, PROJ_MAIN), BF16),
            jax.ShapeDtypeStruct((n, LANES), F32),
        ],
        scratch_shapes=[pltpu.VMEM((IN_TM, D_MODEL), BF16)],
        compiler_params=pltpu.CompilerParams(
            dimension_semantics=("arbitrary",), vmem_limit_bytes=VMEM_LIMIT),
        name="in_proj",
    )(x2d, g, w_main, w_dt)


def _split3(a):
    a1 = a.astype(BF16)
    r1 = a - a1.astype(F32)
    a2 = r1.astype(BF16)
    a3 = (r1 - a2.astype(F32)).astype(BF16)
    return a1, a2, a3


def _dot3(parts, w):
    out = jnp.dot(parts[0], w, preferred_element_type=F32)
    for p in parts[1:]:
        out = out + jnp.dot(p, w, preferred_element_type=F32)
    return out


def _mixer_kernel(u_ref, v_ref, z_ref, xb0_ref, xb1_ref, dt_ref, ga_ref, gb_ref, x_ref,
                  lng_ref, lnb_ref, wsgu_ref, bsgu_ref, convw_ref, convb_ref, dtb_ref, acol_ref,
                  dskip_ref, normw_ref, eexp_ref, shift_ref, triu_ref, wa_ref, wb_ref, wo_ref,
                  x1_ref,
                  tail_scr, xc_scr, state_scr, ya_scr, yb_scr, yap_scr, ybp_scr, *, tps):
    nchunk = MIX_T // SSD_CHUNK
    s = pl.program_id(0)

    @pl.when(s % tps == 0)
    def _():
        tail_scr[...] = jnp.zeros_like(tail_scr)
        state_scr[...] = jnp.zeros_like(state_scr)

    @pl.when(s == 0)
    def _():
        yap_scr[...] = jnp.zeros_like(yap_scr)
        ybp_scr[...] = jnp.zeros_like(ybp_scr)

    xb_refs = (xb0_ref, xb1_ref)
    for c in range(nchunk):
        r0 = c * SSD_CHUNK
        for hf, xb_ref in enumerate(xb_refs):
            cols = slice(hf * PROJ_TILE, (hf + 1) * PROJ_TILE)
            cur = xb_ref[r0:r0 + SSD_CHUNK, :]
            prev = tail_scr[:, cols] if c == 0 else xb_ref[r0 - SSD_CHUNK:r0, :]
            sh = jnp.dot(shift_ref[...], jnp.concatenate([prev, cur], axis=0), preferred_element_type=F32)
            acc = convb_ref[:, cols] + convw_ref[SSM_CONV - 1:SSM_CONV, cols] * cur.astype(F32)
            for j in range(SSM_CONV - 1):
                acc = acc + convw_ref[j:j + 1, cols] * sh[j * SSD_CHUNK:(j + 1) * SSD_CHUNK, :]
            xc_scr[r0:r0 + SSD_CHUNK, cols] = acc * _sigmoid(acc)
    for hf, xb_ref in enumerate(xb_refs):
        tail_scr[:, hf * PROJ_TILE:(hf + 1) * PROJ_TILE] = xb_ref[MIX_T - SSD_CHUNK:MIX_T, :]

    row_i = lax.broadcasted_iota(I32, (SSD_CHUNK, SSD_CHUNK), 0)
    col_i = lax.broadcasted_iota(I32, (SSD_CHUNK, SSD_CHUNK), 1)
    tril = row_i >= col_i
    lane_hi = col_i >= SSM_HEAD_DIM
    eexp = eexp_ref[...]
    triu = triu_ref[...]

    for c in range(nchunk):
        rows = slice(c * SSD_CHUNK, (c + 1) * SSD_CHUNK)

        ug = _gelu_exact(u_ref[rows, :].astype(F32))
        vg = _gelu_exact(v_ref[rows, :].astype(F32))
        mu = jnp.mean(vg, axis=-1, keepdims=True)
        vc = vg - mu
        var = jnp.mean(vc * vc, axis=-1, keepdims=True)
        vn = (vc * lax.rsqrt(var + LN_EPS) * lng_ref[...] + lnb_ref[...]).astype(BF16)
        for g in range(SGU_HEADS):
            cols = slice(g * SGU_HEAD_DIM, (g + 1) * SGU_HEAD_DIM)
            mixed = jnp.dot(wsgu_ref[g], vn[:, cols], preferred_element_type=F32)
            ya_scr[rows, cols] = (ug[:, cols] * (mixed + bsgu_ref[:, cols])).astype(BF16)

        dt_t = _softplus(dt_ref[rows, :].T[0:SSM_HEADS, :] + dtb_ref[...])
        acs_t = _dot3(_split3(dt_t * acol_ref[...]), triu)
        w_t = jnp.exp(acs_t[:, SSD_CHUNK - 1:SSD_CHUNK] - acs_t) * dt_t
        acs = jnp.concatenate([acs_t, jnp.zeros((LANES - SSM_HEADS, SSD_CHUNK), F32)], axis=0).T
        aend = jnp.broadcast_to(acs[SSD_CHUNK - 1:SSD_CHUNK, :], (SUBLANES, LANES))
        dec = jnp.exp(_dot3(_split3(aend), eexp)[0:1, :])

        for gi in range(SSM_GROUPS):
            bg = xc_scr[rows, SSM_INNER + gi * SSM_STATE:SSM_INNER + (gi + 1) * SSM_STATE]
            cg = xc_scr[rows, SSM_INNER + (SSM_GROUPS + gi) * SSM_STATE:
                        SSM_INNER + (SSM_GROUPS + gi + 1) * SSM_STATE]
            bg_t = bg.T
            cb = jnp.dot(cg.astype(BF16), bg_t.astype(BF16), preferred_element_type=F32)
            for pr in range(2):
                lb = gi * 2 + pr
                cols = slice(lb * LANES, (lb + 1) * LANES)
                xpair = xc_scr[rows, cols]
                hpair = state_scr[:, cols]
                acc = jnp.zeros((SSD_CHUNK, LANES), F32)
                st = jnp.zeros((SSM_STATE, LANES), F32)
                for hh in range(2):
                    h = lb * 2 + hh
                    lmask = lane_hi if hh == 1 else jnp.logical_not(lane_hi)
                    xm = jnp.where(lmask, xpair, 0.0).astype(BF16)
                    hm = jnp.where(lmask, hpair, 0.0).astype(BF16)
                    colb = acs[:, h:h + 1]
                    rowb = acs_t[h:h + 1, :]
                    decay = jnp.exp(jnp.where(tril, colb - rowb, -jnp.inf))
                    m_h = (cb * decay * dt_t[h:h + 1, :]).astype(BF16)
                    c_h = (cg * jnp.exp(colb)).astype(BF16)
                    acc = acc + jnp.dot(m_h, xm, preferred_element_type=F32)
                    acc = acc + jnp.dot(c_h, hm, preferred_element_type=F32)
                    btw = (bg_t * w_t[h:h + 1, :]).astype(BF16)
                    st = st + jnp.dot(btw, xm, preferred_element_type=F32)
                state_scr[:, cols] = hpair * dec[:, cols] + st
                y = acc + dskip_ref[:, cols] * xpair
                zz = z_ref[rows, cols].astype(F32)
                xc_scr[rows, cols] = y * (zz * _sigmoid(zz))
            gcols = slice(gi * 2 * LANES, (gi + 1) * 2 * LANES)
            yg = xc_scr[rows, gcols]
            ms = jnp.mean(yg * yg, axis=-1, keepdims=True)
            yb_scr[rows, gcols] = (yg * lax.rsqrt(ms + RMS_EPS) * normw_ref[:, gcols]).astype(BF16)

    ma = jnp.dot(yap_scr[...], wa_ref[...], preferred_element_type=F32)
    mb = jnp.dot(ybp_scr[...], wb_ref[...], preferred_element_type=F32)
    merged = (_sigmoid(ga_ref[...].astype(F32)) * ma + _sigmoid(gb_ref[...].astype(F32)) * mb)
    x1_ref[...] = x_ref[...] + jnp.dot(merged.astype(BF16), wo_ref[...], preferred_element_type=F32)
    yap_scr[...] = ya_scr[...]
    ybp_scr[...] = yb_scr[...]


def _mixer(proj, dt_raw, x2d, bsz, seq, consts, wa, wb, wo):
    n = x2d.shape[0]
    tps = seq // MIX_T
    ntiles = bsz * tps
    cur = lambda s: jnp.minimum(s, ntiles - 1)
    prev = lambda s: jnp.maximum(s - 1, 0)

    def pcol(off, tile):
        return pl.BlockSpec((MIX_T, PROJ_TILE), lambda s: (tile(s), off // PROJ_TILE))

    def full(a):
        nd = a.ndim
        return pl.BlockSpec(a.shape, lambda s, nd=nd: (0,) * nd)

    in_specs = [pcol(OFF_U, cur), pcol(OFF_V, cur), pcol(OFF_Z, cur),
                pcol(OFF_XBC, cur), pcol(OFF_XBC + PROJ_TILE, cur),
                pl.BlockSpec((MIX_T, LANES), lambda s: (cur(s), 0)),
                pcol(OFF_GA - SSM_HEADS, prev), pcol(OFF_GB - SSM_HEADS, prev),
                pl.BlockSpec((MIX_T, D_MODEL), lambda s: (prev(s), 0))]
    in_specs += [full(a) for a in consts] + [full(wa), full(wb), full(wo)]
    return pl.pallas_call(
        functools.partial(_mixer_kernel, tps=tps),
        grid=(ntiles + 1,),
        in_specs=in_specs,
        out_specs=pl.BlockSpec((MIX_T, D_MODEL), lambda s: (prev(s), 0)),
        out_shape=jax.ShapeDtypeStruct((n, D_MODEL), F32),
        scratch_shapes=[
            pltpu.VMEM((SSD_CHUNK, SSM_CONV_DIM), BF16),
            pltpu.VMEM((MIX_T, SSM_CONV_DIM), F32),
            pltpu.VMEM((SSM_STATE, SSM_INNER), F32),
            pltpu.VMEM((MIX_T, SGU_WIDTH), BF16),
            pltpu.VMEM((MIX_T, SSM_INNER), BF16),
            pltpu.VMEM((MIX_T, SGU_WIDTH), BF16),
            pltpu.VMEM((MIX_T, SSM_INNER), BF16),
        ],
        compiler_params=pltpu.CompilerParams(
            dimension_semantics=("arbitrary",), vmem_limit_bytes=VMEM_LIMIT),
        name="mixer",
    )(proj, proj, proj, proj, proj, dt_raw, proj, proj, x2d, *consts, wa, wb, wo)


def _router_kernel(x_ref, g_ref, wr_ref, br_ref, upper_ref, hp_ref, idx_ref, gate_ref, rank_ref, cnt_ref,
                   carry_scr):
    @pl.when(pl.program_id(0) == 0)
    def _():
        carry_scr[...] = jnp.zeros_like(carry_scr)

    h = _rms(x_ref[...], g_ref[...])
    _store_token_tiles(hp_ref, h, ROUTE_T)

    h_hi = h.astype(BF16)
    h_lo = (h - h_hi.astype(F32)).astype(BF16)
    logits = (jnp.dot(h_hi, wr_ref[0], preferred_element_type=F32)
              + jnp.dot(h_hi, wr_ref[1], preferred_element_type=F32)
              + jnp.dot(h_lo, wr_ref[0], preferred_element_type=F32))
    lt = logits.T[0:N_EXPERTS, :] + br_ref[...]
    eidx = lax.broadcasted_iota(I32, (N_EXPERTS, ROUTE_T), 0).astype(F32)
    vals = lt
    sel_any = jnp.zeros((N_EXPERTS, ROUTE_T), F32)
    sels, tops = [], []
    for k in range(TOP_K):
        m = jnp.max(vals, axis=0, keepdims=True)
        first = jnp.min(jnp.where(vals == m, eidx, float(N_EXPERTS)), axis=0, keepdims=True)
        sel = eidx == first
        vals = jnp.where(sel, -jnp.inf, vals)
        sel_f = sel.astype(F32)
        sel_any = sel_any + sel_f
        sels.append(sel_f)
        tops.append(m)
        idx_ref[k:k + 1, :] = first.astype(I32)
    es = [jnp.exp(t - tops[0]) for t in tops]
    denom = functools.reduce(lambda a, b: a + b, es)
    for k in range(TOP_K):
        gate_ref[k:k + 1, :] = es[k] / denom
    gate_ref[TOP_K:, :] = jnp.zeros((SUBLANES - TOP_K, ROUTE_T), F32)

    excl = jnp.dot(sel_any.astype(BF16), upper_ref[...], preferred_element_type=F32) + carry_scr[:, 0:1]
    for k in range(TOP_K):
        rank_ref[k:k + 1, :] = jnp.sum(sels[k] * excl, axis=0, keepdims=True).astype(I32)
    new_carry = carry_scr[...] + jnp.sum(sel_any, axis=1, keepdims=True)
    carry_scr[...] = new_carry
    cnt_ref[...] = new_carry.astype(I32)


def _router(x1, g, wr_pad, br_col, upper):
    n = x1.shape[0]
    per_k = pl.BlockSpec((TOP_K, ROUTE_T), lambda i: (0, i))
    return pl.pallas_call(
        _router_kernel,
        grid=(n // ROUTE_T,),
        in_specs=[
            pl.BlockSpec((ROUTE_T, D_MODEL), lambda i: (i, 0)),
            pl.BlockSpec((1, D_MODEL), lambda i: (0, 0)),
            pl.BlockSpec((2, D_MODEL, LANES), lambda i: (0, 0, 0)),
            pl.BlockSpec((N_EXPERTS, 1), lambda i: (0, 0)),
            pl.BlockSpec((ROUTE_T, ROUTE_T), lambda i: (0, 0)),
        ],
        out_specs=[
            pl.BlockSpec((ROUTE_T * TOK_SUB, LANES), lambda i: (i, 0)),
            per_k,
            pl.BlockSpec((SUBLANES, ROUTE_T), lambda i: (0, i)),
            per_k,
            pl.BlockSpec((N_EXPERTS, LANES), lambda i: (0, 0)),
        ],
        out_shape=[
            jax.ShapeDtypeStruct((n * TOK_SUB, LANES), F32),
            jax.ShapeDtypeStruct((TOP_K, n), I32),
            jax.ShapeDtypeStruct((SUBLANES, n), F32),
            jax.ShapeDtypeStruct((TOP_K, n), I32),
            jax.ShapeDtypeStruct((N_EXPERTS, LANES), I32),
        ],
        scratch_shapes=[pltpu.VMEM((N_EXPERTS, LANES), F32)],
        compiler_params=pltpu.CompilerParams(
            dimension_semantics=("arbitrary",), vmem_limit_bytes=VMEM_LIMIT),
        name="router",
    )(x1, g, wr_pad, br_col, upper)


def _dispatch_kernel(dest_ref, pad_ref, hp_ref, xs_hbm, zero_scr, sem_rows):
    i = pl.program_id(0)
    tbl = i * (TOP_K * DISP_T)

    def zero_copy(s):
        return pltpu.make_async_copy(zero_scr.at[0:TOK_SUB, :], xs_hbm.at[_tile_rows(s), :], sem_rows)

    def zero_run(s):
        rows = pl.ds(pl.multiple_of(s * TOK_SUB, TOK_SUB), ZERO_RUN * TOK_SUB)
        return pltpu.make_async_copy(zero_scr, xs_hbm.at[rows, :], sem_rows)

    def zero_pads(act):
        for e in range(N_EXPERTS):
            lo, hi = pad_ref[e], pad_ref[N_EXPERTS + e]
            nrun = (hi - lo) // ZERO_RUN
            lax.fori_loop(0, nrun, lambda r, c: (act(zero_run(lo + r * ZERO_RUN)), c)[1], 0)
            lax.fori_loop(lo + nrun * ZERO_RUN, hi, lambda s, c: (act(zero_copy(s)), c)[1], 0)

    @pl.when(i == 0)
    def _():
        zero_scr[...] = jnp.zeros_like(zero_scr)
        zero_pads(lambda d: d.start())
        zero_pads(lambda d: d.wait())

    def tile_copy(t, k):
        d = dest_ref[tbl + t * TOP_K + k]
        return pltpu.make_async_copy(hp_ref.at[_tile_rows(t), :], xs_hbm.at[_tile_rows(d), :], sem_rows)

    def issue(tb, carry):
        for r in range(DMA_UNROLL):
            for k in range(TOP_K):
                tile_copy(tb * DMA_UNROLL + r, k).start(priority=(r * TOP_K + k) % 2)
        return carry

    lax.fori_loop(0, DISP_T // DMA_UNROLL, issue, 0)

    def drain(tb, carry):
        for _ in range(DMA_UNROLL * TOP_K):
            zero_copy(0).wait()
        return carry

    lax.fori_loop(0, DISP_T // DMA_UNROLL, drain, 0)


def _dispatch(dest_flat, pad_tbl, hp, cap):
    n = hp.shape[0] // TOK_SUB
    grid_spec = pltpu.PrefetchScalarGridSpec(
        num_scalar_prefetch=2,
        grid=(n // DISP_T,),
        in_specs=[pl.BlockSpec((DISP_T * TOK_SUB, LANES), lambda i, d, p: (i, 0))],
        out_specs=pl.BlockSpec(memory_space=pl.ANY),
        scratch_shapes=[pltpu.VMEM((ZERO_RUN * TOK_SUB, LANES), F32), pltpu.SemaphoreType.DMA],
    )
    return pl.pallas_call(
        _dispatch_kernel,
        grid_spec=grid_spec,
        out_shape=jax.ShapeDtypeStruct((cap * TOK_SUB, LANES), F32),
        compiler_params=pltpu.CompilerParams(dimension_semantics=("arbitrary",)),
        name="dispatch",
    )(dest_flat, pad_tbl, hp)


def _expert_kernel(be_ref, nxt_ref, nv_ref, xs_ref, w1_hbm, b1_ref, w2_hbm, b2_ref, ys_ref,
                   w1f_scr, w2f_scr, w1b_scr, w2b_scr, sems):
    b = pl.program_id(0)
    valid = nv_ref[b]
    used = valid > 0
    new_expert = jnp.logical_or(b == 0, be_ref[b] != be_ref[jnp.maximum(b - 1, 0)])

    def fetch(e):
        return (pltpu.make_async_copy(w1_hbm.at[e], w1f_scr, sems.at[0]),
                pltpu.make_async_copy(w2_hbm.at[e], w2f_scr, sems.at[1]))

    @pl.when(jnp.logical_and(used, b == 0))
    def _():
        for c in fetch(be_ref[0]):
            c.start()

    @pl.when(jnp.logical_and(used, new_expert))
    def _():
        for c in fetch(be_ref[b]):
            c.wait()

        def cast(rb, carry):
            rows = pl.ds(pl.multiple_of(rb * CAST_ROWS, CAST_ROWS), CAST_ROWS)
            w1b_scr[rows, :] = w1f_scr[rows, :].astype(BF16)
            w2b_scr[rows, :] = w2f_scr[rows, :].astype(BF16)
            return carry

        lax.fori_loop(0, D_MODEL // CAST_ROWS, cast, 0)

        @pl.when(nxt_ref[b] >= 0)
        def _():
            for c in fetch(nxt_ref[b]):
                c.start()

    def expert_mlp(rows):
        x = _load_token_tiles(xs_ref, rows).astype(BF16)
        hid = jnp.dot(x, w1b_scr[...], preferred_element_type=F32) + b1_ref[0]
        glu = jnp.minimum(hid[:, :D_FF_EXPERT], SWIGLU_LIMIT)
        lin = jnp.clip(hid[:, D_FF_EXPERT:], -SWIGLU_LIMIT, SWIGLU_LIMIT)
        act = glu * _sigmoid(SWIGLU_ALPHA * glu) * (lin + 1.0)
        y = jnp.dot(act.astype(BF16), w2b_scr[...], preferred_element_type=F32) + b2_ref[0]
        _store_token_tiles(ys_ref, y, rows)

    @pl.when(valid > MOE_SUB)
    def _():
        expert_mlp(MOE_BLOCK)

    @pl.when(jnp.logical_and(used, valid <= MOE_SUB))
    def _():
        expert_mlp(MOE_SUB)


def _experts(blk_expert, blk_next, blk_valid, xs, w1, b1, w2, b2):
    cap = xs.shape[0] // TOK_SUB
    n_blocks = cap // MOE_BLOCK
    grid_spec = pltpu.PrefetchScalarGridSpec(
        num_scalar_prefetch=3,
        grid=(n_blocks,),
        in_specs=[
            pl.BlockSpec((MOE_BLOCK * TOK_SUB, LANES), lambda b, be, nx, nv: (b, 0)),
            pl.BlockSpec(memory_space=pl.ANY),
            pl.BlockSpec((1, 1, 2 * D_FF_EXPERT), lambda b, be, nx, nv: (be[b], 0, 0)),
            pl.BlockSpec(memory_space=pl.ANY),
            pl.BlockSpec((1, 1, D_MODEL), lambda b, be, nx, nv: (be[b], 0, 0)),
        ],
        out_specs=pl.BlockSpec((MOE_BLOCK * TOK_SUB, LANES), lambda b, be, nx, nv: (b, 0)),
        scratch_shapes=[pltpu.VMEM((D_MODEL, 2 * D_FF_EXPERT), F32),
                        pltpu.VMEM((D_FF_EXPERT, D_MODEL), F32),
                        pltpu.VMEM((D_MODEL, 2 * D_FF_EXPERT), BF16),
                        pltpu.VMEM((D_FF_EXPERT, D_MODEL), BF16),
                        pltpu.SemaphoreType.DMA((2,))],
    )
    assert D_MODEL == D_FF_EXPERT
    assert MOE_BLOCK == 2 * MOE_SUB
    return pl.pallas_call(
        _expert_kernel,
        grid_spec=grid_spec,
        out_shape=jax.ShapeDtypeStruct((cap * TOK_SUB, LANES), F32),
        compiler_params=pltpu.CompilerParams(
            dimension_semantics=("arbitrary",), vmem_limit_bytes=VMEM_LIMIT),
        name="experts",
    )(blk_expert, blk_next, blk_valid, xs, w1, b1, w2, b2)


def _combine_kernel(dest_ref, ys_hbm, x1_ref, gate_ref, p_ref, pg_ref, wg_ref, wp_ref, fg_ref,
                    out_ref, ybuf, sems, *, final):
    i = pl.program_id(0)
    nsteps = pl.num_programs(0)
    buf_tokens = TOP_K * COMB_T

    def issue_part(step, slot, part, parts):
        tbl = step * buf_tokens
        per_part = COMB_T // DMA_UNROLL // parts

        def issue(tb, carry):
            for r in range(DMA_UNROLL):
                for k in range(TOP_K):
                    t = tb * DMA_UNROLL + r
                    pltpu.make_async_copy(
                        ys_hbm.at[_tile_rows(dest_ref[tbl + t * TOP_K + k]), :],
                        ybuf.at[_tile_rows(slot * buf_tokens + k * COMB_T + t), :],
                        sems.at[slot]).start(priority=(r * TOP_K + k) % 2)
            return carry

        last = jnp.where(step < nsteps, (part + 1) * per_part, part * per_part)
        lax.fori_loop(part * per_part, last, issue, 0)

    def drain_all(slot):
        def drain(tb, carry):
            for _ in range(DMA_UNROLL * TOP_K):
                pltpu.make_async_copy(ys_hbm.at[_tile_rows(0), :],
                                      ybuf.at[_tile_rows(slot * buf_tokens), :], sems.at[slot]).wait()
            return carry

        lax.fori_loop(0, COMB_T // DMA_UNROLL, drain, 0)

    @pl.when(i == 0)
    def _():
        issue_part(0, 0, 0, 1)

    nsub = COMB_T // COMB_SUB
    for slot in range(2):
        @pl.when(i % 2 == slot)
        def _():
            drain_all(slot)

            for sub in range(nsub):
                if sub % (nsub // ISSUE_PARTS) == 0:
                    issue_part(i + 1, 1 - slot, sub // (nsub // ISSUE_PARTS), ISSUE_PARTS)
                rows = slice(sub * COMB_SUB, (sub + 1) * COMB_SUB)
                x2 = x1_ref[rows, :]
                for k in range(TOP_K):
                    x2 = x2 + gate_ref[rows, k:k + 1] * _load_token_tiles(
                        ybuf, COMB_SUB, slot * buf_tokens + k * COMB_T + sub * COMB_SUB)
                hp = _rms(x2, pg_ref[...]).astype(BF16)
                gate = _sigmoid(jnp.dot(hp, wg_ref[...], preferred_element_type=F32))
                emb = jnp.dot(p_ref[rows, :].astype(BF16), wp_ref[...], preferred_element_type=F32)
                x3 = x2 + gate * emb
                if final:
                    x3 = _rms(x3, fg_ref[...])
                out_ref[rows, :] = x3


def _combine(dest_flat, ys, x1, gates_t, p2d, pg, wg, wp, fg, final):
    n = x1.shape[0]
    grid_spec = pltpu.PrefetchScalarGridSpec(
        num_scalar_prefetch=1,
        grid=(n // COMB_T,),
        in_specs=[
            pl.BlockSpec(memory_space=pl.ANY),
            pl.BlockSpec((COMB_T, D_MODEL), lambda i, d: (i, 0)),
            pl.BlockSpec((COMB_T, SUBLANES), lambda i, d: (i, 0)),
            pl.BlockSpec((COMB_T, PLE_DIM), lambda i, d: (i, 0)),
            pl.BlockSpec((1, D_MODEL), lambda i, d: (0, 0)),
            pl.BlockSpec((D_MODEL, D_MODEL), lambda i, d: (0, 0)),
            pl.BlockSpec((PLE_DIM, D_MODEL), lambda i, d: (0, 0)),
            pl.BlockSpec((1, D_MODEL), lambda i, d: (0, 0)),
        ],
        out_specs=pl.BlockSpec((COMB_T, D_MODEL), lambda i, d: (i, 0)),
        scratch_shapes=[pltpu.VMEM((2 * TOP_K * COMB_T * TOK_SUB, LANES), F32),
                        pltpu.SemaphoreType.DMA((2,))],
    )
    return pl.pallas_call(
        functools.partial(_combine_kernel, final=final),
        grid_spec=grid_spec,
        out_shape=jax.ShapeDtypeStruct((n, D_MODEL), F32),
        compiler_params=pltpu.CompilerParams(
            dimension_semantics=("arbitrary",), vmem_limit_bytes=VMEM_LIMIT),
        name="combine",
    )(dest_flat, ys, x1, gates_t, p2d, pg, wg, wp, fg)


def _layer(x2d, p2d, bsz, seq, mix_norm, w_in, sgu_ln_g, sgu_ln_b, sgu_w, sgu_b, conv_w, conv_b,
           dt_bias, a_log, d_skip, ssm_norm, w_branch_a, w_branch_b, w_out, ffn_norm, w_router,
           b_router, w1, b1, w2, b2, ple_norm, w_ple_gate, w_ple_proj, final_norm, final):
    n = x2d.shape[0]
    row = lambda a: a.reshape(1, -1).astype(F32)

    w_main = jnp.concatenate([w_in[:, :OFF_DT], w_in[:, OFF_GA:]], axis=1).astype(BF16)
    w_dt = jnp.pad(w_in[:, OFF_DT:OFF_GA], ((0, 0), (0, LANES - SSM_HEADS))).astype(BF16)
    pos = jnp.arange(SGU_LEN)
    allowed = (pos[None, :] // CHUNK) <= (pos[:, None] // CHUNK)
    wsgu = jnp.where(allowed[None], sgu_w, 0.0).astype(BF16)
    bsgu = jnp.repeat(sgu_b.T, SGU_HEAD_DIM, axis=1).astype(F32)
    dtb = dt_bias.reshape(-1, 1).astype(F32)
    acol = -jnp.exp(a_log.astype(F32)).reshape(-1, 1)
    dskip = jnp.repeat(d_skip.astype(F32), SSM_HEAD_DIM).reshape(1, -1)
    head_of_col = jnp.arange(SSM_INNER) // SSM_HEAD_DIM
    eexp = (jnp.arange(LANES)[:, None] == head_of_col[None, :]).astype(BF16)
    triu = (jnp.arange(SSD_CHUNK)[:, None] <= jnp.arange(SSD_CHUNK)[None, :]).astype(BF16)
    t_out = jnp.arange((SSM_CONV - 1) * SSD_CHUNK)
    src = SSD_CHUNK + t_out % SSD_CHUNK - (SSM_CONV - 1) + t_out // SSD_CHUNK
    shift = (src[:, None] == jnp.arange(2 * SSD_CHUNK)[None, :]).astype(BF16)
    consts = [row(sgu_ln_g), row(sgu_ln_b), wsgu, bsgu, conv_w.astype(F32), row(conv_b), dtb, acol,
              dskip, row(ssm_norm), eexp, shift, triu]

    proj, dt_raw = _in_proj(x2d, row(mix_norm), w_main, w_dt)
    x1 = _mixer(proj, dt_raw, x2d, bsz, seq, consts, w_branch_a.astype(BF16),
                w_branch_b.astype(BF16), w_out.astype(BF16))

    wr_f = jnp.pad(w_router.astype(F32), ((0, 0), (0, LANES - N_EXPERTS)))
    wr_hi = wr_f.astype(BF16)
    wr_pad = jnp.stack([wr_hi, (wr_f - wr_hi.astype(F32)).astype(BF16)])
    upper = (jnp.arange(ROUTE_T)[:, None] < jnp.arange(ROUTE_T)[None, :]).astype(BF16)
    hp, idx, gates, rank, cnt = _router(x1, row(ffn_norm), wr_pad, b_router.reshape(-1, 1).astype(F32),
                                        upper)

    counts = cnt[:, 0]
    padded = (counts + MOE_BLOCK - 1) // MOE_BLOCK * MOE_BLOCK
    pend = jnp.cumsum(padded)
    pstart = pend - padded
    nk = n * TOP_K
    cap = (nk + MOE_BLOCK - 1) // MOE_BLOCK * MOE_BLOCK + N_EXPERTS * MOE_BLOCK
    n_blocks = cap // MOE_BLOCK
    ex = jnp.arange(N_EXPERTS, dtype=I32)
    dest = rank + jnp.sum(jnp.where(idx[..., None] == ex, pstart.astype(I32), 0), axis=-1)
    blk_start = jnp.arange(n_blocks, dtype=I32) * MOE_BLOCK
    blk_expert = jnp.minimum(jnp.sum((pend[None, :] <= blk_start[:, None]).astype(I32), axis=1),
                             N_EXPERTS - 1).astype(I32)
    of_block = blk_expert[:, None] == ex[None, :]
    per_block = lambda v: jnp.sum(jnp.where(of_block, v[None, :].astype(I32), 0), axis=1)
    blk_valid = jnp.where(blk_start < pend[-1],
                          jnp.clip(per_block(pstart + counts) - blk_start, 0, MOE_BLOCK), 0).astype(I32)
    live = jnp.where(counts > 0, ex, N_EXPERTS)
    later = jnp.concatenate([lax.cummin(live[::-1])[::-1][1:], jnp.full((1,), N_EXPERTS, I32)])
    blk_next = per_block(jnp.where(later < N_EXPERTS, later, -1))

    dest_flat = dest.T.reshape(-1)
    used_end = pstart + (counts + MOE_SUB - 1) // MOE_SUB * MOE_SUB
    pad_tbl = jnp.concatenate([pstart + counts, used_end]).astype(I32)

    xs = _dispatch(dest_flat, pad_tbl, hp, cap)
    ys = _experts(blk_expert, blk_next, blk_valid, xs, w1.astype(F32),
                  b1.reshape(N_EXPERTS, 1, -1).astype(F32), w2.astype(F32),
                  b2.reshape(N_EXPERTS, 1, -1).astype(F32))
    return _combine(dest_flat, ys, x1, gates.T, p2d, row(ple_norm), w_ple_gate.astype(BF16),
                    w_ple_proj.astype(BF16), row(final_norm), final)


def kernel(x, p, mix_norm, w_in, sgu_ln_g, sgu_ln_b, sgu_w, sgu_b, conv_w, conv_b, dt_bias, a_log,
           d_skip, ssm_norm, w_branch_a, w_branch_b, w_out, ffn_norm, w_router, b_router, w1, b1,
           w2, b2, ple_norm, w_ple_gate, w_ple_proj, final_norm):
    bsz, seq, d = x.shape
    depth = w_in.shape[0]
    assert d == D_MODEL and seq % MIX_T == 0 and (bsz * seq) % max(IN_TM, ROUTE_T, DISP_T) == 0
    assert DISP_T == COMB_T
    x2d = x.reshape(bsz * seq, d)
    for i in range(depth):
        x2d = _layer(x2d, p[i].reshape(bsz * seq, PLE_DIM), bsz, seq, mix_norm[i], w_in[i],
                     sgu_ln_g[i], sgu_ln_b[i], sgu_w[i], sgu_b[i], conv_w[i], conv_b[i], dt_bias[i],
                     a_log[i], d_skip[i], ssm_norm[i], w_branch_a[i], w_branch_b[i], w_out[i],
                     ffn_norm[i], w_router[i], b_router[i], w1[i], b1[i], w2[i], b2[i], ple_norm[i],
                     w_ple_gate[i], w_ple_proj[i], final_norm, final=(i == depth - 1))
    return x2d.reshape(bsz, seq, d)
```

```python
import functools
import math

import jax
import jax.numpy as jnp
from jax import lax
from jax.experimental import pallas as pl
from jax.experimental.pallas import tpu as pltpu

F32 = jnp.float32
BF16 = jnp.bfloat16
I32 = jnp.int32

D_MODEL = 1024
CHUNK = 64
PLE_DIM = 256
RMS_EPS = 1e-6
LN_EPS = 1e-5

SGU_HEADS = 8
SGU_HEAD_DIM = 128
SGU_WIDTH = SGU_HEADS * SGU_HEAD_DIM
SGU_LEN = 128

SSM_HEADS = 16
SSM_HEAD_DIM = 64
SSM_INNER = SSM_HEADS * SSM_HEAD_DIM
SSM_GROUPS = 4
SSM_STATE = 128
SSM_CONV = 4
SSD_CHUNK = 128
SSM_CONV_DIM = SSM_INNER + 2 * SSM_GROUPS * SSM_STATE

N_EXPERTS = 32
TOP_K = 4
D_FF_EXPERT = 1024
SWIGLU_LIMIT = 7.0
SWIGLU_ALPHA = 1.702
MOE_BLOCK = 512
MOE_SUB = 256

OFF_U = 0
OFF_V = OFF_U + SGU_WIDTH
OFF_Z = OFF_V + SGU_WIDTH
OFF_XBC = OFF_Z + SSM_INNER
OFF_DT = OFF_XBC + SSM_CONV_DIM
OFF_GA = OFF_DT + SSM_HEADS
OFF_GB = OFF_GA + D_MODEL
IN_PROJ_DIM = OFF_GB + D_MODEL

LANES = 128
SUBLANES = 8
PROJ_MAIN = IN_PROJ_DIM - SSM_HEADS
PROJ_TILE = 1024
TOK_SUB = D_MODEL // LANES

VMEM_LIMIT = 56 * 1024 * 1024

IN_TM = 512
IN_TN = 1792
MIX_T = 512
ROUTE_T = 512
DISP_T = 512
COMB_T = DISP_T
COMB_SUB = 128
ISSUE_PARTS = 2
DMA_UNROLL = 8
ZERO_RUN = 64
CAST_ROWS = 128


def _sigmoid(x):
    return 1.0 / (1.0 + jnp.exp(-x))


def _gelu_exact(x):
    return 0.5 * x * (1.0 + lax.erf(x * (1.0 / math.sqrt(2.0))))


def _softplus(x):
    return jnp.maximum(x, 0.0) + jnp.log1p(jnp.exp(-jnp.abs(x)))


def _rms(x, g):
    ms = jnp.mean(x * x, axis=-1, keepdims=True)
    return x * lax.rsqrt(ms + RMS_EPS) * g


def _store_token_tiles(ref, val, rows, start=0):
    for j in range(TOK_SUB):
        ref[pl.ds(start * TOK_SUB + j, rows, stride=TOK_SUB), :] = val[:, j * LANES:(j + 1) * LANES]


def _load_token_tiles(ref, rows, start=0):
    return jnp.concatenate(
        [ref[pl.ds(start * TOK_SUB + j, rows, stride=TOK_SUB), :] for j in range(TOK_SUB)], axis=1)


def _tile_rows(idx):
    return pl.ds(pl.multiple_of(idx * TOK_SUB, TOK_SUB), TOK_SUB)


def _inproj_kernel(x_ref, g_ref, w_ref, wdt_ref, proj_ref, dt_ref, h_scr):
    @pl.when(pl.program_id(0) == 0)
    def _():
        h_scr[...] = jnp.zeros_like(h_scr)

    h = h_scr[...]
    dt_ref[...] = jnp.dot(h, wdt_ref[...], preferred_element_type=F32)
    for c in range(PROJ_MAIN // IN_TN):
        cols = slice(c * IN_TN, (c + 1) * IN_TN)
        proj_ref[:, cols] = jnp.dot(h, w_ref[:, cols], preferred_element_type=F32).astype(BF16)
    h_scr[...] = _rms(x_ref[...], g_ref[...]).astype(BF16)


def _in_proj(x2d, g, w_main, w_dt):
    n = x2d.shape[0]
    ntiles = n // IN_TM
    return pl.pallas_call(
        _inproj_kernel,
        grid=(ntiles + 1,),
        in_specs=[
            pl.BlockSpec((IN_TM, D_MODEL), lambda s: (jnp.minimum(s, ntiles - 1), 0)),
            pl.BlockSpec((1, D_MODEL), lambda s: (0, 0)),
            pl.BlockSpec((D_MODEL, PROJ_MAIN), lambda s: (0, 0)),
            pl.BlockSpec((D_MODEL, LANES), lambda s: (0, 0)),
        ],
        out_specs=[
            pl.BlockSpec((IN_TM, PROJ_MAIN), lambda s: (jnp.maximum(s - 1, 0), 0)),
            pl.BlockSpec((IN_TM, LANES), lambda s: (jnp.maximum(s - 1, 0), 0)),
        ],
        out_shape=[
            jax.ShapeDtypeStruct((n, PROJ_MAIN), BF16),
            jax.ShapeDtypeStruct((n, LANES), F32),
        ],
        scratch_shapes=[pltpu.VMEM((IN_TM, D_MODEL), BF16)],
        compiler_params=pltpu.CompilerParams(
            dimension_semantics=("arbitrary",), vmem_limit_bytes=VMEM_LIMIT),
        name="in_proj",
    )(x2d, g, w_main, w_dt)


def _split3(a):
    a1 = a.astype(BF16)
    r1 = a - a1.astype(F32)
    a2 = r1.astype(BF16)
    a3 = (r1 - a2.astype(F32)).astype(BF16)
    return a1, a2, a3


def _dot3(parts, w):
    out = jnp.dot(parts[0], w, preferred_element_type=F32)
    for p in parts[1:]:
        out = out + jnp.dot(p, w, preferred_element_type=F32)
    return out


def _mixer_kernel(u_ref, v_ref, z_ref, xb0_ref, xb1_ref, dt_ref, ga_ref, gb_ref, x_ref,
                  lng_ref, lnb_ref, wsgu_ref, bsgu_ref, convw_ref, convb_ref, dtb_ref, acol_ref,
                  dskip_ref, normw_ref, eexp_ref, shift_ref, triu_ref, wa_ref, wb_ref, wo_ref,
                  x1_ref,
                  tail_scr, xc_scr, state_scr, ya_scr, yb_scr, yap_scr, ybp_scr, *, tps):
    nchunk = MIX_T // SSD_CHUNK
    s = pl.program_id(0)

    @pl.when(s % tps == 0)
    def _():
        tail_scr[...] = jnp.zeros_like(tail_scr)
        state_scr[...] = jnp.zeros_like(state_scr)

    @pl.when(s == 0)
    def _():
        yap_scr[...] = jnp.zeros_like(yap_scr)
        ybp_scr[...] = jnp.zeros_like(ybp_scr)

    xb_refs = (xb0_ref, xb1_ref)
    for c in range(nchunk):
        r0 = c * SSD_CHUNK
        for hf, xb_ref in enumerate(xb_refs):
            cols = slice(hf * PROJ_TILE, (hf + 1) * PROJ_TILE)
            cur = xb_ref[r0:r0 + SSD_CHUNK, :]
            prev = tail_scr[:, cols] if c == 0 else xb_ref[r0 - SSD_CHUNK:r0, :]
            sh = jnp.dot(shift_ref[...], jnp.concatenate([prev, cur], axis=0), preferred_element_type=F32)
            acc = convb_ref[:, cols] + convw_ref[SSM_CONV - 1:SSM_CONV, cols] * cur.astype(F32)
            for j in range(SSM_CONV - 1):
                acc = acc + convw_ref[j:j + 1, cols] * sh[j * SSD_CHUNK:(j + 1) * SSD_CHUNK, :]
            xc_scr[r0:r0 + SSD_CHUNK, cols] = acc * _sigmoid(acc)
    for hf, xb_ref in enumerate(xb_refs):
        tail_scr[:, hf * PROJ_TILE:(hf + 1) * PROJ_TILE] = xb_ref[MIX_T - SSD_CHUNK:MIX_T, :]

    row_i = lax.broadcasted_iota(I32, (SSD_CHUNK, SSD_CHUNK), 0)
    col_i = lax.broadcasted_iota(I32, (SSD_CHUNK, SSD_CHUNK), 1)
    tril = row_i >= col_i
    lane_hi = col_i >= SSM_HEAD_DIM
    eexp = eexp_ref[...]
    triu = triu_ref[...]

    for c in range(nchunk):
        rows = slice(c * SSD_CHUNK, (c + 1) * SSD_CHUNK)

        ug = _gelu_exact(u_ref[rows, :].astype(F32))
        vg = _gelu_exact(v_ref[rows, :].astype(F32))
        mu = jnp.mean(vg, axis=-1, keepdims=True)
        vc = vg - mu
        var = jnp.mean(vc * vc, axis=-1, keepdims=True)
        vn = (vc * lax.rsqrt(var + LN_EPS) * lng_ref[...] + lnb_ref[...]).astype(BF16)
        for g in range(SGU_HEADS):
            cols = slice(g * SGU_HEAD_DIM, (g + 1) * SGU_HEAD_DIM)
            mixed = jnp.dot(wsgu_ref[g], vn[:, cols], preferred_element_type=F32)
            ya_scr[rows, cols] = (ug[:, cols] * (mixed + bsgu_ref[:, cols])).astype(BF16)

        dt_t = _softplus(dt_ref[rows, :].T[0:SSM_HEADS, :] + dtb_ref[...])
        acs_t = _dot3(_split3(dt_t * acol_ref[...]), triu)
        w_t = jnp.exp(acs_t[:, SSD_CHUNK - 1:SSD_CHUNK] - acs_t) * dt_t
        acs = jnp.concatenate([acs_t, jnp.zeros((LANES - SSM_HEADS, SSD_CHUNK), F32)], axis=0).T
        aend = jnp.broadcast_to(acs[SSD_CHUNK - 1:SSD_CHUNK, :], (SUBLANES, LANES))
        dec = jnp.exp(_dot3(_split3(aend), eexp)[0:1, :])

        for gi in range(SSM_GROUPS):
            bg = xc_scr[rows, SSM_INNER + gi * SSM_STATE:SSM_INNER + (gi + 1) * SSM_STATE]
            cg = xc_scr[rows, SSM_INNER + (SSM_GROUPS + gi) * SSM_STATE:
                        SSM_INNER + (SSM_GROUPS + gi + 1) * SSM_STATE]
            bg_t = bg.T
            cb = jnp.dot(cg.astype(BF16), bg_t.astype(BF16), preferred_element_type=F32)
            for pr in range(2):
                lb = gi * 2 + pr
                cols = slice(lb * LANES, (lb + 1) * LANES)
                xpair = xc_scr[rows, cols]
                hpair = state_scr[:, cols]
                acc = jnp.zeros((SSD_CHUNK, LANES), F32)
                st = jnp.zeros((SSM_STATE, LANES), F32)
                for hh in range(2):
                    h = lb * 2 + hh
                    lmask = lane_hi if hh == 1 else jnp.logical_not(lane_hi)
                    xm = jnp.where(lmask, xpair, 0.0).astype(BF16)
                    hm = jnp.where(lmask, hpair, 0.0).astype(BF16)
                    colb = acs[:, h:h + 1]
                    rowb = acs_t[h:h + 1, :]
                    decay = jnp.exp(jnp.where(tril, colb - rowb, -jnp.inf))
                    m_h = (cb * decay * dt_t[h:h + 1, :]).astype(BF16)
                    c_h = (cg * jnp.exp(colb)).astype(BF16)
                    acc = acc + jnp.dot(m_h, xm, preferred_element_type=F32)
                    acc = acc + jnp.dot(c_h, hm, preferred_element_type=F32)
                    btw = (bg_t * w_t[h:h + 1, :]).astype(BF16)
                    st = st + jnp.dot(btw, xm, preferred_element_type=F32)
                state_scr[:, cols] = hpair * dec[:, cols] + st
                y = acc + dskip_ref[:, cols] * xpair
                zz = z_ref[rows, cols].astype(F32)
                xc_scr[rows, cols] = y * (zz * _sigmoid(zz))
            gcols = slice(gi * 2 * LANES, (gi + 1) * 2 * LANES)
            yg = xc_scr[rows, gcols]
            ms = jnp.mean(yg * yg, axis=-1, keepdims=True)
            yb_scr[rows, gcols] = (yg * lax.rsqrt(ms + RMS_EPS) * normw_ref[:, gcols]).astype(BF16)

    ma = jnp.dot(yap_scr[...], wa_ref[...], preferred_element_type=F32)
    mb = jnp.dot(ybp_scr[...], wb_ref[...], preferred_element_type=F32)
    merged = (_sigmoid(ga_ref[...].astype(F32)) * ma + _sigmoid(gb_ref[...].astype(F32)) * mb)
    x1_ref[...] = x_ref[...] + jnp.dot(merged.astype(BF16), wo_ref[...], preferred_element_type=F32)
    yap_scr[...] = ya_scr[...]
    ybp_scr[...] = yb_scr[...]


def _mixer(proj, dt_raw, x2d, bsz, seq, consts, wa, wb, wo):
    n = x2d.shape[0]
    tps = seq // MIX_T
    ntiles = bsz * tps
    cur = lambda s: jnp.minimum(s, ntiles - 1)
    prev = lambda s: jnp.maximum(s - 1, 0)

    def pcol(off, tile):
        return pl.BlockSpec((MIX_T, PROJ_TILE), lambda s: (tile(s), off // PROJ_TILE))

    def full(a):
        nd = a.ndim
        return pl.BlockSpec(a.shape, lambda s, nd=nd: (0,) * nd)

    in_specs = [pcol(OFF_U, cur), pcol(OFF_V, cur), pcol(OFF_Z, cur),
                pcol(OFF_XBC, cur), pcol(OFF_XBC + PROJ_TILE, cur),
                pl.BlockSpec((MIX_T, LANES), lambda s: (cur(s), 0)),
                pcol(OFF_GA - SSM_HEADS, prev), pcol(OFF_GB - SSM_HEADS, prev),
                pl.BlockSpec((MIX_T, D_MODEL), lambda s: (prev(s), 0))]
    in_specs += [full(a) for a in consts] + [full(wa), full(wb), full(wo)]
    return pl.pallas_call(
        functools.partial(_mixer_kernel, tps=tps),
        grid=(ntiles + 1,),
        in_specs=in_specs,
        out_specs=pl.BlockSpec((MIX_T, D_MODEL), lambda s: (prev(s), 0)),
        out_shape=jax.ShapeDtypeStruct((n, D_MODEL), F32),
        scratch_shapes=[
            pltpu.VMEM((SSD_CHUNK, SSM_CONV_DIM), BF16),
            pltpu.VMEM((MIX_T, SSM_CONV_DIM), F32),
            pltpu.VMEM((SSM_STATE, SSM_INNER), F32),
            pltpu.VMEM((MIX_T, SGU_WIDTH), BF16),
            pltpu.VMEM((MIX_T, SSM_INNER), BF16),
            pltpu.VMEM((MIX_T, SGU_WIDTH), BF16),
            pltpu.VMEM((MIX_T, SSM_INNER), BF16),
        ],
        compiler_params=pltpu.CompilerParams(
            dimension_semantics=("arbitrary",), vmem_limit_bytes=VMEM_LIMIT),
        name="mixer",
    )(proj, proj, proj, proj, proj, dt_raw, proj, proj, x2d, *consts, wa, wb, wo)


def _router_kernel(x_ref, g_ref, wr_ref, br_ref, upper_ref, hp_ref, idx_ref, gate_ref, rank_ref, cnt_ref,
                   carry_scr):
    @pl.when(pl.program_id(0) == 0)
    def _():
        carry_scr[...] = jnp.zeros_like(carry_scr)

    h = _rms(x_ref[...], g_ref[...])
    _store_token_tiles(hp_ref, h, ROUTE_T)

    h_hi = h.astype(BF16)
    h_lo = (h - h_hi.astype(F32)).astype(BF16)
    logits = (jnp.dot(h_hi, wr_ref[0], preferred_element_type=F32)
              + jnp.dot(h_hi, wr_ref[1], preferred_element_type=F32)
              + jnp.dot(h_lo, wr_ref[0], preferred_element_type=F32))
    lt = logits.T[0:N_EXPERTS, :] + br_ref[...]
    eidx = lax.broadcasted_iota(I32, (N_EXPERTS, ROUTE_T), 0).astype(F32)
    vals = lt
    sel_any = jnp.zeros((N_EXPERTS, ROUTE_T), F32)
    sels, tops = [], []
    for k in range(TOP_K):
        m = jnp.max(vals, axis=0, keepdims=True)
        first = jnp.min(jnp.where(vals == m, eidx, float(N_EXPERTS)), axis=0, keepdims=True)
        sel = eidx == first
        vals = jnp.where(sel, -jnp.inf, vals)
        sel_f = sel.astype(F32)
        sel_any = sel_any + sel_f
        sels.append(sel_f)
        tops.append(m)
        idx_ref[k:k + 1, :] = first.astype(I32)
    es = [jnp.exp(t - tops[0]) for t in tops]
    denom = functools.reduce(lambda a, b: a + b, es)
    for k in range(TOP_K):
        gate_ref[k:k + 1, :] = es[k] / denom
    gate_ref[TOP_K:, :] = jnp.zeros((SUBLANES - TOP_K, ROUTE_T), F32)

    excl = jnp.dot(sel_any.astype(BF16), upper_ref[...], preferred_element_type=F32) + carry_scr[:, 0:1]
    for k in range(TOP_K):
        rank_ref[k:k + 1, :] = jnp.sum(sels[k] * excl, axis=0, keepdims=True).astype(I32)
    new_carry = carry_scr[...] + jnp.sum(sel_any, axis=1, keepdims=True)
    carry_scr[...] = new_carry
    cnt_ref[...] = new_carry.astype(I32)


def _router(x1, g, wr_pad, br_col, upper):
    n = x1.shape[0]
    per_k = pl.BlockSpec((TOP_K, ROUTE_T), lambda i: (0, i))
    return pl.pallas_call(
        _router_kernel,
        grid=(n // ROUTE_T,),
        in_specs=[
            pl.BlockSpec((ROUTE_T, D_MODEL), lambda i: (i, 0)),
            pl.BlockSpec((1, D_MODEL), lambda i: (0, 0)),
            pl.BlockSpec((2, D_MODEL, LANES), lambda i: (0, 0, 0)),
            pl.BlockSpec((N_EXPERTS, 1), lambda i: (0, 0)),
            pl.BlockSpec((ROUTE_T, ROUTE_T), lambda i: (0, 0)),
        ],
        out_specs=[
            pl.BlockSpec((ROUTE_T * TOK_SUB, LANES), lambda i: (i, 0)),
            per_k,
            pl.BlockSpec((SUBLANES, ROUTE_T), lambda i: (0, i)),
            per_k,
            pl.BlockSpec((N_EXPERTS, LANES), lambda i: (0, 0)),
        ],
        out_shape=[
            jax.ShapeDtypeStruct((n * TOK_SUB, LANES), F32),
            jax.ShapeDtypeStruct((TOP_K, n), I32),
            jax.ShapeDtypeStruct((SUBLANES, n), F32),
            jax.ShapeDtypeStruct((TOP_K, n), I32),
            jax.ShapeDtypeStruct((N_EXPERTS, LANES), I32),
        ],
        scratch_shapes=[pltpu.VMEM((N_EXPERTS, LANES), F32)],
        compiler_params=pltpu.CompilerParams(
            dimension_semantics=("arbitrary",), vmem_limit_bytes=VMEM_LIMIT),
        name="router",
    )(x1, g, wr_pad, br_col, upper)


def _dispatch_kernel(dest_ref, pad_ref, hp_ref, xs_hbm, zero_scr, sem_rows):
    i = pl.program_id(0)
    tbl = i * (TOP_K * DISP_T)

    def zero_copy(s):
        return pltpu.make_async_copy(zero_scr.at[0:TOK_SUB, :], xs_hbm.at[_tile_rows(s), :], sem_rows)

    def zero_run(s):
        rows = pl.ds(pl.multiple_of(s * TOK_SUB, TOK_SUB), ZERO_RUN * TOK_SUB)
        return pltpu.make_async_copy(zero_scr, xs_hbm.at[rows, :], sem_rows)

    def zero_pads(act):
        for e in range(N_EXPERTS):
            lo, hi = pad_ref[e], pad_ref[N_EXPERTS + e]
            nrun = (hi - lo) // ZERO_RUN
            lax.fori_loop(0, nrun, lambda r, c: (act(zero_run(lo + r * ZERO_RUN)), c)[1], 0)
            lax.fori_loop(lo + nrun * ZERO_RUN, hi, lambda s, c: (act(zero_copy(s)), c)[1], 0)

    @pl.when(i == 0)
    def _():
        zero_scr[...] = jnp.zeros_like(zero_scr)
        zero_pads(lambda d: d.start())
        zero_pads(lambda d: d.wait())

    def tile_copy(t, k):
        d = dest_ref[tbl + t * TOP_K + k]
        return pltpu.make_async_copy(hp_ref.at[_tile_rows(t), :], xs_hbm.at[_tile_rows(d), :], sem_rows)

    def issue(tb, carry):
        for r in range(DMA_UNROLL):
            for k in range(TOP_K):
                tile_copy(tb * DMA_UNROLL + r, k).start(priority=(r * TOP_K + k) % 2)
        return carry

    lax.fori_loop(0, DISP_T // DMA_UNROLL, issue, 0)

    def drain(tb, carry):
        for _ in range(DMA_UNROLL * TOP_K):
            zero_copy(0).wait()
        return carry

    lax.fori_loop(0, DISP_T // DMA_UNROLL, drain, 0)


def _dispatch(dest_flat, pad_tbl, hp, cap):
    n = hp.shape[0] // TOK_SUB
    grid_spec = pltpu.PrefetchScalarGridSpec(
        num_scalar_prefetch=2,
        grid=(n // DISP_T,),
        in_specs=[pl.BlockSpec((DISP_T * TOK_SUB, LANES), lambda i, d, p: (i, 0))],
        out_specs=pl.BlockSpec(memory_space=pl.ANY),
        scratch_shapes=[pltpu.VMEM((ZERO_RUN * TOK_SUB, LANES), F32), pltpu.SemaphoreType.DMA],
    )
    return pl.pallas_call(
        _dispatch_kernel,
        grid_spec=grid_spec,
        out_shape=jax.ShapeDtypeStruct((cap * TOK_SUB, LANES), F32),
        compiler_params=pltpu.CompilerParams(dimension_semantics=("arbitrary",)),
        name="dispatch",
    )(dest_flat, pad_tbl, hp)


def _expert_kernel(be_ref, nxt_ref, nv_ref, xs_ref, w1_hbm, b1_ref, w2_hbm, b2_ref, ys_ref,
                   w1f_scr, w2f_scr, w1b_scr, w2b_scr, sems):
    b = pl.program_id(0)
    valid = nv_ref[b]
    used = valid > 0
    new_expert = jnp.logical_or(b == 0, be_ref[b] != be_ref[jnp.maximum(b - 1, 0)])

    def fetch(e):
        return (pltpu.make_async_copy(w1_hbm.at[e], w1f_scr, sems.at[0]),
                pltpu.make_async_copy(w2_hbm.at[e], w2f_scr, sems.at[1]))

    @pl.when(jnp.logical_and(used, b == 0))
    def _():
        for c in fetch(be_ref[0]):
            c.start()

    @pl.when(jnp.logical_and(used, new_expert))
    def _():
        for c in fetch(be_ref[b]):
            c.wait()

        def cast(rb, carry):
            rows = pl.ds(pl.multiple_of(rb * CAST_ROWS, CAST_ROWS), CAST_ROWS)
            w1b_scr[rows, :] = w1f_scr[rows, :].astype(BF16)
            w2b_scr[rows, :] = w2f_scr[rows, :].astype(BF16)
            return carry

        lax.fori_loop(0, D_MODEL // CAST_ROWS, cast, 0)

        @pl.when(nxt_ref[b] >= 0)
        def _():
            for c in fetch(nxt_ref[b]):
                c.start()

    def expert_mlp(rows):
        x = _load_token_tiles(xs_ref, rows).astype(BF16)
        hid = jnp.dot(x, w1b_scr[...], preferred_element_type=F32) + b1_ref[0]
        glu = jnp.minimum(hid[:, :D_FF_EXPERT], SWIGLU_LIMIT)
        lin = jnp.clip(hid[:, D_FF_EXPERT:], -SWIGLU_LIMIT, SWIGLU_LIMIT)
        act = glu * _sigmoid(SWIGLU_ALPHA * glu) * (lin + 1.0)
        y = jnp.dot(act.astype(BF16), w2b_scr[...], preferred_element_type=F32) + b2_ref[0]
        _store_token_tiles(ys_ref, y, rows)

    @pl.when(valid > MOE_SUB)
    def _():
        expert_mlp(MOE_BLOCK)

    @pl.when(jnp.logical_and(used, valid <= MOE_SUB))
    def _():
        expert_mlp(MOE_SUB)


def _experts(blk_expert, blk_next, blk_valid, xs, w1, b1, w2, b2):
    cap = xs.shape[0] // TOK_SUB
    n_blocks = cap // MOE_BLOCK
    grid_spec = pltpu.PrefetchScalarGridSpec(
        num_scalar_prefetch=3,
        grid=(n_blocks,),
        in_specs=[
            pl.BlockSpec((MOE_BLOCK * TOK_SUB, LANES), lambda b, be, nx, nv: (b, 0)),
            pl.BlockSpec(memory_space=pl.ANY),
            pl.BlockSpec((1, 1, 2 * D_FF_EXPERT), lambda b, be, nx, nv: (be[b], 0, 0)),
            pl.BlockSpec(memory_space=pl.ANY),
            pl.BlockSpec((1, 1, D_MODEL), lambda b, be, nx, nv: (be[b], 0, 0)),
        ],
        out_specs=pl.BlockSpec((MOE_BLOCK * TOK_SUB, LANES), lambda b, be, nx, nv: (b, 0)),
        scratch_shapes=[pltpu.VMEM((D_MODEL, 2 * D_FF_EXPERT), F32),
                        pltpu.VMEM((D_FF_EXPERT, D_MODEL), F32),
                        pltpu.VMEM((D_MODEL, 2 * D_FF_EXPERT), BF16),
                        pltpu.VMEM((D_FF_EXPERT, D_MODEL), BF16),
                        pltpu.SemaphoreType.DMA((2,))],
    )
    assert D_MODEL == D_FF_EXPERT
    assert MOE_BLOCK == 2 * MOE_SUB
    return pl.pallas_call(
        _expert_kernel,
        grid_spec=grid_spec,
        out_shape=jax.ShapeDtypeStruct((cap * TOK_SUB, LANES), F32),
        compiler_params=pltpu.CompilerParams(
            dimension_semantics=("arbitrary",), vmem_limit_bytes=VMEM_LIMIT),
        name="experts",
    )(blk_expert, blk_next, blk_valid, xs, w1, b1, w2, b2)


def _combine_kernel(dest_ref, ys_hbm, x1_ref, gate_ref, p_ref, pg_ref, wg_ref, wp_ref, fg_ref,
                    out_ref, ybuf, sems, *, final):
    i = pl.program_id(0)
    nsteps = pl.num_programs(0)
    buf_tokens = TOP_K * COMB_T

    def issue_part(step, slot, part, parts):
        tbl = step * buf_tokens
        per_part = COMB_T // DMA_UNROLL // parts

        def issue(tb, carry):
            for r in range(DMA_UNROLL):
                for k in range(TOP_K):
                    t = tb * DMA_UNROLL + r
                    pltpu.make_async_copy(
                        ys_hbm.at[_tile_rows(dest_ref[tbl + t * TOP_K + k]), :],
                        ybuf.at[_tile_rows(slot * buf_tokens + k * COMB_T + t), :],
                        sems.at[slot]).start(priority=(r * TOP_K + k) % 2)
            return carry

        last = jnp.where(step < nsteps, (part + 1) * per_part, part * per_part)
        lax.fori_loop(part * per_part, last, issue, 0)

    def drain_all(slot):
        def drain(tb, carry):
            for _ in range(DMA_UNROLL * TOP_K):
                pltpu.make_async_copy(ys_hbm.at[_tile_rows(0), :],
                                      ybuf.at[_tile_rows(slot * buf_tokens), :], sems.at[slot]).wait()
            return carry

        lax.fori_loop(0, COMB_T // DMA_UNROLL, drain, 0)

    @pl.when(i == 0)
    def _():
        issue_part(0, 0, 0, 1)

    nsub = COMB_T // COMB_SUB
    for slot in range(2):
        @pl.when(i % 2 == slot)
        def _():
            drain_all(slot)

            for sub in range(nsub):
                if sub < ISSUE_PARTS:
                    issue_part(i + 1, 1 - slot, sub, ISSUE_PARTS)
                rows = slice(sub * COMB_SUB, (sub + 1) * COMB_SUB)
                x2 = x1_ref[rows, :]
                for k in range(TOP_K):
                    x2 = x2 + gate_ref[rows, k:k + 1] * _load_token_tiles(
                        ybuf, COMB_SUB, slot * buf_tokens + k * COMB_T + sub * COMB_SUB)
                hp = _rms(x2, pg_ref[...]).astype(BF16)
                gate = _sigmoid(jnp.dot(hp, wg_ref[...], preferred_element_type=F32))
                emb = jnp.dot(p_ref[rows, :].astype(BF16), wp_ref[...], preferred_element_type=F32)
                x3 = x2 + gate * emb
                if final:
                    x3 = _rms(x3, fg_ref[...])
                out_ref[rows, :] = x3


def _combine(dest_flat, ys, x1, gates_t, p2d, pg, wg, wp, fg, final):
    n = x1.shape[0]
    grid_spec = pltpu.PrefetchScalarGridSpec(
        num_scalar_prefetch=1,
        grid=(n // COMB_T,),
        in_specs=[
            pl.BlockSpec(memory_space=pl.ANY),
            pl.BlockSpec((COMB_T, D_MODEL), lambda i, d: (i, 0)),
            pl.BlockSpec((COMB_T, SUBLANES), lambda i, d: (i, 0)),
            pl.BlockSpec((COMB_T, PLE_DIM), lambda i, d: (i, 0)),
            pl.BlockSpec((1, D_MODEL), lambda i, d: (0, 0)),
            pl.BlockSpec((D_MODEL, D_MODEL), lambda i, d: (0, 0)),
            pl.BlockSpec((PLE_DIM, D_MODEL), lambda i, d: (0, 0)),
            pl.BlockSpec((1, D_MODEL), lambda i, d: (0, 0)),
        ],
        out_specs=pl.BlockSpec((COMB_T, D_MODEL), lambda i, d: (i, 0)),
        scratch_shapes=[pltpu.VMEM((2 * TOP_K * COMB_T * TOK_SUB, LANES), F32),
                        pltpu.SemaphoreType.DMA((2,))],
    )
    return pl.pallas_call(
        functools.partial(_combine_kernel, final=final),
        grid_spec=grid_spec,
        out_shape=jax.ShapeDtypeStruct((n, D_MODEL), F32),
        compiler_params=pltpu.CompilerParams(
            dimension_semantics=("arbitrary",), vmem_limit_bytes=VMEM_LIMIT),
        name="combine",
    )(dest_flat, ys, x1, gates_t, p2d, pg, wg, wp, fg)


def _layer(x2d, p2d, bsz, seq, mix_norm, w_in, sgu_ln_g, sgu_ln_b, sgu_w, sgu_b, conv_w, conv_b,
           dt_bias, a_log, d_skip, ssm_norm, w_branch_a, w_branch_b, w_out, ffn_norm, w_router,
           b_router, w1, b1, w2, b2, ple_norm, w_ple_gate, w_ple_proj, final_norm, final):
    n = x2d.shape[0]
    row = lambda a: a.reshape(1, -1).astype(F32)

    w_main = jnp.concatenate([w_in[:, :OFF_DT], w_in[:, OFF_GA:]], axis=1).astype(BF16)
    w_dt = jnp.pad(w_in[:, OFF_DT:OFF_GA], ((0, 0), (0, LANES - SSM_HEADS))).astype(BF16)
    pos = jnp.arange(SGU_LEN)
    allowed = (pos[None, :] // CHUNK) <= (pos[:, None] // CHUNK)
    wsgu = jnp.where(allowed[None], sgu_w, 0.0).astype(BF16)
    bsgu = jnp.repeat(sgu_b.T, SGU_HEAD_DIM, axis=1).astype(F32)
    dtb = dt_bias.reshape(-1, 1).astype(F32)
    acol = -jnp.exp(a_log.astype(F32)).reshape(-1, 1)
    dskip = jnp.repeat(d_skip.astype(F32), SSM_HEAD_DIM).reshape(1, -1)
    head_of_col = jnp.arange(SSM_INNER) // SSM_HEAD_DIM
    eexp = (jnp.arange(LANES)[:, None] == head_of_col[None, :]).astype(BF16)
    triu = (jnp.arange(SSD_CHUNK)[:, None] <= jnp.arange(SSD_CHUNK)[None, :]).astype(BF16)
    t_out = jnp.arange((SSM_CONV - 1) * SSD_CHUNK)
    src = SSD_CHUNK + t_out % SSD_CHUNK - (SSM_CONV - 1) + t_out // SSD_CHUNK
    shift = (src[:, None] == jnp.arange(2 * SSD_CHUNK)[None, :]).astype(BF16)
    consts = [row(sgu_ln_g), row(sgu_ln_b), wsgu, bsgu, conv_w.astype(F32), row(conv_b), dtb, acol,
              dskip, row(ssm_norm), eexp, shift, triu]

    proj, dt_raw = _in_proj(x2d, row(mix_norm), w_main, w_dt)
    x1 = _mixer(proj, dt_raw, x2d, bsz, seq, consts, w_branch_a.astype(BF16),
                w_branch_b.astype(BF16), w_out.astype(BF16))

    wr_f = jnp.pad(w_router.astype(F32), ((0, 0), (0, LANES - N_EXPERTS)))
    wr_hi = wr_f.astype(BF16)
    wr_pad = jnp.stack([wr_hi, (wr_f - wr_hi.astype(F32)).astype(BF16)])
    upper = (jnp.arange(ROUTE_T)[:, None] < jnp.arange(ROUTE_T)[None, :]).astype(BF16)
    hp, idx, gates, rank, cnt = _router(x1, row(ffn_norm), wr_pad, b_router.reshape(-1, 1).astype(F32),
                                        upper)

    counts = cnt[:, 0]
    padded = (counts + MOE_BLOCK - 1) // MOE_BLOCK * MOE_BLOCK
    pend = jnp.cumsum(padded)
    pstart = pend - padded
    nk = n * TOP_K
    cap = (nk + MOE_BLOCK - 1) // MOE_BLOCK * MOE_BLOCK + N_EXPERTS * MOE_BLOCK
    n_blocks = cap // MOE_BLOCK
    ex = jnp.arange(N_EXPERTS, dtype=I32)
    dest = rank + jnp.sum(jnp.where(idx[..., None] == ex, pstart.astype(I32), 0), axis=-1)
    blk_start = jnp.arange(n_blocks, dtype=I32) * MOE_BLOCK
    blk_expert = jnp.minimum(jnp.sum((pend[None, :] <= blk_start[:, None]).astype(I32), axis=1),
                             N_EXPERTS - 1).astype(I32)
    of_block = blk_expert[:, None] == ex[None, :]
    per_block = lambda v: jnp.sum(jnp.where(of_block, v[None, :].astype(I32), 0), axis=1)
    blk_valid = jnp.where(blk_start < pend[-1],
                          jnp.clip(per_block(pstart + counts) - blk_start, 0, MOE_BLOCK), 0).astype(I32)
    live = jnp.where(counts > 0, ex, N_EXPERTS)
    later = jnp.concatenate([lax.cummin(live[::-1])[::-1][1:], jnp.full((1,), N_EXPERTS, I32)])
    blk_next = per_block(jnp.where(later < N_EXPERTS, later, -1))

    dest_flat = dest.T.reshape(-1)
    used_end = pstart + (counts + MOE_SUB - 1) // MOE_SUB * MOE_SUB
    pad_tbl = jnp.concatenate([pstart + counts, used_end]).astype(I32)

    xs = _dispatch(dest_flat, pad_tbl, hp, cap)
    ys = _experts(blk_expert, blk_next, blk_valid, xs, w1.astype(F32),
                  b1.reshape(N_EXPERTS, 1, -1).astype(F32), w2.astype(F32),
                  b2.reshape(N_EXPERTS, 1, -1).astype(F32))
    return _combine(dest_flat, ys, x1, gates.T, p2d, row(ple_norm), w_ple_gate.astype(BF16),
                    w_ple_proj.astype(BF16), row(final_norm), final)


def kernel(x, p, mix_norm, w_in, sgu_ln_g, sgu_ln_b, sgu_w, sgu_b, conv_w, conv_b, dt_bias, a_log,
           d_skip, ssm_norm, w_branch_a, w_branch_b, w_out, ffn_norm, w_router, b_router, w1, b1,
           w2, b2, ple_norm, w_ple_gate, w_ple_proj, final_norm):
    bsz, seq, d = x.shape
    depth = w_in.shape[0]
    assert d == D_MODEL and seq % MIX_T == 0 and (bsz * seq) % max(IN_TM, ROUTE_T, DISP_T) == 0
    assert ROUTE_T == DISP_T == COMB_T
    x2d = x.reshape(bsz * seq, d)
    for i in range(depth):
        x2d = _layer(x2d, p[i].reshape(bsz * seq, PLE_DIM), bsz, seq, mix_norm[i], w_in[i],
                     sgu_ln_g[i], sgu_ln_b[i], sgu_w[i], sgu_b[i], conv_w[i], conv_b[i], dt_bias[i],
                     a_log[i], d_skip[i], ssm_norm[i], w_branch_a[i], w_branch_b[i], w_out[i],
                     ffn_norm[i], w_router[i], b_router[i], w1[i], b1[i], w2[i], b2[i], ple_norm[i],
                     w_ple_gate[i], w_ple_proj[i], final_norm, final=(i == depth - 1))
    return x2d.reshape(bsz, seq, d)
```

```python
import functools
import math

import jax
import jax.numpy as jnp
from jax import lax
from jax.experimental import pallas as pl
from jax.experimental.pallas import tpu as pltpu

F32 = jnp.float32
BF16 = jnp.bfloat16
I32 = jnp.int32

D_MODEL = 1024
CHUNK = 64
PLE_DIM = 256
RMS_EPS = 1e-6
LN_EPS = 1e-5

SGU_HEADS = 8
SGU_HEAD_DIM = 128
SGU_WIDTH = SGU_HEADS * SGU_HEAD_DIM
SGU_LEN = 128

SSM_HEADS = 16
SSM_HEAD_DIM = 64
SSM_INNER = SSM_HEADS * SSM_HEAD_DIM
SSM_GROUPS = 4
SSM_STATE = 128
SSM_CONV = 4
SSD_CHUNK = 128
SSM_CONV_DIM = SSM_INNER + 2 * SSM_GROUPS * SSM_STATE

N_EXPERTS = 32
TOP_K = 4
D_FF_EXPERT = 1024
SWIGLU_LIMIT = 7.0
SWIGLU_ALPHA = 1.702
MOE_BLOCK = 512
MOE_SUB = 256

OFF_U = 0
OFF_V = OFF_U + SGU_WIDTH
OFF_Z = OFF_V + SGU_WIDTH
OFF_XBC = OFF_Z + SSM_INNER
OFF_DT = OFF_XBC + SSM_CONV_DIM
OFF_GA = OFF_DT + SSM_HEADS
OFF_GB = OFF_GA + D_MODEL
IN_PROJ_DIM = OFF_GB + D_MODEL

LANES = 128
SUBLANES = 8
PROJ_MAIN = IN_PROJ_DIM - SSM_HEADS
PROJ_TILE = 1024
TOK_SUB = D_MODEL // LANES

VMEM_LIMIT = 56 * 1024 * 1024

IN_TM = 512
IN_TN = 1792
MIX_T = 512
ROUTE_T = 512
DISP_T = 512
COMB_T = DISP_T
COMB_SUB = 128
ISSUE_PARTS = 2
DMA_UNROLL = 8
ZERO_RUN = 64
CAST_ROWS = 128


def _sigmoid(x):
    return 1.0 / (1.0 + jnp.exp(-x))


def _gelu_exact(x):
    return 0.5 * x * (1.0 + lax.erf(x * (1.0 / math.sqrt(2.0))))


def _softplus(x):
    return jnp.maximum(x, 0.0) + jnp.log1p(jnp.exp(-jnp.abs(x)))


def _rms(x, g):
    ms = jnp.mean(x * x, axis=-1, keepdims=True)
    return x * lax.rsqrt(ms + RMS_EPS) * g


def _store_token_tiles(ref, val, rows, start=0):
    for j in range(TOK_SUB):
        ref[pl.ds(start * TOK_SUB + j, rows, stride=TOK_SUB), :] = val[:, j * LANES:(j + 1) * LANES]


def _load_token_tiles(ref, rows, start=0):
    return jnp.concatenate(
        [ref[pl.ds(start * TOK_SUB + j, rows, stride=TOK_SUB), :] for j in range(TOK_SUB)], axis=1)


def _tile_rows(idx):
    return pl.ds(pl.multiple_of(idx * TOK_SUB, TOK_SUB), TOK_SUB)


def _inproj_kernel(x_ref, g_ref, w_ref, wdt_ref, proj_ref, dt_ref, h_scr):
    @pl.when(pl.program_id(0) == 0)
    def _():
        h_scr[...] = jnp.zeros_like(h_scr)

    h = h_scr[...]
    dt_ref[...] = jnp.dot(h, wdt_ref[...], preferred_element_type=F32)
    for c in range(PROJ_MAIN // IN_TN):
        cols = slice(c * IN_TN, (c + 1) * IN_TN)
        proj_ref[:, cols] = jnp.dot(h, w_ref[:, cols], preferred_element_type=F32).astype(BF16)
    h_scr[...] = _rms(x_ref[...], g_ref[...]).astype(BF16)


def _in_proj(x2d, g, w_main, w_dt):
    n = x2d.shape[0]
    ntiles = n // IN_TM
    return pl.pallas_call(
        _inproj_kernel,
        grid=(ntiles + 1,),
        in_specs=[
            pl.BlockSpec((IN_TM, D_MODEL), lambda s: (jnp.minimum(s, ntiles - 1), 0)),
            pl.BlockSpec((1, D_MODEL), lambda s: (0, 0)),
            pl.BlockSpec((D_MODEL, PROJ_MAIN), lambda s: (0, 0)),
            pl.BlockSpec((D_MODEL, LANES), lambda s: (0, 0)),
        ],
        out_specs=[
            pl.BlockSpec((IN_TM, PROJ_MAIN), lambda s: (jnp.maximum(s - 1, 0), 0)),
            pl.BlockSpec((IN_TM, LANES), lambda s: (jnp.maximum(s - 1, 0), 0)),
        ],
        out_shape=[
            jax.ShapeDtypeStruct((n, PROJ_MAIN), BF16),
            jax.ShapeDtypeStruct((n, LANES), F32),
        ],
        scratch_shapes=[pltpu.VMEM((IN_TM, D_MODEL), BF16)],
        compiler_params=pltpu.CompilerParams(
            dimension_semantics=("arbitrary",), vmem_limit_bytes=VMEM_LIMIT),
        name="in_proj",
    )(x2d, g, w_main, w_dt)


def _split3(a):
    a1 = a.astype(BF16)
    r1 = a - a1.astype(F32)
    a2 = r1.astype(BF16)
    a3 = (r1 - a2.astype(F32)).astype(BF16)
    return a1, a2, a3


def _dot3(parts, w):
    out = jnp.dot(parts[0], w, preferred_element_type=F32)
    for p in parts[1:]:
        out = out + jnp.dot(p, w, preferred_element_type=F32)
    return out


def _mixer_kernel(u_ref, v_ref, z_ref, xb0_ref, xb1_ref, dt_ref, ga_ref, gb_ref, x_ref,
                  lng_ref, lnb_ref, wsgu_ref, bsgu_ref, convw_ref, convb_ref, dtb_ref, acol_ref,
                  dskip_ref, normw_ref, eexp_ref, shift_ref, triu_ref, wa_ref, wb_ref, wo_ref,
                  x1_ref,
                  tail_scr, xc_scr, state_scr, ya_scr, yb_scr, yap_scr, ybp_scr, *, tps):
    nchunk = MIX_T // SSD_CHUNK
    s = pl.program_id(0)

    @pl.when(s % tps == 0)
    def _():
        tail_scr[...] = jnp.zeros_like(tail_scr)
        state_scr[...] = jnp.zeros_like(state_scr)

    @pl.when(s == 0)
    def _():
        yap_scr[...] = jnp.zeros_like(yap_scr)
        ybp_scr[...] = jnp.zeros_like(ybp_scr)

    xb_refs = (xb0_ref, xb1_ref)
    for c in range(nchunk):
        r0 = c * SSD_CHUNK
        for hf, xb_ref in enumerate(xb_refs):
            cols = slice(hf * PROJ_TILE, (hf + 1) * PROJ_TILE)
            cur = xb_ref[r0:r0 + SSD_CHUNK, :]
            prev = tail_scr[:, cols] if c == 0 else xb_ref[r0 - SSD_CHUNK:r0, :]
            sh = jnp.dot(shift_ref[...], jnp.concatenate([prev, cur], axis=0), preferred_element_type=F32)
            acc = convb_ref[:, cols] + convw_ref[SSM_CONV - 1:SSM_CONV, cols] * cur.astype(F32)
            for j in range(SSM_CONV - 1):
                acc = acc + convw_ref[j:j + 1, cols] * sh[j * SSD_CHUNK:(j + 1) * SSD_CHUNK, :]
            xc_scr[r0:r0 + SSD_CHUNK, cols] = acc * _sigmoid(acc)
    for hf, xb_ref in enumerate(xb_refs):
        tail_scr[:, hf * PROJ_TILE:(hf + 1) * PROJ_TILE] = xb_ref[MIX_T - SSD_CHUNK:MIX_T, :]

    row_i = lax.broadcasted_iota(I32, (SSD_CHUNK, SSD_CHUNK), 0)
    col_i = lax.broadcasted_iota(I32, (SSD_CHUNK, SSD_CHUNK), 1)
    tril = row_i >= col_i
    lane_hi = col_i >= SSM_HEAD_DIM
    eexp = eexp_ref[...]
    triu = triu_ref[...]

    for c in range(nchunk):
        rows = slice(c * SSD_CHUNK, (c + 1) * SSD_CHUNK)

        ug = _gelu_exact(u_ref[rows, :].astype(F32))
        vg = _gelu_exact(v_ref[rows, :].astype(F32))
        mu = jnp.mean(vg, axis=-1, keepdims=True)
        vc = vg - mu
        var = jnp.mean(vc * vc, axis=-1, keepdims=True)
        vn = (vc * lax.rsqrt(var + LN_EPS) * lng_ref[...] + lnb_ref[...]).astype(BF16)
        for g in range(SGU_HEADS):
            cols = slice(g * SGU_HEAD_DIM, (g + 1) * SGU_HEAD_DIM)
            mixed = jnp.dot(wsgu_ref[g], vn[:, cols], preferred_element_type=F32)
            ya_scr[rows, cols] = (ug[:, cols] * (mixed + bsgu_ref[:, cols])).astype(BF16)

        dt_t = _softplus(dt_ref[rows, :].T[0:SSM_HEADS, :] + dtb_ref[...])
        acs_t = _dot3(_split3(dt_t * acol_ref[...]), triu)
        w_t = jnp.exp(acs_t[:, SSD_CHUNK - 1:SSD_CHUNK] - acs_t) * dt_t
        acs = jnp.concatenate([acs_t, jnp.zeros((LANES - SSM_HEADS, SSD_CHUNK), F32)], axis=0).T
        aend = jnp.broadcast_to(acs[SSD_CHUNK - 1:SSD_CHUNK, :], (SUBLANES, LANES))
        dec = jnp.exp(_dot3(_split3(aend), eexp)[0:1, :])

        for gi in range(SSM_GROUPS):
            bg = xc_scr[rows, SSM_INNER + gi * SSM_STATE:SSM_INNER + (gi + 1) * SSM_STATE]
            cg = xc_scr[rows, SSM_INNER + (SSM_GROUPS + gi) * SSM_STATE:
                        SSM_INNER + (SSM_GROUPS + gi + 1) * SSM_STATE]
            bg_t = bg.T
            cb = jnp.dot(cg.astype(BF16), bg_t.astype(BF16), preferred_element_type=F32)
            for pr in range(2):
                lb = gi * 2 + pr
                cols = slice(lb * LANES, (lb + 1) * LANES)
                xpair = xc_scr[rows, cols]
                hpair = state_scr[:, cols]
                acc = jnp.zeros((SSD_CHUNK, LANES), F32)
                st = jnp.zeros((SSM_STATE, LANES), F32)
                for hh in range(2):
                    h = lb * 2 + hh
                    lmask = lane_hi if hh == 1 else jnp.logical_not(lane_hi)
                    xm = jnp.where(lmask, xpair, 0.0).astype(BF16)
                    hm = jnp.where(lmask, hpair, 0.0).astype(BF16)
                    colb = acs[:, h:h + 1]
                    rowb = acs_t[h:h + 1, :]
                    decay = jnp.exp(jnp.where(tril, colb - rowb, -jnp.inf))
                    m_h = (cb * decay * dt_t[h:h + 1, :]).astype(BF16)
                    c_h = (cg * jnp.exp(colb)).astype(BF16)
                    acc = acc + jnp.dot(m_h, xm, preferred_element_type=F32)
                    acc = acc + jnp.dot(c_h, hm, preferred_element_type=F32)
                    btw = (bg_t * w_t[h:h + 1, :]).astype(BF16)
                    st = st + jnp.dot(btw, xm, preferred_element_type=F32)
                state_scr[:, cols] = hpair * dec[:, cols] + st
                y = acc + dskip_ref[:, cols] * xpair
                zz = z_ref[rows, cols].astype(F32)
                xc_scr[rows, cols] = y * (zz * _sigmoid(zz))
            gcols = slice(gi * 2 * LANES, (gi + 1) * 2 * LANES)
            yg = xc_scr[rows, gcols]
            ms = jnp.mean(yg * yg, axis=-1, keepdims=True)
            yb_scr[rows, gcols] = (yg * lax.rsqrt(ms + RMS_EPS) * normw_ref[:, gcols]).astype(BF16)

    ma = jnp.dot(yap_scr[...], wa_ref[...], preferred_element_type=F32)
    mb = jnp.dot(ybp_scr[...], wb_ref[...], preferred_element_type=F32)
    merged = (_sigmoid(ga_ref[...].astype(F32)) * ma + _sigmoid(gb_ref[...].astype(F32)) * mb)
    x1_ref[...] = x_ref[...] + jnp.dot(merged.astype(BF16), wo_ref[...], preferred_element_type=F32)
    yap_scr[...] = ya_scr[...]
    ybp_scr[...] = yb_scr[...]


def _mixer(proj, dt_raw, x2d, bsz, seq, consts, wa, wb, wo):
    n = x2d.shape[0]
    tps = seq // MIX_T
    ntiles = bsz * tps
    cur = lambda s: jnp.minimum(s, ntiles - 1)
    prev = lambda s: jnp.maximum(s - 1, 0)

    def pcol(off, tile):
        return pl.BlockSpec((MIX_T, PROJ_TILE), lambda s: (tile(s), off // PROJ_TILE))

    def full(a):
        nd = a.ndim
        return pl.BlockSpec(a.shape, lambda s, nd=nd: (0,) * nd)

    in_specs = [pcol(OFF_U, cur), pcol(OFF_V, cur), pcol(OFF_Z, cur),
                pcol(OFF_XBC, cur), pcol(OFF_XBC + PROJ_TILE, cur),
                pl.BlockSpec((MIX_T, LANES), lambda s: (cur(s), 0)),
                pcol(OFF_GA - SSM_HEADS, prev), pcol(OFF_GB - SSM_HEADS, prev),
                pl.BlockSpec((MIX_T, D_MODEL), lambda s: (prev(s), 0))]
    in_specs += [full(a) for a in consts] + [full(wa), full(wb), full(wo)]
    return pl.pallas_call(
        functools.partial(_mixer_kernel, tps=tps),
        grid=(ntiles + 1,),
        in_specs=in_specs,
        out_specs=pl.BlockSpec((MIX_T, D_MODEL), lambda s: (prev(s), 0)),
        out_shape=jax.ShapeDtypeStruct((n, D_MODEL), F32),
        scratch_shapes=[
            pltpu.VMEM((SSD_CHUNK, SSM_CONV_DIM), BF16),
            pltpu.VMEM((MIX_T, SSM_CONV_DIM), F32),
            pltpu.VMEM((SSM_STATE, SSM_INNER), F32),
            pltpu.VMEM((MIX_T, SGU_WIDTH), BF16),
            pltpu.VMEM((MIX_T, SSM_INNER), BF16),
            pltpu.VMEM((MIX_T, SGU_WIDTH), BF16),
            pltpu.VMEM((MIX_T, SSM_INNER), BF16),
        ],
        compiler_params=pltpu.CompilerParams(
            dimension_semantics=("arbitrary",), vmem_limit_bytes=VMEM_LIMIT),
        name="mixer",
    )(proj, proj, proj, proj, proj, dt_raw, proj, proj, x2d, *consts, wa, wb, wo)


def _router_kernel(x_ref, g_ref, wr_ref, br_ref, upper_ref, hp_ref, idx_ref, gate_ref, rank_ref, cnt_ref,
                   carry_scr):
    @pl.when(pl.program_id(0) == 0)
    def _():
        carry_scr[...] = jnp.zeros_like(carry_scr)

    h = _rms(x_ref[...], g_ref[...])
    _store_token_tiles(hp_ref, h, ROUTE_T)

    h_hi = h.astype(BF16)
    h_lo = (h - h_hi.astype(F32)).astype(BF16)
    logits = (jnp.dot(h_hi, wr_ref[0], preferred_element_type=F32)
              + jnp.dot(h_hi, wr_ref[1], preferred_element_type=F32)
              + jnp.dot(h_lo, wr_ref[0], preferred_element_type=F32))
    lt = logits.T[0:N_EXPERTS, :] + br_ref[...]
    eidx = lax.broadcasted_iota(I32, (N_EXPERTS, ROUTE_T), 0).astype(F32)
    vals = lt
    sel_any = jnp.zeros((N_EXPERTS, ROUTE_T), F32)
    sels, tops = [], []
    for k in range(TOP_K):
        m = jnp.max(vals, axis=0, keepdims=True)
        first = jnp.min(jnp.where(vals == m, eidx, float(N_EXPERTS)), axis=0, keepdims=True)
        sel = eidx == first
        vals = jnp.where(sel, -jnp.inf, vals)
        sel_f = sel.astype(F32)
        sel_any = sel_any + sel_f
        sels.append(sel_f)
        tops.append(m)
        idx_ref[k:k + 1, :] = first.astype(I32)
    es = [jnp.exp(t - tops[0]) for t in tops]
    denom = functools.reduce(lambda a, b: a + b, es)
    for k in range(TOP_K):
        gate_ref[k:k + 1, :] = es[k] / denom
    gate_ref[TOP_K:, :] = jnp.zeros((SUBLANES - TOP_K, ROUTE_T), F32)

    excl = jnp.dot(sel_any.astype(BF16), upper_ref[...], preferred_element_type=F32) + carry_scr[:, 0:1]
    for k in range(TOP_K):
        rank_ref[k:k + 1, :] = jnp.sum(sels[k] * excl, axis=0, keepdims=True).astype(I32)
    new_carry = carry_scr[...] + jnp.sum(sel_any, axis=1, keepdims=True)
    carry_scr[...] = new_carry
    cnt_ref[...] = new_carry.astype(I32)


def _router(x1, g, wr_pad, br_col, upper):
    n = x1.shape[0]
    per_k = pl.BlockSpec((TOP_K, ROUTE_T), lambda i: (0, i))
    return pl.pallas_call(
        _router_kernel,
        grid=(n // ROUTE_T,),
        in_specs=[
            pl.BlockSpec((ROUTE_T, D_MODEL), lambda i: (i, 0)),
            pl.BlockSpec((1, D_MODEL), lambda i: (0, 0)),
            pl.BlockSpec((2, D_MODEL, LANES), lambda i: (0, 0, 0)),
            pl.BlockSpec((N_EXPERTS, 1), lambda i: (0, 0)),
            pl.BlockSpec((ROUTE_T, ROUTE_T), lambda i: (0, 0)),
        ],
        out_specs=[
            pl.BlockSpec((ROUTE_T * TOK_SUB, LANES), lambda i: (i, 0)),
            per_k,
            pl.BlockSpec((SUBLANES, ROUTE_T), lambda i: (0, i)),
            per_k,
            pl.BlockSpec((N_EXPERTS, LANES), lambda i: (0, 0)),
        ],
        out_shape=[
            jax.ShapeDtypeStruct((n * TOK_SUB, LANES), F32),
            jax.ShapeDtypeStruct((TOP_K, n), I32),
            jax.ShapeDtypeStruct((SUBLANES, n), F32),
            jax.ShapeDtypeStruct((TOP_K, n), I32),
            jax.ShapeDtypeStruct((N_EXPERTS, LANES), I32),
        ],
        scratch_shapes=[pltpu.VMEM((N_EXPERTS, LANES), F32)],
        compiler_params=pltpu.CompilerParams(
            dimension_semantics=("arbitrary",), vmem_limit_bytes=VMEM_LIMIT),
        name="router",
    )(x1, g, wr_pad, br_col, upper)


def _dispatch_kernel(dest_ref, pad_ref, hp_ref, xs_hbm, zero_scr, sem_rows):
    i = pl.program_id(0)
    tbl = i * (TOP_K * DISP_T)

    def zero_copy(s):
        return pltpu.make_async_copy(zero_scr.at[0:TOK_SUB, :], xs_hbm.at[_tile_rows(s), :], sem_rows)

    def zero_run(s):
        rows = pl.ds(pl.multiple_of(s * TOK_SUB, TOK_SUB), ZERO_RUN * TOK_SUB)
        return pltpu.make_async_copy(zero_scr, xs_hbm.at[rows, :], sem_rows)

    def zero_pads(act):
        for e in range(N_EXPERTS):
            lo, hi = pad_ref[e], pad_ref[N_EXPERTS + e]
            nrun = (hi - lo) // ZERO_RUN
            lax.fori_loop(0, nrun, lambda r, c: (act(zero_run(lo + r * ZERO_RUN)), c)[1], 0)
            lax.fori_loop(lo + nrun * ZERO_RUN, hi, lambda s, c: (act(zero_copy(s)), c)[1], 0)

    @pl.when(i == 0)
    def _():
        zero_scr[...] = jnp.zeros_like(zero_scr)
        zero_pads(lambda d: d.start())
        zero_pads(lambda d: d.wait())

    def tile_copy(t, k):
        d = dest_ref[tbl + t * TOP_K + k]
        return pltpu.make_async_copy(hp_ref.at[_tile_rows(t), :], xs_hbm.at[_tile_rows(d), :], sem_rows)

    def issue(tb, carry):
        for r in range(DMA_UNROLL):
            for k in range(TOP_K):
                tile_copy(tb * DMA_UNROLL + r, k).start(priority=(r * TOP_K + k) % 2)
        return carry

    lax.fori_loop(0, DISP_T // DMA_UNROLL, issue, 0)

    def drain(tb, carry):
        for _ in range(DMA_UNROLL * TOP_K):
            zero_copy(0).wait()
        return carry

    lax.fori_loop(0, DISP_T // DMA_UNROLL, drain, 0)


def _dispatch(dest_flat, pad_tbl, hp, cap):
    n = hp.shape[0] // TOK_SUB
    grid_spec = pltpu.PrefetchScalarGridSpec(
        num_scalar_prefetch=2,
        grid=(n // DISP_T,),
        in_specs=[pl.BlockSpec((DISP_T * TOK_SUB, LANES), lambda i, d, p: (i, 0))],
        out_specs=pl.BlockSpec(memory_space=pl.ANY),
        scratch_shapes=[pltpu.VMEM((ZERO_RUN * TOK_SUB, LANES), F32), pltpu.SemaphoreType.DMA],
    )
    return pl.pallas_call(
        _dispatch_kernel,
        grid_spec=grid_spec,
        out_shape=jax.ShapeDtypeStruct((cap * TOK_SUB, LANES), F32),
        compiler_params=pltpu.CompilerParams(dimension_semantics=("arbitrary",)),
        name="dispatch",
    )(dest_flat, pad_tbl, hp)


def _expert_kernel(be_ref, nxt_ref, nv_ref, xs_ref, w1_hbm, b1_ref, w2_hbm, b2_ref, ys_ref,
                   w1f_scr, w2f_scr, w1b_scr, w2b_scr, sems):
    b = pl.program_id(0)
    valid = nv_ref[b]
    used = valid > 0
    new_expert = jnp.logical_or(b == 0, be_ref[b] != be_ref[jnp.maximum(b - 1, 0)])

    def fetch(e):
        return (pltpu.make_async_copy(w1_hbm.at[e], w1f_scr, sems.at[0]),
                pltpu.make_async_copy(w2_hbm.at[e], w2f_scr, sems.at[1]))

    @pl.when(jnp.logical_and(used, b == 0))
    def _():
        for c in fetch(be_ref[0]):
            c.start()

    @pl.when(jnp.logical_and(used, new_expert))
    def _():
        for c in fetch(be_ref[b]):
            c.wait()

        def cast(rb, carry):
            rows = pl.ds(pl.multiple_of(rb * CAST_ROWS, CAST_ROWS), CAST_ROWS)
            w1b_scr[rows, :] = w1f_scr[rows, :].astype(BF16)
            w2b_scr[rows, :] = w2f_scr[rows, :].astype(BF16)
            return carry

        lax.fori_loop(0, D_MODEL // CAST_ROWS, cast, 0)

        @pl.when(nxt_ref[b] >= 0)
        def _():
            for c in fetch(nxt_ref[b]):
                c.start()

    def expert_mlp(rows):
        x = _load_token_tiles(xs_ref, rows).astype(BF16)
        hid = jnp.dot(x, w1b_scr[...], preferred_element_type=F32) + b1_ref[0]
        glu = jnp.minimum(hid[:, :D_FF_EXPERT], SWIGLU_LIMIT)
        lin = jnp.clip(hid[:, D_FF_EXPERT:], -SWIGLU_LIMIT, SWIGLU_LIMIT)
        act = glu * _sigmoid(SWIGLU_ALPHA * glu) * (lin + 1.0)
        y = jnp.dot(act.astype(BF16), w2b_scr[...], preferred_element_type=F32) + b2_ref[0]
        _store_token_tiles(ys_ref, y, rows)

    @pl.when(valid > MOE_SUB)
    def _():
        expert_mlp(MOE_BLOCK)

    @pl.when(jnp.logical_and(used, valid <= MOE_SUB))
    def _():
        expert_mlp(MOE_SUB)


def _experts(blk_expert, blk_next, blk_valid, xs, w1, b1, w2, b2):
    cap = xs.shape[0] // TOK_SUB
    n_blocks = cap // MOE_BLOCK
    grid_spec = pltpu.PrefetchScalarGridSpec(
        num_scalar_prefetch=3,
        grid=(n_blocks,),
        in_specs=[
            pl.BlockSpec((MOE_BLOCK * TOK_SUB, LANES), lambda b, be, nx, nv: (b, 0)),
            pl.BlockSpec(memory_space=pl.ANY),
            pl.BlockSpec((1, 1, 2 * D_FF_EXPERT), lambda b, be, nx, nv: (be[b], 0, 0)),
            pl.BlockSpec(memory_space=pl.ANY),
            pl.BlockSpec((1, 1, D_MODEL), lambda b, be, nx, nv: (be[b], 0, 0)),
        ],
        out_specs=pl.BlockSpec((MOE_BLOCK * TOK_SUB, LANES), lambda b, be, nx, nv: (b, 0)),
        scratch_shapes=[pltpu.VMEM((D_MODEL, 2 * D_FF_EXPERT), F32),
                        pltpu.VMEM((D_FF_EXPERT, D_MODEL), F32),
                        pltpu.VMEM((D_MODEL, 2 * D_FF_EXPERT), BF16),
                        pltpu.VMEM((D_FF_EXPERT, D_MODEL), BF16),
                        pltpu.SemaphoreType.DMA((2,))],
    )
    assert D_MODEL == D_FF_EXPERT
    assert MOE_BLOCK == 2 * MOE_SUB
    return pl.pallas_call(
        _expert_kernel,
        grid_spec=grid_spec,
        out_shape=jax.ShapeDtypeStruct((cap * TOK_SUB, LANES), F32),
        compiler_params=pltpu.CompilerParams(
            dimension_semantics=("arbitrary",), vmem_limit_bytes=VMEM_LIMIT),
        name="experts",
    )(blk_expert, blk_next, blk_valid, xs, w1, b1, w2, b2)


def _combine_kernel(dest_ref, ys_hbm, x1_ref, gate_ref, p_ref, pg_ref, wg_ref, wp_ref, fg_ref,
                    out_ref, ybuf, sems, *, final):
    i = pl.program_id(0)
    nsteps = pl.num_programs(0)
    buf_tokens = TOP_K * COMB_T

    def issue_part(step, slot, part, parts):
        tbl = step * buf_tokens
        per_part = COMB_T // DMA_UNROLL // parts

        def issue(tb, carry):
            for r in range(DMA_UNROLL):
                for k in range(TOP_K):
                    t = tb * DMA_UNROLL + r
                    pltpu.make_async_copy(
                        ys_hbm.at[_tile_rows(dest_ref[tbl + t * TOP_K + k]), :],
                        ybuf.at[(slot * buf_tokens + k * COMB_T) // SUBLANES + tb, :, r, :],
                        sems.at[slot]).start(priority=(r * TOP_K + k) % 2)
            return carry

        last = jnp.where(step < nsteps, (part + 1) * per_part, part * per_part)
        lax.fori_loop(part * per_part, last, issue, 0)

    def drain_all(slot):
        def drain(tb, carry):
            for _ in range(DMA_UNROLL * TOP_K):
                pltpu.make_async_copy(ys_hbm.at[_tile_rows(0), :],
                                      ybuf.at[slot * buf_tokens // SUBLANES, :, 0, :], sems.at[slot]).wait()
            return carry

        lax.fori_loop(0, COMB_T // DMA_UNROLL, drain, 0)

    @pl.when(i == 0)
    def _():
        issue_part(0, 0, 0, 1)

    nsub = COMB_T // COMB_SUB
    for slot in range(2):
        @pl.when(i % 2 == slot)
        def _():
            drain_all(slot)

            for sub in range(nsub):
                if sub < ISSUE_PARTS:
                    issue_part(i + 1, 1 - slot, sub, ISSUE_PARTS)
                rows = slice(sub * COMB_SUB, (sub + 1) * COMB_SUB)
                x2 = x1_ref[rows, :]
                for k in range(TOP_K):
                    grp = pl.ds((slot * buf_tokens + k * COMB_T + sub * COMB_SUB) // SUBLANES,
                                COMB_SUB // SUBLANES)
                    yk = jnp.concatenate([ybuf[grp, j, :, :].reshape(COMB_SUB, LANES)
                                          for j in range(TOK_SUB)], axis=1)
                    x2 = x2 + gate_ref[rows, k:k + 1] * yk
                hp = _rms(x2, pg_ref[...]).astype(BF16)
                gate = _sigmoid(jnp.dot(hp, wg_ref[...], preferred_element_type=F32))
                emb = jnp.dot(p_ref[rows, :].astype(BF16), wp_ref[...], preferred_element_type=F32)
                x3 = x2 + gate * emb
                if final:
                    x3 = _rms(x3, fg_ref[...])
                out_ref[rows, :] = x3


def _combine(dest_flat, ys, x1, gates_t, p2d, pg, wg, wp, fg, final):
    n = x1.shape[0]
    grid_spec = pltpu.PrefetchScalarGridSpec(
        num_scalar_prefetch=1,
        grid=(n // COMB_T,),
        in_specs=[
            pl.BlockSpec(memory_space=pl.ANY),
            pl.BlockSpec((COMB_T, D_MODEL), lambda i, d: (i, 0)),
            pl.BlockSpec((COMB_T, SUBLANES), lambda i, d: (i, 0)),
            pl.BlockSpec((COMB_T, PLE_DIM), lambda i, d: (i, 0)),
            pl.BlockSpec((1, D_MODEL), lambda i, d: (0, 0)),
            pl.BlockSpec((D_MODEL, D_MODEL), lambda i, d: (0, 0)),
            pl.BlockSpec((PLE_DIM, D_MODEL), lambda i, d: (0, 0)),
            pl.BlockSpec((1, D_MODEL), lambda i, d: (0, 0)),
        ],
        out_specs=pl.BlockSpec((COMB_T, D_MODEL), lambda i, d: (i, 0)),
        scratch_shapes=[pltpu.VMEM((2 * TOP_K * COMB_T // SUBLANES, TOK_SUB, SUBLANES, LANES), F32),
                        pltpu.SemaphoreType.DMA((2,))],
    )
    return pl.pallas_call(
        functools.partial(_combine_kernel, final=final),
        grid_spec=grid_spec,
        out_shape=jax.ShapeDtypeStruct((n, D_MODEL), F32),
        compiler_params=pltpu.CompilerParams(
            dimension_semantics=("arbitrary",), vmem_limit_bytes=VMEM_LIMIT),
        name="combine",
    )(dest_flat, ys, x1, gates_t, p2d, pg, wg, wp, fg)


def _layer(x2d, p2d, bsz, seq, mix_norm, w_in, sgu_ln_g, sgu_ln_b, sgu_w, sgu_b, conv_w, conv_b,
           dt_bias, a_log, d_skip, ssm_norm, w_branch_a, w_branch_b, w_out, ffn_norm, w_router,
           b_router, w1, b1, w2, b2, ple_norm, w_ple_gate, w_ple_proj, final_norm, final):
    n = x2d.shape[0]
    row = lambda a: a.reshape(1, -1).astype(F32)

    w_main = jnp.concatenate([w_in[:, :OFF_DT], w_in[:, OFF_GA:]], axis=1).astype(BF16)
    w_dt = jnp.pad(w_in[:, OFF_DT:OFF_GA], ((0, 0), (0, LANES - SSM_HEADS))).astype(BF16)
    pos = jnp.arange(SGU_LEN)
    allowed = (pos[None, :] // CHUNK) <= (pos[:, None] // CHUNK)
    wsgu = jnp.where(allowed[None], sgu_w, 0.0).astype(BF16)
    bsgu = jnp.repeat(sgu_b.T, SGU_HEAD_DIM, axis=1).astype(F32)
    dtb = dt_bias.reshape(-1, 1).astype(F32)
    acol = -jnp.exp(a_log.astype(F32)).reshape(-1, 1)
    dskip = jnp.repeat(d_skip.astype(F32), SSM_HEAD_DIM).reshape(1, -1)
    head_of_col = jnp.arange(SSM_INNER) // SSM_HEAD_DIM
    eexp = (jnp.arange(LANES)[:, None] == head_of_col[None, :]).astype(BF16)
    triu = (jnp.arange(SSD_CHUNK)[:, None] <= jnp.arange(SSD_CHUNK)[None, :]).astype(BF16)
    t_out = jnp.arange((SSM_CONV - 1) * SSD_CHUNK)
    src = SSD_CHUNK + t_out % SSD_CHUNK - (SSM_CONV - 1) + t_out // SSD_CHUNK
    shift = (src[:, None] == jnp.arange(2 * SSD_CHUNK)[None, :]).astype(BF16)
    consts = [row(sgu_ln_g), row(sgu_ln_b), wsgu, bsgu, conv_w.astype(F32), row(conv_b), dtb, acol,
              dskip, row(ssm_norm), eexp, shift, triu]

    proj, dt_raw = _in_proj(x2d, row(mix_norm), w_main, w_dt)
    x1 = _mixer(proj, dt_raw, x2d, bsz, seq, consts, w_branch_a.astype(BF16),
                w_branch_b.astype(BF16), w_out.astype(BF16))

    wr_f = jnp.pad(w_router.astype(F32), ((0, 0), (0, LANES - N_EXPERTS)))
    wr_hi = wr_f.astype(BF16)
    wr_pad = jnp.stack([wr_hi, (wr_f - wr_hi.astype(F32)).astype(BF16)])
    upper = (jnp.arange(ROUTE_T)[:, None] < jnp.arange(ROUTE_T)[None, :]).astype(BF16)
    hp, idx, gates, rank, cnt = _router(x1, row(ffn_norm), wr_pad, b_router.reshape(-1, 1).astype(F32),
                                        upper)

    counts = cnt[:, 0]
    padded = (counts + MOE_BLOCK - 1) // MOE_BLOCK * MOE_BLOCK
    pend = jnp.cumsum(padded)
    pstart = pend - padded
    nk = n * TOP_K
    cap = (nk + MOE_BLOCK - 1) // MOE_BLOCK * MOE_BLOCK + N_EXPERTS * MOE_BLOCK
    n_blocks = cap // MOE_BLOCK
    ex = jnp.arange(N_EXPERTS, dtype=I32)
    dest = rank + jnp.sum(jnp.where(idx[..., None] == ex, pstart.astype(I32), 0), axis=-1)
    blk_start = jnp.arange(n_blocks, dtype=I32) * MOE_BLOCK
    blk_expert = jnp.minimum(jnp.sum((pend[None, :] <= blk_start[:, None]).astype(I32), axis=1),
                             N_EXPERTS - 1).astype(I32)
    of_block = blk_expert[:, None] == ex[None, :]
    per_block = lambda v: jnp.sum(jnp.where(of_block, v[None, :].astype(I32), 0), axis=1)
    blk_valid = jnp.where(blk_start < pend[-1],
                          jnp.clip(per_block(pstart + counts) - blk_start, 0, MOE_BLOCK), 0).astype(I32)
    live = jnp.where(counts > 0, ex, N_EXPERTS)
    later = jnp.concatenate([lax.cummin(live[::-1])[::-1][1:], jnp.full((1,), N_EXPERTS, I32)])
    blk_next = per_block(jnp.where(later < N_EXPERTS, later, -1))

    dest_flat = dest.T.reshape(-1)
    used_end = pstart + (counts + MOE_SUB - 1) // MOE_SUB * MOE_SUB
    pad_tbl = jnp.concatenate([pstart + counts, used_end]).astype(I32)

    xs = _dispatch(dest_flat, pad_tbl, hp, cap)
    ys = _experts(blk_expert, blk_next, blk_valid, xs, w1.astype(F32),
                  b1.reshape(N_EXPERTS, 1, -1).astype(F32), w2.astype(F32),
                  b2.reshape(N_EXPERTS, 1, -1).astype(F32))
    return _combine(dest_flat, ys, x1, gates.T, p2d, row(ple_norm), w_ple_gate.astype(BF16),
                    w_ple_proj.astype(BF16), row(final_norm), final)


def kernel(x, p, mix_norm, w_in, sgu_ln_g, sgu_ln_b, sgu_w, sgu_b, conv_w, conv_b, dt_bias, a_log,
           d_skip, ssm_norm, w_branch_a, w_branch_b, w_out, ffn_norm, w_router, b_router, w1, b1,
           w2, b2, ple_norm, w_ple_gate, w_ple_proj, final_norm):
    bsz, seq, d = x.shape
    depth = w_in.shape[0]
    assert d == D_MODEL and seq % MIX_T == 0 and (bsz * seq) % max(IN_TM, ROUTE_T, DISP_T) == 0
    assert ROUTE_T == DISP_T == COMB_T
    assert DMA_UNROLL == SUBLANES
    x2d = x.reshape(bsz * seq, d)
    for i in range(depth):
        x2d = _layer(x2d, p[i].reshape(bsz * seq, PLE_DIM), bsz, seq, mix_norm[i], w_in[i],
                     sgu_ln_g[i], sgu_ln_b[i], sgu_w[i], sgu_b[i], conv_w[i], conv_b[i], dt_bias[i],
                     a_log[i], d_skip[i], ssm_norm[i], w_branch_a[i], w_branch_b[i], w_out[i],
                     ffn_norm[i], w_router[i], b_router[i], w1[i], b1[i], w2[i], b2[i], ple_norm[i],
                     w_ple_gate[i], w_ple_proj[i], final_norm, final=(i == depth - 1))
    return x2d.reshape(bsz, seq, d)
```

```python
import functools
import math

import jax
import jax.numpy as jnp
from jax import lax
from jax.experimental import pallas as pl
from jax.experimental.pallas import tpu as pltpu

F32 = jnp.float32
BF16 = jnp.bfloat16
I32 = jnp.int32

D_MODEL = 1024
CHUNK = 64
PLE_DIM = 256
RMS_EPS = 1e-6
LN_EPS = 1e-5

SGU_HEADS = 8
SGU_HEAD_DIM = 128
SGU_WIDTH = SGU_HEADS * SGU_HEAD_DIM
SGU_LEN = 128

SSM_HEADS = 16
SSM_HEAD_DIM = 64
SSM_INNER = SSM_HEADS * SSM_HEAD_DIM
SSM_GROUPS = 4
SSM_STATE = 128
SSM_CONV = 4
SSD_CHUNK = 128
SSM_CONV_DIM = SSM_INNER + 2 * SSM_GROUPS * SSM_STATE

N_EXPERTS = 32
TOP_K = 4
D_FF_EXPERT = 1024
SWIGLU_LIMIT = 7.0
SWIGLU_ALPHA = 1.702
MOE_BLOCK = 512
MOE_SUB = 256

OFF_U = 0
OFF_V = OFF_U + SGU_WIDTH
OFF_Z = OFF_V + SGU_WIDTH
OFF_XBC = OFF_Z + SSM_INNER
OFF_DT = OFF_XBC + SSM_CONV_DIM
OFF_GA = OFF_DT + SSM_HEADS
OFF_GB = OFF_GA + D_MODEL
IN_PROJ_DIM = OFF_GB + D_MODEL

LANES = 128
SUBLANES = 8
PROJ_MAIN = IN_PROJ_DIM - SSM_HEADS
PROJ_TILE = 1024
TOK_SUB = D_MODEL // LANES

VMEM_LIMIT = 56 * 1024 * 1024

IN_TM = 512
IN_TN = 1792
MIX_T = 512
ROUTE_T = 512
DISP_T = 512
COMB_T = DISP_T
COMB_SUB = 128
ISSUE_PARTS = 2
DMA_UNROLL = 8
ZERO_RUN = 64
CAST_ROWS = 128


def _sigmoid(x):
    return 1.0 / (1.0 + jnp.exp(-x))


def _gelu_exact(x):
    return 0.5 * x * (1.0 + lax.erf(x * (1.0 / math.sqrt(2.0))))


def _softplus(x):
    return jnp.maximum(x, 0.0) + jnp.log1p(jnp.exp(-jnp.abs(x)))


def _rms(x, g):
    ms = jnp.mean(x * x, axis=-1, keepdims=True)
    return x * lax.rsqrt(ms + RMS_EPS) * g


def _store_token_tiles(ref, val, rows, start=0):
    for j in range(TOK_SUB):
        ref[pl.ds(start * TOK_SUB + j, rows, stride=TOK_SUB), :] = val[:, j * LANES:(j + 1) * LANES]


def _load_token_tiles(ref, rows, start=0):
    return jnp.concatenate(
        [ref[pl.ds(start * TOK_SUB + j, rows, stride=TOK_SUB), :] for j in range(TOK_SUB)], axis=1)


def _tile_rows(idx):
    return pl.ds(pl.multiple_of(idx * TOK_SUB, TOK_SUB), TOK_SUB)


def _inproj_kernel(x_ref, g_ref, w_ref, wdt_ref, proj_ref, dt_ref, h_scr):
    @pl.when(pl.program_id(0) == 0)
    def _():
        h_scr[...] = jnp.zeros_like(h_scr)

    h = h_scr[...]
    dt_ref[...] = jnp.dot(h, wdt_ref[...], preferred_element_type=F32)
    for c in range(PROJ_MAIN // IN_TN):
        cols = slice(c * IN_TN, (c + 1) * IN_TN)
        proj_ref[:, cols] = jnp.dot(h, w_ref[:, cols], preferred_element_type=F32).astype(BF16)
    h_scr[...] = _rms(x_ref[...], g_ref[...]).astype(BF16)


def _in_proj(x2d, g, w_main, w_dt):
    n = x2d.shape[0]
    ntiles = n // IN_TM
    return pl.pallas_call(
        _inproj_kernel,
        grid=(ntiles + 1,),
        in_specs=[
            pl.BlockSpec((IN_TM, D_MODEL), lambda s: (jnp.minimum(s, ntiles - 1), 0)),
            pl.BlockSpec((1, D_MODEL), lambda s: (0, 0)),
            pl.BlockSpec((D_MODEL, PROJ_MAIN), lambda s: (0, 0)),
            pl.BlockSpec((D_MODEL, LANES), lambda s: (0, 0)),
        ],
        out_specs=[
            pl.BlockSpec((IN_TM, PROJ_MAIN), lambda s: (jnp.maximum(s - 1, 0), 0)),
            pl.BlockSpec((IN_TM, LANES), lambda s: (jnp.maximum(s - 1, 0), 0)),
        ],
        out_shape=[
            jax.ShapeDtypeStruct((n, PROJ_MAIN), BF16),
            jax.ShapeDtypeStruct((n, LANES), F32),
        ],
        scratch_shapes=[pltpu.VMEM((IN_TM, D_MODEL), BF16)],
        compiler_params=pltpu.CompilerParams(
            dimension_semantics=("arbitrary",), vmem_limit_bytes=VMEM_LIMIT),
        name="in_proj",
    )(x2d, g, w_main, w_dt)


def _split3(a):
    a1 = a.astype(BF16)
    r1 = a - a1.astype(F32)
    a2 = r1.astype(BF16)
    a3 = (r1 - a2.astype(F32)).astype(BF16)
    return a1, a2, a3


def _dot3(parts, w):
    out = jnp.dot(parts[0], w, preferred_element_type=F32)
    for p in parts[1:]:
        out = out + jnp.dot(p, w, preferred_element_type=F32)
    return out


def _mixer_kernel(u_ref, v_ref, z_ref, xb0_ref, xb1_ref, dt_ref, ga_ref, gb_ref, x_ref,
                  lng_ref, lnb_ref, wsgu_ref, bsgu_ref, convw_ref, convb_ref, dtb_ref, acol_ref,
                  dskip_ref, normw_ref, eexp_ref, shift_ref, triu_ref, wa_ref, wb_ref, wo_ref,
                  x1_ref,
                  tail_scr, xc_scr, state_scr, ya_scr, yb_scr, yap_scr, ybp_scr, *, tps):
    nchunk = MIX_T // SSD_CHUNK
    s = pl.program_id(0)

    @pl.when(s % tps == 0)
    def _():
        tail_scr[...] = jnp.zeros_like(tail_scr)
        state_scr[...] = jnp.zeros_like(state_scr)

    @pl.when(s == 0)
    def _():
        yap_scr[...] = jnp.zeros_like(yap_scr)
        ybp_scr[...] = jnp.zeros_like(ybp_scr)

    xb_refs = (xb0_ref, xb1_ref)
    for c in range(nchunk):
        r0 = c * SSD_CHUNK
        for hf, xb_ref in enumerate(xb_refs):
            cols = slice(hf * PROJ_TILE, (hf + 1) * PROJ_TILE)
            cur = xb_ref[r0:r0 + SSD_CHUNK, :]
            prev = tail_scr[:, cols] if c == 0 else xb_ref[r0 - SSD_CHUNK:r0, :]
            sh = jnp.dot(shift_ref[...], jnp.concatenate([prev, cur], axis=0), preferred_element_type=F32)
            acc = convb_ref[:, cols] + convw_ref[SSM_CONV - 1:SSM_CONV, cols] * cur.astype(F32)
            for j in range(SSM_CONV - 1):
                acc = acc + convw_ref[j:j + 1, cols] * sh[j * SSD_CHUNK:(j + 1) * SSD_CHUNK, :]
            xc_scr[r0:r0 + SSD_CHUNK, cols] = acc * _sigmoid(acc)
    for hf, xb_ref in enumerate(xb_refs):
        tail_scr[:, hf * PROJ_TILE:(hf + 1) * PROJ_TILE] = xb_ref[MIX_T - SSD_CHUNK:MIX_T, :]

    row_i = lax.broadcasted_iota(I32, (SSD_CHUNK, SSD_CHUNK), 0)
    col_i = lax.broadcasted_iota(I32, (SSD_CHUNK, SSD_CHUNK), 1)
    tril = row_i >= col_i
    lane_hi = col_i >= SSM_HEAD_DIM
    eexp = eexp_ref[...]
    triu = triu_ref[...]

    for c in range(nchunk):
        rows = slice(c * SSD_CHUNK, (c + 1) * SSD_CHUNK)

        ug = _gelu_exact(u_ref[rows, :].astype(F32))
        vg = _gelu_exact(v_ref[rows, :].astype(F32))
        mu = jnp.mean(vg, axis=-1, keepdims=True)
        vc = vg - mu
        var = jnp.mean(vc * vc, axis=-1, keepdims=True)
        vn = (vc * lax.rsqrt(var + LN_EPS) * lng_ref[...] + lnb_ref[...]).astype(BF16)
        for g in range(SGU_HEADS):
            cols = slice(g * SGU_HEAD_DIM, (g + 1) * SGU_HEAD_DIM)
            mixed = jnp.dot(wsgu_ref[g], vn[:, cols], preferred_element_type=F32)
            ya_scr[rows, cols] = (ug[:, cols] * (mixed + bsgu_ref[:, cols])).astype(BF16)

        dt_t = _softplus(dt_ref[rows, :].T[0:SSM_HEADS, :] + dtb_ref[...])
        acs_t = _dot3(_split3(dt_t * acol_ref[...]), triu)
        w_t = jnp.exp(acs_t[:, SSD_CHUNK - 1:SSD_CHUNK] - acs_t) * dt_t
        acs = jnp.concatenate([acs_t, jnp.zeros((LANES - SSM_HEADS, SSD_CHUNK), F32)], axis=0).T
        aend = jnp.broadcast_to(acs[SSD_CHUNK - 1:SSD_CHUNK, :], (SUBLANES, LANES))
        dec = jnp.exp(_dot3(_split3(aend), eexp)[0:1, :])

        for gi in range(SSM_GROUPS):
            bg = xc_scr[rows, SSM_INNER + gi * SSM_STATE:SSM_INNER + (gi + 1) * SSM_STATE]
            cg = xc_scr[rows, SSM_INNER + (SSM_GROUPS + gi) * SSM_STATE:
                        SSM_INNER + (SSM_GROUPS + gi + 1) * SSM_STATE]
            bg_t = bg.T
            cb = jnp.dot(cg.astype(BF16), bg_t.astype(BF16), preferred_element_type=F32)
            for pr in range(2):
                lb = gi * 2 + pr
                cols = slice(lb * LANES, (lb + 1) * LANES)
                xpair = xc_scr[rows, cols]
                hpair = state_scr[:, cols]
                acc = jnp.zeros((SSD_CHUNK, LANES), F32)
                st = jnp.zeros((SSM_STATE, LANES), F32)
                for hh in range(2):
                    h = lb * 2 + hh
                    lmask = lane_hi if hh == 1 else jnp.logical_not(lane_hi)
                    xm = jnp.where(lmask, xpair, 0.0).astype(BF16)
                    hm = jnp.where(lmask, hpair, 0.0).astype(BF16)
                    colb = acs[:, h:h + 1]
                    rowb = acs_t[h:h + 1, :]
                    decay = jnp.exp(jnp.where(tril, colb - rowb, -jnp.inf))
                    m_h = (cb * decay * dt_t[h:h + 1, :]).astype(BF16)
                    c_h = (cg * jnp.exp(colb)).astype(BF16)
                    acc = acc + jnp.dot(m_h, xm, preferred_element_type=F32)
                    acc = acc + jnp.dot(c_h, hm, preferred_element_type=F32)
                    btw = (bg_t * w_t[h:h + 1, :]).astype(BF16)
                    st = st + jnp.dot(btw, xm, preferred_element_type=F32)
                state_scr[:, cols] = hpair * dec[:, cols] + st
                y = acc + dskip_ref[:, cols] * xpair
                zz = z_ref[rows, cols].astype(F32)
                xc_scr[rows, cols] = y * (zz * _sigmoid(zz))
            gcols = slice(gi * 2 * LANES, (gi + 1) * 2 * LANES)
            yg = xc_scr[rows, gcols]
            ms = jnp.mean(yg * yg, axis=-1, keepdims=True)
            yb_scr[rows, gcols] = (yg * lax.rsqrt(ms + RMS_EPS) * normw_ref[:, gcols]).astype(BF16)

    ma = jnp.dot(yap_scr[...], wa_ref[...], preferred_element_type=F32)
    mb = jnp.dot(ybp_scr[...], wb_ref[...], preferred_element_type=F32)
    merged = (_sigmoid(ga_ref[...].astype(F32)) * ma + _sigmoid(gb_ref[...].astype(F32)) * mb)
    x1_ref[...] = x_ref[...] + jnp.dot(merged.astype(BF16), wo_ref[...], preferred_element_type=F32)
    yap_scr[...] = ya_scr[...]
    ybp_scr[...] = yb_scr[...]


def _mixer(proj, dt_raw, x2d, bsz, seq, consts, wa, wb, wo):
    n = x2d.shape[0]
    tps = seq // MIX_T
    ntiles = bsz * tps
    cur = lambda s: jnp.minimum(s, ntiles - 1)
    prev = lambda s: jnp.maximum(s - 1, 0)

    def pcol(off, tile):
        return pl.BlockSpec((MIX_T, PROJ_TILE), lambda s: (tile(s), off // PROJ_TILE))

    def full(a):
        nd = a.ndim
        return pl.BlockSpec(a.shape, lambda s, nd=nd: (0,) * nd)

    in_specs = [pcol(OFF_U, cur), pcol(OFF_V, cur), pcol(OFF_Z, cur),
                pcol(OFF_XBC, cur), pcol(OFF_XBC + PROJ_TILE, cur),
                pl.BlockSpec((MIX_T, LANES), lambda s: (cur(s), 0)),
                pcol(OFF_GA - SSM_HEADS, prev), pcol(OFF_GB - SSM_HEADS, prev),
                pl.BlockSpec((MIX_T, D_MODEL), lambda s: (prev(s), 0))]
    in_specs += [full(a) for a in consts] + [full(wa), full(wb), full(wo)]
    return pl.pallas_call(
        functools.partial(_mixer_kernel, tps=tps),
        grid=(ntiles + 1,),
        in_specs=in_specs,
        out_specs=pl.BlockSpec((MIX_T, D_MODEL), lambda s: (prev(s), 0)),
        out_shape=jax.ShapeDtypeStruct((n, D_MODEL), F32),
        scratch_shapes=[
            pltpu.VMEM((SSD_CHUNK, SSM_CONV_DIM), BF16),
            pltpu.VMEM((MIX_T, SSM_CONV_DIM), F32),
            pltpu.VMEM((SSM_STATE, SSM_INNER), F32),
            pltpu.VMEM((MIX_T, SGU_WIDTH), BF16),
            pltpu.VMEM((MIX_T, SSM_INNER), BF16),
            pltpu.VMEM((MIX_T, SGU_WIDTH), BF16),
            pltpu.VMEM((MIX_T, SSM_INNER), BF16),
        ],
        compiler_params=pltpu.CompilerParams(
            dimension_semantics=("arbitrary",), vmem_limit_bytes=VMEM_LIMIT),
        name="mixer",
    )(proj, proj, proj, proj, proj, dt_raw, proj, proj, x2d, *consts, wa, wb, wo)


def _router_kernel(x_ref, g_ref, wr_ref, br_ref, upper_ref, hp_ref, idx_ref, gate_ref, rank_ref, cnt_ref,
                   carry_scr):
    @pl.when(pl.program_id(0) == 0)
    def _():
        carry_scr[...] = jnp.zeros_like(carry_scr)

    h = _rms(x_ref[...], g_ref[...])
    _store_token_tiles(hp_ref, h, ROUTE_T)

    h_hi = h.astype(BF16)
    h_lo = (h - h_hi.astype(F32)).astype(BF16)
    logits = (jnp.dot(h_hi, wr_ref[0], preferred_element_type=F32)
              + jnp.dot(h_hi, wr_ref[1], preferred_element_type=F32)
              + jnp.dot(h_lo, wr_ref[0], preferred_element_type=F32))
    lt = logits.T[0:N_EXPERTS, :] + br_ref[...]
    eidx = lax.broadcasted_iota(I32, (N_EXPERTS, ROUTE_T), 0).astype(F32)
    vals = lt
    sel_any = jnp.zeros((N_EXPERTS, ROUTE_T), F32)
    sels, tops = [], []
    for k in range(TOP_K):
        m = jnp.max(vals, axis=0, keepdims=True)
        first = jnp.min(jnp.where(vals == m, eidx, float(N_EXPERTS)), axis=0, keepdims=True)
        sel = eidx == first
        vals = jnp.where(sel, -jnp.inf, vals)
        sel_f = sel.astype(F32)
        sel_any = sel_any + sel_f
        sels.append(sel_f)
        tops.append(m)
        idx_ref[k:k + 1, :] = first.astype(I32)
    es = [jnp.exp(t - tops[0]) for t in tops]
    denom = functools.reduce(lambda a, b: a + b, es)
    for k in range(TOP_K):
        gate_ref[k:k + 1, :] = es[k] / denom
    gate_ref[TOP_K:, :] = jnp.zeros((SUBLANES - TOP_K, ROUTE_T), F32)

    excl = jnp.dot(sel_any.astype(BF16), upper_ref[...], preferred_element_type=F32) + carry_scr[:, 0:1]
    for k in range(TOP_K):
        rank_ref[k:k + 1, :] = jnp.sum(sels[k] * excl, axis=0, keepdims=True).astype(I32)
    new_carry = carry_scr[...] + jnp.sum(sel_any, axis=1, keepdims=True)
    carry_scr[...] = new_carry
    cnt_ref[...] = new_carry.astype(I32)


def _router(x1, g, wr_pad, br_col, upper):
    n = x1.shape[0]
    per_k = pl.BlockSpec((TOP_K, ROUTE_T), lambda i: (0, i))
    return pl.pallas_call(
        _router_kernel,
        grid=(n // ROUTE_T,),
        in_specs=[
            pl.BlockSpec((ROUTE_T, D_MODEL), lambda i: (i, 0)),
            pl.BlockSpec((1, D_MODEL), lambda i: (0, 0)),
            pl.BlockSpec((2, D_MODEL, LANES), lambda i: (0, 0, 0)),
            pl.BlockSpec((N_EXPERTS, 1), lambda i: (0, 0)),
            pl.BlockSpec((ROUTE_T, ROUTE_T), lambda i: (0, 0)),
        ],
        out_specs=[
            pl.BlockSpec((ROUTE_T * TOK_SUB, LANES), lambda i: (i, 0)),
            per_k,
            pl.BlockSpec((SUBLANES, ROUTE_T), lambda i: (0, i)),
            per_k,
            pl.BlockSpec((N_EXPERTS, LANES), lambda i: (0, 0)),
        ],
        out_shape=[
            jax.ShapeDtypeStruct((n * TOK_SUB, LANES), F32),
            jax.ShapeDtypeStruct((TOP_K, n), I32),
            jax.ShapeDtypeStruct((SUBLANES, n), F32),
            jax.ShapeDtypeStruct((TOP_K, n), I32),
            jax.ShapeDtypeStruct((N_EXPERTS, LANES), I32),
        ],
        scratch_shapes=[pltpu.VMEM((N_EXPERTS, LANES), F32)],
        compiler_params=pltpu.CompilerParams(
            dimension_semantics=("arbitrary",), vmem_limit_bytes=VMEM_LIMIT),
        name="router",
    )(x1, g, wr_pad, br_col, upper)


def _dispatch_kernel(dest_ref, pad_ref, hp_ref, xs_hbm, zero_scr, sem_rows):
    i = pl.program_id(0)
    tbl = i * (TOP_K * DISP_T)

    def zero_copy(s):
        return pltpu.make_async_copy(zero_scr.at[0:TOK_SUB, :], xs_hbm.at[_tile_rows(s), :], sem_rows)

    def zero_run(s):
        rows = pl.ds(pl.multiple_of(s * TOK_SUB, TOK_SUB), ZERO_RUN * TOK_SUB)
        return pltpu.make_async_copy(zero_scr, xs_hbm.at[rows, :], sem_rows)

    def zero_pads(act):
        for e in range(N_EXPERTS):
            lo, hi = pad_ref[e], pad_ref[N_EXPERTS + e]
            nrun = (hi - lo) // ZERO_RUN
            lax.fori_loop(0, nrun, lambda r, c: (act(zero_run(lo + r * ZERO_RUN)), c)[1], 0)
            lax.fori_loop(lo + nrun * ZERO_RUN, hi, lambda s, c: (act(zero_copy(s)), c)[1], 0)

    @pl.when(i == 0)
    def _():
        zero_scr[...] = jnp.zeros_like(zero_scr)
        zero_pads(lambda d: d.start())
        zero_pads(lambda d: d.wait())

    def tile_copy(t, k):
        d = dest_ref[tbl + t * TOP_K + k]
        return pltpu.make_async_copy(hp_ref.at[_tile_rows(t), :], xs_hbm.at[_tile_rows(d), :], sem_rows)

    def issue(tb, carry):
        for r in range(DMA_UNROLL):
            for k in range(TOP_K):
                tile_copy(tb * DMA_UNROLL + r, k).start(priority=(r * TOP_K + k) % 2)
        return carry

    lax.fori_loop(0, DISP_T // DMA_UNROLL, issue, 0)

    def drain(tb, carry):
        for _ in range(DMA_UNROLL * TOP_K):
            zero_copy(0).wait()
        return carry

    lax.fori_loop(0, DISP_T // DMA_UNROLL, drain, 0)


def _dispatch(dest_flat, pad_tbl, hp, cap):
    n = hp.shape[0] // TOK_SUB
    grid_spec = pltpu.PrefetchScalarGridSpec(
        num_scalar_prefetch=2,
        grid=(n // DISP_T,),
        in_specs=[pl.BlockSpec((DISP_T * TOK_SUB, LANES), lambda i, d, p: (i, 0))],
        out_specs=pl.BlockSpec(memory_space=pl.ANY),
        scratch_shapes=[pltpu.VMEM((ZERO_RUN * TOK_SUB, LANES), F32), pltpu.SemaphoreType.DMA],
    )
    return pl.pallas_call(
        _dispatch_kernel,
        grid_spec=grid_spec,
        out_shape=jax.ShapeDtypeStruct((cap * TOK_SUB, LANES), F32),
        compiler_params=pltpu.CompilerParams(dimension_semantics=("arbitrary",)),
        name="dispatch",
    )(dest_flat, pad_tbl, hp)


def _expert_kernel(be_ref, nxt_ref, nv_ref, xs_ref, w1_hbm, b1_ref, w2_hbm, b2_ref, ys_ref,
                   w1f_scr, w2f_scr, w1b_scr, w2b_scr, sems):
    b = pl.program_id(0)
    valid = nv_ref[b]
    used = valid > 0
    new_expert = jnp.logical_or(b == 0, be_ref[b] != be_ref[jnp.maximum(b - 1, 0)])

    def fetch(e):
        return (pltpu.make_async_copy(w1_hbm.at[e], w1f_scr, sems.at[0]),
                pltpu.make_async_copy(w2_hbm.at[e], w2f_scr, sems.at[1]))

    @pl.when(jnp.logical_and(used, b == 0))
    def _():
        for c in fetch(be_ref[0]):
            c.start()

    @pl.when(jnp.logical_and(used, new_expert))
    def _():
        for c in fetch(be_ref[b]):
            c.wait()

        def cast(rb, carry):
            rows = pl.ds(pl.multiple_of(rb * CAST_ROWS, CAST_ROWS), CAST_ROWS)
            w1b_scr[rows, :] = w1f_scr[rows, :].astype(BF16)
            w2b_scr[rows, :] = w2f_scr[rows, :].astype(BF16)
            return carry

        lax.fori_loop(0, D_MODEL // CAST_ROWS, cast, 0)

        @pl.when(nxt_ref[b] >= 0)
        def _():
            for c in fetch(nxt_ref[b]):
                c.start()

    def expert_mlp(rows):
        x = _load_token_tiles(xs_ref, rows).astype(BF16)
        hid = jnp.dot(x, w1b_scr[...], preferred_element_type=F32) + b1_ref[0]
        glu = jnp.minimum(hid[:, :D_FF_EXPERT], SWIGLU_LIMIT)
        lin = jnp.clip(hid[:, D_FF_EXPERT:], -SWIGLU_LIMIT, SWIGLU_LIMIT)
        act = glu * _sigmoid(SWIGLU_ALPHA * glu) * (lin + 1.0)
        y = jnp.dot(act.astype(BF16), w2b_scr[...], preferred_element_type=F32) + b2_ref[0]
        _store_token_tiles(ys_ref, y, rows)

    @pl.when(valid > MOE_SUB)
    def _():
        expert_mlp(MOE_BLOCK)

    @pl.when(jnp.logical_and(used, valid <= MOE_SUB))
    def _():
        expert_mlp(MOE_SUB)


def _experts(blk_expert, blk_next, blk_valid, xs, w1, b1, w2, b2):
    cap = xs.shape[0] // TOK_SUB
    n_blocks = cap // MOE_BLOCK
    grid_spec = pltpu.PrefetchScalarGridSpec(
        num_scalar_prefetch=3,
        grid=(n_blocks,),
        in_specs=[
            pl.BlockSpec((MOE_BLOCK * TOK_SUB, LANES), lambda b, be, nx, nv: (b, 0)),
            pl.BlockSpec(memory_space=pl.ANY),
            pl.BlockSpec((1, 1, 2 * D_FF_EXPERT), lambda b, be, nx, nv: (be[b], 0, 0)),
            pl.BlockSpec(memory_space=pl.ANY),
            pl.BlockSpec((1, 1, D_MODEL), lambda b, be, nx, nv: (be[b], 0, 0)),
        ],
        out_specs=pl.BlockSpec((MOE_BLOCK * TOK_SUB, LANES), lambda b, be, nx, nv: (b, 0)),
        scratch_shapes=[pltpu.VMEM((D_MODEL, 2 * D_FF_EXPERT), F32),
                        pltpu.VMEM((D_FF_EXPERT, D_MODEL), F32),
                        pltpu.VMEM((D_MODEL, 2 * D_FF_EXPERT), BF16),
                        pltpu.VMEM((D_FF_EXPERT, D_MODEL), BF16),
                        pltpu.SemaphoreType.DMA((2,))],
    )
    assert D_MODEL == D_FF_EXPERT
    assert MOE_BLOCK == 2 * MOE_SUB
    return pl.pallas_call(
        _expert_kernel,
        grid_spec=grid_spec,
        out_shape=jax.ShapeDtypeStruct((cap * TOK_SUB, LANES), F32),
        compiler_params=pltpu.CompilerParams(
            dimension_semantics=("arbitrary",), vmem_limit_bytes=VMEM_LIMIT),
        name="experts",
    )(blk_expert, blk_next, blk_valid, xs, w1, b1, w2, b2)


def _combine_kernel(dest_ref, ys_hbm, x1_ref, gate_ref, p_ref, pg_ref, wg_ref, wp_ref, fg_ref,
                    out_ref, ybuf, sems, *, final):
    i = pl.program_id(0)
    nsteps = pl.num_programs(0)
    buf_tokens = TOP_K * COMB_T

    def issue_part(step, slot, part, parts):
        tbl = step * buf_tokens
        per_part = COMB_T // DMA_UNROLL // parts

        def issue(tb, carry):
            for r in range(DMA_UNROLL):
                for k in range(TOP_K):
                    t = tb * DMA_UNROLL + r
                    pltpu.make_async_copy(
                        ys_hbm.at[_tile_rows(dest_ref[tbl + t * TOP_K + k]), :],
                        ybuf.at[_tile_rows(slot * buf_tokens + k * COMB_T + t), :],
                        sems.at[slot]).start(priority=(r * TOP_K + k) % 2)
            return carry

        last = jnp.where(step < nsteps, (part + 1) * per_part, part * per_part)
        lax.fori_loop(part * per_part, last, issue, 0)

    def drain_all(slot):
        def drain(tb, carry):
            for _ in range(DMA_UNROLL * TOP_K):
                pltpu.make_async_copy(ys_hbm.at[_tile_rows(0), :],
                                      ybuf.at[_tile_rows(slot * buf_tokens), :], sems.at[slot]).wait()
            return carry

        lax.fori_loop(0, COMB_T // DMA_UNROLL, drain, 0)

    @pl.when(i == 0)
    def _():
        issue_part(0, 0, 0, 1)

    nsub = COMB_T // COMB_SUB
    for slot in range(2):
        @pl.when(i % 2 == slot)
        def _():
            issue_part(i + 1, 1 - slot, 0, ISSUE_PARTS)
            drain_all(slot)
            for sub in range(nsub):
                if 0 < sub < ISSUE_PARTS:
                    issue_part(i + 1, 1 - slot, sub, ISSUE_PARTS)
                rows = slice(sub * COMB_SUB, (sub + 1) * COMB_SUB)
                x2 = x1_ref[rows, :]
                for k in range(TOP_K):
                    x2 = x2 + gate_ref[rows, k:k + 1] * _load_token_tiles(
                        ybuf, COMB_SUB, slot * buf_tokens + k * COMB_T + sub * COMB_SUB)
                hp = _rms(x2, pg_ref[...]).astype(BF16)
                gate = _sigmoid(jnp.dot(hp, wg_ref[...], preferred_element_type=F32))
                emb = jnp.dot(p_ref[rows, :].astype(BF16), wp_ref[...], preferred_element_type=F32)
                x3 = x2 + gate * emb
                if final:
                    x3 = _rms(x3, fg_ref[...])
                out_ref[rows, :] = x3


def _combine(dest_flat, ys, x1, gates_t, p2d, pg, wg, wp, fg, final):
    n = x1.shape[0]
    grid_spec = pltpu.PrefetchScalarGridSpec(
        num_scalar_prefetch=1,
        grid=(n // COMB_T,),
        in_specs=[
            pl.BlockSpec(memory_space=pl.ANY),
            pl.BlockSpec((COMB_T, D_MODEL), lambda i, d: (i, 0)),
            pl.BlockSpec((COMB_T, SUBLANES), lambda i, d: (i, 0)),
            pl.BlockSpec((COMB_T, PLE_DIM), lambda i, d: (i, 0)),
            pl.BlockSpec((1, D_MODEL), lambda i, d: (0, 0)),
            pl.BlockSpec((D_MODEL, D_MODEL), lambda i, d: (0, 0)),
            pl.BlockSpec((PLE_DIM, D_MODEL), lambda i, d: (0, 0)),
            pl.BlockSpec((1, D_MODEL), lambda i, d: (0, 0)),
        ],
        out_specs=pl.BlockSpec((COMB_T, D_MODEL), lambda i, d: (i, 0)),
        scratch_shapes=[pltpu.VMEM((2 * TOP_K * COMB_T * TOK_SUB, LANES), F32),
                        pltpu.SemaphoreType.DMA((2,))],
    )
    return pl.pallas_call(
        functools.partial(_combine_kernel, final=final),
        grid_spec=grid_spec,
        out_shape=jax.ShapeDtypeStruct((n, D_MODEL), F32),
        compiler_params=pltpu.CompilerParams(
            dimension_semantics=("arbitrary",), vmem_limit_bytes=VMEM_LIMIT),
        name="combine",
    )(dest_flat, ys, x1, gates_t, p2d, pg, wg, wp, fg)


def _layer(x2d, p2d, bsz, seq, mix_norm, w_in, sgu_ln_g, sgu_ln_b, sgu_w, sgu_b, conv_w, conv_b,
           dt_bias, a_log, d_skip, ssm_norm, w_branch_a, w_branch_b, w_out, ffn_norm, w_router,
           b_router, w1, b1, w2, b2, ple_norm, w_ple_gate, w_ple_proj, final_norm, final):
    n = x2d.shape[0]
    row = lambda a: a.reshape(1, -1).astype(F32)

    w_main = jnp.concatenate([w_in[:, :OFF_DT], w_in[:, OFF_GA:]], axis=1).astype(BF16)
    w_dt = jnp.pad(w_in[:, OFF_DT:OFF_GA], ((0, 0), (0, LANES - SSM_HEADS))).astype(BF16)
    pos = jnp.arange(SGU_LEN)
    allowed = (pos[None, :] // CHUNK) <= (pos[:, None] // CHUNK)
    wsgu = jnp.where(allowed[None], sgu_w, 0.0).astype(BF16)
    bsgu = jnp.repeat(sgu_b.T, SGU_HEAD_DIM, axis=1).astype(F32)
    dtb = dt_bias.reshape(-1, 1).astype(F32)
    acol = -jnp.exp(a_log.astype(F32)).reshape(-1, 1)
    dskip = jnp.repeat(d_skip.astype(F32), SSM_HEAD_DIM).reshape(1, -1)
    head_of_col = jnp.arange(SSM_INNER) // SSM_HEAD_DIM
    eexp = (jnp.arange(LANES)[:, None] == head_of_col[None, :]).astype(BF16)
    triu = (jnp.arange(SSD_CHUNK)[:, None] <= jnp.arange(SSD_CHUNK)[None, :]).astype(BF16)
    t_out = jnp.arange((SSM_CONV - 1) * SSD_CHUNK)
    src = SSD_CHUNK + t_out % SSD_CHUNK - (SSM_CONV - 1) + t_out // SSD_CHUNK
    shift = (src[:, None] == jnp.arange(2 * SSD_CHUNK)[None, :]).astype(BF16)
    consts = [row(sgu_ln_g), row(sgu_ln_b), wsgu, bsgu, conv_w.astype(F32), row(conv_b), dtb, acol,
              dskip, row(ssm_norm), eexp, shift, triu]

    proj, dt_raw = _in_proj(x2d, row(mix_norm), w_main, w_dt)
    x1 = _mixer(proj, dt_raw, x2d, bsz, seq, consts, w_branch_a.astype(BF16),
                w_branch_b.astype(BF16), w_out.astype(BF16))

    wr_f = jnp.pad(w_router.astype(F32), ((0, 0), (0, LANES - N_EXPERTS)))
    wr_hi = wr_f.astype(BF16)
    wr_pad = jnp.stack([wr_hi, (wr_f - wr_hi.astype(F32)).astype(BF16)])
    upper = (jnp.arange(ROUTE_T)[:, None] < jnp.arange(ROUTE_T)[None, :]).astype(BF16)
    hp, idx, gates, rank, cnt = _router(x1, row(ffn_norm), wr_pad, b_router.reshape(-1, 1).astype(F32),
                                        upper)

    counts = cnt[:, 0]
    padded = (counts + MOE_BLOCK - 1) // MOE_BLOCK * MOE_BLOCK
    pend = jnp.cumsum(padded)
    pstart = pend - padded
    nk = n * TOP_K
    cap = (nk + MOE_BLOCK - 1) // MOE_BLOCK * MOE_BLOCK + N_EXPERTS * MOE_BLOCK
    n_blocks = cap // MOE_BLOCK
    ex = jnp.arange(N_EXPERTS, dtype=I32)
    dest = rank + jnp.sum(jnp.where(idx[..., None] == ex, pstart.astype(I32), 0), axis=-1)
    blk_start = jnp.arange(n_blocks, dtype=I32) * MOE_BLOCK
    blk_expert = jnp.minimum(jnp.sum((pend[None, :] <= blk_start[:, None]).astype(I32), axis=1),
                             N_EXPERTS - 1).astype(I32)
    of_block = blk_expert[:, None] == ex[None, :]
    per_block = lambda v: jnp.sum(jnp.where(of_block, v[None, :].astype(I32), 0), axis=1)
    blk_valid = jnp.where(blk_start < pend[-1],
                          jnp.clip(per_block(pstart + counts) - blk_start, 0, MOE_BLOCK), 0).astype(I32)
    live = jnp.where(counts > 0, ex, N_EXPERTS)
    later = jnp.concatenate([lax.cummin(live[::-1])[::-1][1:], jnp.full((1,), N_EXPERTS, I32)])
    blk_next = per_block(jnp.where(later < N_EXPERTS, later, -1))

    dest_flat = dest.T.reshape(-1)
    used_end = pstart + (counts + MOE_SUB - 1) // MOE_SUB * MOE_SUB
    pad_tbl = jnp.concatenate([pstart + counts, used_end]).astype(I32)

    xs = _dispatch(dest_flat, pad_tbl, hp, cap)
    ys = _experts(blk_expert, blk_next, blk_valid, xs, w1.astype(F32),
                  b1.reshape(N_EXPERTS, 1, -1).astype(F32), w2.astype(F32),
                  b2.reshape(N_EXPERTS, 1, -1).astype(F32))
    return _combine(dest_flat, ys, x1, gates.T, p2d, row(ple_norm), w_ple_gate.astype(BF16),
                    w_ple_proj.astype(BF16), row(final_norm), final)


def kernel(x, p, mix_norm, w_in, sgu_ln_g, sgu_ln_b, sgu_w, sgu_b, conv_w, conv_b, dt_bias, a_log,
           d_skip, ssm_norm, w_branch_a, w_branch_b, w_out, ffn_norm, w_router, b_router, w1, b1,
           w2, b2, ple_norm, w_ple_gate, w_ple_proj, final_norm):
    bsz, seq, d = x.shape
    depth = w_in.shape[0]
    assert d == D_MODEL and seq % MIX_T == 0 and (bsz * seq) % max(IN_TM, ROUTE_T, DISP_T) == 0
    assert ROUTE_T == DISP_T == COMB_T
    x2d = x.reshape(bsz * seq, d)
    for i in range(depth):
        x2d = _layer(x2d, p[i].reshape(bsz * seq, PLE_DIM), bsz, seq, mix_norm[i], w_in[i],
                     sgu_ln_g[i], sgu_ln_b[i], sgu_w[i], sgu_b[i], conv_w[i], conv_b[i], dt_bias[i],
                     a_log[i], d_skip[i], ssm_norm[i], w_branch_a[i], w_branch_b[i], w_out[i],
                     ffn_norm[i], w_router[i], b_router[i], w1[i], b1[i], w2[i], b2[i], ple_norm[i],
                     w_ple_gate[i], w_ple_proj[i], final_norm, final=(i == depth - 1))
    return x2d.reshape(bsz, seq, d)
```
